```python
import math
import jax, jax.numpy as jnp
from jax import lax
import numpy as np

D_MODEL = 2048
BATCH = 4
SEQ = 2048
DEPTH = 4
DEC_BATCH = 8
DEC_SEQ = 1
PAST_LEN = 16384
PAGE_SIZE = 128

N_MIXERS = 4
N_HEADS = 16
HEAD_DIM = 128
N_KV_HEADS = 4
Q_PER_KV = N_HEADS // N_KV_HEADS
Q_DIM = N_HEADS * HEAD_DIM
KV_DIM = N_KV_HEADS * HEAD_DIM
Q_BLOCK = 128
MOBA_BLOCK = 256
MOBA_TOPK = 3
MOBA_Q_CHUNK = 16
SSM_D_INNER = 2 * D_MODEL
SSM_HEAD_DIM = 64
SSM_HEADS = SSM_D_INNER // SSM_HEAD_DIM
SSM_GROUPS = 8
SSM_HPG = SSM_HEADS // SSM_GROUPS
SSM_D_STATE = 128
SSM_CONV = 4
SSM_CHUNK = 128
SSM_CONV_DIM = SSM_D_INNER + 2 * SSM_GROUPS * SSM_D_STATE
SSM_IN_DIM = SSM_D_INNER + SSM_CONV_DIM + SSM_HEADS
CONF_WIDTH = 31
D_FF = 5632
PLE_DIM = 256
LN_EPS = 1e-5
DN_ALPHA = (2 * DEPTH) ** 0.25
DN_BETA = (8 * DEPTH) ** -0.25
N_SB = (DEPTH + 3) // 4
N_SSM = (DEPTH + 2) // 4
N_CONF = (DEPTH + 1) // 4
N_MOBA = DEPTH // 4

kernel_name = 'hybrid_sb_ssd_conformer_moba_step'


def layer_norm(x, g, b):
    xf = x.astype(jnp.float32)
    mu = jnp.mean(xf, axis=-1, keepdims=True)
    var = jnp.mean(jnp.square(xf - mu), axis=-1, keepdims=True)
    return ((xf - mu) * lax.rsqrt(var + LN_EPS) * g + b).astype(x.dtype)


def post_norm(x, y, g, b):
    return layer_norm(DN_ALPHA * x + y, g, b)


def ffn_half(x, w1, w3, w2, g, b):
    h = jax.nn.silu(x @ w1) * (x @ w3)
    return post_norm(x, 0.5 * (h @ w2), g, b)


def ple_add(x, p, w_proj, w_gate, g, b):
    return post_norm(x, jax.nn.sigmoid(x @ w_gate) * (p @ w_proj), g, b)


def causal_dwconv(u, prev, w, b):
    full = jnp.concatenate([prev.astype(u.dtype), u], axis=1)
    out = lax.conv_general_dilated(full, w[:, None, :].astype(u.dtype), window_strides=(1,), padding='VALID',
                                   dimension_numbers=('NWC', 'WIO', 'NWC'), feature_group_count=u.shape[-1])
    return out + b, full[:, full.shape[1] - (w.shape[0] - 1):]


def gather_pages(cache, page_table):
    rows = cache[page_table]
    return rows.reshape((page_table.shape[0], -1) + cache.shape[2:])


def qkv_proj(x, w_qkv):
    bsz, L, _ = x.shape
    q, k, v = jnp.split(x @ w_qkv, [Q_DIM, Q_DIM + KV_DIM], axis=-1)
    return (q.reshape(bsz, L, N_KV_HEADS, Q_PER_KV, HEAD_DIM),
            k.reshape(bsz, L, N_KV_HEADS, HEAD_DIM),
            v.reshape(bsz, L, N_KV_HEADS, HEAD_DIM))


def merge_heads(o):
    return o.reshape(o.shape[:2] + (-1,))


def stick_breaking_attend(q, k, v, q_pos, k_pos):
    z = jnp.einsum('btkgd,bskd->bkgts', q, k).astype(jnp.float32) * (HEAD_DIM ** -0.5)
    mask = k_pos[None, :] < q_pos[:, None]
    log_keep = jnp.where(mask, jax.nn.log_sigmoid(-z), 0.0)
    log_surv = lax.cumsum(log_keep, axis=log_keep.ndim - 1, reverse=True) - log_keep
    w = jnp.where(mask, jnp.exp(jax.nn.log_sigmoid(z) + log_surv), 0.0)
    return jnp.einsum('bkgts,bskd->btkgd', w.astype(v.dtype), v)


def sb_prompt(q, k, v):
    bsz, L = q.shape[:2]
    nb = L // Q_BLOCK
    qb = jnp.moveaxis(q.reshape((bsz, nb, Q_BLOCK) + q.shape[2:]), 1, 0)
    pos = jnp.arange(L, dtype=jnp.int32)
    o = lax.map(lambda a: stick_breaking_attend(a[0], k, v, a[1], pos), (qb, pos.reshape(nb, Q_BLOCK)))
    return jnp.moveaxis(o, 0, 1).reshape(q.shape)


def moba_blocks(k, v):
    bsz, L = k.shape[:2]
    pad = (-L) % MOBA_BLOCK
    nb = (L + pad) // MOBA_BLOCK

    def blk(t):
        t = jnp.pad(t, ((0, 0), (0, pad), (0, 0), (0, 0)))
        return jnp.transpose(t.reshape(bsz, nb, MOBA_BLOCK, N_KV_HEADS, HEAD_DIM), (0, 3, 1, 2, 4))
    kb, vb = blk(k), blk(v)
    k_mean = jnp.mean(kb.astype(jnp.float32), axis=3).astype(k.dtype)
    return kb, vb, k_mean


def moba_attend(q, q_pos, kb, vb, k_mean):
    bsz, T = q.shape[:2]
    nb = kb.shape[2]
    scale = HEAD_DIM ** -0.5
    n_past = q_pos // MOBA_BLOCK
    gate = jnp.einsum('btkgd,bknd->btkgn', q, k_mean).astype(jnp.float32)
    cand = (jnp.arange(nb)[None, :] < n_past[:, None])[None, :, None, None, :]
    gate = jnp.where(cand, gate, -jnp.inf)
    n_sel = min(MOBA_TOPK, nb)
    _, sel = lax.top_k(gate, n_sel)
    sel_ok = (jnp.arange(n_sel)[None, :] < n_past[:, None])[None, :, None, None, :, None]
    bi = jnp.arange(bsz)[:, None, None, None, None]
    hi = jnp.arange(N_KV_HEADS)[None, None, :, None, None]
    k_sel = kb[bi, hi, sel]
    v_sel = vb[bi, hi, sel]
    bo = jnp.arange(bsz)[:, None, None]
    ho = jnp.arange(N_KV_HEADS)[None, None, :]
    oo = n_past[None, :, None]
    k_own = kb[bo, ho, oo]
    v_own = vb[bo, ho, oo]
    own_pos = n_past[:, None] * MOBA_BLOCK + jnp.arange(MOBA_BLOCK)[None, :]
    own_ok = (own_pos <= q_pos[:, None])[None, :, None, None, :]
    s_sel = jnp.einsum('btkgd,btkgjsd->btkgjs', q, k_sel).astype(jnp.float32) * scale
    s_sel = jnp.where(sel_ok, s_sel, -jnp.inf).reshape(bsz, T, N_KV_HEADS, Q_PER_KV, n_sel * MOBA_BLOCK)
    s_own = jnp.einsum('btkgd,btksd->btkgs', q, k_own).astype(jnp.float32) * scale
    s_own = jnp.where(own_ok, s_own, -jnp.inf)
    p = jax.nn.softmax(jnp.concatenate([s_sel, s_own], axis=-1), axis=-1).astype(vb.dtype)
    p_sel = p[..., :n_sel * MOBA_BLOCK].reshape(bsz, T, N_KV_HEADS, Q_PER_KV, n_sel, MOBA_BLOCK)
    p_own = p[..., n_sel * MOBA_BLOCK:]
    return (jnp.einsum('btkgjs,btkgjsd->btkgd', p_sel, v_sel)
            + jnp.einsum('btkgs,btksd->btkgd', p_own, v_own))


def moba_prompt(q, k, v):
    bsz, L = q.shape[:2]
    kb, vb, km = moba_blocks(k, v)
    nc = L // MOBA_Q_CHUNK
    qc = jnp.moveaxis(q.reshape((bsz, nc, MOBA_Q_CHUNK) + q.shape[2:]), 1, 0)
    pos = jnp.arange(L, dtype=jnp.int32).reshape(nc, MOBA_Q_CHUNK)
    o = lax.map(lambda a: moba_attend(a[0], a[1], kb, vb, km), (qc, pos))
    return jnp.moveaxis(o, 0, 1).reshape(q.shape)


def ssd_chunked(x, dt, a, bm, cm, h0, chunk):
    bsz, L = x.shape[:2]
    nc = L // chunk

    def chunks(t):
        return t.reshape((bsz, nc, chunk) + t.shape[2:])
    xc, dtc, bc, cc = chunks(x), chunks(dt), chunks(bm), chunks(cm)
    cum = jnp.cumsum(dtc * a, axis=2)
    causal = jnp.tril(jnp.ones((chunk, chunk), dtype=bool))[:, :, None, None]
    seg = cum[:, :, :, None] - cum[:, :, None, :]
    decay = jnp.exp(jnp.where(causal, seg, -jnp.inf))
    cb = jnp.einsum('bclgn,bcsgn->bclsg', cc, bc)
    y_intra = jnp.einsum('bclsgr,bcsgrp->bclgrp', cb[..., None] * decay * dtc[:, :, None], xc)
    to_end = jnp.exp(cum[:, :, -1:] - cum) * dtc
    s_chunk = jnp.einsum('bclgr,bclgn,bclgrp->bcgrpn', to_end, bc, xc)
    chunk_decay = jnp.exp(cum[:, :, -1])

    def step(h, inp):
        s_c, d_c = inp
        return d_c[..., None, None] * h + s_c, h
    h_last, h_start = lax.scan(step, h0, (jnp.moveaxis(s_chunk, 1, 0), jnp.moveaxis(chunk_decay, 1, 0)))
    h_start = jnp.moveaxis(h_start, 0, 1)
    y_inter = jnp.einsum('bclgn,bcgrpn->bclgrp', cc, h_start) * jnp.exp(cum)[..., None]
    return (y_intra + y_inter).reshape(x.shape), h_last


def gated_rms_norm(y, z, g):
    bsz, L, _ = y.shape
    h = (y * jax.nn.silu(z.astype(jnp.float32))).reshape(bsz, L, SSM_GROUPS, -1)
    h = h * lax.rsqrt(jnp.mean(h * h, axis=-1, keepdims=True) + LN_EPS)
    return h.reshape(bsz, L, -1) * g


def mamba2_mixer(x, h0, conv_buf, w_in, conv_w, conv_b, dt_bias, a_log, d_skip, norm_g, w_out, chunk):
    bsz, L, _ = x.shape
    f32 = jnp.float32
    z, xbc, dt = jnp.split(x @ w_in, [SSM_D_INNER, SSM_D_INNER + SSM_CONV_DIM], axis=-1)
    xbc, new_buf = causal_dwconv(xbc, conv_buf, conv_w, conv_b)
    xbc = jax.nn.silu(xbc)
    xh, bm, cm = jnp.split(xbc, [SSM_D_INNER, SSM_D_INNER + SSM_GROUPS * SSM_D_STATE], axis=-1)
    xh = xh.astype(f32).reshape(bsz, L, SSM_GROUPS, SSM_HPG, SSM_HEAD_DIM)
    bm = bm.astype(f32).reshape(bsz, L, SSM_GROUPS, SSM_D_STATE)
    cm = cm.astype(f32).reshape(bsz, L, SSM_GROUPS, SSM_D_STATE)
    dt = jax.nn.softplus((dt + dt_bias).astype(f32)).reshape(bsz, L, SSM_GROUPS, SSM_HPG)
    a = -jnp.exp(a_log.astype(f32)).reshape(SSM_GROUPS, SSM_HPG)
    h0 = h0.astype(f32).reshape(bsz, SSM_GROUPS, SSM_HPG, SSM_HEAD_DIM, SSM_D_STATE)
    y, h = ssd_chunked(xh, dt, a, bm, cm, h0, chunk)
    y = y + d_skip.astype(f32).reshape(SSM_GROUPS, SSM_HPG)[:, :, None] * xh
    y = gated_rms_norm(y.reshape(bsz, L, SSM_D_INNER), z, norm_g).astype(x.dtype)
    return (y @ w_out, h.reshape(bsz, SSM_HEADS, SSM_HEAD_DIM, SSM_D_STATE).astype(x.dtype), new_buf)


def conformer_conv_mixer(x, buf, w_pw1, b_pw1, w_dw, b_dw, g, b, w_pw2, b_pw2):
    u_a, u_g = jnp.split(x @ w_pw1 + b_pw1, 2, axis=-1)
    u = u_a * jax.nn.sigmoid(u_g)
    u, new_buf = causal_dwconv(u, buf, w_dw, b_dw)
    u = jax.nn.silu(layer_norm(u, g, b))
    return u @ w_pw2 + b_pw2, new_buf


def setup_inputs(seed: int = 0) -> dict:
    key = jax.random.key(seed)
    keys = jax.random.split(key, 48)
    counter = [0]
    f32 = jnp.float32

    def nxt():
        counter[0] += 1
        return keys[counter[0] - 1]

    def nrm(shape, scale=1.0):
        return jax.random.normal(nxt(), shape, f32) * scale

    def dense(shape, out_scale=1.0):
        return nrm(shape, shape[-2] ** -0.5 * out_scale)

    n_pages = PAST_LEN // PAGE_SIZE
    n_used = DEC_BATCH * n_pages
    n_phys = n_used + (n_used + 3) // 4
    inp = {}
    inp['x_prompt'] = nrm((BATCH, SEQ, D_MODEL))
    inp['x_sample'] = nrm((DEC_BATCH, DEC_SEQ, D_MODEL))
    inp['p_prompt'] = nrm((DEPTH, BATCH, SEQ, PLE_DIM))
    inp['p_sample'] = nrm((DEPTH, DEC_BATCH, DEC_SEQ, PLE_DIM))
    inp['cache_sb_k'] = nrm((N_SB, n_phys, PAGE_SIZE, N_KV_HEADS, HEAD_DIM))
    inp['cache_sb_v'] = nrm((N_SB, n_phys, PAGE_SIZE, N_KV_HEADS, HEAD_DIM))
    inp['cache_moba_k'] = nrm((N_MOBA, n_phys, PAGE_SIZE, N_KV_HEADS, HEAD_DIM))
    inp['cache_moba_v'] = nrm((N_MOBA, n_phys, PAGE_SIZE, N_KV_HEADS, HEAD_DIM))
    inp['state_ssm'] = nrm((N_SSM, DEC_BATCH, SSM_HEADS, SSM_HEAD_DIM, SSM_D_STATE), 0.2)
    inp['state_ssm_conv'] = nrm((N_SSM, DEC_BATCH, SSM_CONV - 1, SSM_CONV_DIM))
    inp['state_conf_conv'] = nrm((N_CONF, DEC_BATCH, CONF_WIDTH - 1, D_MODEL))
    inp['page_table'] = jax.random.permutation(nxt(), n_phys)[:n_used].reshape(DEC_BATCH, n_pages).astype(jnp.int32)
    inp['ln_g'] = 1.0 + nrm((DEPTH, 4, D_MODEL), 0.02)
    inp['ln_b'] = nrm((DEPTH, 4, D_MODEL), 0.02)
    inp['ffn_w1'] = dense((DEPTH, 2, D_MODEL, D_FF))
    inp['ffn_w3'] = dense((DEPTH, 2, D_MODEL, D_FF))
    inp['ffn_w2'] = dense((DEPTH, 2, D_FF, D_MODEL), DN_BETA)
    inp['ple_w_proj'] = dense((DEPTH, PLE_DIM, D_MODEL), DN_BETA)
    inp['ple_w_gate'] = dense((DEPTH, D_MODEL, D_MODEL))
    inp['sb_w_qkv'] = dense((N_SB, D_MODEL, Q_DIM + 2 * KV_DIM))
    inp['sb_w_o'] = dense((N_SB, Q_DIM, D_MODEL), DN_BETA)
    inp['ssm_w_in'] = dense((N_SSM, D_MODEL, SSM_IN_DIM))
    inp['ssm_conv_w'] = dense((N_SSM, SSM_CONV, SSM_CONV_DIM))
    inp['ssm_conv_b'] = nrm((N_SSM, SSM_CONV_DIM), 0.02)
    dt0 = jnp.exp(jax.random.uniform(nxt(), (N_SSM, SSM_HEADS), f32, math.log(1e-3), math.log(1e-1)))
    inp['ssm_dt_bias'] = dt0 + jnp.log(-jnp.expm1(-dt0))
    inp['ssm_a_log'] = jnp.log(jax.random.uniform(nxt(), (N_SSM, SSM_HEADS), f32, 1.0, 16.0))
    inp['ssm_d'] = 1.0 + nrm((N_SSM, SSM_HEADS), 0.02)
    inp['ssm_norm_g'] = 1.0 + nrm((N_SSM, SSM_D_INNER), 0.02)
    inp['ssm_w_out'] = dense((N_SSM, SSM_D_INNER, D_MODEL), DN_BETA)
    inp['conf_w_pw1'] = dense((N_CONF, D_MODEL, 2 * D_MODEL))
    inp['conf_b_pw1'] = nrm((N_CONF, 2 * D_MODEL), 0.02)
    inp['conf_w_dw'] = dense((N_CONF, CONF_WIDTH, D_MODEL))
    inp['conf_b_dw'] = nrm((N_CONF, D_MODEL), 0.02)
    inp['conf_ln_g'] = 1.0 + nrm((N_CONF, D_MODEL), 0.02)
    inp['conf_ln_b'] = nrm((N_CONF, D_MODEL), 0.02)
    inp['conf_w_pw2'] = dense((N_CONF, D_MODEL, D_MODEL), DN_BETA)
    inp['conf_b_pw2'] = nrm((N_CONF, D_MODEL), 0.02)
    inp['moba_w_qkv'] = dense((N_MOBA, D_MODEL, Q_DIM + 2 * KV_DIM))
    inp['moba_w_o'] = dense((N_MOBA, Q_DIM, D_MODEL), DN_BETA)
    return inp


def reference(x_prompt, x_sample, p_prompt, p_sample, cache_sb_k, cache_sb_v, cache_moba_k, cache_moba_v,
              state_ssm, state_ssm_conv, state_conf_conv, page_table,
              ln_g, ln_b, ffn_w1, ffn_w3, ffn_w2, ple_w_proj, ple_w_gate, sb_w_qkv, sb_w_o,
              ssm_w_in, ssm_conv_w, ssm_conv_b, ssm_dt_bias, ssm_a_log, ssm_d, ssm_norm_g, ssm_w_out,
              conf_w_pw1, conf_b_pw1, conf_w_dw, conf_b_dw, conf_ln_g, conf_ln_b, conf_w_pw2, conf_b_pw2,
              moba_w_qkv, moba_w_o):
    xp, xs = x_prompt, x_sample
    n_p = xp.shape[0]
    t_new = xs.shape[1]
    past = page_table.shape[1] * PAGE_SIZE
    pos_s = past + jnp.arange(t_new, dtype=jnp.int32)
    kpos_s = jnp.arange(past + t_new, dtype=jnp.int32)
    sb_kp, sb_vp, sb_ks, sb_vs = [], [], [], []
    ssm_hp, ssm_hs, ssm_cp, ssm_cs = [], [], [], []
    conf_cp, conf_cs = [], []
    mo_kp, mo_vp, mo_ks, mo_vs = [], [], [], []
    for i in range(DEPTH):
        m, j = i % N_MIXERS, i // N_MIXERS
        f1 = (ffn_w1[i, 0], ffn_w3[i, 0], ffn_w2[i, 0], ln_g[i, 0], ln_b[i, 0])
        xp, xs = ffn_half(xp, *f1), ffn_half(xs, *f1)
        if m == 0:
            q, k, v = qkv_proj(xp, sb_w_qkv[j])
            yp = merge_heads(sb_prompt(q, k, v)) @ sb_w_o[j]
            sb_kp.append(k)
            sb_vp.append(v)
            q, k, v = qkv_proj(xs, sb_w_qkv[j])
            k_all = jnp.concatenate([gather_pages(cache_sb_k[j], page_table).astype(k.dtype), k], axis=1)
            v_all = jnp.concatenate([gather_pages(cache_sb_v[j], page_table).astype(v.dtype), v], axis=1)
            ys = merge_heads(stick_breaking_attend(q, k_all, v_all, pos_s, kpos_s)) @ sb_w_o[j]
            sb_ks.append(k)
            sb_vs.append(v)
        elif m == 1:
            ssm_w = (ssm_w_in[j], ssm_conv_w[j], ssm_conv_b[j], ssm_dt_bias[j], ssm_a_log[j], ssm_d[j],
                     ssm_norm_g[j], ssm_w_out[j])
            h0p = jnp.zeros((n_p, SSM_HEADS, SSM_HEAD_DIM, SSM_D_STATE), xp.dtype)
            bufp = jnp.zeros((n_p, SSM_CONV - 1, SSM_CONV_DIM), xp.dtype)
            yp, h, buf = mamba2_mixer(xp, h0p, bufp, *ssm_w, SSM_CHUNK)
            ssm_hp.append(h)
            ssm_cp.append(buf)
            ys, h, buf = mamba2_mixer(xs, state_ssm[j], state_ssm_conv[j], *ssm_w, t_new)
            ssm_hs.append(h)
            ssm_cs.append(buf)
        elif m == 2:
            conf_w = (conf_w_pw1[j], conf_b_pw1[j], conf_w_dw[j], conf_b_dw[j], conf_ln_g[j], conf_ln_b[j],
                      conf_w_pw2[j], conf_b_pw2[j])
            bufp = jnp.zeros((n_p, CONF_WIDTH - 1, D_MODEL), xp.dtype)
            yp, buf = conformer_conv_mixer(xp, bufp, *conf_w)
            conf_cp.append(buf)
            ys, buf = conformer_conv_mixer(xs, state_conf_conv[j], *conf_w)
            conf_cs.append(buf)
        else:
            q, k, v = qkv_proj(xp, moba_w_qkv[j])
            yp = merge_heads(moba_prompt(q, k, v)) @ moba_w_o[j]
            mo_kp.append(k)
            mo_vp.append(v)
            q, k, v = qkv_proj(xs, moba_w_qkv[j])
            k_all = jnp.concatenate([gather_pages(cache_moba_k[j], page_table).astype(k.dtype), k], axis=1)
            v_all = jnp.concatenate([gather_pages(cache_moba_v[j], page_table).astype(v.dtype), v], axis=1)
            kb, vb, km = moba_blocks(k_all, v_all)
            ys = merge_heads(moba_attend(q, pos_s, kb, vb, km)) @ moba_w_o[j]
            mo_ks.append(k)
            mo_vs.append(v)
        xp, xs = post_norm(xp, yp, ln_g[i, 1], ln_b[i, 1]), post_norm(xs, ys, ln_g[i, 1], ln_b[i, 1])
        f2 = (ffn_w1[i, 1], ffn_w3[i, 1], ffn_w2[i, 1], ln_g[i, 2], ln_b[i, 2])
        xp, xs = ffn_half(xp, *f2), ffn_half(xs, *f2)
        xp = ple_add(xp, p_prompt[i], ple_w_proj[i], ple_w_gate[i], ln_g[i, 3], ln_b[i, 3])
        xs = ple_add(xs, p_sample[i], ple_w_proj[i], ple_w_gate[i], ln_g[i, 3], ln_b[i, 3])
    return (xp, xs,
            jnp.stack(sb_kp), jnp.stack(sb_vp), jnp.stack(sb_ks), jnp.stack(sb_vs),
            jnp.stack(ssm_hp), jnp.stack(ssm_hs), jnp.stack(ssm_cp), jnp.stack(ssm_cs),
            jnp.stack(conf_cp), jnp.stack(conf_cs),
            jnp.stack(mo_kp), jnp.stack(mo_vp), jnp.stack(mo_ks), jnp.stack(mo_vs))
```

```python
import functools

import jax
import jax.numpy as jnp
from jax import lax
from jax.experimental import pallas as pl
from jax.experimental.pallas import tpu as pltpu

F32 = jnp.float32
BF16 = jnp.bfloat16

N_HEADS = 16
HEAD_DIM = 128
N_KV_HEADS = 4
Q_PER_KV = N_HEADS // N_KV_HEADS
Q_DIM = N_HEADS * HEAD_DIM
KV_DIM = N_KV_HEADS * HEAD_DIM
SB_BLOCK = 128
MOBA_BLOCK = 256
MOBA_TOPK = 3
PAGE_SIZE = 128
SSM_HEAD_DIM = 64
SSM_GROUPS = 8
SSM_D_STATE = 128
SSM_CONV = 4
SSM_CHUNK = 128
CONF_WIDTH = 31
LN_EPS = 1e-5
DEPTH = 4
DN_ALPHA = (2 * DEPTH) ** 0.25
LANES = 128
NEG_BIG = -1e30
VMEM_LIMIT = 56 * 1024 * 1024


def _params(*sem):
    return pltpu.CompilerParams(dimension_semantics=sem, vmem_limit_bytes=VMEM_LIMIT)


def _layer_norm_rows(y, g, b):
    mu = jnp.mean(y, axis=-1, keepdims=True)
    d = y - mu
    var = jnp.mean(d * d, axis=-1, keepdims=True)
    return d * lax.rsqrt(var + LN_EPS) * g + b


def _split3(x):
    hi = x.astype(BF16)
    r1 = x - hi.astype(F32)
    mid = r1.astype(BF16)
    lo = (r1 - mid.astype(F32)).astype(BF16)
    return hi, mid, lo


def _dot(a, b):
    return jnp.dot(a, b, preferred_element_type=F32)


def _dot_nt(a, b):
    return lax.dot_general(a, b, (((1,), (1,)), ((), ())), preferred_element_type=F32)


def _dot01(x, m01):
    hi, mid, lo = _split3(x)
    return _dot(hi, m01) + _dot(mid, m01) + _dot(lo, m01)


def _pick_tile(n, pref):
    for t in pref:
        if n % t == 0:
            return t
    return n


def _mm_kernel(x_ref, w_ref, b_ref, o_ref):
    o_ref[...] = (_dot(x_ref[...], w_ref[...]) + b_ref[...]).astype(o_ref.dtype)


def matmul(x, w, bias=None, out_dtype=F32):
    m, k = x.shape
    n = w.shape[1]
    tm = _pick_tile(m, (1024, 512, 256, 128, 8))
    tn = _pick_tile(n, (512, 256, 128))
    if bias is None:
        bias = jnp.zeros((n,), F32)
    return pl.pallas_call(
        _mm_kernel,
        grid=(m // tm, n // tn),
        in_specs=[pl.BlockSpec((tm, k), lambda i, j: (i, 0)),
                  pl.BlockSpec((k, tn), lambda i, j: (0, j)),
                  pl.BlockSpec((1, tn), lambda i, j: (0, j))],
        out_specs=pl.BlockSpec((tm, tn), lambda i, j: (i, j)),
        out_shape=jax.ShapeDtypeStruct((m, n), out_dtype),
        compiler_params=_params("parallel", "arbitrary"),
        name="mm",
    )(x, w, bias.reshape(1, n).astype(F32))


def _gated_mm_kernel(x_ref, wa_ref, wb_ref, ba_ref, bb_ref, o_ref, *, mode):
    x = x_ref[...]
    a = _dot(x, wa_ref[...]) + ba_ref[...]
    b = _dot(x, wb_ref[...]) + bb_ref[...]
    if mode == "swiglu":
        o = a * jax.nn.sigmoid(a) * b
    else:
        o = a * jax.nn.sigmoid(b)
    o_ref[...] = o.astype(o_ref.dtype)


def gated_matmul(x, wa, wb, ba, bb, mode, out_dtype):
    m, k = x.shape
    n = wa.shape[1]
    tm = _pick_tile(m, (1024, 512, 256, 128, 8))
    tn = _pick_tile(n, (512, 256, 128))
    if ba is None:
        ba = jnp.zeros((n,), F32)
        bb = jnp.zeros((n,), F32)
    return pl.pallas_call(
        functools.partial(_gated_mm_kernel, mode=mode),
        grid=(m // tm, n // tn),
        in_specs=[pl.BlockSpec((tm, k), lambda i, j: (i, 0)),
                  pl.BlockSpec((k, tn), lambda i, j: (0, j)),
                  pl.BlockSpec((k, tn), lambda i, j: (0, j)),
                  pl.BlockSpec((1, tn), lambda i, j: (0, j)),
                  pl.BlockSpec((1, tn), lambda i, j: (0, j))],
        out_specs=pl.BlockSpec((tm, tn), lambda i, j: (i, j)),
        out_shape=jax.ShapeDtypeStruct((m, n), out_dtype),
        compiler_params=_params("parallel", "arbitrary"),
        name="gated_mm",
    )(x, wa, wb, ba.reshape(1, n).astype(F32), bb.reshape(1, n).astype(F32))


def _mm_ln_kernel(a_ref, w_ref, bias_ref, res_ref, g_ref, b_ref, of_ref, ob_ref, acc_ref, *, scale, nk):
    k = pl.program_id(1)

    @pl.when(k == 0)
    def _():
        acc_ref[...] = jnp.zeros_like(acc_ref)

    acc_ref[...] += _dot(a_ref[...], w_ref[...])

    @pl.when(k == nk - 1)
    def _():
        y = DN_ALPHA * res_ref[...] + scale * (acc_ref[...] + bias_ref[...])
        out = _layer_norm_rows(y, g_ref[...], b_ref[...])
        of_ref[...] = out
        ob_ref[...] = out.astype(BF16)


def matmul_postnorm(a, w, bias, res, g, b, scale):
    m, kdim = a.shape
    n = w.shape[1]
    tm = _pick_tile(m, (512, 256, 128, 8))
    tk = _pick_tile(kdim, (1408, 1024, 512, 256, 128))
    nk = kdim // tk
    if bias is None:
        bias = jnp.zeros((n,), F32)
    row = lambda v: v.reshape(1, n).astype(F32)
    return pl.pallas_call(
        functools.partial(_mm_ln_kernel, scale=scale, nk=nk),
        grid=(m // tm, nk),
        in_specs=[pl.BlockSpec((tm, tk), lambda i, k: (i, k)),
                  pl.BlockSpec((tk, n), lambda i, k: (k, 0)),
                  pl.BlockSpec((1, n), lambda i, k: (0, 0)),
                  pl.BlockSpec((tm, n), lambda i, k: (i, 0)),
                  pl.BlockSpec((1, n), lambda i, k: (0, 0)),
                  pl.BlockSpec((1, n), lambda i, k: (0, 0))],
        out_specs=[pl.BlockSpec((tm, n), lambda i, k: (i, 0)),
                   pl.BlockSpec((tm, n), lambda i, k: (i, 0))],
        out_shape=[jax.ShapeDtypeStruct((m, n), F32), jax.ShapeDtypeStruct((m, n), BF16)],
        scratch_shapes=[pltpu.VMEM((tm, n), F32)],
        compiler_params=_params("parallel", "arbitrary"),
        name="mm_postnorm",
    )(a, w, row(bias), res, row(g), row(b))


def _ple_kernel(xb_ref, xf_ref, p_ref, wg_ref, wp_ref, g_ref, b_ref, of_ref, ob_ref):
    gate = jax.nn.sigmoid(_dot(xb_ref[...], wg_ref[...]))
    proj = _dot(p_ref[...].astype(BF16), wp_ref[...])
    y = DN_ALPHA * xf_ref[...] + gate * proj
    out = _layer_norm_rows(y, g_ref[...], b_ref[...])
    of_ref[...] = out
    ob_ref[...] = out.astype(BF16)


def ple_add(xb, xf, p, wg, wp, g, b):
    m, d = xf.shape
    pd = p.shape[1]
    tm = _pick_tile(m, (256, 128, 8))
    row = lambda v: v.reshape(1, d).astype(F32)
    return pl.pallas_call(
        _ple_kernel,
        grid=(m // tm,),
        in_specs=[pl.BlockSpec((tm, d), lambda i: (i, 0)),
                  pl.BlockSpec((tm, d), lambda i: (i, 0)),
                  pl.BlockSpec((tm, pd), lambda i: (i, 0)),
                  pl.BlockSpec((d, d), lambda i: (0, 0)),
                  pl.BlockSpec((pd, d), lambda i: (0, 0)),
                  pl.BlockSpec((1, d), lambda i: (0, 0)),
                  pl.BlockSpec((1, d), lambda i: (0, 0))],
        out_specs=[pl.BlockSpec((tm, d), lambda i: (i, 0)),
                   pl.BlockSpec((tm, d), lambda i: (i, 0))],
        out_shape=[jax.ShapeDtypeStruct((m, d), F32), jax.ShapeDtypeStruct((m, d), BF16)],
        compiler_params=_params("parallel"),
        name="ple_add",
    )(xb, xf, p, wg, wp, row(g), row(b))


def _softplus(z):
    return jnp.maximum(z, 0.0) + jnp.log1p(jnp.exp(-jnp.abs(z)))


def _sb_prompt_kernel(q_ref, k_ref, v_ref, o_ref, *, tq):
    i = pl.program_id(2)
    rows = Q_PER_KV * tq
    scale = HEAD_DIM ** -0.5
    q = jnp.concatenate([q_ref[0, :, g * HEAD_DIM:(g + 1) * HEAD_DIM] for g in range(Q_PER_KV)], axis=0)
    t_loc = lax.broadcasted_iota(jnp.int32, (rows, tq), 0) & (tq - 1)
    s_loc = lax.broadcasted_iota(jnp.int32, (rows, tq), 1)
    later = (lax.broadcasted_iota(jnp.int32, (tq, tq), 0) > lax.broadcasted_iota(jnp.int32, (tq, tq), 1)).astype(BF16)

    def block(j, run, acc, diagonal):
        start = pl.multiple_of(j * tq, tq)
        kb = k_ref[0, pl.ds(start, tq), :].astype(BF16)
        vb = v_ref[0, pl.ds(start, tq), :].astype(BF16)
        z = _dot_nt(q, kb) * scale
        log_keep = -_softplus(z)
        if diagonal:
            valid = s_loc < t_loc
            log_keep = jnp.where(valid, log_keep, 0.0)
        log_w = z + log_keep + _dot01(log_keep, later) + run
        w = jnp.exp(log_w)
        if diagonal:
            w = jnp.where(valid, w, 0.0)
        acc = acc + _dot(w.astype(BF16), vb)
        run = run + jnp.sum(log_keep, axis=1, keepdims=True)
        return run, acc

    run0 = jnp.zeros((rows, 1), F32)
    acc0 = jnp.zeros((rows, HEAD_DIM), F32)
    run, acc = block(i, run0, acc0, True)

    def body(it, carry):
        return block(i - 1 - it, carry[0], carry[1], False)

    run, acc = lax.fori_loop(0, i, body, (run, acc))
    for g in range(Q_PER_KV):
        o_ref[0, :, g * HEAD_DIM:(g + 1) * HEAD_DIM] = acc[g * tq:(g + 1) * tq].astype(o_ref.dtype)


def sb_prompt_attention(q, k, v):
    bsz, L, _ = q.shape
    tq = SB_BLOCK
    gw = Q_PER_KV * HEAD_DIM
    return pl.pallas_call(
        functools.partial(_sb_prompt_kernel, tq=tq),
        grid=(bsz, N_KV_HEADS, L // tq),
        in_specs=[pl.BlockSpec((1, tq, gw), lambda b, h, i: (b, i, h)),
                  pl.BlockSpec((1, L, HEAD_DIM), lambda b, h, i: (b, 0, h)),
                  pl.BlockSpec((1, L, HEAD_DIM), lambda b, h, i: (b, 0, h))],
        out_specs=pl.BlockSpec((1, tq, gw), lambda b, h, i: (b, i, h)),
        out_shape=jax.ShapeDtypeStruct(q.shape, BF16),
        compiler_params=_params("parallel", "parallel", "arbitrary"),
        name="sb_prompt",
    )(q, k, v)


def _block_diag_q(q):
    qt = jnp.concatenate([q] * N_KV_HEADS, axis=1)
    r = lax.broadcasted_iota(jnp.int32, qt.shape, 0) >> 2
    c = lax.broadcasted_iota(jnp.int32, qt.shape, 1) >> 7
    return jnp.where(r == c, qt, 0.0).astype(BF16)


def _own_lanes(acc):
    r = lax.broadcasted_iota(jnp.int32, (N_HEADS, HEAD_DIM), 0) >> 2
    out = jnp.zeros((N_HEADS, HEAD_DIM), F32)
    for h in range(N_KV_HEADS):
        out = out + jnp.where(r == h, acc[:, h * HEAD_DIM:(h + 1) * HEAD_DIM], 0.0)
    return out


def _suffix_sum_lanes(x):
    lane = lax.broadcasted_iota(jnp.int32, x.shape, 1)
    d = 1
    while d < LANES:
        shifted = pltpu.roll(x, LANES - d, axis=1)
        x = x + jnp.where(lane + d < LANES, shifted, 0.0)
        d *= 2
    return x


def _sb_decode_kernel(pt_ref, q_ref, k_ref, v_ref, o_ref, acc_ref, run_ref, *, n_pages):
    p = pl.program_id(1)

    @pl.when(p == 0)
    def _():
        acc_ref[...] = jnp.zeros_like(acc_ref)
        run_ref[...] = jnp.zeros_like(run_ref)

    qbd = _block_diag_q(q_ref[0])
    z = _dot_nt(qbd, k_ref[0].astype(BF16)) * (HEAD_DIM ** -0.5)
    log_keep = -_softplus(z)
    incl = _suffix_sum_lanes(log_keep)
    w = jnp.exp(z + incl + run_ref[...])
    acc_ref[...] += _dot(w.astype(BF16), v_ref[0].astype(BF16))
    run_ref[...] += incl[:, 0:1]

    @pl.when(p == n_pages - 1)
    def _():
        o_ref[0] = _own_lanes(acc_ref[...])


def sb_decode_attention(q, k_cache, v_cache, page_table):
    db, n_pages = page_table.shape
    page = k_cache.shape[1]
    kv_spec = pl.BlockSpec((1, page, KV_DIM), lambda b, p, pt: (pt[b * n_pages + (n_pages - 1 - p)], 0, 0))
    return pl.pallas_call(
        functools.partial(_sb_decode_kernel, n_pages=n_pages),
        grid_spec=pltpu.PrefetchScalarGridSpec(
            num_scalar_prefetch=1,
            grid=(db, n_pages),
            in_specs=[pl.BlockSpec((1, N_HEADS, HEAD_DIM), lambda b, p, pt: (b, 0, 0)), kv_spec, kv_spec],
            out_specs=pl.BlockSpec((1, N_HEADS, HEAD_DIM), lambda b, p, pt: (b, 0, 0)),
            scratch_shapes=[pltpu.VMEM((N_HEADS, KV_DIM), F32), pltpu.VMEM((N_HEADS, LANES), F32)]),
        out_shape=jax.ShapeDtypeStruct((db, N_HEADS, HEAD_DIM), F32),
        compiler_params=_params("parallel", "arbitrary"),
        name="sb_decode",
    )(page_table.reshape(-1), q, k_cache, v_cache)


def _top_k_mask(gate, n_valid, k):
    lane = lax.broadcasted_iota(jnp.int32, gate.shape, 1)
    sel = jnp.zeros(gate.shape, F32)
    picks = []
    for r in range(k):
        m = jnp.max(gate, axis=1, keepdims=True)
        idx = jnp.min(jnp.where(gate == m, lane, gate.shape[1] - 1), axis=1, keepdims=True)
        pick = lane == idx
        sel = jnp.maximum(sel, jnp.where(pick, jnp.where(r < n_valid, 1.0, 0.0), 0.0))
        gate = jnp.where(pick, -jnp.inf, gate)
        picks.append(idx)
    return sel, picks


def _moba_prompt_kernel(q_ref, k_ref, v_ref, o_ref, km_ref, *, tq, nb):
    i = pl.program_id(2)
    rows = Q_PER_KV * tq
    scale = HEAD_DIM ** -0.5

    @pl.when(i == 0)
    def _():
        km_ref[...] = jnp.zeros_like(km_ref)
        for jb in range(nb):
            km_ref[jb:jb + 1, :] = jnp.mean(k_ref[0, jb * tq:(jb + 1) * tq, :], axis=0, keepdims=True)

    q = jnp.concatenate([q_ref[0, :, g * HEAD_DIM:(g + 1) * HEAD_DIM] for g in range(Q_PER_KV)], axis=0)
    lane = lax.broadcasted_iota(jnp.int32, (rows, LANES), 1)
    gate = _dot_nt(q, km_ref[...].astype(BF16))
    gate = jnp.where(lane < i, gate, -jnp.inf)
    sel, _ = _top_k_mask(gate, i, MOBA_TOPK)
    sel = sel.astype(BF16)

    def scores(j):
        start = pl.multiple_of(j * tq, tq)
        kb = k_ref[0, pl.ds(start, tq), :].astype(BF16)
        vb = v_ref[0, pl.ds(start, tq), :].astype(BF16)
        return _dot_nt(q, kb) * scale, vb

    s, vb = scores(i)
    t_loc = lax.broadcasted_iota(jnp.int32, (rows, tq), 0) & (tq - 1)
    s_loc = lax.broadcasted_iota(jnp.int32, (rows, tq), 1)
    s = jnp.where(s_loc <= t_loc, s, NEG_BIG)
    m = jnp.max(s, axis=1, keepdims=True)
    p = jnp.exp(s - m)
    l = jnp.sum(p, axis=1, keepdims=True)
    acc = _dot(p.astype(BF16), vb)

    def body(j, carry):
        m, l, acc = carry
        s, vb = scores(j)
        onehot = (lax.broadcasted_iota(jnp.int32, (LANES, tq), 0) == j).astype(BF16)
        chosen = _dot(sel, onehot) > 0.5
        s = jnp.where(chosen, s, NEG_BIG)
        m_new = jnp.maximum(m, jnp.max(s, axis=1, keepdims=True))
        alpha = jnp.exp(m - m_new)
        p = jnp.exp(s - m_new)
        l = alpha * l + jnp.sum(p, axis=1, keepdims=True)
        acc = alpha * acc + _dot(p.astype(BF16), vb)
        return m_new, l, acc

    m, l, acc = lax.fori_loop(0, i, body, (m, l, acc))
    out = acc / l
    for g in range(Q_PER_KV):
        o_ref[0, :, g * HEAD_DIM:(g + 1) * HEAD_DIM] = out[g * tq:(g + 1) * tq].astype(o_ref.dtype)


def moba_prompt_attention(q, k, v):
    bsz, L, _ = q.shape
    tq = MOBA_BLOCK
    nb = L // tq
    assert L % tq == 0 and nb <= LANES
    gw = Q_PER_KV * HEAD_DIM
    return pl.pallas_call(
        functools.partial(_moba_prompt_kernel, tq=tq, nb=nb),
        grid=(bsz, N_KV_HEADS, nb),
        in_specs=[pl.BlockSpec((1, tq, gw), lambda b, h, i: (b, i, h)),
                  pl.BlockSpec((1, L, HEAD_DIM), lambda b, h, i: (b, 0, h)),
                  pl.BlockSpec((1, L, HEAD_DIM), lambda b, h, i: (b, 0, h))],
        out_specs=pl.BlockSpec((1, tq, gw), lambda b, h, i: (b, i, h)),
        out_shape=jax.ShapeDtypeStruct(q.shape, BF16),
        scratch_shapes=[pltpu.VMEM((LANES, HEAD_DIM), F32)],
        compiler_params=_params("parallel", "arbitrary", "arbitrary"),
        name="moba_prompt",
    )(q, k, v)


def _moba_kmean_kernel(pt_ref, k_ref, o_ref, *, pages_per_block):
    h = pl.program_id(2)
    part = jnp.sum(k_ref[0], axis=0, keepdims=True)

    @pl.when(h == 0)
    def _():
        o_ref[0, 0] = part

    @pl.when(h > 0)
    def _():
        o_ref[0, 0] += part

    @pl.when(h == pages_per_block - 1)
    def _():
        o_ref[0, 0] = o_ref[0, 0] * (1.0 / MOBA_BLOCK)


def moba_cache_block_means(k_cache, page_table):
    db, n_pages = page_table.shape
    page = k_cache.shape[1]
    ppb = MOBA_BLOCK // page
    nblk = n_pages // ppb
    out = pl.pallas_call(
        functools.partial(_moba_kmean_kernel, pages_per_block=ppb),
        grid_spec=pltpu.PrefetchScalarGridSpec(
            num_scalar_prefetch=1,
            grid=(db, nblk, ppb),
            in_specs=[pl.BlockSpec((1, page, KV_DIM),
                                   lambda b, j, h, pt: (pt[b * n_pages + j * ppb + h], 0, 0))],
            out_specs=pl.BlockSpec((1, 1, 1, KV_DIM), lambda b, j, h, pt: (b, j, 0, 0))),
        out_shape=jax.ShapeDtypeStruct((db, nblk, 1, KV_DIM), F32),
        compiler_params=_params("parallel", "arbitrary", "arbitrary"),
        name="moba_kmean",
    )(page_table.reshape(-1), k_cache)
    return out.reshape(db, nblk, KV_DIM)


def _moba_gate_kernel(q_ref, km_ref, o_ref, *, nblk):
    qbd = _block_diag_q(q_ref[0])
    gate = _dot_nt(qbd, km_ref[0].astype(BF16))
    _, picks = _top_k_mask(gate, nblk, MOBA_TOPK)
    lane = lax.broadcasted_iota(jnp.int32, (N_HEADS, LANES), 1)
    out = jnp.zeros((N_HEADS, LANES), jnp.int32)
    for r, idx in enumerate(picks):
        out = jnp.where(lane == r, idx, out)
    o_ref[0] = out


def moba_decode_select(q, kmean):
    db, nblk, _ = kmean.shape
    out = pl.pallas_call(
        functools.partial(_moba_gate_kernel, nblk=nblk),
        grid=(db,),
        in_specs=[pl.BlockSpec((1, N_HEADS, HEAD_DIM), lambda b: (b, 0, 0)),
                  pl.BlockSpec((1, nblk, KV_DIM), lambda b: (b, 0, 0))],
        out_specs=pl.BlockSpec((1, N_HEADS, LANES), lambda b: (b, 0, 0)),
        out_shape=jax.ShapeDtypeStruct((db, N_HEADS, LANES), jnp.int32),
        compiler_params=_params("parallel"),
        name="moba_gate",
    )(q, kmean)
    return out[:, :, :MOBA_TOPK]


def _moba_decode_kernel(sel_ref, pt_ref, q_ref, k_ref, v_ref, kn_ref, vn_ref, o_ref, m_ref, l_ref, acc_ref, *, n_steps):
    s_id = pl.program_id(2)
    scale = HEAD_DIM ** -0.5
    q = jnp.broadcast_to(q_ref[0, 0], (8, HEAD_DIM))

    @pl.when(s_id == 0)
    def _():
        s_own = jnp.sum(q * kn_ref[0, 0], axis=1, keepdims=True) * scale
        m_ref[...] = jnp.broadcast_to(s_own, m_ref.shape)
        l_ref[...] = jnp.ones_like(l_ref)
        acc_ref[...] = jnp.broadcast_to(vn_ref[0, 0], acc_ref.shape)

    s = _dot_nt(q.astype(BF16), k_ref[0].astype(BF16)) * scale
    m_old = m_ref[:, 0:1]
    m_new = jnp.maximum(m_old, jnp.max(s, axis=1, keepdims=True))
    alpha = jnp.exp(m_old - m_new)
    p = jnp.exp(s - m_new)
    l_ref[...] = alpha * l_ref[...] + jnp.sum(p, axis=1, keepdims=True)
    acc_ref[...] = alpha * acc_ref[...] + _dot(p.astype(BF16), v_ref[0].astype(BF16))
    m_ref[...] = jnp.broadcast_to(m_new, m_ref.shape)

    @pl.when(s_id == n_steps - 1)
    def _():
        o_ref[0, 0] = acc_ref[...] / l_ref[...]


def moba_decode_attention(q, k_new, v_new, sel, k_cache, v_cache, page_table):
    db, n_pages = page_table.shape
    page = k_cache.shape[1]
    ppb = MOBA_BLOCK // page
    n_steps = MOBA_TOPK * ppb

    def kv_map(b, h, s, sel_r, pt):
        blk = sel_r[(b * N_HEADS + h) * MOBA_TOPK + s // ppb]
        return (pt[b * n_pages + blk * ppb + s % ppb], 0, h // Q_PER_KV)

    kv_spec = pl.BlockSpec((1, page, HEAD_DIM), kv_map)
    new_spec = pl.BlockSpec((1, 1, 1, HEAD_DIM), lambda b, h, s, sel_r, pt: (b, h // Q_PER_KV, 0, 0))
    out = pl.pallas_call(
        functools.partial(_moba_decode_kernel, n_steps=n_steps),
        grid_spec=pltpu.PrefetchScalarGridSpec(
            num_scalar_prefetch=2,
            grid=(db, N_HEADS, n_steps),
            in_specs=[pl.BlockSpec((1, 1, 1, HEAD_DIM), lambda b, h, s, sel_r, pt: (b, h, 0, 0)),
                      kv_spec, kv_spec, new_spec, new_spec],
            out_specs=pl.BlockSpec((1, 1, 8, HEAD_DIM), lambda b, h, s, sel_r, pt: (b, h, 0, 0)),
            scratch_shapes=[pltpu.VMEM((8, LANES), F32), pltpu.VMEM((8, LANES), F32), pltpu.VMEM((8, HEAD_DIM), F32)]),
        out_shape=jax.ShapeDtypeStruct((db, N_HEADS, 8, HEAD_DIM), F32),
        compiler_params=_params("parallel", "parallel", "arbitrary"),
        name="moba_decode",
    )(sel.reshape(-1), page_table.reshape(-1), q, k_cache, v_cache, k_new, v_new)
    return out[:, :, 0, :]


def _dwconv_kernel(x_ref, hist_ref, w_ref, b_ref, g_ref, beta_ref, o_ref, pad_ref, *, width, tl, hp, post):
    t = pl.program_id(2)

    @pl.when(t == 0)
    def _():
        pad_ref[0:hp, :] = hist_ref[0]

    @pl.when(t > 0)
    def _():
        pad_ref[0:hp, :] = pad_ref[tl:tl + hp, :]

    pad_ref[hp:hp + tl, :] = x_ref[0]
    base = hp - (width - 1)
    acc = jnp.zeros(o_ref.shape[1:], F32) + b_ref[...]
    for k in range(width):
        acc = acc + pad_ref[base + k:base + k + tl, :] * w_ref[k:k + 1, :]
    if post == "ln_silu":
        acc = _layer_norm_rows(acc, g_ref[...], beta_ref[...])
    o_ref[0] = (acc * jax.nn.sigmoid(acc)).astype(o_ref.dtype)


def causal_dwconv(x, hist, w, b, post, ln_g=None, ln_b=None, out_dtype=F32):
    bsz, L, C = x.shape
    width = w.shape[0]
    hp = hist.shape[1]
    assert hp >= width - 1 and hp % 8 == 0
    tc = C if post == "ln_silu" else _pick_tile(C, (1024, 512, 256, 128))
    tl = _pick_tile(L, (256, 128)) if L >= hp else L
    assert tl >= hp or tl == L
    wp = jnp.zeros((-(-width // 8) * 8, C), F32).at[:width].set(w.astype(F32))
    if ln_g is None:
        ln_g = jnp.ones((C,), F32)
        ln_b = jnp.zeros((C,), F32)
    row = lambda v: v.reshape(1, C).astype(F32)
    vec_spec = pl.BlockSpec((1, tc), lambda bb, c, t: (0, c))
    return pl.pallas_call(
        functools.partial(_dwconv_kernel, width=width, tl=tl, hp=hp, post=post),
        grid=(bsz, C // tc, L // tl),
        in_specs=[pl.BlockSpec((1, tl, tc), lambda bb, c, t: (bb, t, c)),
                  pl.BlockSpec((1, hp, tc), lambda bb, c, t: (bb, 0, c)),
                  pl.BlockSpec((wp.shape[0], tc), lambda bb, c, t: (0, c)),
                  vec_spec, vec_spec, vec_spec],
        out_specs=pl.BlockSpec((1, tl, tc), lambda bb, c, t: (bb, t, c)),
        out_shape=jax.ShapeDtypeStruct((bsz, L, C), out_dtype),
        scratch_shapes=[pltpu.VMEM((hp + tl, tc), F32)],
        compiler_params=_params("parallel", "parallel", "arbitrary"),
        name="dwconv_" + post,
    )(x, hist, wp, row(b), row(ln_g), row(ln_b))


def _ssd_prompt_kernel(x_ref, bm_ref, cm_ref, dt_ref, dtb_ref, alog_ref, d_ref, h0_ref, y_ref, h_ref, *, chunk, hpg):
    c = pl.program_id(2)
    P = SSM_HEAD_DIM

    @pl.when(c == 0)
    def _():
        h_ref[...] = h0_ref[...]

    x = x_ref[0]
    bm = bm_ref[0]
    cm = cm_ref[0]
    dt = _softplus(dt_ref[0] + dtb_ref[...])
    a = -jnp.exp(alog_ref[...])
    li = lax.broadcasted_iota(jnp.int32, (chunk, chunk), 0)
    si = lax.broadcasted_iota(jnp.int32, (chunk, chunk), 1)
    causal = li >= si
    lower = causal.astype(BF16)
    hi, mid, lo = _split3(dt * a)
    cum = _dot(lower, hi) + _dot(lower, mid) + _dot(lower, lo)
    cum_t = cum.T
    dt_t = dt.T
    cmb = cm.astype(BF16)
    bmb = bm.astype(BF16)
    cb = _dot_nt(cmb, bmb)
    x_t = x.T
    for r in range(hpg):
        cum_col = cum[:, r:r + 1]
        cum_row = cum_t[r:r + 1, :]
        dt_row = dt_t[r:r + 1, :]
        total = cum_t[r:r + 1, chunk - 1:chunk]
        xh = x[:, r * P:(r + 1) * P]
        decay = jnp.exp(jnp.where(causal, cum_col - cum_row, -jnp.inf))
        mix = (cb * decay * dt_row).astype(BF16)
        y = _dot(mix, xh.astype(BF16))
        h_old = h_ref[0, r]
        y = y + _dot_nt(cmb, h_old.astype(BF16)) * jnp.exp(cum_col)
        y_ref[0, :, r * P:(r + 1) * P] = y + d_ref[:, r:r + 1] * xh
        to_end = jnp.exp(total - cum_row) * dt_row
        s_chunk = _dot((x_t[r * P:(r + 1) * P, :] * to_end).astype(BF16), bmb)
        h_ref[0, r] = jnp.exp(total) * h_old + s_chunk


def ssd_prompt(xbc, dt_raw, dt_bias_g, a_log_g, d_g, h0):
    bsz, L, _ = xbc.shape
    n_heads = h0.shape[1]
    G, N, P = SSM_GROUPS, SSM_D_STATE, SSM_HEAD_DIM
    hpg = n_heads // G
    d_inner = n_heads * P
    gw = hpg * P
    chunk = SSM_CHUNK
    assert gw % LANES == 0 and d_inner % N == 0
    b_off = d_inner // N
    c_off = b_off + G
    vec_spec = pl.BlockSpec((1, LANES), lambda b, g, c: (0, g))
    return pl.pallas_call(
        functools.partial(_ssd_prompt_kernel, chunk=chunk, hpg=hpg),
        grid=(bsz, G, L // chunk),
        in_specs=[pl.BlockSpec((1, chunk, gw), lambda b, g, c: (b, c, g)),
                  pl.BlockSpec((1, chunk, N), lambda b, g, c: (b, c, b_off + g)),
                  pl.BlockSpec((1, chunk, N), lambda b, g, c: (b, c, c_off + g)),
                  pl.BlockSpec((1, chunk, LANES), lambda b, g, c: (b, c, g)),
                  vec_spec, vec_spec, vec_spec,
                  pl.BlockSpec((1, hpg, P, N), lambda b, g, c: (b, g, 0, 0))],
        out_specs=[pl.BlockSpec((1, chunk, gw), lambda b, g, c: (b, c, g)),
                   pl.BlockSpec((1, hpg, P, N), lambda b, g, c: (b, g, 0, 0))],
        out_shape=[jax.ShapeDtypeStruct((bsz, L, d_inner), F32),
                   jax.ShapeDtypeStruct((bsz, n_heads, P, N), F32)],
        compiler_params=_params("parallel", "parallel", "arbitrary"),
        name="ssd_prompt",
    )(xbc, xbc, xbc, dt_raw, dt_bias_g, a_log_g, d_g, h0)


def _ssd_step_kernel(xc_ref, bm_ref, cm_ref, dt_ref, dtb_ref, alog_ref, d_ref, h0_ref, y_ref, h_ref, *, hpg):
    dt = _softplus(dt_ref[0] + dtb_ref[...])
    da = jnp.exp(dt * (-jnp.exp(alog_ref[...])))
    bm = bm_ref[0]
    cm = cm_ref[0]
    for r in range(hpg):
        xcol = xc_ref[0, r]
        h_new = da[:, r:r + 1] * h0_ref[0, r] + xcol * (dt[:, r:r + 1] * bm)
        h_ref[0, r] = h_new
        y_ref[0, r] = jnp.sum(h_new * cm, axis=1, keepdims=True) + d_ref[:, r:r + 1] * xcol


def ssd_step(x_col, bc, dt_raw, dt_bias_g, a_log_g, d_g, h0):
    db, n_heads, P, _ = x_col.shape
    G, N = SSM_GROUPS, SSM_D_STATE
    hpg = n_heads // G
    vec_spec = pl.BlockSpec((1, LANES), lambda b, g: (0, g))
    return pl.pallas_call(
        functools.partial(_ssd_step_kernel, hpg=hpg),
        grid=(db, G),
        in_specs=[pl.BlockSpec((1, hpg, P, 1), lambda b, g: (b, g, 0, 0)),
                  pl.BlockSpec((1, 1, N), lambda b, g: (b, 0, g)),
                  pl.BlockSpec((1, 1, N), lambda b, g: (b, 0, G + g)),
                  pl.BlockSpec((1, 1, LANES), lambda b, g: (b, 0, g)),
                  vec_spec, vec_spec, vec_spec,
                  pl.BlockSpec((1, hpg, P, N), lambda b, g: (b, g, 0, 0))],
        out_specs=[pl.BlockSpec((1, hpg, P, 1), lambda b, g: (b, g, 0, 0)),
                   pl.BlockSpec((1, hpg, P, N), lambda b, g: (b, g, 0, 0))],
        out_shape=[jax.ShapeDtypeStruct((db, n_heads, P, 1), F32),
                   jax.ShapeDtypeStruct((db, n_heads, P, N), F32)],
        compiler_params=_params("parallel", "parallel"),
        name="ssd_step",
    )(x_col, bc, bc, dt_raw, dt_bias_g, a_log_g, d_g, h0)


def _gated_norm_kernel(y_ref, z_ref, g_ref, o_ref):
    z = z_ref[...]
    h = y_ref[...] * (z * jax.nn.sigmoid(z))
    h = h * lax.rsqrt(jnp.mean(h * h, axis=-1, keepdims=True) + LN_EPS)
    o_ref[...] = (h * g_ref[...]).astype(o_ref.dtype)


def gated_rms_norm(y, z, g):
    m, d = y.shape
    gw = d // SSM_GROUPS
    tm = _pick_tile(m, (512, 256, 128, 8))
    spec = pl.BlockSpec((tm, gw), lambda i, j: (i, j))
    return pl.pallas_call(
        _gated_norm_kernel,
        grid=(m // tm, SSM_GROUPS),
        in_specs=[spec, spec, pl.BlockSpec((1, gw), lambda i, j: (0, j))],
        out_specs=spec,
        out_shape=jax.ShapeDtypeStruct((m, d), BF16),
        compiler_params=_params("parallel", "parallel"),
        name="gated_rms_norm",
    )(y, z, g.reshape(1, d).astype(F32))


def _group_pad(v, n_heads):
    hpg = n_heads // SSM_GROUPS
    out = jnp.zeros((SSM_GROUPS, LANES), F32).at[:, :hpg].set(v.astype(F32).reshape(SSM_GROUPS, hpg))
    return out.reshape(1, SSM_GROUPS * LANES)


def _ffn_half(xf, xb, w1, w3, w2, g, b):
    h = gated_matmul(xb, w1, w3, None, None, "swiglu", BF16)
    return matmul_postnorm(h, w2, None, xf, g, b, 0.5)


def _pad_rows(x, rows):
    return jnp.concatenate([x, jnp.zeros((x.shape[0], rows - x.shape[1]) + x.shape[2:], x.dtype)], axis=1)


def _front_pad_rows(x, rows):
    return jnp.concatenate([jnp.zeros((x.shape[0], rows - x.shape[1]) + x.shape[2:], x.dtype), x], axis=1)


def kernel(x_prompt, x_sample, p_prompt, p_sample, cache_sb_k, cache_sb_v, cache_moba_k, cache_moba_v, state_ssm, state_ssm_conv, state_conf_conv, page_table, ln_g, ln_b, ffn_w1, ffn_w3, ffn_w2, ple_w_proj, ple_w_gate, sb_w_qkv, sb_w_o, ssm_w_in, ssm_conv_w, ssm_conv_b, ssm_dt_bias, ssm_a_log, ssm_d, ssm_norm_g, ssm_w_out, conf_w_pw1, conf_b_pw1, conf_w_dw, conf_b_dw, conf_ln_g, conf_ln_b, conf_w_pw2, conf_b_pw2, moba_w_qkv, moba_w_o):
    bsz, seq, d_model = x_prompt.shape
    db = x_sample.shape[0]
    assert x_sample.shape[1] == 1
    mp = bsz * seq
    depth = ffn_w1.shape[0]
    bf = lambda w: w.astype(BF16)

    xp_f = x_prompt.reshape(mp, d_model)
    xs_f = x_sample.reshape(db, d_model)
    xp_b, xs_b = bf(xp_f), bf(xs_f)
    n_phys = cache_sb_k.shape[1]
    outs = {k: [] for k in ("sb_kp", "sb_vp", "sb_ks", "sb_vs", "ssm_hp", "ssm_hs", "ssm_cp", "ssm_cs",
                            "conf_cp", "conf_cs", "mo_kp", "mo_vp", "mo_ks", "mo_vs")}

    for i in range(depth):
        m, j = i % 4, i // 4
        w1, w3, w2 = bf(ffn_w1[i, 0]), bf(ffn_w3[i, 0]), bf(ffn_w2[i, 0])
        xp_f, xp_b = _ffn_half(xp_f, xp_b, w1, w3, w2, ln_g[i, 0], ln_b[i, 0])
        xs_f, xs_b = _ffn_half(xs_f, xs_b, w1, w3, w2, ln_g[i, 0], ln_b[i, 0])
        g1, b1 = ln_g[i, 1], ln_b[i, 1]

        if m == 0 or m == 3:
            w_qkv = bf(sb_w_qkv[j] if m == 0 else moba_w_qkv[j])
            w_o = bf(sb_w_o[j] if m == 0 else moba_w_o[j])
            wq, wk, wv = w_qkv[:, :Q_DIM], w_qkv[:, Q_DIM:Q_DIM + KV_DIM], w_qkv[:, Q_DIM + KV_DIM:]
            q = matmul(xp_b, wq, None, BF16).reshape(bsz, seq, Q_DIM)
            k = matmul(xp_b, wk, None, F32).reshape(bsz, seq, KV_DIM)
            v = matmul(xp_b, wv, None, F32).reshape(bsz, seq, KV_DIM)
            attend = sb_prompt_attention if m == 0 else moba_prompt_attention
            o = attend(q, k, v).reshape(mp, Q_DIM)
            xp_f, xp_b = matmul_postnorm(o, w_o, None, xp_f, g1, b1, 1.0)
            kp = k.reshape(bsz, seq, N_KV_HEADS, HEAD_DIM)
            vp = v.reshape(bsz, seq, N_KV_HEADS, HEAD_DIM)
            qs = matmul(xs_b, wq, None, F32)
            ks = matmul(xs_b, wk, None, F32)
            vs = matmul(xs_b, wv, None, F32)
            if m == 0:
                kc = cache_sb_k[j].reshape(n_phys, PAGE_SIZE, KV_DIM)
                vc = cache_sb_v[j].reshape(n_phys, PAGE_SIZE, KV_DIM)
                os_ = sb_decode_attention(qs.reshape(db, N_HEADS, HEAD_DIM), kc, vc, page_table)
            else:
                kc = cache_moba_k[j].reshape(n_phys, PAGE_SIZE, KV_DIM)
                vc = cache_moba_v[j].reshape(n_phys, PAGE_SIZE, KV_DIM)
                qh = qs.reshape(db, N_HEADS, HEAD_DIM)
                kmean = moba_cache_block_means(kc, page_table)
                sel = moba_decode_select(qh, kmean)
                os_ = moba_decode_attention(qh.reshape(db, N_HEADS, 1, HEAD_DIM),
                                            ks.reshape(db, N_KV_HEADS, 1, HEAD_DIM),
                                            vs.reshape(db, N_KV_HEADS, 1, HEAD_DIM),
                                            sel, kc, vc, page_table)
            xs_f, xs_b = matmul_postnorm(bf(os_.reshape(db, Q_DIM)), w_o, None, xs_f, g1, b1, 1.0)
            ks4 = ks.reshape(db, 1, N_KV_HEADS, HEAD_DIM)
            vs4 = vs.reshape(db, 1, N_KV_HEADS, HEAD_DIM)
            if m == 0:
                outs["sb_kp"].append(kp); outs["sb_vp"].append(vp); outs["sb_ks"].append(ks4); outs["sb_vs"].append(vs4)
            else:
                outs["mo_kp"].append(kp); outs["mo_vp"].append(vp); outs["mo_ks"].append(ks4); outs["mo_vs"].append(vs4)

        elif m == 1:
            n_heads = ssm_dt_bias.shape[1]
            d_inner = n_heads * SSM_HEAD_DIM
            conv_dim = ssm_conv_w.shape[2]
            hpg = n_heads // SSM_GROUPS
            w_in = ssm_w_in[j]
            w_z, w_xbc = bf(w_in[:, :d_inner]), bf(w_in[:, d_inner:d_inner + conv_dim])
            w_dt = w_in[:, d_inner + conv_dim:].reshape(d_model, SSM_GROUPS, hpg)
            w_dt = bf(jnp.zeros((d_model, SSM_GROUPS, LANES), F32).at[:, :, :hpg].set(w_dt)
                      .reshape(d_model, SSM_GROUPS * LANES))
            dtb_g = _group_pad(ssm_dt_bias[j], n_heads)
            alog_g = _group_pad(ssm_a_log[j], n_heads)
            d_g = _group_pad(ssm_d[j], n_heads)
            w_out = bf(ssm_w_out[j])
            hp = 8
            z = matmul(xp_b, w_z, None, F32)
            xbc_raw = matmul(xp_b, w_xbc, None, F32).reshape(bsz, seq, conv_dim)
            dt_raw = matmul(xp_b, w_dt, None, F32).reshape(bsz, seq, SSM_GROUPS * LANES)
            xbc = causal_dwconv(xbc_raw, jnp.zeros((bsz, hp, conv_dim), F32), ssm_conv_w[j], ssm_conv_b[j], "silu")
            h0 = jnp.zeros((bsz, n_heads, SSM_HEAD_DIM, SSM_D_STATE), F32)
            y, h_last = ssd_prompt(xbc, dt_raw, dtb_g, alog_g, d_g, h0)
            yn = gated_rms_norm(y.reshape(mp, d_inner), z, ssm_norm_g[j])
            xp_f, xp_b = matmul_postnorm(yn, w_out, None, xp_f, g1, b1, 1.0)
            outs["ssm_hp"].append(h_last)
            outs["ssm_cp"].append(xbc_raw[:, seq - (SSM_CONV - 1):, :])
            zs = matmul(xs_b, w_z, None, F32)
            xbc_s = matmul(xs_b, w_xbc, None, F32).reshape(db, 1, conv_dim)
            dt_s = matmul(xs_b, w_dt, None, F32).reshape(db, 1, SSM_GROUPS * LANES)
            hist = _front_pad_rows(state_ssm_conv[j], hp)
            xbc_sa = causal_dwconv(_pad_rows(xbc_s, 8), hist, ssm_conv_w[j], ssm_conv_b[j], "silu")[:, :1, :]
            x_col = xbc_sa[:, 0, :d_inner].reshape(db, n_heads, SSM_HEAD_DIM, 1)
            y_col, hs = ssd_step(x_col, xbc_sa[:, :, d_inner:], dt_s, dtb_g, alog_g, d_g, state_ssm[j])
            yns = gated_rms_norm(y_col.reshape(db, d_inner), zs, ssm_norm_g[j])
            xs_f, xs_b = matmul_postnorm(yns, w_out, None, xs_f, g1, b1, 1.0)
            outs["ssm_hs"].append(hs)
            outs["ssm_cs"].append(jnp.concatenate([state_ssm_conv[j], xbc_s], axis=1)[:, 1:, :])

        else:
            w_pw1 = conf_w_pw1[j]
            wa, wg = bf(w_pw1[:, :d_model]), bf(w_pw1[:, d_model:])
            ba, bg = conf_b_pw1[j][:d_model], conf_b_pw1[j][d_model:]
            w_pw2 = bf(conf_w_pw2[j])
            hp = 32
            u = gated_matmul(xp_b, wa, wg, ba, bg, "glu", F32).reshape(bsz, seq, d_model)
            uc = causal_dwconv(u, jnp.zeros((bsz, hp, d_model), F32), conf_w_dw[j], conf_b_dw[j], "ln_silu",
                               conf_ln_g[j], conf_ln_b[j], BF16)
            xp_f, xp_b = matmul_postnorm(uc.reshape(mp, d_model), w_pw2, conf_b_pw2[j], xp_f, g1, b1, 1.0)
            outs["conf_cp"].append(u[:, seq - (CONF_WIDTH - 1):, :])
            us = gated_matmul(xs_b, wa, wg, ba, bg, "glu", F32).reshape(db, 1, d_model)
            hist = _front_pad_rows(state_conf_conv[j], hp)
            ucs = causal_dwconv(_pad_rows(us, 8), hist, conf_w_dw[j], conf_b_dw[j], "ln_silu",
                                conf_ln_g[j], conf_ln_b[j], BF16)[:, 0, :]
            xs_f, xs_b = matmul_postnorm(ucs, w_pw2, conf_b_pw2[j], xs_f, g1, b1, 1.0)
            outs["conf_cs"].append(jnp.concatenate([state_conf_conv[j], us], axis=1)[:, 1:, :])

        w1, w3, w2 = bf(ffn_w1[i, 1]), bf(ffn_w3[i, 1]), bf(ffn_w2[i, 1])
        xp_f, xp_b = _ffn_half(xp_f, xp_b, w1, w3, w2, ln_g[i, 2], ln_b[i, 2])
        xs_f, xs_b = _ffn_half(xs_f, xs_b, w1, w3, w2, ln_g[i, 2], ln_b[i, 2])
        wg_, wp_ = bf(ple_w_gate[i]), bf(ple_w_proj[i])
        xp_f, xp_b = ple_add(xp_b, xp_f, p_prompt[i].reshape(mp, -1), wg_, wp_, ln_g[i, 3], ln_b[i, 3])
        xs_f, xs_b = ple_add(xs_b, xs_f, p_sample[i].reshape(db, -1), wg_, wp_, ln_g[i, 3], ln_b[i, 3])

    st = lambda name: jnp.stack(outs[name])
    return (xp_f.reshape(bsz, seq, d_model), xs_f.reshape(db, 1, d_model),
            st("sb_kp"), st("sb_vp"), st("sb_ks"), st("sb_vs"),
            st("ssm_hp"), st("ssm_hs"), st("ssm_cp"), st("ssm_cs"),
            st("conf_cp"), st("conf_cs"),
            st("mo_kp"), st("mo_vp"), st("mo_ks"), st("mo_vs"))
```

```python
import functools

import jax
import jax.numpy as jnp
from jax import lax
from jax.experimental import pallas as pl
from jax.experimental.pallas import tpu as pltpu

F32 = jnp.float32
BF16 = jnp.bfloat16

N_HEADS = 16
HEAD_DIM = 128
N_KV_HEADS = 4
Q_PER_KV = N_HEADS // N_KV_HEADS
Q_DIM = N_HEADS * HEAD_DIM
KV_DIM = N_KV_HEADS * HEAD_DIM
SB_BLOCK = 128
MOBA_BLOCK = 256
MOBA_TOPK = 3
PAGE_SIZE = 128
SSM_HEAD_DIM = 64
SSM_GROUPS = 8
SSM_D_STATE = 128
SSM_CONV = 4
SSM_CHUNK = 128
CONF_WIDTH = 31
LN_EPS = 1e-5
DEPTH = 4
DN_ALPHA = (2 * DEPTH) ** 0.25
LANES = 128
NEG_BIG = -1e30
SB_LOG_FLOOR = -104.0
VMEM_LIMIT = 56 * 1024 * 1024


def _params(*sem):
    return pltpu.CompilerParams(dimension_semantics=sem, vmem_limit_bytes=VMEM_LIMIT)


def _layer_norm_rows(y, g, b):
    mu = jnp.mean(y, axis=-1, keepdims=True)
    d = y - mu
    var = jnp.mean(d * d, axis=-1, keepdims=True)
    return d * lax.rsqrt(var + LN_EPS) * g + b


def _split3(x):
    hi = x.astype(BF16)
    r1 = x - hi.astype(F32)
    mid = r1.astype(BF16)
    lo = (r1 - mid.astype(F32)).astype(BF16)
    return hi, mid, lo


def _dot(a, b):
    return jnp.dot(a, b, preferred_element_type=F32)


def _dot_nt(a, b):
    return lax.dot_general(a, b, (((1,), (1,)), ((), ())), preferred_element_type=F32)


def _dot01(x, m01):
    hi, mid, lo = _split3(x)
    return _dot(hi, m01) + _dot(mid, m01) + _dot(lo, m01)


def _pick_tile(n, pref):
    for t in pref:
        if n % t == 0:
            return t
    return n


def _mm_kernel(x_ref, w_ref, b_ref, o_ref):
    o_ref[...] = (_dot(x_ref[...], w_ref[...]) + b_ref[...]).astype(o_ref.dtype)


def matmul(x, w, bias=None, out_dtype=F32):
    m, k = x.shape
    n = w.shape[1]
    tm = _pick_tile(m, (1024, 512, 256, 128, 8))
    tn = _pick_tile(n, (512, 256, 128))
    if bias is None:
        bias = jnp.zeros((n,), F32)
    return pl.pallas_call(
        _mm_kernel,
        grid=(m // tm, n // tn),
        in_specs=[pl.BlockSpec((tm, k), lambda i, j: (i, 0)),
                  pl.BlockSpec((k, tn), lambda i, j: (0, j)),
                  pl.BlockSpec((1, tn), lambda i, j: (0, j))],
        out_specs=pl.BlockSpec((tm, tn), lambda i, j: (i, j)),
        out_shape=jax.ShapeDtypeStruct((m, n), out_dtype),
        compiler_params=_params("parallel", "arbitrary"),
        name="mm",
    )(x, w, bias.reshape(1, n).astype(F32))


def _gated_mm_kernel(x_ref, wa_ref, wb_ref, ba_ref, bb_ref, o_ref, *, mode):
    x = x_ref[...]
    a = _dot(x, wa_ref[...]) + ba_ref[...]
    b = _dot(x, wb_ref[...]) + bb_ref[...]
    if mode == "swiglu":
        o = a * jax.nn.sigmoid(a) * b
    else:
        o = a * jax.nn.sigmoid(b)
    o_ref[...] = o.astype(o_ref.dtype)


def gated_matmul(x, wa, wb, ba, bb, mode, out_dtype):
    m, k = x.shape
    n = wa.shape[1]
    tm = _pick_tile(m, (1024, 512, 256, 128, 8))
    tn = _pick_tile(n, (512, 256, 128))
    if ba is None:
        ba = jnp.zeros((n,), F32)
        bb = jnp.zeros((n,), F32)
    return pl.pallas_call(
        functools.partial(_gated_mm_kernel, mode=mode),
        grid=(m // tm, n // tn),
        in_specs=[pl.BlockSpec((tm, k), lambda i, j: (i, 0)),
                  pl.BlockSpec((k, tn), lambda i, j: (0, j)),
                  pl.BlockSpec((k, tn), lambda i, j: (0, j)),
                  pl.BlockSpec((1, tn), lambda i, j: (0, j)),
                  pl.BlockSpec((1, tn), lambda i, j: (0, j))],
        out_specs=pl.BlockSpec((tm, tn), lambda i, j: (i, j)),
        out_shape=jax.ShapeDtypeStruct((m, n), out_dtype),
        compiler_params=_params("parallel", "arbitrary"),
        name="gated_mm",
    )(x, wa, wb, ba.reshape(1, n).astype(F32), bb.reshape(1, n).astype(F32))


def _mm_ln_kernel(a_ref, w_ref, bias_ref, res_ref, g_ref, b_ref, of_ref, ob_ref, acc_ref, *, scale, nk):
    k = pl.program_id(1)

    @pl.when(k == 0)
    def _():
        acc_ref[...] = jnp.zeros_like(acc_ref)

    acc_ref[...] += _dot(a_ref[...], w_ref[...])

    @pl.when(k == nk - 1)
    def _():
        y = DN_ALPHA * res_ref[...] + scale * (acc_ref[...] + bias_ref[...])
        out = _layer_norm_rows(y, g_ref[...], b_ref[...])
        of_ref[...] = out
        ob_ref[...] = out.astype(BF16)


def matmul_postnorm(a, w, bias, res, g, b, scale):
    m, kdim = a.shape
    n = w.shape[1]
    tm = _pick_tile(m, (512, 256, 128, 8))
    tk = _pick_tile(kdim, (512, 256, 128))
    nk = kdim // tk
    if bias is None:
        bias = jnp.zeros((n,), F32)
    row = lambda v: v.reshape(1, n).astype(F32)
    return pl.pallas_call(
        functools.partial(_mm_ln_kernel, scale=scale, nk=nk),
        grid=(m // tm, nk),
        in_specs=[pl.BlockSpec((tm, tk), lambda i, k: (i, k)),
                  pl.BlockSpec((tk, n), lambda i, k: (k, 0)),
                  pl.BlockSpec((1, n), lambda i, k: (0, 0)),
                  pl.BlockSpec((tm, n), lambda i, k: (i, 0)),
                  pl.BlockSpec((1, n), lambda i, k: (0, 0)),
                  pl.BlockSpec((1, n), lambda i, k: (0, 0))],
        out_specs=[pl.BlockSpec((tm, n), lambda i, k: (i, 0)),
                   pl.BlockSpec((tm, n), lambda i, k: (i, 0))],
        out_shape=[jax.ShapeDtypeStruct((m, n), F32), jax.ShapeDtypeStruct((m, n), BF16)],
        scratch_shapes=[pltpu.VMEM((tm, n), F32)],
        compiler_params=_params("parallel", "arbitrary"),
        name="mm_postnorm",
    )(a, w, row(bias), res, row(g), row(b))


def _ple_kernel(xb_ref, xf_ref, p_ref, wg_ref, wp_ref, g_ref, b_ref, of_ref, ob_ref):
    gate = jax.nn.sigmoid(_dot(xb_ref[...], wg_ref[...]))
    proj = _dot(p_ref[...].astype(BF16), wp_ref[...])
    y = DN_ALPHA * xf_ref[...] + gate * proj
    out = _layer_norm_rows(y, g_ref[...], b_ref[...])
    of_ref[...] = out
    ob_ref[...] = out.astype(BF16)


def ple_add(xb, xf, p, wg, wp, g, b):
    m, d = xf.shape
    pd = p.shape[1]
    tm = _pick_tile(m, (256, 128, 8))
    row = lambda v: v.reshape(1, d).astype(F32)
    return pl.pallas_call(
        _ple_kernel,
        grid=(m // tm,),
        in_specs=[pl.BlockSpec((tm, d), lambda i: (i, 0)),
                  pl.BlockSpec((tm, d), lambda i: (i, 0)),
                  pl.BlockSpec((tm, pd), lambda i: (i, 0)),
                  pl.BlockSpec((d, d), lambda i: (0, 0)),
                  pl.BlockSpec((pd, d), lambda i: (0, 0)),
                  pl.BlockSpec((1, d), lambda i: (0, 0)),
                  pl.BlockSpec((1, d), lambda i: (0, 0))],
        out_specs=[pl.BlockSpec((tm, d), lambda i: (i, 0)),
                   pl.BlockSpec((tm, d), lambda i: (i, 0))],
        out_shape=[jax.ShapeDtypeStruct((m, d), F32), jax.ShapeDtypeStruct((m, d), BF16)],
        compiler_params=_params("parallel"),
        name="ple_add",
    )(xb, xf, p, wg, wp, row(g), row(b))


def _softplus(z):
    return jnp.maximum(z, 0.0) + jnp.log1p(jnp.exp(-jnp.abs(z)))


def _sb_prompt_kernel(q_ref, k_ref, v_ref, o_ref, *, tq):
    i = pl.program_id(2)
    rows = Q_PER_KV * tq
    scale = HEAD_DIM ** -0.5
    q = jnp.concatenate([q_ref[0, :, g * HEAD_DIM:(g + 1) * HEAD_DIM] for g in range(Q_PER_KV)], axis=0)
    t_loc = lax.broadcasted_iota(jnp.int32, (rows, tq), 0) & (tq - 1)
    s_loc = lax.broadcasted_iota(jnp.int32, (rows, tq), 1)
    later = (lax.broadcasted_iota(jnp.int32, (tq, tq), 0) > lax.broadcasted_iota(jnp.int32, (tq, tq), 1)).astype(BF16)

    def block(j, run, acc, diagonal):
        start = pl.multiple_of(j * tq, tq)
        kb = k_ref[0, pl.ds(start, tq), :].astype(BF16)
        vb = v_ref[0, pl.ds(start, tq), :].astype(BF16)
        z = _dot_nt(q, kb) * scale
        log_keep = -_softplus(z)
        if diagonal:
            valid = s_loc < t_loc
            log_keep = jnp.where(valid, log_keep, 0.0)
        log_w = z + log_keep + _dot01(log_keep, later) + run
        w = jnp.exp(log_w)
        if diagonal:
            w = jnp.where(valid, w, 0.0)
        acc = acc + _dot(w.astype(BF16), vb)
        run = run + jnp.sum(log_keep, axis=1, keepdims=True)
        return run, acc

    run0 = jnp.zeros((rows, 1), F32)
    acc0 = jnp.zeros((rows, HEAD_DIM), F32)
    run, acc = block(i, run0, acc0, True)

    def cond(carry):
        return jnp.logical_and(carry[0] < i, jnp.max(carry[1]) > SB_LOG_FLOOR)

    def body(carry):
        it, run, acc = carry
        run, acc = block(i - 1 - it, run, acc, False)
        return it + 1, run, acc

    _, run, acc = lax.while_loop(cond, body, (jnp.int32(0), run, acc))
    for g in range(Q_PER_KV):
        o_ref[0, :, g * HEAD_DIM:(g + 1) * HEAD_DIM] = acc[g * tq:(g + 1) * tq].astype(o_ref.dtype)


def sb_prompt_attention(q, k, v):
    bsz, L, _ = q.shape
    tq = SB_BLOCK
    gw = Q_PER_KV * HEAD_DIM
    return pl.pallas_call(
        functools.partial(_sb_prompt_kernel, tq=tq),
        grid=(bsz, N_KV_HEADS, L // tq),
        in_specs=[pl.BlockSpec((1, tq, gw), lambda b, h, i: (b, i, h)),
                  pl.BlockSpec((1, L, HEAD_DIM), lambda b, h, i: (b, 0, h)),
                  pl.BlockSpec((1, L, HEAD_DIM), lambda b, h, i: (b, 0, h))],
        out_specs=pl.BlockSpec((1, tq, gw), lambda b, h, i: (b, i, h)),
        out_shape=jax.ShapeDtypeStruct(q.shape, BF16),
        compiler_params=_params("parallel", "parallel", "arbitrary"),
        name="sb_prompt",
    )(q, k, v)


PAGE_ROWS = PAGE_SIZE * N_KV_HEADS
SB_PAGES_PER_STEP = 8
MOBA_BLOCKS_PER_STEP = 4


def _own_lane_mask(kvh_of_row, shape):
    lane = lax.broadcasted_iota(jnp.int32, shape, 1)
    return (lane & (N_KV_HEADS - 1)) == kvh_of_row


def _suffix_sum_keys(x):
    n = x.shape[1]
    lane = lax.broadcasted_iota(jnp.int32, x.shape, 1)
    d = N_KV_HEADS
    while d < n:
        shifted = pltpu.roll(x, n - d, axis=1)
        x = x + jnp.where(lane + d < n, shifted, 0.0)
        d *= 2
    return x


def _sb_decode_kernel(pt_ref, q_ref, *refs, pps, n_steps):
    k_refs, v_refs = refs[:pps], refs[pps:2 * pps]
    o_ref, acc_ref, run_ref = refs[2 * pps:]
    s = pl.program_id(1)

    @pl.when(s == 0)
    def _():
        acc_ref[...] = jnp.zeros_like(acc_ref)
        run_ref[...] = jnp.zeros_like(run_ref)

    q = q_ref[0].astype(BF16)
    kvh = lax.broadcasted_iota(jnp.int32, (N_HEADS, PAGE_ROWS), 0) >> 2
    own = _own_lane_mask(kvh, (N_HEADS, PAGE_ROWS))
    for u in range(pps):
        @pl.when(jnp.max(run_ref[...]) > SB_LOG_FLOOR)
        def _():
            z = _dot_nt(q, k_refs[u][...].astype(BF16)) * (HEAD_DIM ** -0.5)
            log_keep = jnp.where(own, -_softplus(z), 0.0)
            incl = _suffix_sum_keys(log_keep)
            w = jnp.where(own, jnp.exp(z + incl + run_ref[:, 0:1]), 0.0)
            acc_ref[...] += _dot(w.astype(BF16), v_refs[u][...].astype(BF16))
            run_ref[...] += jnp.sum(log_keep, axis=1, keepdims=True)

    @pl.when(s == n_steps - 1)
    def _():
        o_ref[0] = acc_ref[...]


def sb_decode_attention(q, k_rows, v_rows, page_table):
    db, n_pages = page_table.shape
    pps = SB_PAGES_PER_STEP
    assert n_pages % pps == 0
    n_steps = n_pages // pps

    def page_spec(u):
        return pl.BlockSpec((PAGE_ROWS, HEAD_DIM),
                            lambda b, s, pt: (pt[b * n_pages + (n_pages - 1 - (s * pps + u))], 0))

    specs = [page_spec(u) for u in range(pps)]
    return pl.pallas_call(
        functools.partial(_sb_decode_kernel, pps=pps, n_steps=n_steps),
        grid_spec=pltpu.PrefetchScalarGridSpec(
            num_scalar_prefetch=1,
            grid=(db, n_steps),
            in_specs=[pl.BlockSpec((1, N_HEADS, HEAD_DIM), lambda b, s, pt: (b, 0, 0))] + specs + specs,
            out_specs=pl.BlockSpec((1, N_HEADS, HEAD_DIM), lambda b, s, pt: (b, 0, 0)),
            scratch_shapes=[pltpu.VMEM((N_HEADS, HEAD_DIM), F32), pltpu.VMEM((N_HEADS, LANES), F32)]),
        out_shape=jax.ShapeDtypeStruct((db, N_HEADS, HEAD_DIM), F32),
        compiler_params=_params("parallel", "arbitrary"),
        name="sb_decode",
    )(page_table.reshape(-1), q, *([k_rows] * pps), *([v_rows] * pps))


def _top_k_mask(gate, n_valid, k):
    lane = lax.broadcasted_iota(jnp.int32, gate.shape, 1)
    sel = jnp.zeros(gate.shape, F32)
    picks = []
    for r in range(k):
        m = jnp.max(gate, axis=1, keepdims=True)
        idx = jnp.min(jnp.where(gate == m, lane, gate.shape[1] - 1), axis=1, keepdims=True)
        pick = lane == idx
        sel = jnp.maximum(sel, jnp.where(pick, jnp.where(r < n_valid, 1.0, 0.0), 0.0))
        gate = jnp.where(pick, -jnp.inf, gate)
        picks.append(idx)
    return sel, picks


def _moba_prompt_kernel(q_ref, k_ref, v_ref, o_ref, km_ref, *, tq, nb):
    i = pl.program_id(2)
    rows = Q_PER_KV * tq
    scale = HEAD_DIM ** -0.5

    @pl.when(i == 0)
    def _():
        km_ref[...] = jnp.zeros_like(km_ref)
        for jb in range(nb):
            km_ref[jb:jb + 1, :] = jnp.mean(k_ref[0, jb * tq:(jb + 1) * tq, :], axis=0, keepdims=True)

    q = jnp.concatenate([q_ref[0, :, g * HEAD_DIM:(g + 1) * HEAD_DIM] for g in range(Q_PER_KV)], axis=0)
    lane = lax.broadcasted_iota(jnp.int32, (rows, LANES), 1)
    gate = _dot_nt(q, km_ref[...].astype(BF16))
    gate = jnp.where(lane < i, gate, -jnp.inf)
    sel, _ = _top_k_mask(gate, i, MOBA_TOPK)
    sel = sel.astype(BF16)

    def scores(j):
        start = pl.multiple_of(j * tq, tq)
        kb = k_ref[0, pl.ds(start, tq), :].astype(BF16)
        vb = v_ref[0, pl.ds(start, tq), :].astype(BF16)
        return _dot_nt(q, kb) * scale, vb

    s, vb = scores(i)
    t_loc = lax.broadcasted_iota(jnp.int32, (rows, tq), 0) & (tq - 1)
    s_loc = lax.broadcasted_iota(jnp.int32, (rows, tq), 1)
    s = jnp.where(s_loc <= t_loc, s, NEG_BIG)
    m = jnp.max(s, axis=1, keepdims=True)
    p = jnp.exp(s - m)
    l = jnp.sum(p, axis=1, keepdims=True)
    acc = _dot(p.astype(BF16), vb)

    def body(j, carry):
        m, l, acc = carry
        s, vb = scores(j)
        onehot = (lax.broadcasted_iota(jnp.int32, (LANES, tq), 0) == j).astype(BF16)
        chosen = _dot(sel, onehot) > 0.5
        s = jnp.where(chosen, s, NEG_BIG)
        m_new = jnp.maximum(m, jnp.max(s, axis=1, keepdims=True))
        alpha = jnp.exp(m - m_new)
        p = jnp.exp(s - m_new)
        l = alpha * l + jnp.sum(p, axis=1, keepdims=True)
        acc = alpha * acc + _dot(p.astype(BF16), vb)
        return m_new, l, acc

    m, l, acc = lax.fori_loop(0, i, body, (m, l, acc))
    out = acc / l
    for g in range(Q_PER_KV):
        o_ref[0, :, g * HEAD_DIM:(g + 1) * HEAD_DIM] = out[g * tq:(g + 1) * tq].astype(o_ref.dtype)


def moba_prompt_attention(q, k, v):
    bsz, L, _ = q.shape
    tq = MOBA_BLOCK
    nb = L // tq
    assert L % tq == 0 and nb <= LANES
    gw = Q_PER_KV * HEAD_DIM
    return pl.pallas_call(
        functools.partial(_moba_prompt_kernel, tq=tq, nb=nb),
        grid=(bsz, N_KV_HEADS, nb),
        in_specs=[pl.BlockSpec((1, tq, gw), lambda b, h, i: (b, i, h)),
                  pl.BlockSpec((1, L, HEAD_DIM), lambda b, h, i: (b, 0, h)),
                  pl.BlockSpec((1, L, HEAD_DIM), lambda b, h, i: (b, 0, h))],
        out_specs=pl.BlockSpec((1, tq, gw), lambda b, h, i: (b, i, h)),
        out_shape=jax.ShapeDtypeStruct(q.shape, BF16),
        scratch_shapes=[pltpu.VMEM((LANES, HEAD_DIM), F32)],
        compiler_params=_params("parallel", "arbitrary", "arbitrary"),
        name="moba_prompt",
    )(q, k, v)


def _moba_kmean_kernel(pt_ref, *refs, bps, ppb):
    k_refs, o_ref = refs[:bps * ppb], refs[bps * ppb]
    sub = 8
    for j in range(bps):
        tot = jnp.zeros((sub, HEAD_DIM), F32)
        for h in range(ppb):
            k_ref = k_refs[j * ppb + h]
            for c in range(PAGE_ROWS // sub):
                tot = tot + k_ref[c * sub:(c + 1) * sub, :]
        mean = (tot[:N_KV_HEADS] + tot[N_KV_HEADS:]) * (1.0 / MOBA_BLOCK)
        o_ref[0, j * N_KV_HEADS:(j + 1) * N_KV_HEADS, :] = mean


def moba_cache_block_means(k_rows, page_table):
    db, n_pages = page_table.shape
    ppb = MOBA_BLOCK // PAGE_SIZE
    nblk = n_pages // ppb
    bps = MOBA_BLOCKS_PER_STEP
    assert nblk % bps == 0

    def page_spec(u):
        return pl.BlockSpec((PAGE_ROWS, HEAD_DIM), lambda b, j, pt: (pt[b * n_pages + j * (bps * ppb) + u], 0))

    return pl.pallas_call(
        functools.partial(_moba_kmean_kernel, bps=bps, ppb=ppb),
        grid_spec=pltpu.PrefetchScalarGridSpec(
            num_scalar_prefetch=1,
            grid=(db, nblk // bps),
            in_specs=[page_spec(u) for u in range(bps * ppb)],
            out_specs=pl.BlockSpec((1, bps * N_KV_HEADS, HEAD_DIM), lambda b, j, pt: (b, j, 0))),
        out_shape=jax.ShapeDtypeStruct((db, nblk * N_KV_HEADS, HEAD_DIM), F32),
        compiler_params=_params("parallel", "arbitrary"),
        name="moba_kmean",
    )(page_table.reshape(-1), *([k_rows] * (bps * ppb)))


def _moba_gate_kernel(q_ref, km_ref, o_ref, *, nblk):
    gate = _dot_nt(q_ref[0].astype(BF16), km_ref[0].astype(BF16))
    kvh = lax.broadcasted_iota(jnp.int32, gate.shape, 0) >> 2
    gate = jnp.where(_own_lane_mask(kvh, gate.shape), gate, -jnp.inf)
    _, picks = _top_k_mask(gate, nblk, MOBA_TOPK)
    lane = lax.broadcasted_iota(jnp.int32, (N_HEADS, LANES), 1)
    out = jnp.zeros((N_HEADS, LANES), jnp.int32)
    for r, idx in enumerate(picks):
        out = jnp.where(lane == r, idx >> 2, out)
    o_ref[0] = out


def moba_decode_select(q, kmean):
    db, rows, _ = kmean.shape
    nblk = rows // N_KV_HEADS
    out = pl.pallas_call(
        functools.partial(_moba_gate_kernel, nblk=nblk),
        grid=(db,),
        in_specs=[pl.BlockSpec((1, N_HEADS, HEAD_DIM), lambda b: (b, 0, 0)),
                  pl.BlockSpec((1, rows, HEAD_DIM), lambda b: (b, 0, 0))],
        out_specs=pl.BlockSpec((1, N_HEADS, LANES), lambda b: (b, 0, 0)),
        out_shape=jax.ShapeDtypeStruct((db, N_HEADS, LANES), jnp.int32),
        compiler_params=_params("parallel"),
        name="moba_gate",
    )(q, kmean)
    return out[:, :, :MOBA_TOPK]


def _moba_decode_kernel(sel_ref, pt_ref, q_ref, kn_ref, vn_ref, *refs, n_sel):
    k_refs, v_refs, o_ref = refs[:n_sel], refs[n_sel:2 * n_sel], refs[2 * n_sel]
    scale = HEAD_DIM ** -0.5
    q = jnp.broadcast_to(q_ref[0, 0], (8, HEAD_DIM))
    own = _own_lane_mask(pl.program_id(1) >> 2, (8, PAGE_ROWS))
    m = jnp.sum(q * kn_ref[0, 0], axis=1, keepdims=True) * scale
    l = jnp.ones((8, 1), F32)
    acc = jnp.broadcast_to(vn_ref[0, 0], (8, HEAD_DIM))
    qb = q.astype(BF16)
    for u in range(n_sel):
        s = jnp.where(own, _dot_nt(qb, k_refs[u][...].astype(BF16)) * scale, NEG_BIG)
        m_new = jnp.maximum(m, jnp.max(s, axis=1, keepdims=True))
        alpha = jnp.exp(m - m_new)
        p = jnp.exp(s - m_new)
        l = alpha * l + jnp.sum(p, axis=1, keepdims=True)
        acc = alpha * acc + _dot(p.astype(BF16), v_refs[u][...].astype(BF16))
        m = m_new
    o_ref[0, 0] = acc / l


def moba_decode_attention(q, k_new, v_new, sel, k_rows, v_rows, page_table):
    db, n_pages = page_table.shape
    ppb = MOBA_BLOCK // PAGE_SIZE
    n_sel = MOBA_TOPK * ppb

    def page_spec(u):
        def index(b, h, sel_r, pt):
            blk = sel_r[(b * N_HEADS + h) * MOBA_TOPK + u // ppb]
            return (pt[b * n_pages + blk * ppb + u % ppb], 0)
        return pl.BlockSpec((PAGE_ROWS, HEAD_DIM), index)

    specs = [page_spec(u) for u in range(n_sel)]
    new_spec = pl.BlockSpec((1, 1, 1, HEAD_DIM), lambda b, h, sel_r, pt: (b, h // Q_PER_KV, 0, 0))
    out = pl.pallas_call(
        functools.partial(_moba_decode_kernel, n_sel=n_sel),
        grid_spec=pltpu.PrefetchScalarGridSpec(
            num_scalar_prefetch=2,
            grid=(db, N_HEADS),
            in_specs=[pl.BlockSpec((1, 1, 1, HEAD_DIM), lambda b, h, sel_r, pt: (b, h, 0, 0)),
                      new_spec, new_spec] + specs + specs,
            out_specs=pl.BlockSpec((1, 1, 8, HEAD_DIM), lambda b, h, sel_r, pt: (b, h, 0, 0))),
        out_shape=jax.ShapeDtypeStruct((db, N_HEADS, 8, HEAD_DIM), F32),
        compiler_params=_params("parallel", "parallel"),
        name="moba_decode",
    )(sel.reshape(-1), page_table.reshape(-1), q, k_new, v_new, *([k_rows] * n_sel), *([v_rows] * n_sel))
    return out[:, :, 0, :]


def _dwconv_kernel(x_ref, hist_ref, w_ref, b_ref, g_ref, beta_ref, o_ref, pad_ref, *, width, tl, hp, post):
    t = pl.program_id(2)

    @pl.when(t == 0)
    def _():
        pad_ref[0:hp, :] = hist_ref[0]

    @pl.when(t > 0)
    def _():
        pad_ref[0:hp, :] = pad_ref[tl:tl + hp, :]

    pad_ref[hp:hp + tl, :] = x_ref[0]
    base = hp - (width - 1)
    acc = jnp.zeros(o_ref.shape[1:], F32) + b_ref[...]
    for k in range(width):
        acc = acc + pad_ref[base + k:base + k + tl, :] * w_ref[k:k + 1, :]
    if post == "ln_silu":
        acc = _layer_norm_rows(acc, g_ref[...], beta_ref[...])
    o_ref[0] = (acc * jax.nn.sigmoid(acc)).astype(o_ref.dtype)


def causal_dwconv(x, hist, w, b, post, ln_g=None, ln_b=None, out_dtype=F32):
    bsz, L, C = x.shape
    width = w.shape[0]
    hp = hist.shape[1]
    assert hp >= width - 1 and hp % 8 == 0
    tc = C if post == "ln_silu" else _pick_tile(C, (1024, 512, 256, 128))
    tl = _pick_tile(L, (256, 128)) if L >= hp else L
    assert tl >= hp or tl == L
    wp = jnp.zeros((-(-width // 8) * 8, C), F32).at[:width].set(w.astype(F32))
    if ln_g is None:
        ln_g = jnp.ones((C,), F32)
        ln_b = jnp.zeros((C,), F32)
    row = lambda v: v.reshape(1, C).astype(F32)
    vec_spec = pl.BlockSpec((1, tc), lambda bb, c, t: (0, c))
    return pl.pallas_call(
        functools.partial(_dwconv_kernel, width=width, tl=tl, hp=hp, post=post),
        grid=(bsz, C // tc, L // tl),
        in_specs=[pl.BlockSpec((1, tl, tc), lambda bb, c, t: (bb, t, c)),
                  pl.BlockSpec((1, hp, tc), lambda bb, c, t: (bb, 0, c)),
                  pl.BlockSpec((wp.shape[0], tc), lambda bb, c, t: (0, c)),
                  vec_spec, vec_spec, vec_spec],
        out_specs=pl.BlockSpec((1, tl, tc), lambda bb, c, t: (bb, t, c)),
        out_shape=jax.ShapeDtypeStruct((bsz, L, C), out_dtype),
        scratch_shapes=[pltpu.VMEM((hp + tl, tc), F32)],
        compiler_params=_params("parallel", "parallel", "arbitrary"),
        name="dwconv_" + post,
    )(x, hist, wp, row(b), row(ln_g), row(ln_b))


def _ssd_prompt_kernel(x_ref, bm_ref, cm_ref, dt_ref, dtb_ref, alog_ref, d_ref, h0_ref, y_ref, h_ref, *, chunk, hpg):
    c = pl.program_id(2)
    P = SSM_HEAD_DIM

    @pl.when(c == 0)
    def _():
        h_ref[...] = h0_ref[...]

    x = x_ref[0]
    bm = bm_ref[0]
    cm = cm_ref[0]
    dt = _softplus(dt_ref[0] + dtb_ref[...])
    a = -jnp.exp(alog_ref[...])
    li = lax.broadcasted_iota(jnp.int32, (chunk, chunk), 0)
    si = lax.broadcasted_iota(jnp.int32, (chunk, chunk), 1)
    causal = li >= si
    lower = causal.astype(BF16)
    hi, mid, lo = _split3(dt * a)
    cum = _dot(lower, hi) + _dot(lower, mid) + _dot(lower, lo)
    cum_t = cum.T
    dt_t = dt.T
    cmb = cm.astype(BF16)
    bmb = bm.astype(BF16)
    cb = _dot_nt(cmb, bmb)
    x_t = x.T
    for r in range(hpg):
        cum_col = cum[:, r:r + 1]
        cum_row = cum_t[r:r + 1, :]
        dt_row = dt_t[r:r + 1, :]
        total = cum_t[r:r + 1, chunk - 1:chunk]
        xh = x[:, r * P:(r + 1) * P]
        decay = jnp.exp(jnp.where(causal, cum_col - cum_row, -jnp.inf))
        mix = (cb * decay * dt_row).astype(BF16)
        y = _dot(mix, xh.astype(BF16))
        h_old = h_ref[0, r]
        y = y + _dot_nt(cmb, h_old.astype(BF16)) * jnp.exp(cum_col)
        y_ref[0, :, r * P:(r + 1) * P] = y + d_ref[:, r:r + 1] * xh
        to_end = jnp.exp(total - cum_row) * dt_row
        s_chunk = _dot((x_t[r * P:(r + 1) * P, :] * to_end).astype(BF16), bmb)
        h_ref[0, r] = jnp.exp(total) * h_old + s_chunk


def ssd_prompt(xbc, dt_raw, dt_bias_g, a_log_g, d_g, h0):
    bsz, L, _ = xbc.shape
    n_heads = h0.shape[1]
    G, N, P = SSM_GROUPS, SSM_D_STATE, SSM_HEAD_DIM
    hpg = n_heads // G
    d_inner = n_heads * P
    gw = hpg * P
    chunk = SSM_CHUNK
    assert gw % LANES == 0 and d_inner % N == 0
    b_off = d_inner // N
    c_off = b_off + G
    vec_spec = pl.BlockSpec((1, LANES), lambda b, g, c: (0, g))
    return pl.pallas_call(
        functools.partial(_ssd_prompt_kernel, chunk=chunk, hpg=hpg),
        grid=(bsz, G, L // chunk),
        in_specs=[pl.BlockSpec((1, chunk, gw), lambda b, g, c: (b, c, g)),
                  pl.BlockSpec((1, chunk, N), lambda b, g, c: (b, c, b_off + g)),
                  pl.BlockSpec((1, chunk, N), lambda b, g, c: (b, c, c_off + g)),
                  pl.BlockSpec((1, chunk, LANES), lambda b, g, c: (b, c, g)),
                  vec_spec, vec_spec, vec_spec,
                  pl.BlockSpec((1, hpg, P, N), lambda b, g, c: (b, g, 0, 0))],
        out_specs=[pl.BlockSpec((1, chunk, gw), lambda b, g, c: (b, c, g)),
                   pl.BlockSpec((1, hpg, P, N), lambda b, g, c: (b, g, 0, 0))],
        out_shape=[jax.ShapeDtypeStruct((bsz, L, d_inner), F32),
                   jax.ShapeDtypeStruct((bsz, n_heads, P, N), F32)],
        compiler_params=_params("parallel", "parallel", "arbitrary"),
        name="ssd_prompt",
    )(xbc, xbc, xbc, dt_raw, dt_bias_g, a_log_g, d_g, h0)


def _ssd_step_kernel(xc_ref, bm_ref, cm_ref, dt_ref, dtb_ref, alog_ref, d_ref, h0_ref, y_ref, h_ref, *, hpg):
    dt = _softplus(dt_ref[0] + dtb_ref[...])
    da = jnp.exp(dt * (-jnp.exp(alog_ref[...])))
    bm = bm_ref[0]
    cm = cm_ref[0]
    for r in range(hpg):
        xcol = xc_ref[0, r]
        h_new = da[:, r:r + 1] * h0_ref[0, r] + xcol * (dt[:, r:r + 1] * bm)
        h_ref[0, r] = h_new
        y_ref[0, r] = jnp.sum(h_new * cm, axis=1, keepdims=True) + d_ref[:, r:r + 1] * xcol


def ssd_step(x_col, bc, dt_raw, dt_bias_g, a_log_g, d_g, h0):
    db, n_heads, P, _ = x_col.shape
    G, N = SSM_GROUPS, SSM_D_STATE
    hpg = n_heads // G
    vec_spec = pl.BlockSpec((1, LANES), lambda b, g: (0, g))
    return pl.pallas_call(
        functools.partial(_ssd_step_kernel, hpg=hpg),
        grid=(db, G),
        in_specs=[pl.BlockSpec((1, hpg, P, 1), lambda b, g: (b, g, 0, 0)),
                  pl.BlockSpec((1, 1, N), lambda b, g: (b, 0, g)),
                  pl.BlockSpec((1, 1, N), lambda b, g: (b, 0, G + g)),
                  pl.BlockSpec((1, 1, LANES), lambda b, g: (b, 0, g)),
                  vec_spec, vec_spec, vec_spec,
                  pl.BlockSpec((1, hpg, P, N), lambda b, g: (b, g, 0, 0))],
        out_specs=[pl.BlockSpec((1, hpg, P, 1), lambda b, g: (b, g, 0, 0)),
                   pl.BlockSpec((1, hpg, P, N), lambda b, g: (b, g, 0, 0))],
        out_shape=[jax.ShapeDtypeStruct((db, n_heads, P, 1), F32),
                   jax.ShapeDtypeStruct((db, n_heads, P, N), F32)],
        compiler_params=_params("parallel", "parallel"),
        name="ssd_step",
    )(x_col, bc, bc, dt_raw, dt_bias_g, a_log_g, d_g, h0)


def _gated_norm_kernel(y_ref, z_ref, g_ref, o_ref):
    z = z_ref[...]
    h = y_ref[...] * (z * jax.nn.sigmoid(z))
    h = h * lax.rsqrt(jnp.mean(h * h, axis=-1, keepdims=True) + LN_EPS)
    o_ref[...] = (h * g_ref[...]).astype(o_ref.dtype)


def gated_rms_norm(y, z, g):
    m, d = y.shape
    gw = d // SSM_GROUPS
    tm = _pick_tile(m, (512, 256, 128, 8))
    spec = pl.BlockSpec((tm, gw), lambda i, j: (i, j))
    return pl.pallas_call(
        _gated_norm_kernel,
        grid=(m // tm, SSM_GROUPS),
        in_specs=[spec, spec, pl.BlockSpec((1, gw), lambda i, j: (0, j))],
        out_specs=spec,
        out_shape=jax.ShapeDtypeStruct((m, d), BF16),
        compiler_params=_params("parallel", "parallel"),
        name="gated_rms_norm",
    )(y, z, g.reshape(1, d).astype(F32))


def _group_pad(v, n_heads):
    hpg = n_heads // SSM_GROUPS
    out = jnp.zeros((SSM_GROUPS, LANES), F32).at[:, :hpg].set(v.astype(F32).reshape(SSM_GROUPS, hpg))
    return out.reshape(1, SSM_GROUPS * LANES)


def _ffn_half(xf, xb, w1, w3, w2, g, b):
    h = gated_matmul(xb, w1, w3, None, None, "swiglu", BF16)
    return matmul_postnorm(h, w2, None, xf, g, b, 0.5)


def _pad_rows(x, rows):
    return jnp.concatenate([x, jnp.zeros((x.shape[0], rows - x.shape[1]) + x.shape[2:], x.dtype)], axis=1)


def _front_pad_rows(x, rows):
    return jnp.concatenate([jnp.zeros((x.shape[0], rows - x.shape[1]) + x.shape[2:], x.dtype), x], axis=1)


def kernel(x_prompt, x_sample, p_prompt, p_sample, cache_sb_k, cache_sb_v, cache_moba_k, cache_moba_v, state_ssm, state_ssm_conv, state_conf_conv, page_table, ln_g, ln_b, ffn_w1, ffn_w3, ffn_w2, ple_w_proj, ple_w_gate, sb_w_qkv, sb_w_o, ssm_w_in, ssm_conv_w, ssm_conv_b, ssm_dt_bias, ssm_a_log, ssm_d, ssm_norm_g, ssm_w_out, conf_w_pw1, conf_b_pw1, conf_w_dw, conf_b_dw, conf_ln_g, conf_ln_b, conf_w_pw2, conf_b_pw2, moba_w_qkv, moba_w_o):
    bsz, seq, d_model = x_prompt.shape
    db = x_sample.shape[0]
    assert x_sample.shape[1] == 1
    mp = bsz * seq
    depth = ffn_w1.shape[0]
    bf = lambda w: w.astype(BF16)

    xp_f = x_prompt.reshape(mp, d_model)
    xs_f = x_sample.reshape(db, d_model)
    xp_b, xs_b = bf(xp_f), bf(xs_f)
    n_phys = cache_sb_k.shape[1]
    outs = {k: [] for k in ("sb_kp", "sb_vp", "sb_ks", "sb_vs", "ssm_hp", "ssm_hs", "ssm_cp", "ssm_cs",
                            "conf_cp", "conf_cs", "mo_kp", "mo_vp", "mo_ks", "mo_vs")}

    for i in range(depth):
        m, j = i % 4, i // 4
        w1, w3, w2 = bf(ffn_w1[i, 0]), bf(ffn_w3[i, 0]), bf(ffn_w2[i, 0])
        xp_f, xp_b = _ffn_half(xp_f, xp_b, w1, w3, w2, ln_g[i, 0], ln_b[i, 0])
        xs_f, xs_b = _ffn_half(xs_f, xs_b, w1, w3, w2, ln_g[i, 0], ln_b[i, 0])
        g1, b1 = ln_g[i, 1], ln_b[i, 1]

        if m == 0 or m == 3:
            w_qkv = bf(sb_w_qkv[j] if m == 0 else moba_w_qkv[j])
            w_o = bf(sb_w_o[j] if m == 0 else moba_w_o[j])
            wq, wk, wv = w_qkv[:, :Q_DIM], w_qkv[:, Q_DIM:Q_DIM + KV_DIM], w_qkv[:, Q_DIM + KV_DIM:]
            q = matmul(xp_b, wq, None, BF16).reshape(bsz, seq, Q_DIM)
            k = matmul(xp_b, wk, None, F32).reshape(bsz, seq, KV_DIM)
            v = matmul(xp_b, wv, None, F32).reshape(bsz, seq, KV_DIM)
            attend = sb_prompt_attention if m == 0 else moba_prompt_attention
            o = attend(q, k, v).reshape(mp, Q_DIM)
            xp_f, xp_b = matmul_postnorm(o, w_o, None, xp_f, g1, b1, 1.0)
            kp = k.reshape(bsz, seq, N_KV_HEADS, HEAD_DIM)
            vp = v.reshape(bsz, seq, N_KV_HEADS, HEAD_DIM)
            qs = matmul(xs_b, wq, None, F32)
            ks = matmul(xs_b, wk, None, F32)
            vs = matmul(xs_b, wv, None, F32)
            if m == 0:
                kc = cache_sb_k[j].reshape(n_phys * PAGE_ROWS, HEAD_DIM)
                vc = cache_sb_v[j].reshape(n_phys * PAGE_ROWS, HEAD_DIM)
                os_ = sb_decode_attention(qs.reshape(db, N_HEADS, HEAD_DIM), kc, vc, page_table)
            else:
                kc = cache_moba_k[j].reshape(n_phys * PAGE_ROWS, HEAD_DIM)
                vc = cache_moba_v[j].reshape(n_phys * PAGE_ROWS, HEAD_DIM)
                qh = qs.reshape(db, N_HEADS, HEAD_DIM)
                kmean = moba_cache_block_means(kc, page_table)
                sel = moba_decode_select(qh, kmean)
                os_ = moba_decode_attention(qh.reshape(db, N_HEADS, 1, HEAD_DIM),
                                            ks.reshape(db, N_KV_HEADS, 1, HEAD_DIM),
                                            vs.reshape(db, N_KV_HEADS, 1, HEAD_DIM),
                                            sel, kc, vc, page_table)
            xs_f, xs_b = matmul_postnorm(bf(os_.reshape(db, Q_DIM)), w_o, None, xs_f, g1, b1, 1.0)
            ks4 = ks.reshape(db, 1, N_KV_HEADS, HEAD_DIM)
            vs4 = vs.reshape(db, 1, N_KV_HEADS, HEAD_DIM)
            if m == 0:
                outs["sb_kp"].append(kp); outs["sb_vp"].append(vp); outs["sb_ks"].append(ks4); outs["sb_vs"].append(vs4)
            else:
                outs["mo_kp"].append(kp); outs["mo_vp"].append(vp); outs["mo_ks"].append(ks4); outs["mo_vs"].append(vs4)

        elif m == 1:
            n_heads = ssm_dt_bias.shape[1]
            d_inner = n_heads * SSM_HEAD_DIM
            conv_dim = ssm_conv_w.shape[2]
            hpg = n_heads // SSM_GROUPS
            w_in = ssm_w_in[j]
            w_z, w_xbc = bf(w_in[:, :d_inner]), bf(w_in[:, d_inner:d_inner + conv_dim])
            w_dt = w_in[:, d_inner + conv_dim:].reshape(d_model, SSM_GROUPS, hpg)
            w_dt = bf(jnp.zeros((d_model, SSM_GROUPS, LANES), F32).at[:, :, :hpg].set(w_dt)
                      .reshape(d_model, SSM_GROUPS * LANES))
            dtb_g = _group_pad(ssm_dt_bias[j], n_heads)
            alog_g = _group_pad(ssm_a_log[j], n_heads)
            d_g = _group_pad(ssm_d[j], n_heads)
            w_out = bf(ssm_w_out[j])
            hp = 8
            z = matmul(xp_b, w_z, None, F32)
            xbc_raw = matmul(xp_b, w_xbc, None, F32).reshape(bsz, seq, conv_dim)
            dt_raw = matmul(xp_b, w_dt, None, F32).reshape(bsz, seq, SSM_GROUPS * LANES)
            xbc = causal_dwconv(xbc_raw, jnp.zeros((bsz, hp, conv_dim), F32), ssm_conv_w[j], ssm_conv_b[j], "silu")
            h0 = jnp.zeros((bsz, n_heads, SSM_HEAD_DIM, SSM_D_STATE), F32)
            y, h_last = ssd_prompt(xbc, dt_raw, dtb_g, alog_g, d_g, h0)
            yn = gated_rms_norm(y.reshape(mp, d_inner), z, ssm_norm_g[j])
            xp_f, xp_b = matmul_postnorm(yn, w_out, None, xp_f, g1, b1, 1.0)
            outs["ssm_hp"].append(h_last)
            outs["ssm_cp"].append(xbc_raw[:, seq - (SSM_CONV - 1):, :])
            zs = matmul(xs_b, w_z, None, F32)
            xbc_s = matmul(xs_b, w_xbc, None, F32).reshape(db, 1, conv_dim)
            dt_s = matmul(xs_b, w_dt, None, F32).reshape(db, 1, SSM_GROUPS * LANES)
            hist = _front_pad_rows(state_ssm_conv[j], hp)
            xbc_sa = causal_dwconv(_pad_rows(xbc_s, 8), hist, ssm_conv_w[j], ssm_conv_b[j], "silu")[:, :1, :]
            x_col = xbc_sa[:, 0, :d_inner].reshape(db, n_heads, SSM_HEAD_DIM, 1)
            y_col, hs = ssd_step(x_col, xbc_sa[:, :, d_inner:], dt_s, dtb_g, alog_g, d_g, state_ssm[j])
            yns = gated_rms_norm(y_col.reshape(db, d_inner), zs, ssm_norm_g[j])
            xs_f, xs_b = matmul_postnorm(yns, w_out, None, xs_f, g1, b1, 1.0)
            outs["ssm_hs"].append(hs)
            outs["ssm_cs"].append(jnp.concatenate([state_ssm_conv[j], xbc_s], axis=1)[:, 1:, :])

        else:
            w_pw1 = conf_w_pw1[j]
            wa, wg = bf(w_pw1[:, :d_model]), bf(w_pw1[:, d_model:])
            ba, bg = conf_b_pw1[j][:d_model], conf_b_pw1[j][d_model:]
            w_pw2 = bf(conf_w_pw2[j])
            hp = 32
            u = gated_matmul(xp_b, wa, wg, ba, bg, "glu", F32).reshape(bsz, seq, d_model)
            uc = causal_dwconv(u, jnp.zeros((bsz, hp, d_model), F32), conf_w_dw[j], conf_b_dw[j], "ln_silu",
                               conf_ln_g[j], conf_ln_b[j], BF16)
            xp_f, xp_b = matmul_postnorm(uc.reshape(mp, d_model), w_pw2, conf_b_pw2[j], xp_f, g1, b1, 1.0)
            outs["conf_cp"].append(u[:, seq - (CONF_WIDTH - 1):, :])
            us = gated_matmul(xs_b, wa, wg, ba, bg, "glu", F32).reshape(db, 1, d_model)
            hist = _front_pad_rows(state_conf_conv[j], hp)
            ucs = causal_dwconv(_pad_rows(us, 8), hist, conf_w_dw[j], conf_b_dw[j], "ln_silu",
                                conf_ln_g[j], conf_ln_b[j], BF16)[:, 0, :]
            xs_f, xs_b = matmul_postnorm(ucs, w_pw2, conf_b_pw2[j], xs_f, g1, b1, 1.0)
            outs["conf_cs"].append(jnp.concatenate([state_conf_conv[j], us], axis=1)[:, 1:, :])

        w1, w3, w2 = bf(ffn_w1[i, 1]), bf(ffn_w3[i, 1]), bf(ffn_w2[i, 1])
        xp_f, xp_b = _ffn_half(xp_f, xp_b, w1, w3, w2, ln_g[i, 2], ln_b[i, 2])
        xs_f, xs_b = _ffn_half(xs_f, xs_b, w1, w3, w2, ln_g[i, 2], ln_b[i, 2])
        wg_, wp_ = bf(ple_w_gate[i]), bf(ple_w_proj[i])
        xp_f, xp_b = ple_add(xp_b, xp_f, p_prompt[i].reshape(mp, -1), wg_, wp_, ln_g[i, 3], ln_b[i, 3])
        xs_f, xs_b = ple_add(xs_b, xs_f, p_sample[i].reshape(db, -1), wg_, wp_, ln_g[i, 3], ln_b[i, 3])

    st = lambda name: jnp.stack(outs[name])
    return (xp_f.reshape(bsz, seq, d_model), xs_f.reshape(db, 1, d_model),
            st("sb_kp"), st("sb_vp"), st("sb_ks"), st("sb_vs"),
            st("ssm_hp"), st("ssm_hs"), st("ssm_cp"), st("ssm_cs"),
            st("conf_cp"), st("conf_cs"),
            st("mo_kp"), st("mo_vp"), st("mo_ks"), st("mo_vs"))
```

```python
import functools

import jax
import jax.numpy as jnp
from jax import lax
from jax.experimental import pallas as pl
from jax.experimental.pallas import tpu as pltpu

F32 = jnp.float32
BF16 = jnp.bfloat16

N_HEADS = 16
HEAD_DIM = 128
N_KV_HEADS = 4
Q_PER_KV = N_HEADS // N_KV_HEADS
Q_DIM = N_HEADS * HEAD_DIM
KV_DIM = N_KV_HEADS * HEAD_DIM
SB_BLOCK = 128
MOBA_BLOCK = 256
MOBA_TOPK = 3
PAGE_SIZE = 128
SSM_HEAD_DIM = 64
SSM_GROUPS = 8
SSM_D_STATE = 128
SSM_CONV = 4
SSM_CHUNK = 128
CONF_WIDTH = 31
LN_EPS = 1e-5
DEPTH = 4
DN_ALPHA = (2 * DEPTH) ** 0.25
LANES = 128
NEG_BIG = -1e30
DWCONV_ROW_TILE = 64
DWCONV_LANE_CHUNK = 256
SB_LOG_FLOOR = -104.0
VMEM_LIMIT = 56 * 1024 * 1024


def _params(*sem):
    return pltpu.CompilerParams(dimension_semantics=sem, vmem_limit_bytes=VMEM_LIMIT)


def _layer_norm_rows(y, g, b):
    mu = jnp.mean(y, axis=-1, keepdims=True)
    d = y - mu
    var = jnp.mean(d * d, axis=-1, keepdims=True)
    return d * lax.rsqrt(var + LN_EPS) * g + b


def _split3(x):
    hi = x.astype(BF16)
    r1 = x - hi.astype(F32)
    mid = r1.astype(BF16)
    lo = (r1 - mid.astype(F32)).astype(BF16)
    return hi, mid, lo


def _dot(a, b):
    return jnp.dot(a, b, preferred_element_type=F32)


def _dot_nt(a, b):
    return lax.dot_general(a, b, (((1,), (1,)), ((), ())), preferred_element_type=F32)


def _dot01(x, m01):
    hi, mid, lo = _split3(x)
    return _dot(hi, m01) + _dot(mid, m01) + _dot(lo, m01)


def _pick_tile(n, pref):
    for t in pref:
        if n % t == 0:
            return t
    return n


def _mm_kernel(x_ref, w_ref, b_ref, o_ref):
    o_ref[...] = (_dot(x_ref[...], w_ref[...]) + b_ref[...]).astype(o_ref.dtype)


def matmul(x, w, bias=None, out_dtype=F32):
    m, k = x.shape
    n = w.shape[1]
    tm = _pick_tile(m, (1024, 512, 256, 128, 8))
    tn = _pick_tile(n, (512, 256, 128))
    if bias is None:
        bias = jnp.zeros((n,), F32)
    return pl.pallas_call(
        _mm_kernel,
        grid=(m // tm, n // tn),
        in_specs=[pl.BlockSpec((tm, k), lambda i, j: (i, 0)),
                  pl.BlockSpec((k, tn), lambda i, j: (0, j)),
                  pl.BlockSpec((1, tn), lambda i, j: (0, j))],
        out_specs=pl.BlockSpec((tm, tn), lambda i, j: (i, j)),
        out_shape=jax.ShapeDtypeStruct((m, n), out_dtype),
        compiler_params=_params("parallel", "arbitrary"),
        name="mm",
    )(x, w, bias.reshape(1, n).astype(F32))


def _gated_mm_kernel(x_ref, wa_ref, wb_ref, ba_ref, bb_ref, o_ref, *, mode):
    x = x_ref[...]
    a = _dot(x, wa_ref[...]) + ba_ref[...]
    b = _dot(x, wb_ref[...]) + bb_ref[...]
    if mode == "swiglu":
        o = a * jax.nn.sigmoid(a) * b
    else:
        o = a * jax.nn.sigmoid(b)
    o_ref[...] = o.astype(o_ref.dtype)


def gated_matmul(x, wa, wb, ba, bb, mode, out_dtype):
    m, k = x.shape
    n = wa.shape[1]
    tm = _pick_tile(m, (1024, 512, 256, 128, 8))
    tn = _pick_tile(n, (512, 256, 128))
    if ba is None:
        ba = jnp.zeros((n,), F32)
        bb = jnp.zeros((n,), F32)
    return pl.pallas_call(
        functools.partial(_gated_mm_kernel, mode=mode),
        grid=(m // tm, n // tn),
        in_specs=[pl.BlockSpec((tm, k), lambda i, j: (i, 0)),
                  pl.BlockSpec((k, tn), lambda i, j: (0, j)),
                  pl.BlockSpec((k, tn), lambda i, j: (0, j)),
                  pl.BlockSpec((1, tn), lambda i, j: (0, j)),
                  pl.BlockSpec((1, tn), lambda i, j: (0, j))],
        out_specs=pl.BlockSpec((tm, tn), lambda i, j: (i, j)),
        out_shape=jax.ShapeDtypeStruct((m, n), out_dtype),
        compiler_params=_params("parallel", "arbitrary"),
        name="gated_mm",
    )(x, wa, wb, ba.reshape(1, n).astype(F32), bb.reshape(1, n).astype(F32))


def _mm_ln_kernel(a_ref, w_ref, bias_ref, res_ref, g_ref, b_ref, of_ref, ob_ref, *, scale, nk):
    k = pl.program_id(1)

    def finish(acc):
        y = DN_ALPHA * res_ref[...] + scale * (acc + bias_ref[...])
        out = _layer_norm_rows(y, g_ref[...], b_ref[...])
        of_ref[...] = out
        ob_ref[...] = out.astype(BF16)

    if nk == 1:
        finish(_dot(a_ref[...], w_ref[...]))
        return

    @pl.when(k == 0)
    def _():
        of_ref[...] = _dot(a_ref[...], w_ref[...])

    @pl.when(jnp.logical_and(k > 0, k < nk - 1))
    def _():
        of_ref[...] += _dot(a_ref[...], w_ref[...])

    @pl.when(k == nk - 1)
    def _():
        finish(of_ref[...] + _dot(a_ref[...], w_ref[...]))


def matmul_postnorm(a, w, bias, res, g, b, scale):
    m, kdim = a.shape
    n = w.shape[1]
    tm = _pick_tile(m, (512, 256, 128, 8))
    tk = _pick_tile(kdim, (2816, 2048, 1408, 1024, 512, 256, 128))
    nk = kdim // tk
    if bias is None:
        bias = jnp.zeros((n,), F32)
    row = lambda v: v.reshape(1, n).astype(F32)
    return pl.pallas_call(
        functools.partial(_mm_ln_kernel, scale=scale, nk=nk),
        grid=(m // tm, nk),
        in_specs=[pl.BlockSpec((tm, tk), lambda i, k: (i, k)),
                  pl.BlockSpec((tk, n), lambda i, k: (k, 0)),
                  pl.BlockSpec((1, n), lambda i, k: (0, 0)),
                  pl.BlockSpec((tm, n), lambda i, k: (i, 0)),
                  pl.BlockSpec((1, n), lambda i, k: (0, 0)),
                  pl.BlockSpec((1, n), lambda i, k: (0, 0))],
        out_specs=[pl.BlockSpec((tm, n), lambda i, k: (i, 0)),
                   pl.BlockSpec((tm, n), lambda i, k: (i, 0))],
        out_shape=[jax.ShapeDtypeStruct((m, n), F32), jax.ShapeDtypeStruct((m, n), BF16)],
        compiler_params=_params("parallel", "arbitrary"),
        name="mm_postnorm",
    )(a, w, row(bias), res, row(g), row(b))


def _ple_kernel(xb_ref, xf_ref, p_ref, wg_ref, wp_ref, g_ref, b_ref, of_ref, ob_ref):
    gate = jax.nn.sigmoid(_dot(xb_ref[...], wg_ref[...]))
    proj = _dot(p_ref[...].astype(BF16), wp_ref[...])
    y = DN_ALPHA * xf_ref[...] + gate * proj
    out = _layer_norm_rows(y, g_ref[...], b_ref[...])
    of_ref[...] = out
    ob_ref[...] = out.astype(BF16)


def ple_add(xb, xf, p, wg, wp, g, b):
    m, d = xf.shape
    pd = p.shape[1]
    tm = _pick_tile(m, (256, 128, 8))
    row = lambda v: v.reshape(1, d).astype(F32)
    return pl.pallas_call(
        _ple_kernel,
        grid=(m // tm,),
        in_specs=[pl.BlockSpec((tm, d), lambda i: (i, 0)),
                  pl.BlockSpec((tm, d), lambda i: (i, 0)),
                  pl.BlockSpec((tm, pd), lambda i: (i, 0)),
                  pl.BlockSpec((d, d), lambda i: (0, 0)),
                  pl.BlockSpec((pd, d), lambda i: (0, 0)),
                  pl.BlockSpec((1, d), lambda i: (0, 0)),
                  pl.BlockSpec((1, d), lambda i: (0, 0))],
        out_specs=[pl.BlockSpec((tm, d), lambda i: (i, 0)),
                   pl.BlockSpec((tm, d), lambda i: (i, 0))],
        out_shape=[jax.ShapeDtypeStruct((m, d), F32), jax.ShapeDtypeStruct((m, d), BF16)],
        compiler_params=_params("parallel"),
        name="ple_add",
    )(xb, xf, p, wg, wp, row(g), row(b))


def _softplus(z):
    return jnp.maximum(z, 0.0) + jnp.log1p(jnp.exp(-jnp.abs(z)))


def _sb_prompt_kernel(q_ref, k_ref, v_ref, o_ref, *, tq):
    i = pl.program_id(2)
    rows = Q_PER_KV * tq
    scale = HEAD_DIM ** -0.5
    q = jnp.concatenate([q_ref[0, :, g * HEAD_DIM:(g + 1) * HEAD_DIM] for g in range(Q_PER_KV)], axis=0)
    t_loc = lax.broadcasted_iota(jnp.int32, (rows, tq), 0) & (tq - 1)
    s_loc = lax.broadcasted_iota(jnp.int32, (rows, tq), 1)
    later = (lax.broadcasted_iota(jnp.int32, (tq, tq), 0) > lax.broadcasted_iota(jnp.int32, (tq, tq), 1)).astype(BF16)

    def block(j, run, acc, diagonal):
        start = pl.multiple_of(j * tq, tq)
        kb = k_ref[0, pl.ds(start, tq), :].astype(BF16)
        vb = v_ref[0, pl.ds(start, tq), :].astype(BF16)
        z = _dot_nt(q, kb) * scale
        log_keep = -_softplus(z)
        if diagonal:
            valid = s_loc < t_loc
            log_keep = jnp.where(valid, log_keep, 0.0)
        log_w = z + log_keep + _dot01(log_keep, later) + run
        w = jnp.exp(log_w)
        if diagonal:
            w = jnp.where(valid, w, 0.0)
        acc = acc + _dot(w.astype(BF16), vb)
        run = run + jnp.sum(log_keep, axis=1, keepdims=True)
        return run, acc

    run0 = jnp.zeros((rows, 1), F32)
    acc0 = jnp.zeros((rows, HEAD_DIM), F32)
    run, acc = block(i, run0, acc0, True)

    def cond(carry):
        return jnp.logical_and(carry[0] < i, jnp.max(carry[1]) > SB_LOG_FLOOR)

    def body(carry):
        it, run, acc = carry
        run, acc = block(i - 1 - it, run, acc, False)
        return it + 1, run, acc

    _, run, acc = lax.while_loop(cond, body, (jnp.int32(0), run, acc))
    for g in range(Q_PER_KV):
        o_ref[0, :, g * HEAD_DIM:(g + 1) * HEAD_DIM] = acc[g * tq:(g + 1) * tq].astype(o_ref.dtype)


def sb_prompt_attention(q, k, v):
    bsz, L, _ = q.shape
    tq = SB_BLOCK
    gw = Q_PER_KV * HEAD_DIM
    return pl.pallas_call(
        functools.partial(_sb_prompt_kernel, tq=tq),
        grid=(bsz, N_KV_HEADS, L // tq),
        in_specs=[pl.BlockSpec((1, tq, gw), lambda b, h, i: (b, i, h)),
                  pl.BlockSpec((1, L, HEAD_DIM), lambda b, h, i: (b, 0, h)),
                  pl.BlockSpec((1, L, HEAD_DIM), lambda b, h, i: (b, 0, h))],
        out_specs=pl.BlockSpec((1, tq, gw), lambda b, h, i: (b, i, h)),
        out_shape=jax.ShapeDtypeStruct(q.shape, BF16),
        compiler_params=_params("parallel", "parallel", "arbitrary"),
        name="sb_prompt",
    )(q, k, v)


PAGE_ROWS = PAGE_SIZE * N_KV_HEADS
SB_PAGES_PER_STEP = 8
MOBA_BLOCKS_PER_STEP = 4


def _own_lane_mask(kvh_of_row, shape):
    lane = lax.broadcasted_iota(jnp.int32, shape, 1)
    return (lane & (N_KV_HEADS - 1)) == kvh_of_row


def _suffix_sum_keys(x):
    n = x.shape[1]
    lane = lax.broadcasted_iota(jnp.int32, x.shape, 1)
    d = N_KV_HEADS
    while d < n:
        shifted = pltpu.roll(x, n - d, axis=1)
        x = x + jnp.where(lane + d < n, shifted, 0.0)
        d *= 2
    return x


def _sb_decode_kernel(pt_ref, q_ref, *refs, pps, n_steps):
    k_refs, v_refs = refs[:pps], refs[pps:2 * pps]
    o_ref, acc_ref, run_ref = refs[2 * pps:]
    s = pl.program_id(1)

    @pl.when(s == 0)
    def _():
        acc_ref[...] = jnp.zeros_like(acc_ref)
        run_ref[...] = jnp.zeros_like(run_ref)

    q = q_ref[0].astype(BF16)
    kvh = lax.broadcasted_iota(jnp.int32, (N_HEADS, PAGE_ROWS), 0) >> 2
    own = _own_lane_mask(kvh, (N_HEADS, PAGE_ROWS))
    for u in range(pps):
        @pl.when(jnp.max(run_ref[...]) > SB_LOG_FLOOR)
        def _():
            z = _dot_nt(q, k_refs[u][...].astype(BF16)) * (HEAD_DIM ** -0.5)
            log_keep = jnp.where(own, -_softplus(z), 0.0)
            incl = _suffix_sum_keys(log_keep)
            w = jnp.where(own, jnp.exp(z + incl + run_ref[:, 0:1]), 0.0)
            acc_ref[...] += _dot(w.astype(BF16), v_refs[u][...].astype(BF16))
            run_ref[...] += jnp.sum(log_keep, axis=1, keepdims=True)

    @pl.when(s == n_steps - 1)
    def _():
        o_ref[0] = acc_ref[...]


def sb_decode_attention(q, k_rows, v_rows, page_table):
    db, n_pages = page_table.shape
    pps = SB_PAGES_PER_STEP
    assert n_pages % pps == 0
    n_steps = n_pages // pps

    def page_spec(u):
        return pl.BlockSpec((PAGE_ROWS, HEAD_DIM),
                            lambda b, s, pt: (pt[b * n_pages + (n_pages - 1 - (s * pps + u))], 0))

    specs = [page_spec(u) for u in range(pps)]
    return pl.pallas_call(
        functools.partial(_sb_decode_kernel, pps=pps, n_steps=n_steps),
        grid_spec=pltpu.PrefetchScalarGridSpec(
            num_scalar_prefetch=1,
            grid=(db, n_steps),
            in_specs=[pl.BlockSpec((1, N_HEADS, HEAD_DIM), lambda b, s, pt: (b, 0, 0))] + specs + specs,
            out_specs=pl.BlockSpec((1, N_HEADS, HEAD_DIM), lambda b, s, pt: (b, 0, 0)),
            scratch_shapes=[pltpu.VMEM((N_HEADS, HEAD_DIM), F32), pltpu.VMEM((N_HEADS, LANES), F32)]),
        out_shape=jax.ShapeDtypeStruct((db, N_HEADS, HEAD_DIM), F32),
        compiler_params=_params("parallel", "arbitrary"),
        name="sb_decode",
    )(page_table.reshape(-1), q, *([k_rows] * pps), *([v_rows] * pps))


def _top_k_mask(gate, n_valid, k):
    lane = lax.broadcasted_iota(jnp.int32, gate.shape, 1)
    sel = jnp.zeros(gate.shape, F32)
    picks = []
    for r in range(k):
        m = jnp.max(gate, axis=1, keepdims=True)
        idx = jnp.min(jnp.where(gate == m, lane, gate.shape[1] - 1), axis=1, keepdims=True)
        pick = lane == idx
        sel = jnp.maximum(sel, jnp.where(pick, jnp.where(r < n_valid, 1.0, 0.0), 0.0))
        gate = jnp.where(pick, -jnp.inf, gate)
        picks.append(idx)
    return sel, picks


def _moba_prompt_kernel(q_ref, k_ref, v_ref, o_ref, km_ref, *, tq, nb):
    i = pl.program_id(2)
    rows = Q_PER_KV * tq
    scale = HEAD_DIM ** -0.5

    @pl.when(i == 0)
    def _():
        km_ref[...] = jnp.zeros_like(km_ref)
        for jb in range(nb):
            km_ref[jb:jb + 1, :] = jnp.mean(k_ref[0, jb * tq:(jb + 1) * tq, :], axis=0, keepdims=True)

    q = jnp.concatenate([q_ref[0, :, g * HEAD_DIM:(g + 1) * HEAD_DIM] for g in range(Q_PER_KV)], axis=0)
    lane = lax.broadcasted_iota(jnp.int32, (rows, LANES), 1)
    gate = _dot_nt(q, km_ref[...].astype(BF16))
    gate = jnp.where(lane < i, gate, -jnp.inf)
    sel, _ = _top_k_mask(gate, i, MOBA_TOPK)
    sel = sel.astype(BF16)

    def scores(j):
        start = pl.multiple_of(j * tq, tq)
        kb = k_ref[0, pl.ds(start, tq), :].astype(BF16)
        vb = v_ref[0, pl.ds(start, tq), :].astype(BF16)
        return _dot_nt(q, kb) * scale, vb

    s, vb = scores(i)
    t_loc = lax.broadcasted_iota(jnp.int32, (rows, tq), 0) & (tq - 1)
    s_loc = lax.broadcasted_iota(jnp.int32, (rows, tq), 1)
    s = jnp.where(s_loc <= t_loc, s, NEG_BIG)
    m = jnp.max(s, axis=1, keepdims=True)
    p = jnp.exp(s - m)
    l = jnp.sum(p, axis=1, keepdims=True)
    acc = _dot(p.astype(BF16), vb)

    def body(j, carry):
        m, l, acc = carry
        s, vb = scores(j)
        onehot = (lax.broadcasted_iota(jnp.int32, (LANES, tq), 0) == j).astype(BF16)
        chosen = _dot(sel, onehot) > 0.5
        s = jnp.where(chosen, s, NEG_BIG)
        m_new = jnp.maximum(m, jnp.max(s, axis=1, keepdims=True))
        alpha = jnp.exp(m - m_new)
        p = jnp.exp(s - m_new)
        l = alpha * l + jnp.sum(p, axis=1, keepdims=True)
        acc = alpha * acc + _dot(p.astype(BF16), vb)
        return m_new, l, acc

    m, l, acc = lax.fori_loop(0, i, body, (m, l, acc))
    out = acc / l
    for g in range(Q_PER_KV):
        o_ref[0, :, g * HEAD_DIM:(g + 1) * HEAD_DIM] = out[g * tq:(g + 1) * tq].astype(o_ref.dtype)


def moba_prompt_attention(q, k, v):
    bsz, L, _ = q.shape
    tq = MOBA_BLOCK
    nb = L // tq
    assert L % tq == 0 and nb <= LANES
    gw = Q_PER_KV * HEAD_DIM
    return pl.pallas_call(
        functools.partial(_moba_prompt_kernel, tq=tq, nb=nb),
        grid=(bsz, N_KV_HEADS, nb),
        in_specs=[pl.BlockSpec((1, tq, gw), lambda b, h, i: (b, i, h)),
                  pl.BlockSpec((1, L, HEAD_DIM), lambda b, h, i: (b, 0, h)),
                  pl.BlockSpec((1, L, HEAD_DIM), lambda b, h, i: (b, 0, h))],
        out_specs=pl.BlockSpec((1, tq, gw), lambda b, h, i: (b, i, h)),
        out_shape=jax.ShapeDtypeStruct(q.shape, BF16),
        scratch_shapes=[pltpu.VMEM((LANES, HEAD_DIM), F32)],
        compiler_params=_params("parallel", "arbitrary", "arbitrary"),
        name="moba_prompt",
    )(q, k, v)


def _moba_kmean_kernel(pt_ref, *refs, bps, ppb):
    k_refs, o_ref = refs[:bps * ppb], refs[bps * ppb]
    sub = 8
    for j in range(bps):
        tot = jnp.zeros((sub, HEAD_DIM), F32)
        for h in range(ppb):
            k_ref = k_refs[j * ppb + h]
            for c in range(PAGE_ROWS // sub):
                tot = tot + k_ref[c * sub:(c + 1) * sub, :]
        mean = (tot[:N_KV_HEADS] + tot[N_KV_HEADS:]) * (1.0 / MOBA_BLOCK)
        o_ref[0, j * N_KV_HEADS:(j + 1) * N_KV_HEADS, :] = mean


def moba_cache_block_means(k_rows, page_table):
    db, n_pages = page_table.shape
    ppb = MOBA_BLOCK // PAGE_SIZE
    nblk = n_pages // ppb
    bps = MOBA_BLOCKS_PER_STEP
    assert nblk % bps == 0

    def page_spec(u):
        return pl.BlockSpec((PAGE_ROWS, HEAD_DIM), lambda b, j, pt: (pt[b * n_pages + j * (bps * ppb) + u], 0))

    return pl.pallas_call(
        functools.partial(_moba_kmean_kernel, bps=bps, ppb=ppb),
        grid_spec=pltpu.PrefetchScalarGridSpec(
            num_scalar_prefetch=1,
            grid=(db, nblk // bps),
            in_specs=[page_spec(u) for u in range(bps * ppb)],
            out_specs=pl.BlockSpec((1, bps * N_KV_HEADS, HEAD_DIM), lambda b, j, pt: (b, j, 0))),
        out_shape=jax.ShapeDtypeStruct((db, nblk * N_KV_HEADS, HEAD_DIM), F32),
        compiler_params=_params("parallel", "arbitrary"),
        name="moba_kmean",
    )(page_table.reshape(-1), *([k_rows] * (bps * ppb)))


def _moba_gate_kernel(q_ref, km_ref, o_ref, *, nblk):
    gate = _dot_nt(q_ref[0].astype(BF16), km_ref[0].astype(BF16))
    kvh = lax.broadcasted_iota(jnp.int32, gate.shape, 0) >> 2
    gate = jnp.where(_own_lane_mask(kvh, gate.shape), gate, -jnp.inf)
    _, picks = _top_k_mask(gate, nblk, MOBA_TOPK)
    lane = lax.broadcasted_iota(jnp.int32, (N_HEADS, LANES), 1)
    out = jnp.zeros((N_HEADS, LANES), jnp.int32)
    for r, idx in enumerate(picks):
        out = jnp.where(lane == r, idx >> 2, out)
    o_ref[0] = out


def moba_decode_select(q, kmean):
    db, rows, _ = kmean.shape
    nblk = rows // N_KV_HEADS
    out = pl.pallas_call(
        functools.partial(_moba_gate_kernel, nblk=nblk),
        grid=(db,),
        in_specs=[pl.BlockSpec((1, N_HEADS, HEAD_DIM), lambda b: (b, 0, 0)),
                  pl.BlockSpec((1, rows, HEAD_DIM), lambda b: (b, 0, 0))],
        out_specs=pl.BlockSpec((1, N_HEADS, LANES), lambda b: (b, 0, 0)),
        out_shape=jax.ShapeDtypeStruct((db, N_HEADS, LANES), jnp.int32),
        compiler_params=_params("parallel"),
        name="moba_gate",
    )(q, kmean)
    return out[:, :, :MOBA_TOPK]


def _moba_decode_kernel(sel_ref, pt_ref, q_ref, kn_ref, vn_ref, *refs, n_sel):
    k_refs, v_refs, o_ref = refs[:n_sel], refs[n_sel:2 * n_sel], refs[2 * n_sel]
    scale = HEAD_DIM ** -0.5
    q = jnp.broadcast_to(q_ref[0, 0], (8, HEAD_DIM))
    own = _own_lane_mask(pl.program_id(1) >> 2, (8, PAGE_ROWS))
    m = jnp.sum(q * kn_ref[0, 0], axis=1, keepdims=True) * scale
    l = jnp.ones((8, 1), F32)
    acc = jnp.broadcast_to(vn_ref[0, 0], (8, HEAD_DIM))
    qb = q.astype(BF16)
    for u in range(n_sel):
        s = jnp.where(own, _dot_nt(qb, k_refs[u][...].astype(BF16)) * scale, NEG_BIG)
        m_new = jnp.maximum(m, jnp.max(s, axis=1, keepdims=True))
        alpha = jnp.exp(m - m_new)
        p = jnp.exp(s - m_new)
        l = alpha * l + jnp.sum(p, axis=1, keepdims=True)
        acc = alpha * acc + _dot(p.astype(BF16), v_refs[u][...].astype(BF16))
        m = m_new
    o_ref[0, 0] = acc / l


def moba_decode_attention(q, k_new, v_new, sel, k_rows, v_rows, page_table):
    db, n_pages = page_table.shape
    ppb = MOBA_BLOCK // PAGE_SIZE
    n_sel = MOBA_TOPK * ppb

    def page_spec(u):
        def index(b, h, sel_r, pt):
            blk = sel_r[(b * N_HEADS + h) * MOBA_TOPK + u // ppb]
            return (pt[b * n_pages + blk * ppb + u % ppb], 0)
        return pl.BlockSpec((PAGE_ROWS, HEAD_DIM), index)

    specs = [page_spec(u) for u in range(n_sel)]
    new_spec = pl.BlockSpec((1, 1, 1, HEAD_DIM), lambda b, h, sel_r, pt: (b, h // Q_PER_KV, 0, 0))
    out = pl.pallas_call(
        functools.partial(_moba_decode_kernel, n_sel=n_sel),
        grid_spec=pltpu.PrefetchScalarGridSpec(
            num_scalar_prefetch=2,
            grid=(db, N_HEADS),
            in_specs=[pl.BlockSpec((1, 1, 1, HEAD_DIM), lambda b, h, sel_r, pt: (b, h, 0, 0)),
                      new_spec, new_spec] + specs + specs,
            out_specs=pl.BlockSpec((1, 1, 8, HEAD_DIM), lambda b, h, sel_r, pt: (b, h, 0, 0))),
        out_shape=jax.ShapeDtypeStruct((db, N_HEADS, 8, HEAD_DIM), F32),
        compiler_params=_params("parallel", "parallel"),
        name="moba_decode",
    )(sel.reshape(-1), page_table.reshape(-1), q, k_new, v_new, *([k_rows] * n_sel), *([v_rows] * n_sel))
    return out[:, :, 0, :]


def _dwconv_kernel(x_ref, hist_ref, w_ref, b_ref, g_ref, beta_ref, o_ref, pad_ref, *shift_refs, width, tl, hp, post):
    t = pl.program_id(2)

    @pl.when(t == 0)
    def _():
        pad_ref[0:hp, :] = hist_ref[0]

    @pl.when(t > 0)
    def _():
        pad_ref[0:hp, :] = pad_ref[tl:tl + hp, :]

    pad_ref[hp:hp + tl, :] = x_ref[0]

    def finish(acc):
        if post == "ln_silu":
            acc = _layer_norm_rows(acc, g_ref[...], beta_ref[...])
        return (acc * jax.nn.sigmoid(acc)).astype(o_ref.dtype)

    if not shift_refs:
        base = hp - (width - 1)
        acc = jnp.zeros(o_ref.shape[1:], F32) + b_ref[...]
        for k in range(width):
            acc = acc + pad_ref[base + k:base + k + tl, :] * w_ref[k:k + 1, :]
        o_ref[0] = finish(acc)
        return

    sh_ref, wb_ref = shift_refs
    halo = 8 * ((width - 1) // 8)
    for r in range(1, 8):
        sh_ref[r - 1] = pad_ref[hp - halo - r:hp + tl - r, :]
    rt = min(DWCONV_ROW_TILE, tl)
    tc = o_ref.shape[2]
    lc = DWCONV_LANE_CHUNK if tc % DWCONV_LANE_CHUNK == 0 else tc
    for k in range(width):
        wb_ref[k] = jnp.broadcast_to(w_ref[k:k + 1, :], (8, tc))

    def row_tile(row0):
        parts = []
        for c0 in range(0, tc, lc):
            acc = jnp.zeros((rt // 8, 8, lc), F32) + b_ref[:, c0:c0 + lc]
            for r in range(8):
                for a in range((width - 1 - r) // 8 + 1):
                    k = width - 1 - (8 * a + r)
                    if r == 0:
                        src = pad_ref[pl.ds(row0 + (hp - 8 * a), rt), c0:c0 + lc]
                    else:
                        src = sh_ref[r - 1, pl.ds(row0 + (halo - 8 * a), rt), c0:c0 + lc]
                    acc = acc + src.reshape(rt // 8, 8, lc) * wb_ref[k, :, c0:c0 + lc]
            parts.append(acc.reshape(rt, lc))
        o_ref[0, pl.ds(row0, rt), :] = finish(jnp.concatenate(parts, axis=1))

    if tl == rt:
        row_tile(0)
    else:
        def body(it, carry):
            row_tile(pl.multiple_of(it * rt, rt))
            return carry

        lax.fori_loop(0, tl // rt, body, 0)


def causal_dwconv(x, hist, w, b, post, ln_g=None, ln_b=None, out_dtype=F32):
    bsz, L, C = x.shape
    width = w.shape[0]
    hp = hist.shape[1]
    assert hp % 8 == 0 and hp >= 8 * ((width - 1) // 8) + min(7, width - 1)
    tc = C if post == "ln_silu" else _pick_tile(C, (1024, 512, 256, 128))
    tl = _pick_tile(L, (256, 128)) if L >= hp else L
    assert tl >= hp or tl == L
    wp = jnp.zeros((-(-width // 8) * 8, C), F32).at[:width].set(w.astype(F32))
    if ln_g is None:
        ln_g = jnp.ones((C,), F32)
        ln_b = jnp.zeros((C,), F32)
    row = lambda v: v.reshape(1, C).astype(F32)
    vec_spec = pl.BlockSpec((1, tc), lambda bb, c, t: (0, c))
    return pl.pallas_call(
        functools.partial(_dwconv_kernel, width=width, tl=tl, hp=hp, post=post),
        grid=(bsz, C // tc, L // tl),
        in_specs=[pl.BlockSpec((1, tl, tc), lambda bb, c, t: (bb, t, c)),
                  pl.BlockSpec((1, hp, tc), lambda bb, c, t: (bb, 0, c)),
                  pl.BlockSpec((wp.shape[0], tc), lambda bb, c, t: (0, c)),
                  vec_spec, vec_spec, vec_spec],
        out_specs=pl.BlockSpec((1, tl, tc), lambda bb, c, t: (bb, t, c)),
        out_shape=jax.ShapeDtypeStruct((bsz, L, C), out_dtype),
        scratch_shapes=[pltpu.VMEM((hp + tl, tc), F32)] + (
            [pltpu.VMEM((7, 8 * ((width - 1) // 8) + tl, tc), F32),
             pltpu.VMEM((width, 8, tc), F32)] if width > 8 else []),
        compiler_params=_params("parallel", "parallel", "arbitrary"),
        name="dwconv_" + post,
    )(x, hist, wp, row(b), row(ln_g), row(ln_b))


def _ssd_prompt_kernel(x_ref, bm_ref, cm_ref, dt_ref, z_ref, dtb_ref, alog_ref, d_ref, ng_ref, h0_ref,
                       yn_ref, h_ref, y_buf, *, chunk, hpg):
    c = pl.program_id(1)
    P, N = SSM_HEAD_DIM, SSM_D_STATE
    gw = hpg * P

    @pl.when(c == 0)
    def _():
        h_ref[...] = h0_ref[...]

    li = lax.broadcasted_iota(jnp.int32, (chunk, chunk), 0)
    si = lax.broadcasted_iota(jnp.int32, (chunk, chunk), 1)
    causal = li >= si
    lower = causal.astype(BF16)
    for g in range(SSM_GROUPS):
        lanes = slice(g * LANES, (g + 1) * LANES)
        x = x_ref[0, :, g * gw:(g + 1) * gw]
        dt = _softplus(dt_ref[0, :, lanes] + dtb_ref[:, lanes])
        a = -jnp.exp(alog_ref[:, lanes])
        hi, mid, lo = _split3(dt * a)
        cum = _dot(lower, hi) + _dot(lower, mid) + _dot(lower, lo)
        cum_t = cum.T
        dt_t = dt.T
        cmb = cm_ref[0, :, g * N:(g + 1) * N].astype(BF16)
        bmb = bm_ref[0, :, g * N:(g + 1) * N].astype(BF16)
        cb = _dot_nt(cmb, bmb)
        x_t = x.T
        for r in range(hpg):
            head = g * hpg + r
            cum_col = cum[:, r:r + 1]
            cum_row = cum_t[r:r + 1, :]
            dt_row = dt_t[r:r + 1, :]
            total = cum_t[r:r + 1, chunk - 1:chunk]
            xh = x[:, r * P:(r + 1) * P]
            decay = jnp.exp(jnp.where(causal, cum_col - cum_row, -jnp.inf))
            mix = (cb * decay * dt_row).astype(BF16)
            y = _dot(mix, xh.astype(BF16))
            h_old = h_ref[0, head]
            y = y + _dot_nt(cmb, h_old.astype(BF16)) * jnp.exp(cum_col)
            y_buf[:, r * P:(r + 1) * P] = y + d_ref[:, g * LANES + r:g * LANES + r + 1] * xh
            to_end = jnp.exp(total - cum_row) * dt_row
            s_chunk = _dot((x_t[r * P:(r + 1) * P, :] * to_end).astype(BF16), bmb)
            h_ref[0, head] = jnp.exp(total) * h_old + s_chunk
        z = z_ref[0, :, g * gw:(g + 1) * gw]
        hg = y_buf[...] * (z * jax.nn.sigmoid(z))
        hg = hg * lax.rsqrt(jnp.mean(hg * hg, axis=-1, keepdims=True) + LN_EPS)
        yn_ref[0, :, g * gw:(g + 1) * gw] = (hg * ng_ref[:, g * gw:(g + 1) * gw]).astype(yn_ref.dtype)


def ssd_prompt(xbc, dt_raw, z, dt_bias_g, a_log_g, d_g, norm_g, h0):
    bsz, L, _ = xbc.shape
    n_heads = h0.shape[1]
    G, N, P = SSM_GROUPS, SSM_D_STATE, SSM_HEAD_DIM
    hpg = n_heads // G
    d_inner = n_heads * P
    gn = G * N
    chunk = SSM_CHUNK
    assert d_inner % gn == 0 and (hpg * P) % LANES == 0
    b_off = d_inner // gn
    vec_spec = pl.BlockSpec((1, G * LANES), lambda b, c: (0, 0))
    state_spec = pl.BlockSpec((1, n_heads, P, N), lambda b, c: (b, 0, 0, 0))
    return pl.pallas_call(
        functools.partial(_ssd_prompt_kernel, chunk=chunk, hpg=hpg),
        grid=(bsz, L // chunk),
        in_specs=[pl.BlockSpec((1, chunk, d_inner), lambda b, c: (b, c, 0)),
                  pl.BlockSpec((1, chunk, gn), lambda b, c: (b, c, b_off)),
                  pl.BlockSpec((1, chunk, gn), lambda b, c: (b, c, b_off + 1)),
                  pl.BlockSpec((1, chunk, G * LANES), lambda b, c: (b, c, 0)),
                  pl.BlockSpec((1, chunk, d_inner), lambda b, c: (b, c, 0)),
                  vec_spec, vec_spec, vec_spec,
                  pl.BlockSpec((1, d_inner), lambda b, c: (0, 0)),
                  state_spec],
        out_specs=[pl.BlockSpec((1, chunk, d_inner), lambda b, c: (b, c, 0)), state_spec],
        out_shape=[jax.ShapeDtypeStruct((bsz, L, d_inner), BF16),
                   jax.ShapeDtypeStruct((bsz, n_heads, P, N), F32)],
        scratch_shapes=[pltpu.VMEM((chunk, hpg * P), F32)],
        compiler_params=_params("parallel", "arbitrary"),
        name="ssd_prompt",
    )(xbc, xbc, xbc, dt_raw, z, dt_bias_g, a_log_g, d_g, norm_g.reshape(1, d_inner).astype(F32), h0)


def _ssd_step_kernel(xc_ref, bm_ref, cm_ref, dt_ref, dtb_ref, alog_ref, d_ref, h0_ref, y_ref, h_ref, *, hpg):
    dt = _softplus(dt_ref[0] + dtb_ref[...])
    da = jnp.exp(dt * (-jnp.exp(alog_ref[...])))
    bm = bm_ref[0]
    cm = cm_ref[0]
    for r in range(hpg):
        xcol = xc_ref[0, r]
        h_new = da[:, r:r + 1] * h0_ref[0, r] + xcol * (dt[:, r:r + 1] * bm)
        h_ref[0, r] = h_new
        y_ref[0, r] = jnp.sum(h_new * cm, axis=1, keepdims=True) + d_ref[:, r:r + 1] * xcol


def ssd_step(x_col, bc, dt_raw, dt_bias_g, a_log_g, d_g, h0):
    db, n_heads, P, _ = x_col.shape
    G, N = SSM_GROUPS, SSM_D_STATE
    hpg = n_heads // G
    vec_spec = pl.BlockSpec((1, LANES), lambda b, g: (0, g))
    return pl.pallas_call(
        functools.partial(_ssd_step_kernel, hpg=hpg),
        grid=(db, G),
        in_specs=[pl.BlockSpec((1, hpg, P, 1), lambda b, g: (b, g, 0, 0)),
                  pl.BlockSpec((1, 1, N), lambda b, g: (b, 0, g)),
                  pl.BlockSpec((1, 1, N), lambda b, g: (b, 0, G + g)),
                  pl.BlockSpec((1, 1, LANES), lambda b, g: (b, 0, g)),
                  vec_spec, vec_spec, vec_spec,
                  pl.BlockSpec((1, hpg, P, N), lambda b, g: (b, g, 0, 0))],
        out_specs=[pl.BlockSpec((1, hpg, P, 1), lambda b, g: (b, g, 0, 0)),
                   pl.BlockSpec((1, hpg, P, N), lambda b, g: (b, g, 0, 0))],
        out_shape=[jax.ShapeDtypeStruct((db, n_heads, P, 1), F32),
                   jax.ShapeDtypeStruct((db, n_heads, P, N), F32)],
        compiler_params=_params("parallel", "parallel"),
        name="ssd_step",
    )(x_col, bc, bc, dt_raw, dt_bias_g, a_log_g, d_g, h0)


def _gated_norm_kernel(y_ref, z_ref, g_ref, o_ref):
    z = z_ref[...]
    h = y_ref[...] * (z * jax.nn.sigmoid(z))
    h = h * lax.rsqrt(jnp.mean(h * h, axis=-1, keepdims=True) + LN_EPS)
    o_ref[...] = (h * g_ref[...]).astype(o_ref.dtype)


def gated_rms_norm(y, z, g):
    m, d = y.shape
    gw = d // SSM_GROUPS
    tm = _pick_tile(m, (512, 256, 128, 8))
    spec = pl.BlockSpec((tm, gw), lambda i, j: (i, j))
    return pl.pallas_call(
        _gated_norm_kernel,
        grid=(m // tm, SSM_GROUPS),
        in_specs=[spec, spec, pl.BlockSpec((1, gw), lambda i, j: (0, j))],
        out_specs=spec,
        out_shape=jax.ShapeDtypeStruct((m, d), BF16),
        compiler_params=_params("parallel", "parallel"),
        name="gated_rms_norm",
    )(y, z, g.reshape(1, d).astype(F32))


def _group_pad(v, n_heads):
    hpg = n_heads // SSM_GROUPS
    out = jnp.zeros((SSM_GROUPS, LANES), F32).at[:, :hpg].set(v.astype(F32).reshape(SSM_GROUPS, hpg))
    return out.reshape(1, SSM_GROUPS * LANES)


def _ffn_half(xf, xb, w1, w3, w2, g, b):
    h = gated_matmul(xb, w1, w3, None, None, "swiglu", BF16)
    return matmul_postnorm(h, w2, None, xf, g, b, 0.5)


def _pad_rows(x, rows):
    return jnp.concatenate([x, jnp.zeros((x.shape[0], rows - x.shape[1]) + x.shape[2:], x.dtype)], axis=1)


def _front_pad_rows(x, rows):
    return jnp.concatenate([jnp.zeros((x.shape[0], rows - x.shape[1]) + x.shape[2:], x.dtype), x], axis=1)


def kernel(x_prompt, x_sample, p_prompt, p_sample, cache_sb_k, cache_sb_v, cache_moba_k, cache_moba_v, state_ssm, state_ssm_conv, state_conf_conv, page_table, ln_g, ln_b, ffn_w1, ffn_w3, ffn_w2, ple_w_proj, ple_w_gate, sb_w_qkv, sb_w_o, ssm_w_in, ssm_conv_w, ssm_conv_b, ssm_dt_bias, ssm_a_log, ssm_d, ssm_norm_g, ssm_w_out, conf_w_pw1, conf_b_pw1, conf_w_dw, conf_b_dw, conf_ln_g, conf_ln_b, conf_w_pw2, conf_b_pw2, moba_w_qkv, moba_w_o):
    bsz, seq, d_model = x_prompt.shape
    db = x_sample.shape[0]
    assert x_sample.shape[1] == 1
    mp = bsz * seq
    depth = ffn_w1.shape[0]
    bf = lambda w: w.astype(BF16)

    xp_f = x_prompt.reshape(mp, d_model)
    xs_f = x_sample.reshape(db, d_model)
    xp_b, xs_b = bf(xp_f), bf(xs_f)
    n_phys = cache_sb_k.shape[1]
    outs = {k: [] for k in ("sb_kp", "sb_vp", "sb_ks", "sb_vs", "ssm_hp", "ssm_hs", "ssm_cp", "ssm_cs",
                            "conf_cp", "conf_cs", "mo_kp", "mo_vp", "mo_ks", "mo_vs")}

    for i in range(depth):
        m, j = i % 4, i // 4
        w1, w3, w2 = bf(ffn_w1[i, 0]), bf(ffn_w3[i, 0]), bf(ffn_w2[i, 0])
        xp_f, xp_b = _ffn_half(xp_f, xp_b, w1, w3, w2, ln_g[i, 0], ln_b[i, 0])
        xs_f, xs_b = _ffn_half(xs_f, xs_b, w1, w3, w2, ln_g[i, 0], ln_b[i, 0])
        g1, b1 = ln_g[i, 1], ln_b[i, 1]

        if m == 0 or m == 3:
            w_qkv = bf(sb_w_qkv[j] if m == 0 else moba_w_qkv[j])
            w_o = bf(sb_w_o[j] if m == 0 else moba_w_o[j])
            wq, wk, wv = w_qkv[:, :Q_DIM], w_qkv[:, Q_DIM:Q_DIM + KV_DIM], w_qkv[:, Q_DIM + KV_DIM:]
            q = matmul(xp_b, wq, None, BF16).reshape(bsz, seq, Q_DIM)
            k = matmul(xp_b, wk, None, F32).reshape(bsz, seq, KV_DIM)
            v = matmul(xp_b, wv, None, F32).reshape(bsz, seq, KV_DIM)
            attend = sb_prompt_attention if m == 0 else moba_prompt_attention
            o = attend(q, k, v).reshape(mp, Q_DIM)
            xp_f, xp_b = matmul_postnorm(o, w_o, None, xp_f, g1, b1, 1.0)
            kp = k.reshape(bsz, seq, N_KV_HEADS, HEAD_DIM)
            vp = v.reshape(bsz, seq, N_KV_HEADS, HEAD_DIM)
            qs = matmul(xs_b, wq, None, F32)
            ks = matmul(xs_b, wk, None, F32)
            vs = matmul(xs_b, wv, None, F32)
            if m == 0:
                kc = cache_sb_k[j].reshape(n_phys * PAGE_ROWS, HEAD_DIM)
                vc = cache_sb_v[j].reshape(n_phys * PAGE_ROWS, HEAD_DIM)
                os_ = sb_decode_attention(qs.reshape(db, N_HEADS, HEAD_DIM), kc, vc, page_table)
            else:
                kc = cache_moba_k[j].reshape(n_phys * PAGE_ROWS, HEAD_DIM)
                vc = cache_moba_v[j].reshape(n_phys * PAGE_ROWS, HEAD_DIM)
                qh = qs.reshape(db, N_HEADS, HEAD_DIM)
                kmean = moba_cache_block_means(kc, page_table)
                sel = moba_decode_select(qh, kmean)
                os_ = moba_decode_attention(qh.reshape(db, N_HEADS, 1, HEAD_DIM),
                                            ks.reshape(db, N_KV_HEADS, 1, HEAD_DIM),
                                            vs.reshape(db, N_KV_HEADS, 1, HEAD_DIM),
                                            sel, kc, vc, page_table)
            xs_f, xs_b = matmul_postnorm(bf(os_.reshape(db, Q_DIM)), w_o, None, xs_f, g1, b1, 1.0)
            ks4 = ks.reshape(db, 1, N_KV_HEADS, HEAD_DIM)
            vs4 = vs.reshape(db, 1, N_KV_HEADS, HEAD_DIM)
            if m == 0:
                outs["sb_kp"].append(kp); outs["sb_vp"].append(vp); outs["sb_ks"].append(ks4); outs["sb_vs"].append(vs4)
            else:
                outs["mo_kp"].append(kp); outs["mo_vp"].append(vp); outs["mo_ks"].append(ks4); outs["mo_vs"].append(vs4)

        elif m == 1:
            n_heads = ssm_dt_bias.shape[1]
            d_inner = n_heads * SSM_HEAD_DIM
            conv_dim = ssm_conv_w.shape[2]
            hpg = n_heads // SSM_GROUPS
            w_in = ssm_w_in[j]
            w_z, w_xbc = bf(w_in[:, :d_inner]), bf(w_in[:, d_inner:d_inner + conv_dim])
            w_dt = w_in[:, d_inner + conv_dim:].reshape(d_model, SSM_GROUPS, hpg)
            w_dt = bf(jnp.zeros((d_model, SSM_GROUPS, LANES), F32).at[:, :, :hpg].set(w_dt)
                      .reshape(d_model, SSM_GROUPS * LANES))
            dtb_g = _group_pad(ssm_dt_bias[j], n_heads)
            alog_g = _group_pad(ssm_a_log[j], n_heads)
            d_g = _group_pad(ssm_d[j], n_heads)
            w_out = bf(ssm_w_out[j])
            hp = 8
            z = matmul(xp_b, w_z, None, F32)
            xbc_raw = matmul(xp_b, w_xbc, None, F32).reshape(bsz, seq, conv_dim)
            dt_raw = matmul(xp_b, w_dt, None, F32).reshape(bsz, seq, SSM_GROUPS * LANES)
            xbc = causal_dwconv(xbc_raw, jnp.zeros((bsz, hp, conv_dim), F32), ssm_conv_w[j], ssm_conv_b[j], "silu")
            h0 = jnp.zeros((bsz, n_heads, SSM_HEAD_DIM, SSM_D_STATE), F32)
            yn, h_last = ssd_prompt(xbc, dt_raw, z.reshape(bsz, seq, d_inner), dtb_g, alog_g, d_g, ssm_norm_g[j], h0)
            xp_f, xp_b = matmul_postnorm(yn.reshape(mp, d_inner), w_out, None, xp_f, g1, b1, 1.0)
            outs["ssm_hp"].append(h_last)
            outs["ssm_cp"].append(xbc_raw[:, seq - (SSM_CONV - 1):, :])
            zs = matmul(xs_b, w_z, None, F32)
            xbc_s = matmul(xs_b, w_xbc, None, F32).reshape(db, 1, conv_dim)
            dt_s = matmul(xs_b, w_dt, None, F32).reshape(db, 1, SSM_GROUPS * LANES)
            hist = _front_pad_rows(state_ssm_conv[j], hp)
            xbc_sa = causal_dwconv(_pad_rows(xbc_s, 8), hist, ssm_conv_w[j], ssm_conv_b[j], "silu")[:, :1, :]
            x_col = xbc_sa[:, 0, :d_inner].reshape(db, n_heads, SSM_HEAD_DIM, 1)
            y_col, hs = ssd_step(x_col, xbc_sa[:, :, d_inner:], dt_s, dtb_g, alog_g, d_g, state_ssm[j])
            yns = gated_rms_norm(y_col.reshape(db, d_inner), zs, ssm_norm_g[j])
            xs_f, xs_b = matmul_postnorm(yns, w_out, None, xs_f, g1, b1, 1.0)
            outs["ssm_hs"].append(hs)
            outs["ssm_cs"].append(jnp.concatenate([state_ssm_conv[j], xbc_s], axis=1)[:, 1:, :])

        else:
            w_pw1 = conf_w_pw1[j]
            wa, wg = bf(w_pw1[:, :d_model]), bf(w_pw1[:, d_model:])
            ba, bg = conf_b_pw1[j][:d_model], conf_b_pw1[j][d_model:]
            w_pw2 = bf(conf_w_pw2[j])
            hp = 32
            u = gated_matmul(xp_b, wa, wg, ba, bg, "glu", F32).reshape(bsz, seq, d_model)
            uc = causal_dwconv(u, jnp.zeros((bsz, hp, d_model), F32), conf_w_dw[j], conf_b_dw[j], "ln_silu",
                               conf_ln_g[j], conf_ln_b[j], BF16)
            xp_f, xp_b = matmul_postnorm(uc.reshape(mp, d_model), w_pw2, conf_b_pw2[j], xp_f, g1, b1, 1.0)
            outs["conf_cp"].append(u[:, seq - (CONF_WIDTH - 1):, :])
            us = gated_matmul(xs_b, wa, wg, ba, bg, "glu", F32).reshape(db, 1, d_model)
            hist = _front_pad_rows(state_conf_conv[j], hp)
            ucs = causal_dwconv(_pad_rows(us, 8), hist, conf_w_dw[j], conf_b_dw[j], "ln_silu",
                                conf_ln_g[j], conf_ln_b[j], BF16)[:, 0, :]
            xs_f, xs_b = matmul_postnorm(ucs, w_pw2, conf_b_pw2[j], xs_f, g1, b1, 1.0)
            outs["conf_cs"].append(jnp.concatenate([state_conf_conv[j], us], axis=1)[:, 1:, :])

        w1, w3, w2 = bf(ffn_w1[i, 1]), bf(ffn_w3[i, 1]), bf(ffn_w2[i, 1])
        xp_f, xp_b = _ffn_half(xp_f, xp_b, w1, w3, w2, ln_g[i, 2], ln_b[i, 2])
        xs_f, xs_b = _ffn_half(xs_f, xs_b, w1, w3, w2, ln_g[i, 2], ln_b[i, 2])
        wg_, wp_ = bf(ple_w_gate[i]), bf(ple_w_proj[i])
        xp_f, xp_b = ple_add(xp_b, xp_f, p_prompt[i].reshape(mp, -1), wg_, wp_, ln_g[i, 3], ln_b[i, 3])
        xs_f, xs_b = ple_add(xs_b, xs_f, p_sample[i].reshape(db, -1), wg_, wp_, ln_g[i, 3], ln_b[i, 3])

    st = lambda name: jnp.stack(outs[name])
    return (xp_f.reshape(bsz, seq, d_model), xs_f.reshape(db, 1, d_model),
            st("sb_kp"), st("sb_vp"), st("sb_ks"), st("sb_vs"),
            st("ssm_hp"), st("ssm_hs"), st("ssm_cp"), st("ssm_cs"),
            st("conf_cp"), st("conf_cs"),
            st("mo_kp"), st("mo_vp"), st("mo_ks"), st("mo_vs"))
```

```python
import functools

import jax
import jax.numpy as jnp
from jax import lax
from jax.experimental import pallas as pl
from jax.experimental.pallas import tpu as pltpu

F32 = jnp.float32
BF16 = jnp.bfloat16

N_HEADS = 16
HEAD_DIM = 128
N_KV_HEADS = 4
Q_PER_KV = N_HEADS // N_KV_HEADS
Q_DIM = N_HEADS * HEAD_DIM
KV_DIM = N_KV_HEADS * HEAD_DIM
SB_BLOCK = 128
MOBA_BLOCK = 256
MOBA_TOPK = 3
PAGE_SIZE = 128
SSM_HEAD_DIM = 64
SSM_GROUPS = 8
SSM_D_STATE = 128
SSM_CONV = 4
SSM_CHUNK = 128
CONF_WIDTH = 31
LN_EPS = 1e-5
DEPTH = 4
DN_ALPHA = (2 * DEPTH) ** 0.25
LANES = 128
NEG_BIG = -1e30
DWCONV_ROW_TILE = 64
DWCONV_LANE_CHUNK = 256
SB_LOG_FLOOR = -104.0
VMEM_LIMIT = 56 * 1024 * 1024


def _params(*sem):
    return pltpu.CompilerParams(dimension_semantics=sem, vmem_limit_bytes=VMEM_LIMIT)


def _layer_norm_rows(y, g, b):
    mu = jnp.mean(y, axis=-1, keepdims=True)
    d = y - mu
    var = jnp.mean(d * d, axis=-1, keepdims=True)
    return d * lax.rsqrt(var + LN_EPS) * g + b


def _split3(x):
    hi = x.astype(BF16)
    r1 = x - hi.astype(F32)
    mid = r1.astype(BF16)
    lo = (r1 - mid.astype(F32)).astype(BF16)
    return hi, mid, lo


def _dot(a, b):
    return jnp.dot(a, b, preferred_element_type=F32)


def _dot_nt(a, b):
    return lax.dot_general(a, b, (((1,), (1,)), ((), ())), preferred_element_type=F32)


def _dot01(x, m01):
    hi, mid, lo = _split3(x)
    return _dot(hi, m01) + _dot(mid, m01) + _dot(lo, m01)


def _pick_tile(n, pref):
    for t in pref:
        if n % t == 0:
            return t
    return n


def _mm_kernel(x_ref, w_ref, b_ref, o_ref):
    o_ref[...] = (_dot(x_ref[...], w_ref[...]) + b_ref[...]).astype(o_ref.dtype)


def matmul(x, w, bias=None, out_dtype=F32, col0=0, n=None):
    m, k = x.shape
    n = w.shape[1] - col0 if n is None else n
    tm = _pick_tile(m, (1024, 512, 256, 128, 8))
    tn = _pick_tile(n, (512, 256, 128))
    assert col0 % tn == 0 and n % tn == 0
    j0 = col0 // tn
    if bias is None:
        bias = jnp.zeros((n,), F32)
    return pl.pallas_call(
        _mm_kernel,
        grid=(m // tm, n // tn),
        in_specs=[pl.BlockSpec((tm, k), lambda i, j: (i, 0)),
                  pl.BlockSpec((k, tn), lambda i, j: (0, j + j0)),
                  pl.BlockSpec((1, tn), lambda i, j: (0, j))],
        out_specs=pl.BlockSpec((tm, tn), lambda i, j: (i, j)),
        out_shape=jax.ShapeDtypeStruct((m, n), out_dtype),
        compiler_params=_params("parallel", "arbitrary"),
        name="mm",
    )(x, w, bias.reshape(1, n).astype(F32))


def _gated_mm_kernel(x_ref, xs_ref, wa_ref, wb_ref, ba_ref, bb_ref, o_ref, os_ref, *, mode):
    wa = wa_ref[...].astype(BF16)
    wb = wb_ref[...].astype(BF16)

    def gated(x):
        a = _dot(x, wa) + ba_ref[...]
        b = _dot(x, wb) + bb_ref[...]
        return a * jax.nn.sigmoid(a) * b if mode == "swiglu" else a * jax.nn.sigmoid(b)

    o_ref[...] = gated(x_ref[...]).astype(o_ref.dtype)

    @pl.when(pl.program_id(0) == 0)
    def _():
        os_ref[0] = gated(xs_ref[...]).astype(os_ref.dtype)

    @pl.when(pl.program_id(0) > 0)
    def _():
        os_ref[...] = jnp.zeros_like(os_ref)


def gated_matmul(x, xs, wa, wb, ba, bb, mode, out_dtype, lead=(), cola=0, colb=0, n=None):
    m, k = x.shape
    s = xs.shape[0]
    n = wa.shape[-1] - cola if n is None else n
    wblock = (None,) * len(lead) + (k, None)
    tm = _pick_tile(m, (1024, 512, 256, 128, 8))
    tn = _pick_tile(n, (512, 256, 128))
    assert cola % tn == 0 and colb % tn == 0 and n % tn == 0
    ja, jb = cola // tn, colb // tn
    if ba is None:
        ba = jnp.zeros((n,), F32)
        bb = jnp.zeros((n,), F32)
    out, out_s = pl.pallas_call(
        functools.partial(_gated_mm_kernel, mode=mode),
        grid=(m // tm, n // tn),
        in_specs=[pl.BlockSpec((tm, k), lambda i, j: (i, 0)),
                  pl.BlockSpec((s, k), lambda i, j: (0, 0)),
                  pl.BlockSpec(wblock[:-1] + (tn,), lambda i, j: lead + (0, j + ja)),
                  pl.BlockSpec(wblock[:-1] + (tn,), lambda i, j: lead + (0, j + jb)),
                  pl.BlockSpec((1, tn), lambda i, j: (0, j)),
                  pl.BlockSpec((1, tn), lambda i, j: (0, j))],
        out_specs=[pl.BlockSpec((tm, tn), lambda i, j: (i, j)),
                   pl.BlockSpec((1, s, tn), lambda i, j: (i, 0, j))],
        out_shape=[jax.ShapeDtypeStruct((m, n), out_dtype),
                   jax.ShapeDtypeStruct((m // tm, s, n), out_dtype)],
        compiler_params=_params("arbitrary", "arbitrary"),
        name="gated_mm",
    )(x, xs, wa, wb, ba.reshape(1, n).astype(F32), bb.reshape(1, n).astype(F32))
    return out, out_s[0]


def _mm_ln_kernel(a_ref, w_ref, bias_ref, res_ref, g_ref, b_ref, of_ref, ob_ref, *, scale, nk):
    k = pl.program_id(1)

    def finish(acc):
        y = DN_ALPHA * res_ref[...] + scale * (acc + bias_ref[...])
        out = _layer_norm_rows(y, g_ref[...], b_ref[...])
        of_ref[...] = out
        ob_ref[...] = out.astype(BF16)

    if nk == 1:
        finish(_dot(a_ref[...], w_ref[...]))
        return

    @pl.when(k == 0)
    def _():
        of_ref[...] = _dot(a_ref[...], w_ref[...])

    @pl.when(jnp.logical_and(k > 0, k < nk - 1))
    def _():
        of_ref[...] += _dot(a_ref[...], w_ref[...])

    @pl.when(k == nk - 1)
    def _():
        finish(of_ref[...] + _dot(a_ref[...], w_ref[...]))


def matmul_postnorm(a, w, bias, res, g, b, scale):
    m, kdim = a.shape
    n = w.shape[1]
    tm = _pick_tile(m, (512, 256, 128, 8))
    tk = _pick_tile(kdim, (2816, 2048, 1408, 1024, 512, 256, 128))
    nk = kdim // tk
    if bias is None:
        bias = jnp.zeros((n,), F32)
    row = lambda v: v.reshape(1, n).astype(F32)
    return pl.pallas_call(
        functools.partial(_mm_ln_kernel, scale=scale, nk=nk),
        grid=(m // tm, nk),
        in_specs=[pl.BlockSpec((tm, tk), lambda i, k: (i, k)),
                  pl.BlockSpec((tk, n), lambda i, k: (k, 0)),
                  pl.BlockSpec((1, n), lambda i, k: (0, 0)),
                  pl.BlockSpec((tm, n), lambda i, k: (i, 0)),
                  pl.BlockSpec((1, n), lambda i, k: (0, 0)),
                  pl.BlockSpec((1, n), lambda i, k: (0, 0))],
        out_specs=[pl.BlockSpec((tm, n), lambda i, k: (i, 0)),
                   pl.BlockSpec((tm, n), lambda i, k: (i, 0))],
        out_shape=[jax.ShapeDtypeStruct((m, n), F32), jax.ShapeDtypeStruct((m, n), BF16)],
        compiler_params=_params("parallel", "arbitrary"),
        name="mm_postnorm",
    )(a, w, row(bias), res, row(g), row(b))


def _ple_kernel(xb_ref, xf_ref, p_ref, wg_ref, wp_ref, g_ref, b_ref, of_ref, ob_ref):
    gate = jax.nn.sigmoid(_dot(xb_ref[...], wg_ref[...]))
    proj = _dot(p_ref[...].astype(BF16), wp_ref[...])
    y = DN_ALPHA * xf_ref[...] + gate * proj
    out = _layer_norm_rows(y, g_ref[...], b_ref[...])
    of_ref[...] = out
    ob_ref[...] = out.astype(BF16)


def ple_add(xb, xf, p, wg, wp, g, b):
    m, d = xf.shape
    pd = p.shape[1]
    tm = _pick_tile(m, (512, 256, 128, 8))
    row = lambda v: v.reshape(1, d).astype(F32)
    return pl.pallas_call(
        _ple_kernel,
        grid=(m // tm,),
        in_specs=[pl.BlockSpec((tm, d), lambda i: (i, 0)),
                  pl.BlockSpec((tm, d), lambda i: (i, 0)),
                  pl.BlockSpec((tm, pd), lambda i: (i, 0)),
                  pl.BlockSpec((d, d), lambda i: (0, 0)),
                  pl.BlockSpec((pd, d), lambda i: (0, 0)),
                  pl.BlockSpec((1, d), lambda i: (0, 0)),
                  pl.BlockSpec((1, d), lambda i: (0, 0))],
        out_specs=[pl.BlockSpec((tm, d), lambda i: (i, 0)),
                   pl.BlockSpec((tm, d), lambda i: (i, 0))],
        out_shape=[jax.ShapeDtypeStruct((m, d), F32), jax.ShapeDtypeStruct((m, d), BF16)],
        compiler_params=_params("parallel"),
        name="ple_add",
    )(xb, xf, p, wg, wp, row(g), row(b))


def _softplus(z):
    return jnp.maximum(z, 0.0) + jnp.log1p(jnp.exp(-jnp.abs(z)))


def _sb_prompt_kernel(q_ref, k_ref, v_ref, o_ref, *, tq):
    i = pl.program_id(2)
    rows = Q_PER_KV * tq
    scale = HEAD_DIM ** -0.5
    q = jnp.concatenate([q_ref[0, :, g * HEAD_DIM:(g + 1) * HEAD_DIM] for g in range(Q_PER_KV)], axis=0)
    t_loc = lax.broadcasted_iota(jnp.int32, (rows, tq), 0) & (tq - 1)
    s_loc = lax.broadcasted_iota(jnp.int32, (rows, tq), 1)
    later = (lax.broadcasted_iota(jnp.int32, (tq, tq), 0) > lax.broadcasted_iota(jnp.int32, (tq, tq), 1)).astype(BF16)

    def block(j, run, acc, diagonal):
        start = pl.multiple_of(j * tq, tq)
        kb = k_ref[0, pl.ds(start, tq), :].astype(BF16)
        vb = v_ref[0, pl.ds(start, tq), :].astype(BF16)
        z = _dot_nt(q, kb) * scale
        log_keep = -_softplus(z)
        if diagonal:
            valid = s_loc < t_loc
            log_keep = jnp.where(valid, log_keep, 0.0)
        log_w = z + log_keep + _dot01(log_keep, later) + run
        w = jnp.exp(log_w)
        if diagonal:
            w = jnp.where(valid, w, 0.0)
        acc = acc + _dot(w.astype(BF16), vb)
        run = run + jnp.sum(log_keep, axis=1, keepdims=True)
        return run, acc

    run0 = jnp.zeros((rows, 1), F32)
    acc0 = jnp.zeros((rows, HEAD_DIM), F32)
    run, acc = block(i, run0, acc0, True)

    def cond(carry):
        return jnp.logical_and(carry[0] < i, jnp.max(carry[1]) > SB_LOG_FLOOR)

    def body(carry):
        it, run, acc = carry
        run, acc = block(i - 1 - it, run, acc, False)
        return it + 1, run, acc

    _, run, acc = lax.while_loop(cond, body, (jnp.int32(0), run, acc))
    for g in range(Q_PER_KV):
        o_ref[0, :, g * HEAD_DIM:(g + 1) * HEAD_DIM] = acc[g * tq:(g + 1) * tq].astype(o_ref.dtype)


def sb_prompt_attention(q, k, v):
    bsz, L, _ = q.shape
    tq = SB_BLOCK
    gw = Q_PER_KV * HEAD_DIM
    return pl.pallas_call(
        functools.partial(_sb_prompt_kernel, tq=tq),
        grid=(bsz, N_KV_HEADS, L // tq),
        in_specs=[pl.BlockSpec((1, tq, gw), lambda b, h, i: (b, i, h)),
                  pl.BlockSpec((1, L, HEAD_DIM), lambda b, h, i: (b, 0, h)),
                  pl.BlockSpec((1, L, HEAD_DIM), lambda b, h, i: (b, 0, h))],
        out_specs=pl.BlockSpec((1, tq, gw), lambda b, h, i: (b, i, h)),
        out_shape=jax.ShapeDtypeStruct(q.shape, BF16),
        compiler_params=_params("parallel", "parallel", "arbitrary"),
        name="sb_prompt",
    )(q, k, v)


PAGE_ROWS = PAGE_SIZE * N_KV_HEADS
SB_PAGES_PER_STEP = 8
MOBA_BLOCKS_PER_STEP = 4


def _own_lane_mask(kvh_of_row, shape):
    lane = lax.broadcasted_iota(jnp.int32, shape, 1)
    return (lane & (N_KV_HEADS - 1)) == kvh_of_row


def _suffix_sum_keys(x):
    n = x.shape[1]
    lane = lax.broadcasted_iota(jnp.int32, x.shape, 1)
    d = N_KV_HEADS
    while d < n:
        shifted = pltpu.roll(x, n - d, axis=1)
        x = x + jnp.where(lane + d < n, shifted, 0.0)
        d *= 2
    return x


def _sb_decode_kernel(pt_ref, q_ref, *refs, pps, n_steps):
    k_refs, v_refs = refs[:pps], refs[pps:2 * pps]
    o_ref, acc_ref, run_ref = refs[2 * pps:]
    s = pl.program_id(1)

    @pl.when(s == 0)
    def _():
        acc_ref[...] = jnp.zeros_like(acc_ref)
        run_ref[...] = jnp.zeros_like(run_ref)

    q = q_ref[0].astype(BF16)
    kvh = lax.broadcasted_iota(jnp.int32, (N_HEADS, PAGE_ROWS), 0) >> 2
    own = _own_lane_mask(kvh, (N_HEADS, PAGE_ROWS))
    for u in range(pps):
        @pl.when(jnp.max(run_ref[...]) > SB_LOG_FLOOR)
        def _():
            z = _dot_nt(q, k_refs[u][...].astype(BF16)) * (HEAD_DIM ** -0.5)
            log_keep = jnp.where(own, -_softplus(z), 0.0)
            incl = _suffix_sum_keys(log_keep)
            w = jnp.where(own, jnp.exp(z + incl + run_ref[:, 0:1]), 0.0)
            acc_ref[...] += _dot(w.astype(BF16), v_refs[u][...].astype(BF16))
            run_ref[...] += jnp.sum(log_keep, axis=1, keepdims=True)

    @pl.when(s == n_steps - 1)
    def _():
        o_ref[0] = acc_ref[...]


def sb_decode_attention(q, k_rows, v_rows, page_table):
    db, n_pages = page_table.shape
    pps = SB_PAGES_PER_STEP
    assert n_pages % pps == 0
    n_steps = n_pages // pps

    def page_spec(u):
        return pl.BlockSpec((PAGE_ROWS, HEAD_DIM),
                            lambda b, s, pt: (pt[b * n_pages + (n_pages - 1 - (s * pps + u))], 0))

    specs = [page_spec(u) for u in range(pps)]
    return pl.pallas_call(
        functools.partial(_sb_decode_kernel, pps=pps, n_steps=n_steps),
        grid_spec=pltpu.PrefetchScalarGridSpec(
            num_scalar_prefetch=1,
            grid=(db, n_steps),
            in_specs=[pl.BlockSpec((1, N_HEADS, HEAD_DIM), lambda b, s, pt: (b, 0, 0))] + specs + specs,
            out_specs=pl.BlockSpec((1, N_HEADS, HEAD_DIM), lambda b, s, pt: (b, 0, 0)),
            scratch_shapes=[pltpu.VMEM((N_HEADS, HEAD_DIM), F32), pltpu.VMEM((N_HEADS, LANES), F32)]),
        out_shape=jax.ShapeDtypeStruct((db, N_HEADS, HEAD_DIM), F32),
        compiler_params=_params("parallel", "arbitrary"),
        name="sb_decode",
    )(page_table.reshape(-1), q, *([k_rows] * pps), *([v_rows] * pps))


def _top_k_mask(gate, n_valid, k):
    lane = lax.broadcasted_iota(jnp.int32, gate.shape, 1)
    sel = jnp.zeros(gate.shape, F32)
    picks = []
    for r in range(k):
        m = jnp.max(gate, axis=1, keepdims=True)
        idx = jnp.min(jnp.where(gate == m, lane, gate.shape[1] - 1), axis=1, keepdims=True)
        pick = lane == idx
        sel = jnp.maximum(sel, jnp.where(pick, jnp.where(r < n_valid, 1.0, 0.0), 0.0))
        gate = jnp.where(pick, -jnp.inf, gate)
        picks.append(idx)
    return sel, picks


def _moba_prompt_kernel(q_ref, k_ref, v_ref, o_ref, km_ref, *, tq, nb):
    i = pl.program_id(2)
    rows = Q_PER_KV * tq
    scale = HEAD_DIM ** -0.5

    @pl.when(i == 0)
    def _():
        km_ref[...] = jnp.zeros_like(km_ref)
        for jb in range(nb):
            km_ref[jb:jb + 1, :] = jnp.mean(k_ref[0, jb * tq:(jb + 1) * tq, :], axis=0, keepdims=True)

    q = jnp.concatenate([q_ref[0, :, g * HEAD_DIM:(g + 1) * HEAD_DIM] for g in range(Q_PER_KV)], axis=0)
    lane = lax.broadcasted_iota(jnp.int32, (rows, LANES), 1)
    gate = _dot_nt(q, km_ref[...].astype(BF16))
    gate = jnp.where(lane < i, gate, -jnp.inf)
    sel, _ = _top_k_mask(gate, i, MOBA_TOPK)
    sel = sel.astype(BF16)

    def scores(j):
        start = pl.multiple_of(j * tq, tq)
        kb = k_ref[0, pl.ds(start, tq), :].astype(BF16)
        vb = v_ref[0, pl.ds(start, tq), :].astype(BF16)
        return _dot_nt(q, kb) * scale, vb

    s, vb = scores(i)
    t_loc = lax.broadcasted_iota(jnp.int32, (rows, tq), 0) & (tq - 1)
    s_loc = lax.broadcasted_iota(jnp.int32, (rows, tq), 1)
    s = jnp.where(s_loc <= t_loc, s, NEG_BIG)
    m = jnp.max(s, axis=1, keepdims=True)
    p = jnp.exp(s - m)
    l = jnp.sum(p, axis=1, keepdims=True)
    acc = _dot(p.astype(BF16), vb)

    def body(j, carry):
        m, l, acc = carry
        s, vb = scores(j)
        onehot = (lax.broadcasted_iota(jnp.int32, (LANES, tq), 0) == j).astype(BF16)
        chosen = _dot(sel, onehot) > 0.5
        s = jnp.where(chosen, s, NEG_BIG)
        m_new = jnp.maximum(m, jnp.max(s, axis=1, keepdims=True))
        alpha = jnp.exp(m - m_new)
        p = jnp.exp(s - m_new)
        l = alpha * l + jnp.sum(p, axis=1, keepdims=True)
        acc = alpha * acc + _dot(p.astype(BF16), vb)
        return m_new, l, acc

    m, l, acc = lax.fori_loop(0, i, body, (m, l, acc))
    out = acc / l
    for g in range(Q_PER_KV):
        o_ref[0, :, g * HEAD_DIM:(g + 1) * HEAD_DIM] = out[g * tq:(g + 1) * tq].astype(o_ref.dtype)


def moba_prompt_attention(q, k, v):
    bsz, L, _ = q.shape
    tq = MOBA_BLOCK
    nb = L // tq
    assert L % tq == 0 and nb <= LANES
    gw = Q_PER_KV * HEAD_DIM
    return pl.pallas_call(
        functools.partial(_moba_prompt_kernel, tq=tq, nb=nb),
        grid=(bsz, N_KV_HEADS, nb),
        in_specs=[pl.BlockSpec((1, tq, gw), lambda b, h, i: (b, i, h)),
                  pl.BlockSpec((1, L, HEAD_DIM), lambda b, h, i: (b, 0, h)),
                  pl.BlockSpec((1, L, HEAD_DIM), lambda b, h, i: (b, 0, h))],
        out_specs=pl.BlockSpec((1, tq, gw), lambda b, h, i: (b, i, h)),
        out_shape=jax.ShapeDtypeStruct(q.shape, BF16),
        scratch_shapes=[pltpu.VMEM((LANES, HEAD_DIM), F32)],
        compiler_params=_params("parallel", "arbitrary", "arbitrary"),
        name="moba_prompt",
    )(q, k, v)


def _moba_kmean_kernel(pt_ref, *refs, bps, ppb):
    k_refs, o_ref = refs[:bps * ppb], refs[bps * ppb]
    sub = 8
    for j in range(bps):
        tot = jnp.zeros((sub, HEAD_DIM), F32)
        for h in range(ppb):
            k_ref = k_refs[j * ppb + h]
            for c in range(PAGE_ROWS // sub):
                tot = tot + k_ref[c * sub:(c + 1) * sub, :]
        mean = (tot[:N_KV_HEADS] + tot[N_KV_HEADS:]) * (1.0 / MOBA_BLOCK)
        o_ref[0, j * N_KV_HEADS:(j + 1) * N_KV_HEADS, :] = mean


def moba_cache_block_means(k_rows, page_table):
    db, n_pages = page_table.shape
    ppb = MOBA_BLOCK // PAGE_SIZE
    nblk = n_pages // ppb
    bps = MOBA_BLOCKS_PER_STEP
    assert nblk % bps == 0

    def page_spec(u):
        return pl.BlockSpec((PAGE_ROWS, HEAD_DIM), lambda b, j, pt: (pt[b * n_pages + j * (bps * ppb) + u], 0))

    return pl.pallas_call(
        functools.partial(_moba_kmean_kernel, bps=bps, ppb=ppb),
        grid_spec=pltpu.PrefetchScalarGridSpec(
            num_scalar_prefetch=1,
            grid=(db, nblk // bps),
            in_specs=[page_spec(u) for u in range(bps * ppb)],
            out_specs=pl.BlockSpec((1, bps * N_KV_HEADS, HEAD_DIM), lambda b, j, pt: (b, j, 0))),
        out_shape=jax.ShapeDtypeStruct((db, nblk * N_KV_HEADS, HEAD_DIM), F32),
        compiler_params=_params("parallel", "arbitrary"),
        name="moba_kmean",
    )(page_table.reshape(-1), *([k_rows] * (bps * ppb)))


def _moba_gate_kernel(q_ref, km_ref, o_ref, *, nblk):
    gate = _dot_nt(q_ref[0].astype(BF16), km_ref[0].astype(BF16))
    kvh = lax.broadcasted_iota(jnp.int32, gate.shape, 0) >> 2
    gate = jnp.where(_own_lane_mask(kvh, gate.shape), gate, -jnp.inf)
    _, picks = _top_k_mask(gate, nblk, MOBA_TOPK)
    lane = lax.broadcasted_iota(jnp.int32, (N_HEADS, LANES), 1)
    out = jnp.zeros((N_HEADS, LANES), jnp.int32)
    for r, idx in enumerate(picks):
        out = jnp.where(lane == r, idx >> 2, out)
    o_ref[0] = out


def moba_decode_select(q, kmean):
    db, rows, _ = kmean.shape
    nblk = rows // N_KV_HEADS
    out = pl.pallas_call(
        functools.partial(_moba_gate_kernel, nblk=nblk),
        grid=(db,),
        in_specs=[pl.BlockSpec((1, N_HEADS, HEAD_DIM), lambda b: (b, 0, 0)),
                  pl.BlockSpec((1, rows, HEAD_DIM), lambda b: (b, 0, 0))],
        out_specs=pl.BlockSpec((1, N_HEADS, LANES), lambda b: (b, 0, 0)),
        out_shape=jax.ShapeDtypeStruct((db, N_HEADS, LANES), jnp.int32),
        compiler_params=_params("parallel"),
        name="moba_gate",
    )(q, kmean)
    return out[:, :, :MOBA_TOPK]


def _moba_decode_kernel(sel_ref, pt_ref, q_ref, kn_ref, vn_ref, *refs, n_sel):
    k_refs, v_refs, o_ref = refs[:n_sel], refs[n_sel:2 * n_sel], refs[2 * n_sel]
    scale = HEAD_DIM ** -0.5
    q = jnp.broadcast_to(q_ref[0, 0], (8, HEAD_DIM))
    own = _own_lane_mask(pl.program_id(1) >> 2, (8, PAGE_ROWS))
    m = jnp.sum(q * kn_ref[0, 0], axis=1, keepdims=True) * scale
    l = jnp.ones((8, 1), F32)
    acc = jnp.broadcast_to(vn_ref[0, 0], (8, HEAD_DIM))
    qb = q.astype(BF16)
    for u in range(n_sel):
        s = jnp.where(own, _dot_nt(qb, k_refs[u][...].astype(BF16)) * scale, NEG_BIG)
        m_new = jnp.maximum(m, jnp.max(s, axis=1, keepdims=True))
        alpha = jnp.exp(m - m_new)
        p = jnp.exp(s - m_new)
        l = alpha * l + jnp.sum(p, axis=1, keepdims=True)
        acc = alpha * acc + _dot(p.astype(BF16), v_refs[u][...].astype(BF16))
        m = m_new
    o_ref[0, 0] = acc / l


def moba_decode_attention(q, k_new, v_new, sel, k_rows, v_rows, page_table):
    db, n_pages = page_table.shape
    ppb = MOBA_BLOCK // PAGE_SIZE
    n_sel = MOBA_TOPK * ppb

    def page_spec(u):
        def index(b, h, sel_r, pt):
            blk = sel_r[(b * N_HEADS + h) * MOBA_TOPK + u // ppb]
            return (pt[b * n_pages + blk * ppb + u % ppb], 0)
        return pl.BlockSpec((PAGE_ROWS, HEAD_DIM), index)

    specs = [page_spec(u) for u in range(n_sel)]
    new_spec = pl.BlockSpec((1, 1, 1, HEAD_DIM), lambda b, h, sel_r, pt: (b, h // Q_PER_KV, 0, 0))
    out = pl.pallas_call(
        functools.partial(_moba_decode_kernel, n_sel=n_sel),
        grid_spec=pltpu.PrefetchScalarGridSpec(
            num_scalar_prefetch=2,
            grid=(db, N_HEADS),
            in_specs=[pl.BlockSpec((1, 1, 1, HEAD_DIM), lambda b, h, sel_r, pt: (b, h, 0, 0)),
                      new_spec, new_spec] + specs + specs,
            out_specs=pl.BlockSpec((1, 1, 8, HEAD_DIM), lambda b, h, sel_r, pt: (b, h, 0, 0))),
        out_shape=jax.ShapeDtypeStruct((db, N_HEADS, 8, HEAD_DIM), F32),
        compiler_params=_params("parallel", "parallel"),
        name="moba_decode",
    )(sel.reshape(-1), page_table.reshape(-1), q, k_new, v_new, *([k_rows] * n_sel), *([v_rows] * n_sel))
    return out[:, :, 0, :]


def _dwconv_kernel(x_ref, hist_ref, w_ref, b_ref, g_ref, beta_ref, o_ref, pad_ref, *shift_refs, width, tl, hp, post):
    t = pl.program_id(2)

    @pl.when(t == 0)
    def _():
        pad_ref[0:hp, :] = hist_ref[0]

    @pl.when(t > 0)
    def _():
        pad_ref[0:hp, :] = pad_ref[tl:tl + hp, :]

    pad_ref[hp:hp + tl, :] = x_ref[0]

    def finish(acc):
        if post == "ln_silu":
            acc = _layer_norm_rows(acc, g_ref[...], beta_ref[...])
        return (acc * jax.nn.sigmoid(acc)).astype(o_ref.dtype)

    if not shift_refs:
        base = hp - (width - 1)
        acc = jnp.zeros(o_ref.shape[1:], F32) + b_ref[...]
        for k in range(width):
            acc = acc + pad_ref[base + k:base + k + tl, :] * w_ref[k:k + 1, :]
        o_ref[0] = finish(acc)
        return

    sh_ref, wb_ref = shift_refs
    halo = 8 * ((width - 1) // 8)
    for r in range(1, 8):
        sh_ref[r - 1] = pad_ref[hp - halo - r:hp + tl - r, :]
    rt = min(DWCONV_ROW_TILE, tl)
    tc = o_ref.shape[2]
    lc = DWCONV_LANE_CHUNK if tc % DWCONV_LANE_CHUNK == 0 else tc
    for k in range(width):
        wb_ref[k] = jnp.broadcast_to(w_ref[k:k + 1, :], (8, tc))

    def row_tile(row0):
        parts = []
        for c0 in range(0, tc, lc):
            acc = jnp.zeros((rt // 8, 8, lc), F32) + b_ref[:, c0:c0 + lc]
            for r in range(8):
                for a in range((width - 1 - r) // 8 + 1):
                    k = width - 1 - (8 * a + r)
                    if r == 0:
                        src = pad_ref[pl.ds(row0 + (hp - 8 * a), rt), c0:c0 + lc]
                    else:
                        src = sh_ref[r - 1, pl.ds(row0 + (halo - 8 * a), rt), c0:c0 + lc]
                    acc = acc + src.reshape(rt // 8, 8, lc) * wb_ref[k, :, c0:c0 + lc]
            parts.append(acc.reshape(rt, lc))
        o_ref[0, pl.ds(row0, rt), :] = finish(jnp.concatenate(parts, axis=1))

    if tl == rt:
        row_tile(0)
    else:
        def body(it, carry):
            row_tile(pl.multiple_of(it * rt, rt))
            return carry

        lax.fori_loop(0, tl // rt, body, 0)


def causal_dwconv(x, hist, w, b, post, ln_g=None, ln_b=None, out_dtype=F32):
    bsz, L, C = x.shape
    width = w.shape[0]
    hp = hist.shape[1]
    assert hp % 8 == 0 and hp >= 8 * ((width - 1) // 8) + min(7, width - 1)
    tc = C if post == "ln_silu" else _pick_tile(C, (1024, 512, 256, 128))
    tl = _pick_tile(L, (256, 128)) if L >= hp else L
    assert tl >= hp or tl == L
    wp = jnp.zeros((-(-width // 8) * 8, C), F32).at[:width].set(w.astype(F32))
    if ln_g is None:
        ln_g = jnp.ones((C,), F32)
        ln_b = jnp.zeros((C,), F32)
    row = lambda v: v.reshape(1, C).astype(F32)
    vec_spec = pl.BlockSpec((1, tc), lambda bb, c, t: (0, c))
    return pl.pallas_call(
        functools.partial(_dwconv_kernel, width=width, tl=tl, hp=hp, post=post),
        grid=(bsz, C // tc, L // tl),
        in_specs=[pl.BlockSpec((1, tl, tc), lambda bb, c, t: (bb, t, c)),
                  pl.BlockSpec((1, hp, tc), lambda bb, c, t: (bb, 0, c)),
                  pl.BlockSpec((wp.shape[0], tc), lambda bb, c, t: (0, c)),
                  vec_spec, vec_spec, vec_spec],
        out_specs=pl.BlockSpec((1, tl, tc), lambda bb, c, t: (bb, t, c)),
        out_shape=jax.ShapeDtypeStruct((bsz, L, C), out_dtype),
        scratch_shapes=[pltpu.VMEM((hp + tl, tc), F32)] + (
            [pltpu.VMEM((7, 8 * ((width - 1) // 8) + tl, tc), F32),
             pltpu.VMEM((width, 8, tc), F32)] if width > 8 else []),
        compiler_params=_params("parallel", "parallel", "arbitrary"),
        name="dwconv_" + post,
    )(x, hist, wp, row(b), row(ln_g), row(ln_b))


def _ssd_prompt_kernel(x_ref, bm_ref, cm_ref, dt_ref, z_ref, dtb_ref, alog_ref, d_ref, ng_ref, h0_ref,
                       yn_ref, h_ref, y_buf, *, chunk, hpg):
    c = pl.program_id(1)
    P, N = SSM_HEAD_DIM, SSM_D_STATE
    gw = hpg * P

    @pl.when(c == 0)
    def _():
        h_ref[...] = h0_ref[...]

    li = lax.broadcasted_iota(jnp.int32, (chunk, chunk), 0)
    si = lax.broadcasted_iota(jnp.int32, (chunk, chunk), 1)
    causal = li >= si
    lower = causal.astype(BF16)
    dt = _softplus(dt_ref[0] + dtb_ref[...])
    a = -jnp.exp(alog_ref[...])
    hi, mid, lo = _split3(dt * a)
    cum = _dot(lower, hi) + _dot(lower, mid) + _dot(lower, lo)
    cum_t = cum.T
    dt_t = dt.T
    for g in range(SSM_GROUPS):
        x = x_ref[0, :, g * gw:(g + 1) * gw]
        cmb = cm_ref[0, :, g * N:(g + 1) * N].astype(BF16)
        bmb = bm_ref[0, :, g * N:(g + 1) * N].astype(BF16)
        cb = _dot_nt(cmb, bmb)
        x_t = x.T
        for r in range(hpg):
            head = g * hpg + r
            cum_col = cum[:, head:head + 1]
            cum_row = cum_t[head:head + 1, :]
            dt_row = dt_t[head:head + 1, :]
            total = cum_t[head:head + 1, chunk - 1:chunk]
            xh = x[:, r * P:(r + 1) * P]
            decay = jnp.exp(jnp.where(causal, cum_col - cum_row, -jnp.inf))
            mix = (cb * decay * dt_row).astype(BF16)
            y = _dot(mix, xh.astype(BF16))
            h_old = h_ref[0, head]
            y = y + _dot_nt(cmb, h_old.astype(BF16)) * jnp.exp(cum_col)
            y_buf[:, r * P:(r + 1) * P] = y + d_ref[:, head:head + 1] * xh
            to_end = jnp.exp(total - cum_row) * dt_row
            s_chunk = _dot((x_t[r * P:(r + 1) * P, :] * to_end).astype(BF16), bmb)
            h_ref[0, head] = jnp.exp(total) * h_old + s_chunk
        z = z_ref[0, :, g * gw:(g + 1) * gw]
        hg = y_buf[...] * (z * jax.nn.sigmoid(z))
        hg = hg * lax.rsqrt(jnp.mean(hg * hg, axis=-1, keepdims=True) + LN_EPS)
        yn_ref[0, :, g * gw:(g + 1) * gw] = (hg * ng_ref[:, g * gw:(g + 1) * gw]).astype(yn_ref.dtype)


def ssd_prompt(xbc, dt_raw, z, dt_bias_g, a_log_g, d_g, norm_g, h0):
    bsz, L, _ = xbc.shape
    n_heads = h0.shape[1]
    G, N, P = SSM_GROUPS, SSM_D_STATE, SSM_HEAD_DIM
    hpg = n_heads // G
    d_inner = n_heads * P
    gn = G * N
    chunk = SSM_CHUNK
    assert d_inner % gn == 0 and (hpg * P) % LANES == 0 and n_heads <= LANES
    b_off = d_inner // gn
    vec_spec = pl.BlockSpec((1, LANES), lambda b, c: (0, 0))
    state_spec = pl.BlockSpec((1, n_heads, P, N), lambda b, c: (b, 0, 0, 0))
    return pl.pallas_call(
        functools.partial(_ssd_prompt_kernel, chunk=chunk, hpg=hpg),
        grid=(bsz, L // chunk),
        in_specs=[pl.BlockSpec((1, chunk, d_inner), lambda b, c: (b, c, 0)),
                  pl.BlockSpec((1, chunk, gn), lambda b, c: (b, c, b_off)),
                  pl.BlockSpec((1, chunk, gn), lambda b, c: (b, c, b_off + 1)),
                  pl.BlockSpec((1, chunk, LANES), lambda b, c: (b, c, 0)),
                  pl.BlockSpec((1, chunk, d_inner), lambda b, c: (b, c, 0)),
                  vec_spec, vec_spec, vec_spec,
                  pl.BlockSpec((1, d_inner), lambda b, c: (0, 0)),
                  state_spec],
        out_specs=[pl.BlockSpec((1, chunk, d_inner), lambda b, c: (b, c, 0)), state_spec],
        out_shape=[jax.ShapeDtypeStruct((bsz, L, d_inner), BF16),
                   jax.ShapeDtypeStruct((bsz, n_heads, P, N), F32)],
        scratch_shapes=[pltpu.VMEM((chunk, hpg * P), F32)],
        compiler_params=_params("parallel", "arbitrary"),
        name="ssd_prompt",
    )(xbc, xbc, xbc, dt_raw, z, dt_bias_g, a_log_g, d_g, norm_g.reshape(1, d_inner).astype(F32), h0)


def _ssd_step_kernel(xc_ref, bm_ref, cm_ref, dt_ref, dtb_ref, alog_ref, d_ref, h0_ref, y_ref, h_ref, *, hpg):
    dt = _softplus(dt_ref[0] + dtb_ref[...])
    da = jnp.exp(dt * (-jnp.exp(alog_ref[...])))
    bm = bm_ref[0]
    cm = cm_ref[0]
    for r in range(hpg):
        xcol = xc_ref[0, r]
        h_new = da[:, r:r + 1] * h0_ref[0, r] + xcol * (dt[:, r:r + 1] * bm)
        h_ref[0, r] = h_new
        y_ref[0, r] = jnp.sum(h_new * cm, axis=1, keepdims=True) + d_ref[:, r:r + 1] * xcol


def ssd_step(x_col, bc, dt_raw, dt_bias_g, a_log_g, d_g, h0):
    db, n_heads, P, _ = x_col.shape
    G, N = SSM_GROUPS, SSM_D_STATE
    hpg = n_heads // G
    vec_spec = pl.BlockSpec((1, LANES), lambda b, g: (0, g))
    return pl.pallas_call(
        functools.partial(_ssd_step_kernel, hpg=hpg),
        grid=(db, G),
        in_specs=[pl.BlockSpec((1, hpg, P, 1), lambda b, g: (b, g, 0, 0)),
                  pl.BlockSpec((1, 1, N), lambda b, g: (b, 0, g)),
                  pl.BlockSpec((1, 1, N), lambda b, g: (b, 0, G + g)),
                  pl.BlockSpec((1, 1, LANES), lambda b, g: (b, 0, g)),
                  vec_spec, vec_spec, vec_spec,
                  pl.BlockSpec((1, hpg, P, N), lambda b, g: (b, g, 0, 0))],
        out_specs=[pl.BlockSpec((1, hpg, P, 1), lambda b, g: (b, g, 0, 0)),
                   pl.BlockSpec((1, hpg, P, N), lambda b, g: (b, g, 0, 0))],
        out_shape=[jax.ShapeDtypeStruct((db, n_heads, P, 1), F32),
                   jax.ShapeDtypeStruct((db, n_heads, P, N), F32)],
        compiler_params=_params("parallel", "parallel"),
        name="ssd_step",
    )(x_col, bc, bc, dt_raw, dt_bias_g, a_log_g, d_g, h0)


def _gated_norm_kernel(y_ref, z_ref, g_ref, o_ref):
    z = z_ref[...]
    h = y_ref[...] * (z * jax.nn.sigmoid(z))
    h = h * lax.rsqrt(jnp.mean(h * h, axis=-1, keepdims=True) + LN_EPS)
    o_ref[...] = (h * g_ref[...]).astype(o_ref.dtype)


def gated_rms_norm(y, z, g):
    m, d = y.shape
    gw = d // SSM_GROUPS
    tm = _pick_tile(m, (512, 256, 128, 8))
    spec = pl.BlockSpec((tm, gw), lambda i, j: (i, j))
    return pl.pallas_call(
        _gated_norm_kernel,
        grid=(m // tm, SSM_GROUPS),
        in_specs=[spec, spec, pl.BlockSpec((1, gw), lambda i, j: (0, j))],
        out_specs=spec,
        out_shape=jax.ShapeDtypeStruct((m, d), BF16),
        compiler_params=_params("parallel", "parallel"),
        name="gated_rms_norm",
    )(y, z, g.reshape(1, d).astype(F32))


def _group_pad(v, n_heads):
    hpg = n_heads // SSM_GROUPS
    out = jnp.zeros((SSM_GROUPS, LANES), F32).at[:, :hpg].set(v.astype(F32).reshape(SSM_GROUPS, hpg))
    return out.reshape(1, SSM_GROUPS * LANES)


def _lane_pad(v):
    return jnp.zeros((1, LANES), F32).at[0, :v.shape[0]].set(v.astype(F32))


def _ffn_half(xp_f, xp_b, xs_f, xs_b, w1, w3, lead, w2, g, b):
    hp, hs = gated_matmul(xp_b, xs_b, w1, w3, None, None, "swiglu", BF16, lead=lead)
    return (matmul_postnorm(hp, w2, None, xp_f, g, b, 0.5), matmul_postnorm(hs, w2, None, xs_f, g, b, 0.5))


def _pad_rows(x, rows):
    return jnp.concatenate([x, jnp.zeros((x.shape[0], rows - x.shape[1]) + x.shape[2:], x.dtype)], axis=1)


def _front_pad_rows(x, rows):
    return jnp.concatenate([jnp.zeros((x.shape[0], rows - x.shape[1]) + x.shape[2:], x.dtype), x], axis=1)


def kernel(x_prompt, x_sample, p_prompt, p_sample, cache_sb_k, cache_sb_v, cache_moba_k, cache_moba_v, state_ssm, state_ssm_conv, state_conf_conv, page_table, ln_g, ln_b, ffn_w1, ffn_w3, ffn_w2, ple_w_proj, ple_w_gate, sb_w_qkv, sb_w_o, ssm_w_in, ssm_conv_w, ssm_conv_b, ssm_dt_bias, ssm_a_log, ssm_d, ssm_norm_g, ssm_w_out, conf_w_pw1, conf_b_pw1, conf_w_dw, conf_b_dw, conf_ln_g, conf_ln_b, conf_w_pw2, conf_b_pw2, moba_w_qkv, moba_w_o):
    bsz, seq, d_model = x_prompt.shape
    db = x_sample.shape[0]
    assert x_sample.shape[1] == 1
    mp = bsz * seq
    depth = ffn_w1.shape[0]
    bf = lambda w: w.astype(BF16)

    xp_f = x_prompt.reshape(mp, d_model)
    xs_f = x_sample.reshape(db, d_model)
    xp_b, xs_b = bf(xp_f), bf(xs_f)
    n_phys = cache_sb_k.shape[1]
    outs = {k: [] for k in ("sb_kp", "sb_vp", "sb_ks", "sb_vs", "ssm_hp", "ssm_hs", "ssm_cp", "ssm_cs",
                            "conf_cp", "conf_cs", "mo_kp", "mo_vp", "mo_ks", "mo_vs")}

    for i in range(depth):
        m, j = i % 4, i // 4
        (xp_f, xp_b), (xs_f, xs_b) = _ffn_half(xp_f, xp_b, xs_f, xs_b, ffn_w1, ffn_w3, (i, 0), bf(ffn_w2[i, 0]),
                                               ln_g[i, 0], ln_b[i, 0])
        g1, b1 = ln_g[i, 1], ln_b[i, 1]

        if m == 0 or m == 3:
            w_qkv = bf(sb_w_qkv[j] if m == 0 else moba_w_qkv[j])
            w_o = bf(sb_w_o[j] if m == 0 else moba_w_o[j])
            qkv = lambda xb, which, dt: matmul(xb, w_qkv, None, dt, *((0, Q_DIM), (Q_DIM, KV_DIM),
                                                                     (Q_DIM + KV_DIM, KV_DIM))[which])
            q = qkv(xp_b, 0, BF16).reshape(bsz, seq, Q_DIM)
            k = qkv(xp_b, 1, F32).reshape(bsz, seq, KV_DIM)
            v = qkv(xp_b, 2, F32).reshape(bsz, seq, KV_DIM)
            attend = sb_prompt_attention if m == 0 else moba_prompt_attention
            o = attend(q, k, v).reshape(mp, Q_DIM)
            xp_f, xp_b = matmul_postnorm(o, w_o, None, xp_f, g1, b1, 1.0)
            kp = k.reshape(bsz, seq, N_KV_HEADS, HEAD_DIM)
            vp = v.reshape(bsz, seq, N_KV_HEADS, HEAD_DIM)
            qs, ks, vs = qkv(xs_b, 0, F32), qkv(xs_b, 1, F32), qkv(xs_b, 2, F32)
            if m == 0:
                kc = cache_sb_k[j].reshape(n_phys * PAGE_ROWS, HEAD_DIM)
                vc = cache_sb_v[j].reshape(n_phys * PAGE_ROWS, HEAD_DIM)
                os_ = sb_decode_attention(qs.reshape(db, N_HEADS, HEAD_DIM), kc, vc, page_table)
            else:
                kc = cache_moba_k[j].reshape(n_phys * PAGE_ROWS, HEAD_DIM)
                vc = cache_moba_v[j].reshape(n_phys * PAGE_ROWS, HEAD_DIM)
                qh = qs.reshape(db, N_HEADS, HEAD_DIM)
                kmean = moba_cache_block_means(kc, page_table)
                sel = moba_decode_select(qh, kmean)
                os_ = moba_decode_attention(qh.reshape(db, N_HEADS, 1, HEAD_DIM),
                                            ks.reshape(db, N_KV_HEADS, 1, HEAD_DIM),
                                            vs.reshape(db, N_KV_HEADS, 1, HEAD_DIM),
                                            sel, kc, vc, page_table)
            xs_f, xs_b = matmul_postnorm(bf(os_.reshape(db, Q_DIM)), w_o, None, xs_f, g1, b1, 1.0)
            ks4 = ks.reshape(db, 1, N_KV_HEADS, HEAD_DIM)
            vs4 = vs.reshape(db, 1, N_KV_HEADS, HEAD_DIM)
            if m == 0:
                outs["sb_kp"].append(kp); outs["sb_vp"].append(vp); outs["sb_ks"].append(ks4); outs["sb_vs"].append(vs4)
            else:
                outs["mo_kp"].append(kp); outs["mo_vp"].append(vp); outs["mo_ks"].append(ks4); outs["mo_vs"].append(vs4)

        elif m == 1:
            n_heads = ssm_dt_bias.shape[1]
            d_inner = n_heads * SSM_HEAD_DIM
            conv_dim = ssm_conv_w.shape[2]
            hpg = n_heads // SSM_GROUPS
            w_in = bf(ssm_w_in[j])
            w_dt_cols = ssm_w_in[j][:, d_inner + conv_dim:]
            w_dt = bf(jnp.zeros((d_model, LANES), F32).at[:, :n_heads].set(w_dt_cols))
            w_dt_g = bf(jnp.zeros((d_model, SSM_GROUPS, LANES), F32).at[:, :, :hpg]
                        .set(w_dt_cols.reshape(d_model, SSM_GROUPS, hpg)).reshape(d_model, SSM_GROUPS * LANES))
            dtb_g = _group_pad(ssm_dt_bias[j], n_heads)
            alog_g = _group_pad(ssm_a_log[j], n_heads)
            d_g = _group_pad(ssm_d[j], n_heads)
            w_out = bf(ssm_w_out[j])
            hp = 8
            z = matmul(xp_b, w_in, None, F32, 0, d_inner)
            xbc_raw = matmul(xp_b, w_in, None, F32, d_inner, conv_dim).reshape(bsz, seq, conv_dim)
            dt_raw = matmul(xp_b, w_dt, None, F32).reshape(bsz, seq, LANES)
            xbc = causal_dwconv(xbc_raw, jnp.zeros((bsz, hp, conv_dim), F32), ssm_conv_w[j], ssm_conv_b[j], "silu")
            h0 = jnp.zeros((bsz, n_heads, SSM_HEAD_DIM, SSM_D_STATE), F32)
            yn, h_last = ssd_prompt(xbc, dt_raw, z.reshape(bsz, seq, d_inner), _lane_pad(ssm_dt_bias[j]),
                                    _lane_pad(ssm_a_log[j]), _lane_pad(ssm_d[j]), ssm_norm_g[j], h0)
            xp_f, xp_b = matmul_postnorm(yn.reshape(mp, d_inner), w_out, None, xp_f, g1, b1, 1.0)
            outs["ssm_hp"].append(h_last)
            outs["ssm_cp"].append(xbc_raw[:, seq - (SSM_CONV - 1):, :])
            zs = matmul(xs_b, w_in, None, F32, 0, d_inner)
            xbc_s = matmul(xs_b, w_in, None, F32, d_inner, conv_dim).reshape(db, 1, conv_dim)
            dt_s = matmul(xs_b, w_dt_g, None, F32).reshape(db, 1, SSM_GROUPS * LANES)
            hist = _front_pad_rows(state_ssm_conv[j], hp)
            xbc_sa = causal_dwconv(_pad_rows(xbc_s, 8), hist, ssm_conv_w[j], ssm_conv_b[j], "silu")[:, :1, :]
            x_col = xbc_sa[:, 0, :d_inner].reshape(db, n_heads, SSM_HEAD_DIM, 1)
            y_col, hs = ssd_step(x_col, xbc_sa[:, :, d_inner:], dt_s, dtb_g, alog_g, d_g, state_ssm[j])
            yns = gated_rms_norm(y_col.reshape(db, d_inner), zs, ssm_norm_g[j])
            xs_f, xs_b = matmul_postnorm(yns, w_out, None, xs_f, g1, b1, 1.0)
            outs["ssm_hs"].append(hs)
            outs["ssm_cs"].append(jnp.concatenate([state_ssm_conv[j], xbc_s], axis=1)[:, 1:, :])

        else:
            ba, bg = conf_b_pw1[j][:d_model], conf_b_pw1[j][d_model:]
            w_pw2 = bf(conf_w_pw2[j])
            hp = 32
            u, us = gated_matmul(xp_b, xs_b, conf_w_pw1, conf_w_pw1, ba, bg, "glu", F32, lead=(j,),
                                 cola=0, colb=d_model, n=d_model)
            u = u.reshape(bsz, seq, d_model)
            us = us.reshape(db, 1, d_model)
            uc = causal_dwconv(u, jnp.zeros((bsz, hp, d_model), F32), conf_w_dw[j], conf_b_dw[j], "ln_silu",
                               conf_ln_g[j], conf_ln_b[j], BF16)
            xp_f, xp_b = matmul_postnorm(uc.reshape(mp, d_model), w_pw2, conf_b_pw2[j], xp_f, g1, b1, 1.0)
            outs["conf_cp"].append(u[:, seq - (CONF_WIDTH - 1):, :])
            hist = _front_pad_rows(state_conf_conv[j], hp)
            ucs = causal_dwconv(_pad_rows(us, 8), hist, conf_w_dw[j], conf_b_dw[j], "ln_silu",
                                conf_ln_g[j], conf_ln_b[j], BF16)[:, 0, :]
            xs_f, xs_b = matmul_postnorm(ucs, w_pw2, conf_b_pw2[j], xs_f, g1, b1, 1.0)
            outs["conf_cs"].append(jnp.concatenate([state_conf_conv[j], us], axis=1)[:, 1:, :])

        (xp_f, xp_b), (xs_f, xs_b) = _ffn_half(xp_f, xp_b, xs_f, xs_b, ffn_w1, ffn_w3, (i, 1), bf(ffn_w2[i, 1]),
                                               ln_g[i, 2], ln_b[i, 2])
        wg_, wp_ = bf(ple_w_gate[i]), bf(ple_w_proj[i])
        xp_f, xp_b = ple_add(xp_b, xp_f, p_prompt[i].reshape(mp, -1), wg_, wp_, ln_g[i, 3], ln_b[i, 3])
        xs_f, xs_b = ple_add(xs_b, xs_f, p_sample[i].reshape(db, -1), wg_, wp_, ln_g[i, 3], ln_b[i, 3])

    st = lambda name: jnp.stack(outs[name])
    return (xp_f.reshape(bsz, seq, d_model), xs_f.reshape(db, 1, d_model),
            st("sb_kp"), st("sb_vp"), st("sb_ks"), st("sb_vs"),
            st("ssm_hp"), st("ssm_hs"), st("ssm_cp"), st("ssm_cs"),
            st("conf_cp"), st("conf_cs"),
            st("mo_kp"), st("mo_vp"), st("mo_ks"), st("mo_vs"))
```

```python
import functools

import jax
import jax.numpy as jnp
from jax import lax
from jax.experimental import pallas as pl
from jax.experimental.pallas import tpu as pltpu

F32 = jnp.float32
BF16 = jnp.bfloat16

N_HEADS = 16
HEAD_DIM = 128
N_KV_HEADS = 4
Q_PER_KV = N_HEADS // N_KV_HEADS
Q_DIM = N_HEADS * HEAD_DIM
KV_DIM = N_KV_HEADS * HEAD_DIM
SB_BLOCK = 128
SB_KV_PER_STEP = 2
MOBA_BLOCK = 256
MOBA_TOPK = 3
PAGE_SIZE = 128
SSM_HEAD_DIM = 64
SSM_GROUPS = 8
SSM_D_STATE = 128
SSM_CONV = 4
SSM_CHUNK = 128
CONF_WIDTH = 31
LN_EPS = 1e-5
DEPTH = 4
DN_ALPHA = (2 * DEPTH) ** 0.25
LANES = 128
NEG_BIG = -1e30
DWCONV_ROW_TILE = 64
DWCONV_LANE_CHUNK = 256
SB_LOG_FLOOR = -104.0
VMEM_LIMIT = 56 * 1024 * 1024


def _params(*sem):
    return pltpu.CompilerParams(dimension_semantics=sem, vmem_limit_bytes=VMEM_LIMIT)


def _layer_norm_rows(y, g, b):
    mu = jnp.mean(y, axis=-1, keepdims=True)
    d = y - mu
    var = jnp.mean(d * d, axis=-1, keepdims=True)
    return d * lax.rsqrt(var + LN_EPS) * g + b


def _split3(x):
    hi = x.astype(BF16)
    r1 = x - hi.astype(F32)
    mid = r1.astype(BF16)
    lo = (r1 - mid.astype(F32)).astype(BF16)
    return hi, mid, lo


def _dot(a, b):
    return jnp.dot(a, b, preferred_element_type=F32)


def _dot_nt(a, b):
    return lax.dot_general(a, b, (((1,), (1,)), ((), ())), preferred_element_type=F32)


def _dot01(x, m01):
    hi, mid, lo = _split3(x)
    return _dot(hi, m01) + _dot(mid, m01) + _dot(lo, m01)


def _pick_tile(n, pref):
    for t in pref:
        if n % t == 0:
            return t
    return n


def _mm_kernel(x_ref, w_ref, b_ref, o_ref):
    o_ref[...] = (_dot(x_ref[...], w_ref[...]) + b_ref[...]).astype(o_ref.dtype)


def matmul(x, w, bias=None, out_dtype=F32, col0=0, n=None):
    m, k = x.shape
    n = w.shape[1] - col0 if n is None else n
    tm = _pick_tile(m, (1024, 512, 256, 128, 8))
    tn = _pick_tile(n, (512, 256, 128))
    assert col0 % tn == 0 and n % tn == 0
    j0 = col0 // tn
    if bias is None:
        bias = jnp.zeros((n,), F32)
    return pl.pallas_call(
        _mm_kernel,
        grid=(m // tm, n // tn),
        in_specs=[pl.BlockSpec((tm, k), lambda i, j: (i, 0)),
                  pl.BlockSpec((k, tn), lambda i, j: (0, j + j0)),
                  pl.BlockSpec((1, tn), lambda i, j: (0, j))],
        out_specs=pl.BlockSpec((tm, tn), lambda i, j: (i, j)),
        out_shape=jax.ShapeDtypeStruct((m, n), out_dtype),
        compiler_params=_params("parallel", "arbitrary"),
        name="mm",
    )(x, w, bias.reshape(1, n).astype(F32))


def _gated_mm_kernel(x_ref, xs_ref, wa_ref, wb_ref, ba_ref, bb_ref, o_ref, os_ref, *, mode):
    wa = wa_ref[...].astype(BF16)
    wb = wb_ref[...].astype(BF16)

    def gated(x):
        a = _dot(x, wa) + ba_ref[...]
        b = _dot(x, wb) + bb_ref[...]
        return a * jax.nn.sigmoid(a) * b if mode == "swiglu" else a * jax.nn.sigmoid(b)

    o_ref[...] = gated(x_ref[...]).astype(o_ref.dtype)

    @pl.when(pl.program_id(0) == 0)
    def _():
        os_ref[0] = gated(xs_ref[...]).astype(os_ref.dtype)

    @pl.when(pl.program_id(0) > 0)
    def _():
        os_ref[...] = jnp.zeros_like(os_ref)


def gated_matmul(x, xs, wa, wb, ba, bb, mode, out_dtype, lead=(), cola=0, colb=0, n=None):
    m, k = x.shape
    s = xs.shape[0]
    n = wa.shape[-1] - cola if n is None else n
    wblock = (None,) * len(lead) + (k, None)
    tm = _pick_tile(m, (1024, 512, 256, 128, 8))
    tn = _pick_tile(n, (512, 256, 128))
    assert cola % tn == 0 and colb % tn == 0 and n % tn == 0
    ja, jb = cola // tn, colb // tn
    if ba is None:
        ba = jnp.zeros((n,), F32)
        bb = jnp.zeros((n,), F32)
    out, out_s = pl.pallas_call(
        functools.partial(_gated_mm_kernel, mode=mode),
        grid=(m // tm, n // tn),
        in_specs=[pl.BlockSpec((tm, k), lambda i, j: (i, 0)),
                  pl.BlockSpec((s, k), lambda i, j: (0, 0)),
                  pl.BlockSpec(wblock[:-1] + (tn,), lambda i, j: lead + (0, j + ja)),
                  pl.BlockSpec(wblock[:-1] + (tn,), lambda i, j: lead + (0, j + jb)),
                  pl.BlockSpec((1, tn), lambda i, j: (0, j)),
                  pl.BlockSpec((1, tn), lambda i, j: (0, j))],
        out_specs=[pl.BlockSpec((tm, tn), lambda i, j: (i, j)),
                   pl.BlockSpec((1, s, tn), lambda i, j: (i, 0, j))],
        out_shape=[jax.ShapeDtypeStruct((m, n), out_dtype),
                   jax.ShapeDtypeStruct((m // tm, s, n), out_dtype)],
        compiler_params=_params("arbitrary", "arbitrary"),
        name="gated_mm",
    )(x, xs, wa, wb, ba.reshape(1, n).astype(F32), bb.reshape(1, n).astype(F32))
    return out, out_s[0]


def _mm_ln_kernel(a_ref, as_ref, w_ref, bias_ref, res_ref, ress_ref, g_ref, b_ref, of_ref, ob_ref, ofs_ref, obs_ref,
                  accs_ref, *, scale, nk):
    i, k = pl.program_id(0), pl.program_id(1)

    def post_norm(res, acc):
        y = DN_ALPHA * res + scale * (acc + bias_ref[...])
        return _layer_norm_rows(y, g_ref[...], b_ref[...])

    def finish(acc):
        out = post_norm(res_ref[...], acc)
        of_ref[...] = out
        ob_ref[...] = out.astype(BF16)

    @pl.when(i == 0)
    def _():
        part = _dot(as_ref[...], w_ref[...])
        if nk > 1:
            @pl.when(k > 0)
            def _():
                accs_ref[...] += part

            @pl.when(k == 0)
            def _():
                accs_ref[...] = part
        else:
            accs_ref[...] = part

    @pl.when(k == nk - 1)
    def _():
        out = post_norm(ress_ref[...], accs_ref[...])
        first = i == 0
        ofs_ref[0] = jnp.where(first, out, 0.0)
        obs_ref[0] = jnp.where(first, out, 0.0).astype(BF16)

    if nk == 1:
        finish(_dot(a_ref[...], w_ref[...]))
        return

    @pl.when(k == 0)
    def _():
        of_ref[...] = _dot(a_ref[...], w_ref[...])

    @pl.when(jnp.logical_and(k > 0, k < nk - 1))
    def _():
        of_ref[...] += _dot(a_ref[...], w_ref[...])

    @pl.when(k == nk - 1)
    def _():
        finish(of_ref[...] + _dot(a_ref[...], w_ref[...]))


def matmul_postnorm(a, a_s, w, bias, res, res_s, g, b, scale, lead=()):
    m, kdim = a.shape
    s = a_s.shape[0]
    n = w.shape[-1]
    tm = _pick_tile(m, (512, 256, 128, 8))
    tk = _pick_tile(kdim, (2816, 2048, 1408, 1024, 512, 256, 128))
    nk = kdim // tk
    if bias is None:
        bias = jnp.zeros((n,), F32)
    row = lambda v: v.reshape(1, n).astype(F32)
    vec = pl.BlockSpec((1, n), lambda i, k: (0, 0))
    of, ob, ofs, obs = pl.pallas_call(
        functools.partial(_mm_ln_kernel, scale=scale, nk=nk),
        grid=(m // tm, nk),
        in_specs=[pl.BlockSpec((tm, tk), lambda i, k: (i, k)),
                  pl.BlockSpec((s, tk), lambda i, k: (0, k)),
                  pl.BlockSpec((None,) * len(lead) + (tk, n), lambda i, k: lead + (k, 0)),
                  vec,
                  pl.BlockSpec((tm, n), lambda i, k: (i, 0)),
                  pl.BlockSpec((s, n), lambda i, k: (0, 0)),
                  vec, vec],
        out_specs=[pl.BlockSpec((tm, n), lambda i, k: (i, 0)),
                   pl.BlockSpec((tm, n), lambda i, k: (i, 0)),
                   pl.BlockSpec((1, s, n), lambda i, k: (i, 0, 0)),
                   pl.BlockSpec((1, s, n), lambda i, k: (i, 0, 0))],
        out_shape=[jax.ShapeDtypeStruct((m, n), F32), jax.ShapeDtypeStruct((m, n), BF16),
                   jax.ShapeDtypeStruct((m // tm, s, n), F32), jax.ShapeDtypeStruct((m // tm, s, n), BF16)],
        scratch_shapes=[pltpu.VMEM((s, n), F32)],
        compiler_params=_params("arbitrary", "arbitrary"),
        name="mm_postnorm",
    )(a, a_s, w, row(bias), res, res_s, row(g), row(b))
    return (of, ob), (ofs[0], obs[0])


def _ple_kernel(xb_ref, xf_ref, p_ref, wg_ref, wp_ref, g_ref, b_ref, of_ref, ob_ref):
    gate = jax.nn.sigmoid(_dot(xb_ref[...], wg_ref[...]))
    proj = _dot(p_ref[...].astype(BF16), wp_ref[...])
    y = DN_ALPHA * xf_ref[...] + gate * proj
    out = _layer_norm_rows(y, g_ref[...], b_ref[...])
    of_ref[...] = out
    ob_ref[...] = out.astype(BF16)


def ple_add(xb, xf, p, wg, wp, g, b):
    m, d = xf.shape
    pd = p.shape[1]
    tm = _pick_tile(m, (512, 256, 128, 8))
    row = lambda v: v.reshape(1, d).astype(F32)
    return pl.pallas_call(
        _ple_kernel,
        grid=(m // tm,),
        in_specs=[pl.BlockSpec((tm, d), lambda i: (i, 0)),
                  pl.BlockSpec((tm, d), lambda i: (i, 0)),
                  pl.BlockSpec((tm, pd), lambda i: (i, 0)),
                  pl.BlockSpec((d, d), lambda i: (0, 0)),
                  pl.BlockSpec((pd, d), lambda i: (0, 0)),
                  pl.BlockSpec((1, d), lambda i: (0, 0)),
                  pl.BlockSpec((1, d), lambda i: (0, 0))],
        out_specs=[pl.BlockSpec((tm, d), lambda i: (i, 0)),
                   pl.BlockSpec((tm, d), lambda i: (i, 0))],
        out_shape=[jax.ShapeDtypeStruct((m, d), F32), jax.ShapeDtypeStruct((m, d), BF16)],
        compiler_params=_params("parallel"),
        name="ple_add",
    )(xb, xf, p, wg, wp, row(g), row(b))


def _softplus(z):
    return jnp.maximum(z, 0.0) + jnp.log1p(jnp.exp(-jnp.abs(z)))


def _sb_prompt_kernel(q_ref, k_ref, v_ref, o_ref, *, tq, nkv):
    i = pl.program_id(2)
    rows = Q_PER_KV * tq
    scale = HEAD_DIM ** -0.5
    gw = Q_PER_KV * HEAD_DIM
    qs = [jnp.concatenate([q_ref[0, :, h * gw + g * HEAD_DIM:h * gw + (g + 1) * HEAD_DIM] for g in range(Q_PER_KV)],
                          axis=0) for h in range(nkv)]
    t_loc = lax.broadcasted_iota(jnp.int32, (rows, tq), 0) & (tq - 1)
    s_loc = lax.broadcasted_iota(jnp.int32, (rows, tq), 1)
    later = (lax.broadcasted_iota(jnp.int32, (tq, tq), 0) > lax.broadcasted_iota(jnp.int32, (tq, tq), 1)).astype(BF16)

    def block(h, j, run, acc, diagonal):
        start = pl.multiple_of(j * tq, tq)
        kb = k_ref[0, pl.ds(start, tq), h * HEAD_DIM:(h + 1) * HEAD_DIM].astype(BF16)
        vb = v_ref[0, pl.ds(start, tq), h * HEAD_DIM:(h + 1) * HEAD_DIM].astype(BF16)
        z = _dot_nt(qs[h], kb) * scale
        log_keep = -_softplus(z)
        if diagonal:
            valid = s_loc < t_loc
            log_keep = jnp.where(valid, log_keep, 0.0)
        log_w = z + log_keep + _dot01(log_keep, later) + run
        w = jnp.exp(log_w)
        if diagonal:
            w = jnp.where(valid, w, 0.0)
        acc = acc + _dot(w.astype(BF16), vb)
        run = run + jnp.sum(log_keep, axis=1, keepdims=True)
        return run, acc

    run0 = jnp.zeros((rows, 1), F32)
    acc0 = jnp.zeros((rows, HEAD_DIM), F32)
    state = []
    for h in range(nkv):
        state += list(block(h, i, run0, acc0, True))

    def cond(carry):
        live = jnp.max(carry[1])
        for h in range(1, nkv):
            live = jnp.maximum(live, jnp.max(carry[1 + 2 * h]))
        return jnp.logical_and(carry[0] < i, live > SB_LOG_FLOOR)

    def body(carry):
        it = carry[0]
        out = [it + 1]
        for h in range(nkv):
            out += list(block(h, i - 1 - it, carry[1 + 2 * h], carry[2 + 2 * h], False))
        return tuple(out)

    final = lax.while_loop(cond, body, (jnp.int32(0),) + tuple(state))
    for h in range(nkv):
        acc = final[2 + 2 * h]
        for g in range(Q_PER_KV):
            o_ref[0, :, h * gw + g * HEAD_DIM:h * gw + (g + 1) * HEAD_DIM] = acc[g * tq:(g + 1) * tq].astype(o_ref.dtype)


def sb_prompt_attention(q, k, v):
    bsz, L, _ = q.shape
    tq = SB_BLOCK
    nkv = SB_KV_PER_STEP
    gw = nkv * Q_PER_KV * HEAD_DIM
    return pl.pallas_call(
        functools.partial(_sb_prompt_kernel, tq=tq, nkv=nkv),
        grid=(bsz, N_KV_HEADS // nkv, L // tq),
        in_specs=[pl.BlockSpec((1, tq, gw), lambda b, h, i: (b, i, h)),
                  pl.BlockSpec((1, L, nkv * HEAD_DIM), lambda b, h, i: (b, 0, h)),
                  pl.BlockSpec((1, L, nkv * HEAD_DIM), lambda b, h, i: (b, 0, h))],
        out_specs=pl.BlockSpec((1, tq, gw), lambda b, h, i: (b, i, h)),
        out_shape=jax.ShapeDtypeStruct(q.shape, BF16),
        compiler_params=_params("parallel", "parallel", "arbitrary"),
        name="sb_prompt",
    )(q, k, v)


PAGE_ROWS = PAGE_SIZE * N_KV_HEADS
SB_PAGES_PER_STEP = 8
MOBA_BLOCKS_PER_STEP = 4


def _own_lane_mask(kvh_of_row, shape):
    lane = lax.broadcasted_iota(jnp.int32, shape, 1)
    return (lane & (N_KV_HEADS - 1)) == kvh_of_row


def _suffix_sum_keys(x):
    n = x.shape[1]
    lane = lax.broadcasted_iota(jnp.int32, x.shape, 1)
    d = N_KV_HEADS
    while d < n:
        shifted = pltpu.roll(x, n - d, axis=1)
        x = x + jnp.where(lane + d < n, shifted, 0.0)
        d *= 2
    return x


def _sb_decode_kernel(pt_ref, q_ref, *refs, pps, n_steps):
    k_refs, v_refs = refs[:pps], refs[pps:2 * pps]
    o_ref, acc_ref, run_ref = refs[2 * pps:]
    s = pl.program_id(1)

    @pl.when(s == 0)
    def _():
        acc_ref[...] = jnp.zeros_like(acc_ref)
        run_ref[...] = jnp.zeros_like(run_ref)

    q = q_ref[0].astype(BF16)
    kvh = lax.broadcasted_iota(jnp.int32, (N_HEADS, PAGE_ROWS), 0) >> 2
    own = _own_lane_mask(kvh, (N_HEADS, PAGE_ROWS))
    for u in range(pps):
        @pl.when(jnp.max(run_ref[...]) > SB_LOG_FLOOR)
        def _():
            z = _dot_nt(q, k_refs[u][...].astype(BF16)) * (HEAD_DIM ** -0.5)
            log_keep = jnp.where(own, -_softplus(z), 0.0)
            incl = _suffix_sum_keys(log_keep)
            w = jnp.where(own, jnp.exp(z + incl + run_ref[:, 0:1]), 0.0)
            acc_ref[...] += _dot(w.astype(BF16), v_refs[u][...].astype(BF16))
            run_ref[...] += jnp.sum(log_keep, axis=1, keepdims=True)

    @pl.when(s == n_steps - 1)
    def _():
        o_ref[0] = acc_ref[...]


def sb_decode_attention(q, k_rows, v_rows, page_table):
    db, n_pages = page_table.shape
    pps = SB_PAGES_PER_STEP
    assert n_pages % pps == 0
    n_steps = n_pages // pps

    def page_spec(u):
        return pl.BlockSpec((PAGE_ROWS, HEAD_DIM),
                            lambda b, s, pt: (pt[b * n_pages + (n_pages - 1 - (s * pps + u))], 0))

    specs = [page_spec(u) for u in range(pps)]
    return pl.pallas_call(
        functools.partial(_sb_decode_kernel, pps=pps, n_steps=n_steps),
        grid_spec=pltpu.PrefetchScalarGridSpec(
            num_scalar_prefetch=1,
            grid=(db, n_steps),
            in_specs=[pl.BlockSpec((1, N_HEADS, HEAD_DIM), lambda b, s, pt: (b, 0, 0))] + specs + specs,
            out_specs=pl.BlockSpec((1, N_HEADS, HEAD_DIM), lambda b, s, pt: (b, 0, 0)),
            scratch_shapes=[pltpu.VMEM((N_HEADS, HEAD_DIM), F32), pltpu.VMEM((N_HEADS, LANES), F32)]),
        out_shape=jax.ShapeDtypeStruct((db, N_HEADS, HEAD_DIM), F32),
        compiler_params=_params("parallel", "arbitrary"),
        name="sb_decode",
    )(page_table.reshape(-1), q, *([k_rows] * pps), *([v_rows] * pps))


def _top_k_mask(gate, n_valid, k):
    lane = lax.broadcasted_iota(jnp.int32, gate.shape, 1)
    sel = jnp.zeros(gate.shape, F32)
    picks = []
    for r in range(k):
        m = jnp.max(gate, axis=1, keepdims=True)
        idx = jnp.min(jnp.where(gate == m, lane, gate.shape[1] - 1), axis=1, keepdims=True)
        pick = lane == idx
        sel = jnp.maximum(sel, jnp.where(pick, jnp.where(r < n_valid, 1.0, 0.0), 0.0))
        gate = jnp.where(pick, -jnp.inf, gate)
        picks.append(idx)
    return sel, picks


def _moba_prompt_kernel(q_ref, k_ref, v_ref, o_ref, km_ref, *, tq, nb):
    i = pl.program_id(2)
    rows = Q_PER_KV * tq
    scale = HEAD_DIM ** -0.5

    @pl.when(i == 0)
    def _():
        km_ref[...] = jnp.zeros_like(km_ref)
        for jb in range(nb):
            km_ref[jb:jb + 1, :] = jnp.mean(k_ref[0, jb * tq:(jb + 1) * tq, :], axis=0, keepdims=True)

    q = jnp.concatenate([q_ref[0, :, g * HEAD_DIM:(g + 1) * HEAD_DIM] for g in range(Q_PER_KV)], axis=0)
    lane = lax.broadcasted_iota(jnp.int32, (rows, LANES), 1)
    gate = _dot_nt(q, km_ref[...].astype(BF16))
    gate = jnp.where(lane < i, gate, -jnp.inf)
    sel, _ = _top_k_mask(gate, i, MOBA_TOPK)
    sel = sel.astype(BF16)

    def scores(j):
        start = pl.multiple_of(j * tq, tq)
        kb = k_ref[0, pl.ds(start, tq), :].astype(BF16)
        vb = v_ref[0, pl.ds(start, tq), :].astype(BF16)
        return _dot_nt(q, kb) * scale, vb

    s, vb = scores(i)
    t_loc = lax.broadcasted_iota(jnp.int32, (rows, tq), 0) & (tq - 1)
    s_loc = lax.broadcasted_iota(jnp.int32, (rows, tq), 1)
    s = jnp.where(s_loc <= t_loc, s, NEG_BIG)
    m = jnp.max(s, axis=1, keepdims=True)
    p = jnp.exp(s - m)
    l = jnp.sum(p, axis=1, keepdims=True)
    acc = _dot(p.astype(BF16), vb)

    def body(j, carry):
        m, l, acc = carry
        s, vb = scores(j)
        onehot = (lax.broadcasted_iota(jnp.int32, (LANES, tq), 0) == j).astype(BF16)
        chosen = _dot(sel, onehot) > 0.5
        s = jnp.where(chosen, s, NEG_BIG)
        m_new = jnp.maximum(m, jnp.max(s, axis=1, keepdims=True))
        alpha = jnp.exp(m - m_new)
        p = jnp.exp(s - m_new)
        l = alpha * l + jnp.sum(p, axis=1, keepdims=True)
        acc = alpha * acc + _dot(p.astype(BF16), vb)
        return m_new, l, acc

    m, l, acc = lax.fori_loop(0, i, body, (m, l, acc))
    out = acc / l
    for g in range(Q_PER_KV):
        o_ref[0, :, g * HEAD_DIM:(g + 1) * HEAD_DIM] = out[g * tq:(g + 1) * tq].astype(o_ref.dtype)


def moba_prompt_attention(q, k, v):
    bsz, L, _ = q.shape
    tq = MOBA_BLOCK
    nb = L // tq
    assert L % tq == 0 and nb <= LANES
    gw = Q_PER_KV * HEAD_DIM
    return pl.pallas_call(
        functools.partial(_moba_prompt_kernel, tq=tq, nb=nb),
        grid=(bsz, N_KV_HEADS, nb),
        in_specs=[pl.BlockSpec((1, tq, gw), lambda b, h, i: (b, i, h)),
                  pl.BlockSpec((1, L, HEAD_DIM), lambda b, h, i: (b, 0, h)),
                  pl.BlockSpec((1, L, HEAD_DIM), lambda b, h, i: (b, 0, h))],
        out_specs=pl.BlockSpec((1, tq, gw), lambda b, h, i: (b, i, h)),
        out_shape=jax.ShapeDtypeStruct(q.shape, BF16),
        scratch_shapes=[pltpu.VMEM((LANES, HEAD_DIM), F32)],
        compiler_params=_params("parallel", "arbitrary", "arbitrary"),
        name="moba_prompt",
    )(q, k, v)


def _moba_kmean_kernel(pt_ref, *refs, bps, ppb):
    k_refs, o_ref = refs[:bps * ppb], refs[bps * ppb]
    sub = 8
    for j in range(bps):
        tot = jnp.zeros((sub, HEAD_DIM), F32)
        for h in range(ppb):
            k_ref = k_refs[j * ppb + h]
            for c in range(PAGE_ROWS // sub):
                tot = tot + k_ref[c * sub:(c + 1) * sub, :]
        mean = (tot[:N_KV_HEADS] + tot[N_KV_HEADS:]) * (1.0 / MOBA_BLOCK)
        o_ref[0, j * N_KV_HEADS:(j + 1) * N_KV_HEADS, :] = mean


def moba_cache_block_means(k_rows, page_table):
    db, n_pages = page_table.shape
    ppb = MOBA_BLOCK // PAGE_SIZE
    nblk = n_pages // ppb
    bps = MOBA_BLOCKS_PER_STEP
    assert nblk % bps == 0

    def page_spec(u):
        return pl.BlockSpec((PAGE_ROWS, HEAD_DIM), lambda b, j, pt: (pt[b * n_pages + j * (bps * ppb) + u], 0))

    return pl.pallas_call(
        functools.partial(_moba_kmean_kernel, bps=bps, ppb=ppb),
        grid_spec=pltpu.PrefetchScalarGridSpec(
            num_scalar_prefetch=1,
            grid=(db, nblk // bps),
            in_specs=[page_spec(u) for u in range(bps * ppb)],
            out_specs=pl.BlockSpec((1, bps * N_KV_HEADS, HEAD_DIM), lambda b, j, pt: (b, j, 0))),
        out_shape=jax.ShapeDtypeStruct((db, nblk * N_KV_HEADS, HEAD_DIM), F32),
        compiler_params=_params("parallel", "arbitrary"),
        name="moba_kmean",
    )(page_table.reshape(-1), *([k_rows] * (bps * ppb)))


def _moba_gate_kernel(q_ref, km_ref, o_ref, *, nblk):
    gate = _dot_nt(q_ref[0].astype(BF16), km_ref[0].astype(BF16))
    kvh = lax.broadcasted_iota(jnp.int32, gate.shape, 0) >> 2
    gate = jnp.where(_own_lane_mask(kvh, gate.shape), gate, -jnp.inf)
    _, picks = _top_k_mask(gate, nblk, MOBA_TOPK)
    lane = lax.broadcasted_iota(jnp.int32, (N_HEADS, LANES), 1)
    out = jnp.zeros((N_HEADS, LANES), jnp.int32)
    for r, idx in enumerate(picks):
        out = jnp.where(lane == r, idx >> 2, out)
    o_ref[0] = out


def moba_decode_select(q, kmean):
    db, rows, _ = kmean.shape
    nblk = rows // N_KV_HEADS
    out = pl.pallas_call(
        functools.partial(_moba_gate_kernel, nblk=nblk),
        grid=(db,),
        in_specs=[pl.BlockSpec((1, N_HEADS, HEAD_DIM), lambda b: (b, 0, 0)),
                  pl.BlockSpec((1, rows, HEAD_DIM), lambda b: (b, 0, 0))],
        out_specs=pl.BlockSpec((1, N_HEADS, LANES), lambda b: (b, 0, 0)),
        out_shape=jax.ShapeDtypeStruct((db, N_HEADS, LANES), jnp.int32),
        compiler_params=_params("parallel"),
        name="moba_gate",
    )(q, kmean)
    return out[:, :, :MOBA_TOPK]


def _moba_decode_kernel(sel_ref, pt_ref, q_ref, kn_ref, vn_ref, *refs, n_sel):
    k_refs, v_refs, o_ref = refs[:n_sel], refs[n_sel:2 * n_sel], refs[2 * n_sel]
    scale = HEAD_DIM ** -0.5
    q = jnp.broadcast_to(q_ref[0, 0], (8, HEAD_DIM))
    own = _own_lane_mask(pl.program_id(1) >> 2, (8, PAGE_ROWS))
    m = jnp.sum(q * kn_ref[0, 0], axis=1, keepdims=True) * scale
    l = jnp.ones((8, 1), F32)
    acc = jnp.broadcast_to(vn_ref[0, 0], (8, HEAD_DIM))
    qb = q.astype(BF16)
    for u in range(n_sel):
        s = jnp.where(own, _dot_nt(qb, k_refs[u][...].astype(BF16)) * scale, NEG_BIG)
        m_new = jnp.maximum(m, jnp.max(s, axis=1, keepdims=True))
        alpha = jnp.exp(m - m_new)
        p = jnp.exp(s - m_new)
        l = alpha * l + jnp.sum(p, axis=1, keepdims=True)
        acc = alpha * acc + _dot(p.astype(BF16), v_refs[u][...].astype(BF16))
        m = m_new
    o_ref[0, 0] = acc / l


def moba_decode_attention(q, k_new, v_new, sel, k_rows, v_rows, page_table):
    db, n_pages = page_table.shape
    ppb = MOBA_BLOCK // PAGE_SIZE
    n_sel = MOBA_TOPK * ppb

    def page_spec(u):
        def index(b, h, sel_r, pt):
            blk = sel_r[(b * N_HEADS + h) * MOBA_TOPK + u // ppb]
            return (pt[b * n_pages + blk * ppb + u % ppb], 0)
        return pl.BlockSpec((PAGE_ROWS, HEAD_DIM), index)

    specs = [page_spec(u) for u in range(n_sel)]
    new_spec = pl.BlockSpec((1, 1, 1, HEAD_DIM), lambda b, h, sel_r, pt: (b, h // Q_PER_KV, 0, 0))
    out = pl.pallas_call(
        functools.partial(_moba_decode_kernel, n_sel=n_sel),
        grid_spec=pltpu.PrefetchScalarGridSpec(
            num_scalar_prefetch=2,
            grid=(db, N_HEADS),
            in_specs=[pl.BlockSpec((1, 1, 1, HEAD_DIM), lambda b, h, sel_r, pt: (b, h, 0, 0)),
                      new_spec, new_spec] + specs + specs,
            out_specs=pl.BlockSpec((1, 1, 8, HEAD_DIM), lambda b, h, sel_r, pt: (b, h, 0, 0))),
        out_shape=jax.ShapeDtypeStruct((db, N_HEADS, 8, HEAD_DIM), F32),
        compiler_params=_params("parallel", "parallel"),
        name="moba_decode",
    )(sel.reshape(-1), page_table.reshape(-1), q, k_new, v_new, *([k_rows] * n_sel), *([v_rows] * n_sel))
    return out[:, :, 0, :]


def _dwconv_kernel(x_ref, hist_ref, w_ref, b_ref, g_ref, beta_ref, o_ref, pad_ref, *shift_refs, width, tl, hp, post):
    t = pl.program_id(2)

    @pl.when(t == 0)
    def _():
        pad_ref[0:hp, :] = hist_ref[0]

    @pl.when(t > 0)
    def _():
        pad_ref[0:hp, :] = pad_ref[tl:tl + hp, :]

    pad_ref[hp:hp + tl, :] = x_ref[0]

    def finish(acc):
        if post == "ln_silu":
            acc = _layer_norm_rows(acc, g_ref[...], beta_ref[...])
        return (acc * jax.nn.sigmoid(acc)).astype(o_ref.dtype)

    if not shift_refs:
        base = hp - (width - 1)
        acc = jnp.zeros(o_ref.shape[1:], F32) + b_ref[...]
        for k in range(width):
            acc = acc + pad_ref[base + k:base + k + tl, :] * w_ref[k:k + 1, :]
        o_ref[0] = finish(acc)
        return

    sh_ref, wb_ref = shift_refs
    halo = 8 * ((width - 1) // 8)
    for r in range(1, 8):
        sh_ref[r - 1] = pad_ref[hp - halo - r:hp + tl - r, :]
    rt = min(DWCONV_ROW_TILE, tl)
    tc = o_ref.shape[2]
    lc = DWCONV_LANE_CHUNK if tc % DWCONV_LANE_CHUNK == 0 else tc
    for k in range(width):
        wb_ref[k] = jnp.broadcast_to(w_ref[k:k + 1, :], (8, tc))

    def row_tile(row0):
        parts = []
        for c0 in range(0, tc, lc):
            acc = jnp.zeros((rt // 8, 8, lc), F32) + b_ref[:, c0:c0 + lc]
            for r in range(8):
                for a in range((width - 1 - r) // 8 + 1):
                    k = width - 1 - (8 * a + r)
                    if r == 0:
                        src = pad_ref[pl.ds(row0 + (hp - 8 * a), rt), c0:c0 + lc]
                    else:
                        src = sh_ref[r - 1, pl.ds(row0 + (halo - 8 * a), rt), c0:c0 + lc]
                    acc = acc + src.reshape(rt // 8, 8, lc) * wb_ref[k, :, c0:c0 + lc]
            parts.append(acc.reshape(rt, lc))
        o_ref[0, pl.ds(row0, rt), :] = finish(jnp.concatenate(parts, axis=1))

    if tl == rt:
        row_tile(0)
    else:
        def body(it, carry):
            row_tile(pl.multiple_of(it * rt, rt))
            return carry

        lax.fori_loop(0, tl // rt, body, 0)


def causal_dwconv(x, hist, w, b, post, ln_g=None, ln_b=None, out_dtype=F32):
    bsz, L, C = x.shape
    width = w.shape[0]
    hp = hist.shape[1]
    assert hp % 8 == 0 and hp >= 8 * ((width - 1) // 8) + min(7, width - 1)
    tc = C if post == "ln_silu" else _pick_tile(C, (1024, 512, 256, 128))
    tl = _pick_tile(L, (256, 128)) if L >= hp else L
    assert tl >= hp or tl == L
    wp = jnp.zeros((-(-width // 8) * 8, C), F32).at[:width].set(w.astype(F32))
    if ln_g is None:
        ln_g = jnp.ones((C,), F32)
        ln_b = jnp.zeros((C,), F32)
    row = lambda v: v.reshape(1, C).astype(F32)
    vec_spec = pl.BlockSpec((1, tc), lambda bb, c, t: (0, c))
    return pl.pallas_call(
        functools.partial(_dwconv_kernel, width=width, tl=tl, hp=hp, post=post),
        grid=(bsz, C // tc, L // tl),
        in_specs=[pl.BlockSpec((1, tl, tc), lambda bb, c, t: (bb, t, c)),
                  pl.BlockSpec((1, hp, tc), lambda bb, c, t: (bb, 0, c)),
                  pl.BlockSpec((wp.shape[0], tc), lambda bb, c, t: (0, c)),
                  vec_spec, vec_spec, vec_spec],
        out_specs=pl.BlockSpec((1, tl, tc), lambda bb, c, t: (bb, t, c)),
        out_shape=jax.ShapeDtypeStruct((bsz, L, C), out_dtype),
        scratch_shapes=[pltpu.VMEM((hp + tl, tc), F32)] + (
            [pltpu.VMEM((7, 8 * ((width - 1) // 8) + tl, tc), F32),
             pltpu.VMEM((width, 8, tc), F32)] if width > 8 else []),
        compiler_params=_params("parallel", "parallel", "arbitrary"),
        name="dwconv_" + post,
    )(x, hist, wp, row(b), row(ln_g), row(ln_b))


def _ssd_prompt_kernel(x_ref, bm_ref, cm_ref, dt_ref, z_ref, dtb_ref, alog_ref, d_ref, ng_ref, h0_ref,
                       yn_ref, h_ref, y_buf, *, chunk, hpg):
    c = pl.program_id(1)
    P, N = SSM_HEAD_DIM, SSM_D_STATE
    gw = hpg * P

    @pl.when(c == 0)
    def _():
        h_ref[...] = h0_ref[...]

    li = lax.broadcasted_iota(jnp.int32, (chunk, chunk), 0)
    si = lax.broadcasted_iota(jnp.int32, (chunk, chunk), 1)
    causal = li >= si
    lower = causal.astype(BF16)
    dt = _softplus(dt_ref[0] + dtb_ref[...])
    a = -jnp.exp(alog_ref[...])
    hi, mid, lo = _split3(dt * a)
    cum = _dot(lower, hi) + _dot(lower, mid) + _dot(lower, lo)
    cum_t = cum.T
    dt_t = dt.T
    for g in range(SSM_GROUPS):
        x = x_ref[0, :, g * gw:(g + 1) * gw]
        cmb = cm_ref[0, :, g * N:(g + 1) * N].astype(BF16)
        bmb = bm_ref[0, :, g * N:(g + 1) * N].astype(BF16)
        cb = _dot_nt(cmb, bmb)
        x_t = x.T
        for r in range(hpg):
            head = g * hpg + r
            cum_col = cum[:, head:head + 1]
            cum_row = cum_t[head:head + 1, :]
            dt_row = dt_t[head:head + 1, :]
            total = cum_t[head:head + 1, chunk - 1:chunk]
            xh = x[:, r * P:(r + 1) * P]
            decay = jnp.exp(jnp.where(causal, cum_col - cum_row, -jnp.inf))
            mix = (cb * decay * dt_row).astype(BF16)
            y = _dot(mix, xh.astype(BF16))
            h_old = h_ref[0, head]
            y = y + _dot_nt(cmb, h_old.astype(BF16)) * jnp.exp(cum_col)
            y_buf[:, r * P:(r + 1) * P] = y + d_ref[:, head:head + 1] * xh
            to_end = jnp.exp(total - cum_row) * dt_row
            s_chunk = _dot((x_t[r * P:(r + 1) * P, :] * to_end).astype(BF16), bmb)
            h_ref[0, head] = jnp.exp(total) * h_old + s_chunk
        z = z_ref[0, :, g * gw:(g + 1) * gw]
        hg = y_buf[...] * (z * jax.nn.sigmoid(z))
        hg = hg * lax.rsqrt(jnp.mean(hg * hg, axis=-1, keepdims=True) + LN_EPS)
        yn_ref[0, :, g * gw:(g + 1) * gw] = (hg * ng_ref[:, g * gw:(g + 1) * gw]).astype(yn_ref.dtype)


def ssd_prompt(xbc, dt_raw, z, dt_bias_g, a_log_g, d_g, norm_g, h0):
    bsz, L, _ = xbc.shape
    n_heads = h0.shape[1]
    G, N, P = SSM_GROUPS, SSM_D_STATE, SSM_HEAD_DIM
    hpg = n_heads // G
    d_inner = n_heads * P
    gn = G * N
    chunk = SSM_CHUNK
    assert d_inner % gn == 0 and (hpg * P) % LANES == 0 and n_heads <= LANES
    b_off = d_inner // gn
    vec_spec = pl.BlockSpec((1, LANES), lambda b, c: (0, 0))
    state_spec = pl.BlockSpec((1, n_heads, P, N), lambda b, c: (b, 0, 0, 0))
    return pl.pallas_call(
        functools.partial(_ssd_prompt_kernel, chunk=chunk, hpg=hpg),
        grid=(bsz, L // chunk),
        in_specs=[pl.BlockSpec((1, chunk, d_inner), lambda b, c: (b, c, 0)),
                  pl.BlockSpec((1, chunk, gn), lambda b, c: (b, c, b_off)),
                  pl.BlockSpec((1, chunk, gn), lambda b, c: (b, c, b_off + 1)),
                  pl.BlockSpec((1, chunk, LANES), lambda b, c: (b, c, 0)),
                  pl.BlockSpec((1, chunk, d_inner), lambda b, c: (b, c, 0)),
                  vec_spec, vec_spec, vec_spec,
                  pl.BlockSpec((1, d_inner), lambda b, c: (0, 0)),
                  state_spec],
        out_specs=[pl.BlockSpec((1, chunk, d_inner), lambda b, c: (b, c, 0)), state_spec],
        out_shape=[jax.ShapeDtypeStruct((bsz, L, d_inner), BF16),
                   jax.ShapeDtypeStruct((bsz, n_heads, P, N), F32)],
        scratch_shapes=[pltpu.VMEM((chunk, hpg * P), F32)],
        compiler_params=_params("parallel", "arbitrary"),
        name="ssd_prompt",
    )(xbc, xbc, xbc, dt_raw, z, dt_bias_g, a_log_g, d_g, norm_g.reshape(1, d_inner).astype(F32), h0)


def _ssd_step_kernel(xc_ref, bc_ref, dt_ref, dtb_ref, alog_ref, d_ref, h0_ref, y_ref, h_ref, *, n_heads):
    N = SSM_D_STATE
    hpg = n_heads // SSM_GROUPS
    dt = _softplus(dt_ref[0] + dtb_ref[...])
    da = jnp.exp(dt * (-jnp.exp(alog_ref[...])))
    for h in range(n_heads):
        g = h // hpg
        bm = bc_ref[0, :, g * N:(g + 1) * N]
        cm = bc_ref[0, :, (SSM_GROUPS + g) * N:(SSM_GROUPS + g + 1) * N]
        xcol = xc_ref[0, h]
        h_new = da[:, h:h + 1] * h0_ref[0, h] + xcol * (dt[:, h:h + 1] * bm)
        h_ref[0, h] = h_new
        y_ref[0, h] = jnp.sum(h_new * cm, axis=1, keepdims=True) + d_ref[:, h:h + 1] * xcol


def ssd_step(x_col, bc, dt_raw, dt_bias, a_log, d_skip, h0):
    db, n_heads, P, _ = x_col.shape
    N = SSM_D_STATE
    vec_spec = pl.BlockSpec((1, LANES), lambda b: (0, 0))
    state_spec = pl.BlockSpec((1, n_heads, P, N), lambda b: (b, 0, 0, 0))
    col_spec = pl.BlockSpec((1, n_heads, P, 1), lambda b: (b, 0, 0, 0))
    return pl.pallas_call(
        functools.partial(_ssd_step_kernel, n_heads=n_heads),
        grid=(db,),
        in_specs=[col_spec,
                  pl.BlockSpec((1, 1, bc.shape[2]), lambda b: (b, 0, 0)),
                  pl.BlockSpec((1, 1, LANES), lambda b: (b, 0, 0)),
                  vec_spec, vec_spec, vec_spec, state_spec],
        out_specs=[col_spec, state_spec],
        out_shape=[jax.ShapeDtypeStruct((db, n_heads, P, 1), F32),
                   jax.ShapeDtypeStruct((db, n_heads, P, N), F32)],
        compiler_params=_params("parallel"),
        name="ssd_step",
    )(x_col, bc, dt_raw, dt_bias, a_log, d_skip, h0)


def _gated_norm_kernel(y_ref, z_ref, g_ref, o_ref):
    z = z_ref[...]
    h = y_ref[...] * (z * jax.nn.sigmoid(z))
    h = h * lax.rsqrt(jnp.mean(h * h, axis=-1, keepdims=True) + LN_EPS)
    o_ref[...] = (h * g_ref[...]).astype(o_ref.dtype)


def gated_rms_norm(y, z, g):
    m, d = y.shape
    gw = d // SSM_GROUPS
    tm = _pick_tile(m, (512, 256, 128, 8))
    spec = pl.BlockSpec((tm, gw), lambda i, j: (i, j))
    return pl.pallas_call(
        _gated_norm_kernel,
        grid=(m // tm, SSM_GROUPS),
        in_specs=[spec, spec, pl.BlockSpec((1, gw), lambda i, j: (0, j))],
        out_specs=spec,
        out_shape=jax.ShapeDtypeStruct((m, d), BF16),
        compiler_params=_params("parallel", "parallel"),
        name="gated_rms_norm",
    )(y, z, g.reshape(1, d).astype(F32))


def _lane_pad(v):
    return jnp.zeros((1, LANES), F32).at[0, :v.shape[0]].set(v.astype(F32))


def _ffn_half(xp_f, xp_b, xs_f, xs_b, w1, w3, lead, w2, g, b):
    hp, hs = gated_matmul(xp_b, xs_b, w1, w3, None, None, "swiglu", BF16, lead=lead)
    return matmul_postnorm(hp, hs, w2, None, xp_f, xs_f, g, b, 0.5, lead=lead)


def _pad_rows(x, rows):
    return jnp.concatenate([x, jnp.zeros((x.shape[0], rows - x.shape[1]) + x.shape[2:], x.dtype)], axis=1)


def _front_pad_rows(x, rows):
    return jnp.concatenate([jnp.zeros((x.shape[0], rows - x.shape[1]) + x.shape[2:], x.dtype), x], axis=1)


def kernel(x_prompt, x_sample, p_prompt, p_sample, cache_sb_k, cache_sb_v, cache_moba_k, cache_moba_v, state_ssm, state_ssm_conv, state_conf_conv, page_table, ln_g, ln_b, ffn_w1, ffn_w3, ffn_w2, ple_w_proj, ple_w_gate, sb_w_qkv, sb_w_o, ssm_w_in, ssm_conv_w, ssm_conv_b, ssm_dt_bias, ssm_a_log, ssm_d, ssm_norm_g, ssm_w_out, conf_w_pw1, conf_b_pw1, conf_w_dw, conf_b_dw, conf_ln_g, conf_ln_b, conf_w_pw2, conf_b_pw2, moba_w_qkv, moba_w_o):
    bsz, seq, d_model = x_prompt.shape
    db = x_sample.shape[0]
    assert x_sample.shape[1] == 1
    mp = bsz * seq
    depth = ffn_w1.shape[0]
    bf = lambda w: w.astype(BF16)

    xp_f = x_prompt.reshape(mp, d_model)
    xs_f = x_sample.reshape(db, d_model)
    xp_b, xs_b = bf(xp_f), bf(xs_f)
    n_phys = cache_sb_k.shape[1]
    ffn_w2_b = bf(ffn_w2)
    outs ={k: [] for k in ("sb_kp", "sb_vp", "sb_ks", "sb_vs", "ssm_hp", "ssm_hs", "ssm_cp", "ssm_cs",
                            "conf_cp", "conf_cs", "mo_kp", "mo_vp", "mo_ks", "mo_vs")}

    for i in range(depth):
        m, j = i % 4, i // 4
        (xp_f, xp_b), (xs_f, xs_b) = _ffn_half(xp_f, xp_b, xs_f, xs_b, ffn_w1, ffn_w3, (i, 0), ffn_w2_b,
                                               ln_g[i, 0], ln_b[i, 0])
        g1, b1 = ln_g[i, 1], ln_b[i, 1]

        if m == 0 or m == 3:
            w_qkv = bf(sb_w_qkv[j] if m == 0 else moba_w_qkv[j])
            w_o = bf(sb_w_o[j] if m == 0 else moba_w_o[j])
            qkv = lambda xb, which, dt: matmul(xb, w_qkv, None, dt, *((0, Q_DIM), (Q_DIM, KV_DIM),
                                                                     (Q_DIM + KV_DIM, KV_DIM))[which])
            q = qkv(xp_b, 0, BF16).reshape(bsz, seq, Q_DIM)
            k = qkv(xp_b, 1, F32).reshape(bsz, seq, KV_DIM)
            v = qkv(xp_b, 2, F32).reshape(bsz, seq, KV_DIM)
            attend = sb_prompt_attention if m == 0 else moba_prompt_attention
            o = attend(q, k, v).reshape(mp, Q_DIM)
            kp = k.reshape(bsz, seq, N_KV_HEADS, HEAD_DIM)
            vp = v.reshape(bsz, seq, N_KV_HEADS, HEAD_DIM)
            qs, ks, vs = qkv(xs_b, 0, F32), qkv(xs_b, 1, F32), qkv(xs_b, 2, F32)
            if m == 0:
                kc = cache_sb_k[j].reshape(n_phys * PAGE_ROWS, HEAD_DIM)
                vc = cache_sb_v[j].reshape(n_phys * PAGE_ROWS, HEAD_DIM)
                os_ = sb_decode_attention(qs.reshape(db, N_HEADS, HEAD_DIM), kc, vc, page_table)
            else:
                kc = cache_moba_k[j].reshape(n_phys * PAGE_ROWS, HEAD_DIM)
                vc = cache_moba_v[j].reshape(n_phys * PAGE_ROWS, HEAD_DIM)
                qh = qs.reshape(db, N_HEADS, HEAD_DIM)
                kmean = moba_cache_block_means(kc, page_table)
                sel = moba_decode_select(qh, kmean)
                os_ = moba_decode_attention(qh.reshape(db, N_HEADS, 1, HEAD_DIM),
                                            ks.reshape(db, N_KV_HEADS, 1, HEAD_DIM),
                                            vs.reshape(db, N_KV_HEADS, 1, HEAD_DIM),
                                            sel, kc, vc, page_table)
            (xp_f, xp_b), (xs_f, xs_b) = matmul_postnorm(o, bf(os_.reshape(db, Q_DIM)), w_o, None, xp_f, xs_f,
                                                         g1, b1, 1.0)
            ks4 = ks.reshape(db, 1, N_KV_HEADS, HEAD_DIM)
            vs4 = vs.reshape(db, 1, N_KV_HEADS, HEAD_DIM)
            if m == 0:
                outs["sb_kp"].append(kp); outs["sb_vp"].append(vp); outs["sb_ks"].append(ks4); outs["sb_vs"].append(vs4)
            else:
                outs["mo_kp"].append(kp); outs["mo_vp"].append(vp); outs["mo_ks"].append(ks4); outs["mo_vs"].append(vs4)

        elif m == 1:
            n_heads = ssm_dt_bias.shape[1]
            d_inner = n_heads * SSM_HEAD_DIM
            conv_dim = ssm_conv_w.shape[2]
            hpg = n_heads // SSM_GROUPS
            w_in = bf(ssm_w_in[j])
            w_dt_cols = ssm_w_in[j][:, d_inner + conv_dim:]
            w_dt = bf(jnp.zeros((d_model, LANES), F32).at[:, :n_heads].set(w_dt_cols))
            dtb, alog, dskip = _lane_pad(ssm_dt_bias[j]), _lane_pad(ssm_a_log[j]), _lane_pad(ssm_d[j])
            w_out = bf(ssm_w_out[j])
            hp = 8
            z = matmul(xp_b, w_in, None, F32, 0, d_inner)
            xbc_raw = matmul(xp_b, w_in, None, F32, d_inner, conv_dim).reshape(bsz, seq, conv_dim)
            dt_raw = matmul(xp_b, w_dt, None, F32).reshape(bsz, seq, LANES)
            xbc = causal_dwconv(xbc_raw, jnp.zeros((bsz, hp, conv_dim), F32), ssm_conv_w[j], ssm_conv_b[j], "silu")
            h0 = jnp.zeros((bsz, n_heads, SSM_HEAD_DIM, SSM_D_STATE), F32)
            yn, h_last = ssd_prompt(xbc, dt_raw, z.reshape(bsz, seq, d_inner), dtb, alog, dskip, ssm_norm_g[j], h0)
            outs["ssm_hp"].append(h_last)
            outs["ssm_cp"].append(xbc_raw[:, seq - (SSM_CONV - 1):, :])
            zs = matmul(xs_b, w_in, None, F32, 0, d_inner)
            xbc_s = matmul(xs_b, w_in, None, F32, d_inner, conv_dim).reshape(db, 1, conv_dim)
            dt_s = matmul(xs_b, w_dt, None, F32).reshape(db, 1, LANES)
            hist = _front_pad_rows(state_ssm_conv[j], hp)
            xbc_sa = causal_dwconv(_pad_rows(xbc_s, 8), hist, ssm_conv_w[j], ssm_conv_b[j], "silu")[:, :1, :]
            x_col = xbc_sa[:, 0, :d_inner].reshape(db, n_heads, SSM_HEAD_DIM, 1)
            y_col, hs = ssd_step(x_col, xbc_sa[:, :, d_inner:], dt_s, dtb, alog, dskip, state_ssm[j])
            yns = gated_rms_norm(y_col.reshape(db, d_inner), zs, ssm_norm_g[j])
            (xp_f, xp_b), (xs_f, xs_b) = matmul_postnorm(yn.reshape(mp, d_inner), yns, w_out, None, xp_f, xs_f,
                                                         g1, b1, 1.0)
            outs["ssm_hs"].append(hs)
            outs["ssm_cs"].append(jnp.concatenate([state_ssm_conv[j], xbc_s], axis=1)[:, 1:, :])

        else:
            ba, bg = conf_b_pw1[j][:d_model], conf_b_pw1[j][d_model:]
            w_pw2 = bf(conf_w_pw2[j])
            hp = 32
            u, us = gated_matmul(xp_b, xs_b, conf_w_pw1, conf_w_pw1, ba, bg, "glu", F32, lead=(j,),
                                 cola=0, colb=d_model, n=d_model)
            u = u.reshape(bsz, seq, d_model)
            us = us.reshape(db, 1, d_model)
            uc = causal_dwconv(u, jnp.zeros((bsz, hp, d_model), F32), conf_w_dw[j], conf_b_dw[j], "ln_silu",
                               conf_ln_g[j], conf_ln_b[j], BF16)
            outs["conf_cp"].append(u[:, seq - (CONF_WIDTH - 1):, :])
            hist = _front_pad_rows(state_conf_conv[j], hp)
            ucs = causal_dwconv(_pad_rows(us, 8), hist, conf_w_dw[j], conf_b_dw[j], "ln_silu",
                                conf_ln_g[j], conf_ln_b[j], BF16)[:, 0, :]
            (xp_f, xp_b), (xs_f, xs_b) = matmul_postnorm(uc.reshape(mp, d_model), ucs, w_pw2, conf_b_pw2[j],
                                                         xp_f, xs_f, g1, b1, 1.0)
            outs["conf_cs"].append(jnp.concatenate([state_conf_conv[j], us], axis=1)[:, 1:, :])

        (xp_f, xp_b), (xs_f, xs_b) = _ffn_half(xp_f, xp_b, xs_f, xs_b, ffn_w1, ffn_w3, (i, 1), ffn_w2_b,
                                               ln_g[i, 2], ln_b[i, 2])
        wg_, wp_ = bf(ple_w_gate[i]), bf(ple_w_proj[i])
        xp_f, xp_b = ple_add(xp_b, xp_f, p_prompt[i].reshape(mp, -1), wg_, wp_, ln_g[i, 3], ln_b[i, 3])
        xs_f, xs_b = ple_add(xs_b, xs_f, p_sample[i].reshape(db, -1), wg_, wp_, ln_g[i, 3], ln_b[i, 3])

    st = lambda name: jnp.stack(outs[name])
    return (xp_f.reshape(bsz, seq, d_model), xs_f.reshape(db, 1, d_model),
            st("sb_kp"), st("sb_vp"), st("sb_ks"), st("sb_vs"),
            st("ssm_hp"), st("ssm_hs"), st("ssm_cp"), st("ssm_cs"),
            st("conf_cp"), st("conf_cs"),
            st("mo_kp"), st("mo_vp"), st("mo_ks"), st("mo_vs"))
```

```python
import functools

import jax
import jax.numpy as jnp
from jax import lax
from jax.experimental import pallas as pl
from jax.experimental.pallas import tpu as pltpu

F32 = jnp.float32
BF16 = jnp.bfloat16

N_HEADS = 16
HEAD_DIM = 128
N_KV_HEADS = 4
Q_PER_KV = N_HEADS // N_KV_HEADS
Q_DIM = N_HEADS * HEAD_DIM
KV_DIM = N_KV_HEADS * HEAD_DIM
SB_BLOCK = 128
SB_KV_PER_STEP = 2
MOBA_BLOCK = 256
MOBA_TOPK = 3
PAGE_SIZE = 128
SSM_HEAD_DIM = 64
SSM_GROUPS = 8
SSM_D_STATE = 128
SSM_CONV = 4
SSM_CHUNK = 128
CONF_WIDTH = 31
LN_EPS = 1e-5
DEPTH = 4
DN_ALPHA = (2 * DEPTH) ** 0.25
LANES = 128
NEG_BIG = -1e30
DWCONV_ROW_TILE = 64
DWCONV_LANE_CHUNK = 256
SB_LOG_FLOOR = -104.0
VMEM_LIMIT = 56 * 1024 * 1024


def _params(*sem):
    return pltpu.CompilerParams(dimension_semantics=sem, vmem_limit_bytes=VMEM_LIMIT)


def _layer_norm_rows(y, g, b):
    mu = jnp.mean(y, axis=-1, keepdims=True)
    d = y - mu
    var = jnp.mean(d * d, axis=-1, keepdims=True)
    return d * lax.rsqrt(var + LN_EPS) * g + b


def _split3(x):
    hi = x.astype(BF16)
    r1 = x - hi.astype(F32)
    mid = r1.astype(BF16)
    lo = (r1 - mid.astype(F32)).astype(BF16)
    return hi, mid, lo


def _dot(a, b):
    return jnp.dot(a, b, preferred_element_type=F32)


def _dot_nt(a, b):
    return lax.dot_general(a, b, (((1,), (1,)), ((), ())), preferred_element_type=F32)


def _dot01(x, m01):
    hi, mid, lo = _split3(x)
    return _dot(hi, m01) + _dot(mid, m01) + _dot(lo, m01)


def _pick_tile(n, pref):
    for t in pref:
        if n % t == 0:
            return t
    return n


def _mm_kernel(x_ref, xs_ref, w_ref, b_ref, o_ref, os_ref):
    o_ref[...] = (_dot(x_ref[...], w_ref[...]) + b_ref[...]).astype(o_ref.dtype)
    first = pl.program_id(0) == 0

    @pl.when(first)
    def _():
        os_ref[0] = (_dot(xs_ref[...], w_ref[...]) + b_ref[...]).astype(os_ref.dtype)

    @pl.when(jnp.logical_not(first))
    def _():
        os_ref[...] = jnp.zeros_like(os_ref)


def matmul(x, xs, w, bias=None, out_dtype=F32, col0=0, n=None):
    m, k = x.shape
    s = xs.shape[0]
    n = w.shape[1] - col0 if n is None else n
    tm = _pick_tile(m, (1024, 512, 256, 128, 8))
    tn = _pick_tile(n, (512, 256, 128))
    assert col0 % tn == 0 and n % tn == 0
    j0 = col0 // tn
    if bias is None:
        bias = jnp.zeros((n,), F32)
    out, out_s = pl.pallas_call(
        _mm_kernel,
        grid=(m // tm, n // tn),
        in_specs=[pl.BlockSpec((tm, k), lambda i, j: (i, 0)),
                  pl.BlockSpec((s, k), lambda i, j: (0, 0)),
                  pl.BlockSpec((k, tn), lambda i, j: (0, j + j0)),
                  pl.BlockSpec((1, tn), lambda i, j: (0, j))],
        out_specs=[pl.BlockSpec((tm, tn), lambda i, j: (i, j)),
                   pl.BlockSpec((1, s, tn), lambda i, j: (i, 0, j))],
        out_shape=[jax.ShapeDtypeStruct((m, n), out_dtype),
                   jax.ShapeDtypeStruct((m // tm, s, n), F32)],
        compiler_params=_params("arbitrary", "arbitrary"),
        name="mm",
    )(x, xs, w, bias.reshape(1, n).astype(F32))
    return out, out_s[0]


def _gated_mm_kernel(x_ref, xs_ref, wa_ref, wb_ref, ba_ref, bb_ref, o_ref, os_ref, wa_s, wb_s, *, mode):
    first = pl.program_id(1) == 0

    def gated(x):
        a = _dot(x, wa_s[...]) + ba_ref[...]
        b = _dot(x, wb_s[...]) + bb_ref[...]
        return a * jax.nn.sigmoid(a) * b if mode == "swiglu" else a * jax.nn.sigmoid(b)

    @pl.when(first)
    def _():
        wa_s[...] = wa_ref[...].astype(BF16)
        wb_s[...] = wb_ref[...].astype(BF16)
        os_ref[0] = gated(xs_ref[...]).astype(os_ref.dtype)

    @pl.when(jnp.logical_not(first))
    def _():
        os_ref[...] = jnp.zeros_like(os_ref)

    o_ref[...] = gated(x_ref[...]).astype(o_ref.dtype)


def gated_matmul(x, xs, wa, wb, ba, bb, mode, out_dtype, lead=(), cola=0, colb=0, n=None):
    m, k = x.shape
    s = xs.shape[0]
    n = wa.shape[-1] - cola if n is None else n
    wblock = (None,) * len(lead) + (k, None)
    tm = _pick_tile(m, (1024, 512, 256, 128, 8))
    tn = _pick_tile(n, (512, 256, 128))
    assert cola % tn == 0 and colb % tn == 0 and n % tn == 0
    ja, jb = cola // tn, colb // tn
    if ba is None:
        ba = jnp.zeros((n,), F32)
        bb = jnp.zeros((n,), F32)
    out, out_s = pl.pallas_call(
        functools.partial(_gated_mm_kernel, mode=mode),
        grid=(n // tn, m // tm),
        in_specs=[pl.BlockSpec((tm, k), lambda j, i: (i, 0)),
                  pl.BlockSpec((s, k), lambda j, i: (0, 0)),
                  pl.BlockSpec(wblock[:-1] + (tn,), lambda j, i: lead + (0, j + ja)),
                  pl.BlockSpec(wblock[:-1] + (tn,), lambda j, i: lead + (0, j + jb)),
                  pl.BlockSpec((1, tn), lambda j, i: (0, j)),
                  pl.BlockSpec((1, tn), lambda j, i: (0, j))],
        out_specs=[pl.BlockSpec((tm, tn), lambda j, i: (i, j)),
                   pl.BlockSpec((1, s, tn), lambda j, i: (i, 0, j))],
        out_shape=[jax.ShapeDtypeStruct((m, n), out_dtype),
                   jax.ShapeDtypeStruct((m // tm, s, n), out_dtype)],
        scratch_shapes=[pltpu.VMEM((k, tn), BF16), pltpu.VMEM((k, tn), BF16)],
        compiler_params=_params("arbitrary", "arbitrary"),
        name="gated_mm",
    )(x, xs, wa, wb, ba.reshape(1, n).astype(F32), bb.reshape(1, n).astype(F32))
    return out, out_s[0]


def _mm_ln_kernel(a_ref, as_ref, w_ref, bias_ref, res_ref, ress_ref, g_ref, b_ref, of_ref, ob_ref, ofs_ref, obs_ref,
                  accs_ref, *, scale, nk):
    i, k = pl.program_id(0), pl.program_id(1)

    def post_norm(res, acc):
        y = DN_ALPHA * res + scale * (acc + bias_ref[...])
        return _layer_norm_rows(y, g_ref[...], b_ref[...])

    def finish(acc):
        out = post_norm(res_ref[...], acc)
        of_ref[...] = out
        ob_ref[...] = out.astype(BF16)

    @pl.when(i == 0)
    def _():
        part = _dot(as_ref[...], w_ref[...])
        if nk > 1:
            @pl.when(k > 0)
            def _():
                accs_ref[...] += part

            @pl.when(k == 0)
            def _():
                accs_ref[...] = part
        else:
            accs_ref[...] = part

    @pl.when(k == nk - 1)
    def _():
        out = post_norm(ress_ref[...], accs_ref[...])
        first = i == 0
        ofs_ref[0] = jnp.where(first, out, 0.0)
        obs_ref[0] = jnp.where(first, out, 0.0).astype(BF16)

    if nk == 1:
        finish(_dot(a_ref[...], w_ref[...]))
        return

    @pl.when(k == 0)
    def _():
        of_ref[...] = _dot(a_ref[...], w_ref[...])

    @pl.when(jnp.logical_and(k > 0, k < nk - 1))
    def _():
        of_ref[...] += _dot(a_ref[...], w_ref[...])

    @pl.when(k == nk - 1)
    def _():
        finish(of_ref[...] + _dot(a_ref[...], w_ref[...]))


def matmul_postnorm(a, a_s, w, bias, res, res_s, g, b, scale, lead=()):
    m, kdim = a.shape
    s = a_s.shape[0]
    n = w.shape[-1]
    tm = _pick_tile(m, (512, 256, 128, 8))
    tk = _pick_tile(kdim, (2816, 2048, 1408, 1024, 512, 256, 128))
    nk = kdim // tk
    if bias is None:
        bias = jnp.zeros((n,), F32)
    row = lambda v: v.reshape(1, n).astype(F32)
    vec = pl.BlockSpec((1, n), lambda i, k: (0, 0))
    of, ob, ofs, obs = pl.pallas_call(
        functools.partial(_mm_ln_kernel, scale=scale, nk=nk),
        grid=(m // tm, nk),
        in_specs=[pl.BlockSpec((tm, tk), lambda i, k: (i, k)),
                  pl.BlockSpec((s, tk), lambda i, k: (0, k)),
                  pl.BlockSpec((None,) * len(lead) + (tk, n), lambda i, k: lead + (k, 0)),
                  vec,
                  pl.BlockSpec((tm, n), lambda i, k: (i, 0)),
                  pl.BlockSpec((s, n), lambda i, k: (0, 0)),
                  vec, vec],
        out_specs=[pl.BlockSpec((tm, n), lambda i, k: (i, 0)),
                   pl.BlockSpec((tm, n), lambda i, k: (i, 0)),
                   pl.BlockSpec((1, s, n), lambda i, k: (i, 0, 0)),
                   pl.BlockSpec((1, s, n), lambda i, k: (i, 0, 0))],
        out_shape=[jax.ShapeDtypeStruct((m, n), F32), jax.ShapeDtypeStruct((m, n), BF16),
                   jax.ShapeDtypeStruct((m // tm, s, n), F32), jax.ShapeDtypeStruct((m // tm, s, n), BF16)],
        scratch_shapes=[pltpu.VMEM((s, n), F32)],
        compiler_params=_params("arbitrary", "arbitrary"),
        name="mm_postnorm",
    )(a, a_s, w, row(bias), res, res_s, row(g), row(b))
    return (of, ob), (ofs[0], obs[0])


def _ple_kernel(xb_ref, xf_ref, p_ref, wg_ref, wp_ref, g_ref, b_ref, of_ref, ob_ref):
    gate = jax.nn.sigmoid(_dot(xb_ref[...], wg_ref[...]))
    proj = _dot(p_ref[...].astype(BF16), wp_ref[...])
    y = DN_ALPHA * xf_ref[...] + gate * proj
    out = _layer_norm_rows(y, g_ref[...], b_ref[...])
    of_ref[...] = out
    ob_ref[...] = out.astype(BF16)


def ple_add(xb, xf, p, wg, wp, g, b):
    m, d = xf.shape
    pd = p.shape[1]
    tm = _pick_tile(m, (512, 256, 128, 8))
    row = lambda v: v.reshape(1, d).astype(F32)
    return pl.pallas_call(
        _ple_kernel,
        grid=(m // tm,),
        in_specs=[pl.BlockSpec((tm, d), lambda i: (i, 0)),
                  pl.BlockSpec((tm, d), lambda i: (i, 0)),
                  pl.BlockSpec((tm, pd), lambda i: (i, 0)),
                  pl.BlockSpec((d, d), lambda i: (0, 0)),
                  pl.BlockSpec((pd, d), lambda i: (0, 0)),
                  pl.BlockSpec((1, d), lambda i: (0, 0)),
                  pl.BlockSpec((1, d), lambda i: (0, 0))],
        out_specs=[pl.BlockSpec((tm, d), lambda i: (i, 0)),
                   pl.BlockSpec((tm, d), lambda i: (i, 0))],
        out_shape=[jax.ShapeDtypeStruct((m, d), F32), jax.ShapeDtypeStruct((m, d), BF16)],
        compiler_params=_params("parallel"),
        name="ple_add",
    )(xb, xf, p, wg, wp, row(g), row(b))


def _softplus(z):
    return jnp.maximum(z, 0.0) + jnp.log1p(jnp.exp(-jnp.abs(z)))


def _sb_prompt_kernel(q_ref, k_ref, v_ref, o_ref, *, tq, nkv):
    i = pl.program_id(2)
    rows = Q_PER_KV * tq
    scale = HEAD_DIM ** -0.5
    gw = Q_PER_KV * HEAD_DIM
    qs = [jnp.concatenate([q_ref[0, :, h * gw + g * HEAD_DIM:h * gw + (g + 1) * HEAD_DIM] for g in range(Q_PER_KV)],
                          axis=0) for h in range(nkv)]
    t_loc = lax.broadcasted_iota(jnp.int32, (rows, tq), 0) & (tq - 1)
    s_loc = lax.broadcasted_iota(jnp.int32, (rows, tq), 1)
    later = (lax.broadcasted_iota(jnp.int32, (tq, tq), 0) > lax.broadcasted_iota(jnp.int32, (tq, tq), 1)).astype(BF16)

    def block(h, j, run, acc, diagonal):
        start = pl.multiple_of(j * tq, tq)
        kb = k_ref[0, pl.ds(start, tq), h * HEAD_DIM:(h + 1) * HEAD_DIM].astype(BF16)
        vb = v_ref[0, pl.ds(start, tq), h * HEAD_DIM:(h + 1) * HEAD_DIM].astype(BF16)
        z = _dot_nt(qs[h], kb) * scale
        log_keep = -_softplus(z)
        if diagonal:
            valid = s_loc < t_loc
            log_keep = jnp.where(valid, log_keep, 0.0)
        log_w = z + log_keep + _dot01(log_keep, later) + run
        w = jnp.exp(log_w)
        if diagonal:
            w = jnp.where(valid, w, 0.0)
        acc = acc + _dot(w.astype(BF16), vb)
        run = run + jnp.sum(log_keep, axis=1, keepdims=True)
        return run, acc

    run0 = jnp.zeros((rows, 1), F32)
    acc0 = jnp.zeros((rows, HEAD_DIM), F32)
    state = []
    for h in range(nkv):
        state += list(block(h, i, run0, acc0, True))

    def cond(carry):
        live = jnp.max(carry[1])
        for h in range(1, nkv):
            live = jnp.maximum(live, jnp.max(carry[1 + 2 * h]))
        return jnp.logical_and(carry[0] < i, live > SB_LOG_FLOOR)

    def body(carry):
        it = carry[0]
        out = [it + 1]
        for h in range(nkv):
            out += list(block(h, i - 1 - it, carry[1 + 2 * h], carry[2 + 2 * h], False))
        return tuple(out)

    final = lax.while_loop(cond, body, (jnp.int32(0),) + tuple(state))
    for h in range(nkv):
        acc = final[2 + 2 * h]
        for g in range(Q_PER_KV):
            o_ref[0, :, h * gw + g * HEAD_DIM:h * gw + (g + 1) * HEAD_DIM] = acc[g * tq:(g + 1) * tq].astype(o_ref.dtype)


def sb_prompt_attention(q, k, v):
    bsz, L, _ = q.shape
    tq = SB_BLOCK
    nkv = SB_KV_PER_STEP
    gw = nkv * Q_PER_KV * HEAD_DIM
    return pl.pallas_call(
        functools.partial(_sb_prompt_kernel, tq=tq, nkv=nkv),
        grid=(bsz, N_KV_HEADS // nkv, L // tq),
        in_specs=[pl.BlockSpec((1, tq, gw), lambda b, h, i: (b, i, h)),
                  pl.BlockSpec((1, L, nkv * HEAD_DIM), lambda b, h, i: (b, 0, h)),
                  pl.BlockSpec((1, L, nkv * HEAD_DIM), lambda b, h, i: (b, 0, h))],
        out_specs=pl.BlockSpec((1, tq, gw), lambda b, h, i: (b, i, h)),
        out_shape=jax.ShapeDtypeStruct(q.shape, BF16),
        compiler_params=_params("parallel", "parallel", "arbitrary"),
        name="sb_prompt",
    )(q, k, v)


PAGE_ROWS = PAGE_SIZE * N_KV_HEADS
SB_PAGES_PER_STEP = 8
MOBA_BLOCKS_PER_STEP = 4


def _own_lane_mask(kvh_of_row, shape):
    lane = lax.broadcasted_iota(jnp.int32, shape, 1)
    return (lane & (N_KV_HEADS - 1)) == kvh_of_row


def _suffix_sum_keys(x):
    n = x.shape[1]
    lane = lax.broadcasted_iota(jnp.int32, x.shape, 1)
    d = N_KV_HEADS
    while d < n:
        shifted = pltpu.roll(x, n - d, axis=1)
        x = x + jnp.where(lane + d < n, shifted, 0.0)
        d *= 2
    return x


def _sb_decode_kernel(pt_ref, q_ref, *refs, pps, n_steps):
    k_refs, v_refs = refs[:pps], refs[pps:2 * pps]
    o_ref, acc_ref, run_ref = refs[2 * pps:]
    s = pl.program_id(1)

    @pl.when(s == 0)
    def _():
        acc_ref[...] = jnp.zeros_like(acc_ref)
        run_ref[...] = jnp.zeros_like(run_ref)

    q = q_ref[0].astype(BF16)
    kvh = lax.broadcasted_iota(jnp.int32, (N_HEADS, PAGE_ROWS), 0) >> 2
    own = _own_lane_mask(kvh, (N_HEADS, PAGE_ROWS))
    for u in range(pps):
        @pl.when(jnp.max(run_ref[...]) > SB_LOG_FLOOR)
        def _():
            z = _dot_nt(q, k_refs[u][...].astype(BF16)) * (HEAD_DIM ** -0.5)
            log_keep = jnp.where(own, -_softplus(z), 0.0)
            incl = _suffix_sum_keys(log_keep)
            w = jnp.where(own, jnp.exp(z + incl + run_ref[:, 0:1]), 0.0)
            acc_ref[...] += _dot(w.astype(BF16), v_refs[u][...].astype(BF16))
            run_ref[...] += jnp.sum(log_keep, axis=1, keepdims=True)

    @pl.when(s == n_steps - 1)
    def _():
        o_ref[0] = acc_ref[...]


def sb_decode_attention(q, k_rows, v_rows, page_table):
    db, n_pages = page_table.shape
    pps = SB_PAGES_PER_STEP
    assert n_pages % pps == 0
    n_steps = n_pages // pps

    def page_spec(u):
        return pl.BlockSpec((PAGE_ROWS, HEAD_DIM),
                            lambda b, s, pt: (pt[b * n_pages + (n_pages - 1 - (s * pps + u))], 0))

    specs = [page_spec(u) for u in range(pps)]
    return pl.pallas_call(
        functools.partial(_sb_decode_kernel, pps=pps, n_steps=n_steps),
        grid_spec=pltpu.PrefetchScalarGridSpec(
            num_scalar_prefetch=1,
            grid=(db, n_steps),
            in_specs=[pl.BlockSpec((1, N_HEADS, HEAD_DIM), lambda b, s, pt: (b, 0, 0))] + specs + specs,
            out_specs=pl.BlockSpec((1, N_HEADS, HEAD_DIM), lambda b, s, pt: (b, 0, 0)),
            scratch_shapes=[pltpu.VMEM((N_HEADS, HEAD_DIM), F32), pltpu.VMEM((N_HEADS, LANES), F32)]),
        out_shape=jax.ShapeDtypeStruct((db, N_HEADS, HEAD_DIM), F32),
        compiler_params=_params("parallel", "arbitrary"),
        name="sb_decode",
    )(page_table.reshape(-1), q, *([k_rows] * pps), *([v_rows] * pps))


def _top_k_mask(gate, n_valid, k):
    lane = lax.broadcasted_iota(jnp.int32, gate.shape, 1)
    sel = jnp.zeros(gate.shape, F32)
    picks = []
    for r in range(k):
        m = jnp.max(gate, axis=1, keepdims=True)
        idx = jnp.min(jnp.where(gate == m, lane, gate.shape[1] - 1), axis=1, keepdims=True)
        pick = lane == idx
        sel = jnp.maximum(sel, jnp.where(pick, jnp.where(r < n_valid, 1.0, 0.0), 0.0))
        gate = jnp.where(pick, -jnp.inf, gate)
        picks.append(idx)
    return sel, picks


def _moba_prompt_kernel(q_ref, k_ref, v_ref, o_ref, km_ref, *, tq, nb):
    i = pl.program_id(2)
    rows = Q_PER_KV * tq
    scale = HEAD_DIM ** -0.5

    @pl.when(i == 0)
    def _():
        km_ref[...] = jnp.zeros_like(km_ref)
        for jb in range(nb):
            km_ref[jb:jb + 1, :] = jnp.mean(k_ref[0, jb * tq:(jb + 1) * tq, :], axis=0, keepdims=True)

    q = jnp.concatenate([q_ref[0, :, g * HEAD_DIM:(g + 1) * HEAD_DIM] for g in range(Q_PER_KV)], axis=0)
    lane = lax.broadcasted_iota(jnp.int32, (rows, LANES), 1)
    gate = _dot_nt(q, km_ref[...].astype(BF16))
    gate = jnp.where(lane < i, gate, -jnp.inf)
    sel, _ = _top_k_mask(gate, i, MOBA_TOPK)
    sel = sel.astype(BF16)

    def scores(j):
        start = pl.multiple_of(j * tq, tq)
        kb = k_ref[0, pl.ds(start, tq), :].astype(BF16)
        vb = v_ref[0, pl.ds(start, tq), :].astype(BF16)
        return _dot_nt(q, kb) * scale, vb

    s, vb = scores(i)
    t_loc = lax.broadcasted_iota(jnp.int32, (rows, tq), 0) & (tq - 1)
    s_loc = lax.broadcasted_iota(jnp.int32, (rows, tq), 1)
    s = jnp.where(s_loc <= t_loc, s, NEG_BIG)
    m = jnp.max(s, axis=1, keepdims=True)
    p = jnp.exp(s - m)
    l = jnp.sum(p, axis=1, keepdims=True)
    acc = _dot(p.astype(BF16), vb)

    def body(j, carry):
        m, l, acc = carry
        s, vb = scores(j)
        onehot = (lax.broadcasted_iota(jnp.int32, (LANES, tq), 0) == j).astype(BF16)
        chosen = _dot(sel, onehot) > 0.5
        s = jnp.where(chosen, s, NEG_BIG)
        m_new = jnp.maximum(m, jnp.max(s, axis=1, keepdims=True))
        alpha = jnp.exp(m - m_new)
        p = jnp.exp(s - m_new)
        l = alpha * l + jnp.sum(p, axis=1, keepdims=True)
        acc = alpha * acc + _dot(p.astype(BF16), vb)
        return m_new, l, acc

    m, l, acc = lax.fori_loop(0, i, body, (m, l, acc))
    out = acc / l
    for g in range(Q_PER_KV):
        o_ref[0, :, g * HEAD_DIM:(g + 1) * HEAD_DIM] = out[g * tq:(g + 1) * tq].astype(o_ref.dtype)


def moba_prompt_attention(q, k, v):
    bsz, L, _ = q.shape
    tq = MOBA_BLOCK
    nb = L // tq
    assert L % tq == 0 and nb <= LANES
    gw = Q_PER_KV * HEAD_DIM
    return pl.pallas_call(
        functools.partial(_moba_prompt_kernel, tq=tq, nb=nb),
        grid=(bsz, N_KV_HEADS, nb),
        in_specs=[pl.BlockSpec((1, tq, gw), lambda b, h, i: (b, i, h)),
                  pl.BlockSpec((1, L, HEAD_DIM), lambda b, h, i: (b, 0, h)),
                  pl.BlockSpec((1, L, HEAD_DIM), lambda b, h, i: (b, 0, h))],
        out_specs=pl.BlockSpec((1, tq, gw), lambda b, h, i: (b, i, h)),
        out_shape=jax.ShapeDtypeStruct(q.shape, BF16),
        scratch_shapes=[pltpu.VMEM((LANES, HEAD_DIM), F32)],
        compiler_params=_params("parallel", "arbitrary", "arbitrary"),
        name="moba_prompt",
    )(q, k, v)


def _moba_kmean_kernel(pt_ref, *refs, bps, ppb):
    k_refs, o_ref = refs[:bps * ppb], refs[bps * ppb]
    sub = 8
    for j in range(bps):
        tot = jnp.zeros((sub, HEAD_DIM), F32)
        for h in range(ppb):
            k_ref = k_refs[j * ppb + h]
            for c in range(PAGE_ROWS // sub):
                tot = tot + k_ref[c * sub:(c + 1) * sub, :]
        mean = (tot[:N_KV_HEADS] + tot[N_KV_HEADS:]) * (1.0 / MOBA_BLOCK)
        o_ref[0, j * N_KV_HEADS:(j + 1) * N_KV_HEADS, :] = mean


def moba_cache_block_means(k_rows, page_table):
    db, n_pages = page_table.shape
    ppb = MOBA_BLOCK // PAGE_SIZE
    nblk = n_pages // ppb
    bps = MOBA_BLOCKS_PER_STEP
    assert nblk % bps == 0

    def page_spec(u):
        return pl.BlockSpec((PAGE_ROWS, HEAD_DIM), lambda b, j, pt: (pt[b * n_pages + j * (bps * ppb) + u], 0))

    return pl.pallas_call(
        functools.partial(_moba_kmean_kernel, bps=bps, ppb=ppb),
        grid_spec=pltpu.PrefetchScalarGridSpec(
            num_scalar_prefetch=1,
            grid=(db, nblk // bps),
            in_specs=[page_spec(u) for u in range(bps * ppb)],
            out_specs=pl.BlockSpec((1, bps * N_KV_HEADS, HEAD_DIM), lambda b, j, pt: (b, j, 0))),
        out_shape=jax.ShapeDtypeStruct((db, nblk * N_KV_HEADS, HEAD_DIM), F32),
        compiler_params=_params("parallel", "arbitrary"),
        name="moba_kmean",
    )(page_table.reshape(-1), *([k_rows] * (bps * ppb)))


def _moba_gate_kernel(q_ref, km_ref, o_ref, *, nblk):
    gate = _dot_nt(q_ref[0].astype(BF16), km_ref[0].astype(BF16))
    kvh = lax.broadcasted_iota(jnp.int32, gate.shape, 0) >> 2
    gate = jnp.where(_own_lane_mask(kvh, gate.shape), gate, -jnp.inf)
    _, picks = _top_k_mask(gate, nblk, MOBA_TOPK)
    lane = lax.broadcasted_iota(jnp.int32, (N_HEADS, LANES), 1)
    out = jnp.zeros((N_HEADS, LANES), jnp.int32)
    for r, idx in enumerate(picks):
        out = jnp.where(lane == r, idx >> 2, out)
    o_ref[0] = out


def moba_decode_select(q, kmean):
    db, rows, _ = kmean.shape
    nblk = rows // N_KV_HEADS
    out = pl.pallas_call(
        functools.partial(_moba_gate_kernel, nblk=nblk),
        grid=(db,),
        in_specs=[pl.BlockSpec((1, N_HEADS, HEAD_DIM), lambda b: (b, 0, 0)),
                  pl.BlockSpec((1, rows, HEAD_DIM), lambda b: (b, 0, 0))],
        out_specs=pl.BlockSpec((1, N_HEADS, LANES), lambda b: (b, 0, 0)),
        out_shape=jax.ShapeDtypeStruct((db, N_HEADS, LANES), jnp.int32),
        compiler_params=_params("parallel"),
        name="moba_gate",
    )(q, kmean)
    return out[:, :, :MOBA_TOPK]


def _moba_decode_kernel(sel_ref, pt_ref, q_ref, kn_ref, vn_ref, *refs, n_sel):
    k_refs, v_refs, o_ref = refs[:n_sel], refs[n_sel:2 * n_sel], refs[2 * n_sel]
    scale = HEAD_DIM ** -0.5
    q = jnp.broadcast_to(q_ref[0, 0], (8, HEAD_DIM))
    own = _own_lane_mask(pl.program_id(1) >> 2, (8, PAGE_ROWS))
    m = jnp.sum(q * kn_ref[0, 0], axis=1, keepdims=True) * scale
    l = jnp.ones((8, 1), F32)
    acc = jnp.broadcast_to(vn_ref[0, 0], (8, HEAD_DIM))
    qb = q.astype(BF16)
    for u in range(n_sel):
        s = jnp.where(own, _dot_nt(qb, k_refs[u][...].astype(BF16)) * scale, NEG_BIG)
        m_new = jnp.maximum(m, jnp.max(s, axis=1, keepdims=True))
        alpha = jnp.exp(m - m_new)
        p = jnp.exp(s - m_new)
        l = alpha * l + jnp.sum(p, axis=1, keepdims=True)
        acc = alpha * acc + _dot(p.astype(BF16), v_refs[u][...].astype(BF16))
        m = m_new
    o_ref[0, 0] = acc / l


def moba_decode_attention(q, k_new, v_new, sel, k_rows, v_rows, page_table):
    db, n_pages = page_table.shape
    ppb = MOBA_BLOCK // PAGE_SIZE
    n_sel = MOBA_TOPK * ppb

    def page_spec(u):
        def index(b, h, sel_r, pt):
            blk = sel_r[(b * N_HEADS + h) * MOBA_TOPK + u // ppb]
            return (pt[b * n_pages + blk * ppb + u % ppb], 0)
        return pl.BlockSpec((PAGE_ROWS, HEAD_DIM), index)

    specs = [page_spec(u) for u in range(n_sel)]
    new_spec = pl.BlockSpec((1, 1, 1, HEAD_DIM), lambda b, h, sel_r, pt: (b, h // Q_PER_KV, 0, 0))
    out = pl.pallas_call(
        functools.partial(_moba_decode_kernel, n_sel=n_sel),
        grid_spec=pltpu.PrefetchScalarGridSpec(
            num_scalar_prefetch=2,
            grid=(db, N_HEADS),
            in_specs=[pl.BlockSpec((1, 1, 1, HEAD_DIM), lambda b, h, sel_r, pt: (b, h, 0, 0)),
                      new_spec, new_spec] + specs + specs,
            out_specs=pl.BlockSpec((1, 1, 8, HEAD_DIM), lambda b, h, sel_r, pt: (b, h, 0, 0))),
        out_shape=jax.ShapeDtypeStruct((db, N_HEADS, 8, HEAD_DIM), F32),
        compiler_params=_params("parallel", "parallel"),
        name="moba_decode",
    )(sel.reshape(-1), page_table.reshape(-1), q, k_new, v_new, *([k_rows] * n_sel), *([v_rows] * n_sel))
    return out[:, :, 0, :]


def _dwconv_kernel(x_ref, hist_ref, w_ref, b_ref, g_ref, beta_ref, o_ref, pad_ref, *shift_refs, width, tl, hp, post):
    t = pl.program_id(2)

    @pl.when(t == 0)
    def _():
        pad_ref[0:hp, :] = hist_ref[0]

    @pl.when(t > 0)
    def _():
        pad_ref[0:hp, :] = pad_ref[tl:tl + hp, :]

    pad_ref[hp:hp + tl, :] = x_ref[0]

    def finish(acc):
        if post == "ln_silu":
            acc = _layer_norm_rows(acc, g_ref[...], beta_ref[...])
        return (acc * jax.nn.sigmoid(acc)).astype(o_ref.dtype)

    if not shift_refs:
        base = hp - (width - 1)
        acc = jnp.zeros(o_ref.shape[1:], F32) + b_ref[...]
        for k in range(width):
            acc = acc + pad_ref[base + k:base + k + tl, :] * w_ref[k:k + 1, :]
        o_ref[0] = finish(acc)
        return

    sh_ref, wb_ref = shift_refs
    halo = 8 * ((width - 1) // 8)
    for r in range(1, 8):
        sh_ref[r - 1] = pad_ref[hp - halo - r:hp + tl - r, :]
    rt = min(DWCONV_ROW_TILE, tl)
    tc = o_ref.shape[2]
    lc = DWCONV_LANE_CHUNK if tc % DWCONV_LANE_CHUNK == 0 else tc
    for k in range(width):
        wb_ref[k] = jnp.broadcast_to(w_ref[k:k + 1, :], (8, tc))

    def row_tile(row0):
        parts = []
        for c0 in range(0, tc, lc):
            acc = jnp.zeros((rt // 8, 8, lc), F32) + b_ref[:, c0:c0 + lc]
            for r in range(8):
                for a in range((width - 1 - r) // 8 + 1):
                    k = width - 1 - (8 * a + r)
                    if r == 0:
                        src = pad_ref[pl.ds(row0 + (hp - 8 * a), rt), c0:c0 + lc]
                    else:
                        src = sh_ref[r - 1, pl.ds(row0 + (halo - 8 * a), rt), c0:c0 + lc]
                    acc = acc + src.reshape(rt // 8, 8, lc) * wb_ref[k, :, c0:c0 + lc]
            parts.append(acc.reshape(rt, lc))
        o_ref[0, pl.ds(row0, rt), :] = finish(jnp.concatenate(parts, axis=1))

    if tl == rt:
        row_tile(0)
    else:
        def body(it, carry):
            row_tile(pl.multiple_of(it * rt, rt))
            return carry

        lax.fori_loop(0, tl // rt, body, 0)


def causal_dwconv(x, hist, w, b, post, ln_g=None, ln_b=None, out_dtype=F32):
    bsz, L, C = x.shape
    width = w.shape[0]
    hp = hist.shape[1]
    assert hp % 8 == 0 and hp >= 8 * ((width - 1) // 8) + min(7, width - 1)
    tc = C if post == "ln_silu" else _pick_tile(C, (1024, 512, 256, 128))
    tl = _pick_tile(L, (256, 128)) if L >= hp else L
    assert tl >= hp or tl == L
    wp = jnp.zeros((-(-width // 8) * 8, C), F32).at[:width].set(w.astype(F32))
    if ln_g is None:
        ln_g = jnp.ones((C,), F32)
        ln_b = jnp.zeros((C,), F32)
    row = lambda v: v.reshape(1, C).astype(F32)
    vec_spec = pl.BlockSpec((1, tc), lambda bb, c, t: (0, c))
    return pl.pallas_call(
        functools.partial(_dwconv_kernel, width=width, tl=tl, hp=hp, post=post),
        grid=(bsz, C // tc, L // tl),
        in_specs=[pl.BlockSpec((1, tl, tc), lambda bb, c, t: (bb, t, c)),
                  pl.BlockSpec((1, hp, tc), lambda bb, c, t: (bb, 0, c)),
                  pl.BlockSpec((wp.shape[0], tc), lambda bb, c, t: (0, c)),
                  vec_spec, vec_spec, vec_spec],
        out_specs=pl.BlockSpec((1, tl, tc), lambda bb, c, t: (bb, t, c)),
        out_shape=jax.ShapeDtypeStruct((bsz, L, C), out_dtype),
        scratch_shapes=[pltpu.VMEM((hp + tl, tc), F32)] + (
            [pltpu.VMEM((7, 8 * ((width - 1) // 8) + tl, tc), F32),
             pltpu.VMEM((width, 8, tc), F32)] if width > 8 else []),
        compiler_params=_params("parallel", "parallel", "arbitrary"),
        name="dwconv_" + post,
    )(x, hist, wp, row(b), row(ln_g), row(ln_b))


def _ssd_prompt_kernel(x_ref, bm_ref, cm_ref, dt_ref, z_ref, dtb_ref, alog_ref, d_ref, ng_ref, h0_ref,
                       yn_ref, h_ref, y_buf, *, chunk, hpg):
    c = pl.program_id(1)
    P, N = SSM_HEAD_DIM, SSM_D_STATE
    gw = hpg * P

    @pl.when(c == 0)
    def _():
        h_ref[...] = h0_ref[...]

    li = lax.broadcasted_iota(jnp.int32, (chunk, chunk), 0)
    si = lax.broadcasted_iota(jnp.int32, (chunk, chunk), 1)
    causal = li >= si
    lower = causal.astype(BF16)
    dt = _softplus(dt_ref[0] + dtb_ref[...])
    a = -jnp.exp(alog_ref[...])
    hi, mid, lo = _split3(dt * a)
    cum = _dot(lower, hi) + _dot(lower, mid) + _dot(lower, lo)
    cum_t = cum.T
    dt_t = dt.T
    for g in range(SSM_GROUPS):
        x = x_ref[0, :, g * gw:(g + 1) * gw]
        cmb = cm_ref[0, :, g * N:(g + 1) * N].astype(BF16)
        bmb = bm_ref[0, :, g * N:(g + 1) * N].astype(BF16)
        cb = _dot_nt(cmb, bmb)
        x_t = x.T
        for r in range(hpg):
            head = g * hpg + r
            cum_col = cum[:, head:head + 1]
            cum_row = cum_t[head:head + 1, :]
            dt_row = dt_t[head:head + 1, :]
            total = cum_t[head:head + 1, chunk - 1:chunk]
            xh = x[:, r * P:(r + 1) * P]
            decay = jnp.exp(jnp.where(causal, cum_col - cum_row, -jnp.inf))
            mix = (cb * decay * dt_row).astype(BF16)
            y = _dot(mix, xh.astype(BF16))
            h_old = h_ref[0, head]
            y = y + _dot_nt(cmb, h_old.astype(BF16)) * jnp.exp(cum_col)
            y_buf[:, r * P:(r + 1) * P] = y + d_ref[:, head:head + 1] * xh
            to_end = jnp.exp(total - cum_row) * dt_row
            s_chunk = _dot((x_t[r * P:(r + 1) * P, :] * to_end).astype(BF16), bmb)
            h_ref[0, head] = jnp.exp(total) * h_old + s_chunk
        z = z_ref[0, :, g * gw:(g + 1) * gw]
        hg = y_buf[...] * (z * jax.nn.sigmoid(z))
        hg = hg * lax.rsqrt(jnp.mean(hg * hg, axis=-1, keepdims=True) + LN_EPS)
        yn_ref[0, :, g * gw:(g + 1) * gw] = (hg * ng_ref[:, g * gw:(g + 1) * gw]).astype(yn_ref.dtype)


def ssd_prompt(xbc, dt_raw, z, dt_bias_g, a_log_g, d_g, norm_g, h0):
    bsz, L, _ = xbc.shape
    n_heads = h0.shape[1]
    G, N, P = SSM_GROUPS, SSM_D_STATE, SSM_HEAD_DIM
    hpg = n_heads // G
    d_inner = n_heads * P
    gn = G * N
    chunk = SSM_CHUNK
    assert d_inner % gn == 0 and (hpg * P) % LANES == 0 and n_heads <= LANES
    b_off = d_inner // gn
    vec_spec = pl.BlockSpec((1, LANES), lambda b, c: (0, 0))
    state_spec = pl.BlockSpec((1, n_heads, P, N), lambda b, c: (b, 0, 0, 0))
    return pl.pallas_call(
        functools.partial(_ssd_prompt_kernel, chunk=chunk, hpg=hpg),
        grid=(bsz, L // chunk),
        in_specs=[pl.BlockSpec((1, chunk, d_inner), lambda b, c: (b, c, 0)),
                  pl.BlockSpec((1, chunk, gn), lambda b, c: (b, c, b_off)),
                  pl.BlockSpec((1, chunk, gn), lambda b, c: (b, c, b_off + 1)),
                  pl.BlockSpec((1, chunk, LANES), lambda b, c: (b, c, 0)),
                  pl.BlockSpec((1, chunk, d_inner), lambda b, c: (b, c, 0)),
                  vec_spec, vec_spec, vec_spec,
                  pl.BlockSpec((1, d_inner), lambda b, c: (0, 0)),
                  state_spec],
        out_specs=[pl.BlockSpec((1, chunk, d_inner), lambda b, c: (b, c, 0)), state_spec],
        out_shape=[jax.ShapeDtypeStruct((bsz, L, d_inner), BF16),
                   jax.ShapeDtypeStruct((bsz, n_heads, P, N), F32)],
        scratch_shapes=[pltpu.VMEM((chunk, hpg * P), F32)],
        compiler_params=_params("parallel", "arbitrary"),
        name="ssd_prompt",
    )(xbc, xbc, xbc, dt_raw, z, dt_bias_g, a_log_g, d_g, norm_g.reshape(1, d_inner).astype(F32), h0)


def _ssd_step_kernel(xc_ref, bc_ref, dt_ref, dtb_ref, alog_ref, d_ref, h0_ref, y_ref, h_ref, *, n_heads):
    N = SSM_D_STATE
    hpg = n_heads // SSM_GROUPS
    dt = _softplus(dt_ref[0] + dtb_ref[...])
    da = jnp.exp(dt * (-jnp.exp(alog_ref[...])))
    for h in range(n_heads):
        g = h // hpg
        bm = bc_ref[0, :, g * N:(g + 1) * N]
        cm = bc_ref[0, :, (SSM_GROUPS + g) * N:(SSM_GROUPS + g + 1) * N]
        xcol = xc_ref[0, h]
        h_new = da[:, h:h + 1] * h0_ref[0, h] + xcol * (dt[:, h:h + 1] * bm)
        h_ref[0, h] = h_new
        y_ref[0, h] = jnp.sum(h_new * cm, axis=1, keepdims=True) + d_ref[:, h:h + 1] * xcol


def ssd_step(x_col, bc, dt_raw, dt_bias, a_log, d_skip, h0):
    db, n_heads, P, _ = x_col.shape
    N = SSM_D_STATE
    vec_spec = pl.BlockSpec((1, LANES), lambda b: (0, 0))
    state_spec = pl.BlockSpec((1, n_heads, P, N), lambda b: (b, 0, 0, 0))
    col_spec = pl.BlockSpec((1, n_heads, P, 1), lambda b: (b, 0, 0, 0))
    return pl.pallas_call(
        functools.partial(_ssd_step_kernel, n_heads=n_heads),
        grid=(db,),
        in_specs=[col_spec,
                  pl.BlockSpec((1, 1, bc.shape[2]), lambda b: (b, 0, 0)),
                  pl.BlockSpec((1, 1, LANES), lambda b: (b, 0, 0)),
                  vec_spec, vec_spec, vec_spec, state_spec],
        out_specs=[col_spec, state_spec],
        out_shape=[jax.ShapeDtypeStruct((db, n_heads, P, 1), F32),
                   jax.ShapeDtypeStruct((db, n_heads, P, N), F32)],
        compiler_params=_params("parallel"),
        name="ssd_step",
    )(x_col, bc, dt_raw, dt_bias, a_log, d_skip, h0)


def _gated_norm_kernel(y_ref, z_ref, g_ref, o_ref):
    z = z_ref[...]
    h = y_ref[...] * (z * jax.nn.sigmoid(z))
    h = h * lax.rsqrt(jnp.mean(h * h, axis=-1, keepdims=True) + LN_EPS)
    o_ref[...] = (h * g_ref[...]).astype(o_ref.dtype)


def gated_rms_norm(y, z, g):
    m, d = y.shape
    gw = d // SSM_GROUPS
    tm = _pick_tile(m, (512, 256, 128, 8))
    spec = pl.BlockSpec((tm, gw), lambda i, j: (i, j))
    return pl.pallas_call(
        _gated_norm_kernel,
        grid=(m // tm, SSM_GROUPS),
        in_specs=[spec, spec, pl.BlockSpec((1, gw), lambda i, j: (0, j))],
        out_specs=spec,
        out_shape=jax.ShapeDtypeStruct((m, d), BF16),
        compiler_params=_params("parallel", "parallel"),
        name="gated_rms_norm",
    )(y, z, g.reshape(1, d).astype(F32))


def _lane_pad(v):
    return jnp.zeros((1, LANES), F32).at[0, :v.shape[0]].set(v.astype(F32))


def _ffn_half(xp_f, xp_b, xs_f, xs_b, w1, w3, lead, w2, g, b):
    hp, hs = gated_matmul(xp_b, xs_b, w1, w3, None, None, "swiglu", BF16, lead=lead)
    return matmul_postnorm(hp, hs, w2, None, xp_f, xs_f, g, b, 0.5, lead=lead)


def _pad_rows(x, rows):
    return jnp.concatenate([x, jnp.zeros((x.shape[0], rows - x.shape[1]) + x.shape[2:], x.dtype)], axis=1)


def _front_pad_rows(x, rows):
    return jnp.concatenate([jnp.zeros((x.shape[0], rows - x.shape[1]) + x.shape[2:], x.dtype), x], axis=1)


def kernel(x_prompt, x_sample, p_prompt, p_sample, cache_sb_k, cache_sb_v, cache_moba_k, cache_moba_v, state_ssm, state_ssm_conv, state_conf_conv, page_table, ln_g, ln_b, ffn_w1, ffn_w3, ffn_w2, ple_w_proj, ple_w_gate, sb_w_qkv, sb_w_o, ssm_w_in, ssm_conv_w, ssm_conv_b, ssm_dt_bias, ssm_a_log, ssm_d, ssm_norm_g, ssm_w_out, conf_w_pw1, conf_b_pw1, conf_w_dw, conf_b_dw, conf_ln_g, conf_ln_b, conf_w_pw2, conf_b_pw2, moba_w_qkv, moba_w_o):
    bsz, seq, d_model = x_prompt.shape
    db = x_sample.shape[0]
    assert x_sample.shape[1] == 1
    mp = bsz * seq
    depth = ffn_w1.shape[0]
    bf = lambda w: w.astype(BF16)

    xp_f = x_prompt.reshape(mp, d_model)
    xs_f = x_sample.reshape(db, d_model)
    xp_b, xs_b = bf(xp_f), bf(xs_f)
    n_phys = cache_sb_k.shape[1]
    ffn_w2_b = bf(ffn_w2)
    outs ={k: [] for k in ("sb_kp", "sb_vp", "sb_ks", "sb_vs", "ssm_hp", "ssm_hs", "ssm_cp", "ssm_cs",
                            "conf_cp", "conf_cs", "mo_kp", "mo_vp", "mo_ks", "mo_vs")}

    for i in range(depth):
        m, j = i % 4, i // 4
        (xp_f, xp_b), (xs_f, xs_b) = _ffn_half(xp_f, xp_b, xs_f, xs_b, ffn_w1, ffn_w3, (i, 0), ffn_w2_b,
                                               ln_g[i, 0], ln_b[i, 0])
        g1, b1 = ln_g[i, 1], ln_b[i, 1]

        if m == 0 or m == 3:
            w_qkv = bf(sb_w_qkv[j] if m == 0 else moba_w_qkv[j])
            w_o = bf(sb_w_o[j] if m == 0 else moba_w_o[j])
            q, qs = matmul(xp_b, xs_b, w_qkv, None, BF16, 0, Q_DIM)
            k, ks = matmul(xp_b, xs_b, w_qkv, None, F32, Q_DIM, KV_DIM)
            v, vs = matmul(xp_b, xs_b, w_qkv, None, F32, Q_DIM + KV_DIM, KV_DIM)
            q = q.reshape(bsz, seq, Q_DIM)
            k = k.reshape(bsz, seq, KV_DIM)
            v = v.reshape(bsz, seq, KV_DIM)
            attend = sb_prompt_attention if m == 0 else moba_prompt_attention
            o = attend(q, k, v).reshape(mp, Q_DIM)
            kp = k.reshape(bsz, seq, N_KV_HEADS, HEAD_DIM)
            vp = v.reshape(bsz, seq, N_KV_HEADS, HEAD_DIM)
            if m == 0:
                kc = cache_sb_k[j].reshape(n_phys * PAGE_ROWS, HEAD_DIM)
                vc = cache_sb_v[j].reshape(n_phys * PAGE_ROWS, HEAD_DIM)
                os_ = sb_decode_attention(qs.reshape(db, N_HEADS, HEAD_DIM), kc, vc, page_table)
            else:
                kc = cache_moba_k[j].reshape(n_phys * PAGE_ROWS, HEAD_DIM)
                vc = cache_moba_v[j].reshape(n_phys * PAGE_ROWS, HEAD_DIM)
                qh = qs.reshape(db, N_HEADS, HEAD_DIM)
                kmean = moba_cache_block_means(kc, page_table)
                sel = moba_decode_select(qh, kmean)
                os_ = moba_decode_attention(qh.reshape(db, N_HEADS, 1, HEAD_DIM),
                                            ks.reshape(db, N_KV_HEADS, 1, HEAD_DIM),
                                            vs.reshape(db, N_KV_HEADS, 1, HEAD_DIM),
                                            sel, kc, vc, page_table)
            (xp_f, xp_b), (xs_f, xs_b) = matmul_postnorm(o, bf(os_.reshape(db, Q_DIM)), w_o, None, xp_f, xs_f,
                                                         g1, b1, 1.0)
            ks4 = ks.reshape(db, 1, N_KV_HEADS, HEAD_DIM)
            vs4 = vs.reshape(db, 1, N_KV_HEADS, HEAD_DIM)
            if m == 0:
                outs["sb_kp"].append(kp); outs["sb_vp"].append(vp); outs["sb_ks"].append(ks4); outs["sb_vs"].append(vs4)
            else:
                outs["mo_kp"].append(kp); outs["mo_vp"].append(vp); outs["mo_ks"].append(ks4); outs["mo_vs"].append(vs4)

        elif m == 1:
            n_heads = ssm_dt_bias.shape[1]
            d_inner = n_heads * SSM_HEAD_DIM
            conv_dim = ssm_conv_w.shape[2]
            hpg = n_heads // SSM_GROUPS
            w_in = bf(ssm_w_in[j])
            w_dt_cols = ssm_w_in[j][:, d_inner + conv_dim:]
            w_dt = bf(jnp.zeros((d_model, LANES), F32).at[:, :n_heads].set(w_dt_cols))
            dtb, alog, dskip = _lane_pad(ssm_dt_bias[j]), _lane_pad(ssm_a_log[j]), _lane_pad(ssm_d[j])
            w_out = bf(ssm_w_out[j])
            hp = 8
            z, zs = matmul(xp_b, xs_b, w_in, None, F32, 0, d_inner)
            xbc_raw, xbc_s = matmul(xp_b, xs_b, w_in, None, F32, d_inner, conv_dim)
            dt_raw, dt_s = matmul(xp_b, xs_b, w_dt, None, F32)
            xbc_raw = xbc_raw.reshape(bsz, seq, conv_dim)
            dt_raw = dt_raw.reshape(bsz, seq, LANES)
            xbc = causal_dwconv(xbc_raw, jnp.zeros((bsz, hp, conv_dim), F32), ssm_conv_w[j], ssm_conv_b[j], "silu")
            h0 = jnp.zeros((bsz, n_heads, SSM_HEAD_DIM, SSM_D_STATE), F32)
            yn, h_last = ssd_prompt(xbc, dt_raw, z.reshape(bsz, seq, d_inner), dtb, alog, dskip, ssm_norm_g[j], h0)
            outs["ssm_hp"].append(h_last)
            outs["ssm_cp"].append(xbc_raw[:, seq - (SSM_CONV - 1):, :])
            xbc_s = xbc_s.reshape(db, 1, conv_dim)
            dt_s = dt_s.reshape(db, 1, LANES)
            hist = _front_pad_rows(state_ssm_conv[j], hp)
            xbc_sa = causal_dwconv(_pad_rows(xbc_s, 8), hist, ssm_conv_w[j], ssm_conv_b[j], "silu")[:, :1, :]
            x_col = xbc_sa[:, 0, :d_inner].reshape(db, n_heads, SSM_HEAD_DIM, 1)
            y_col, hs = ssd_step(x_col, xbc_sa[:, :, d_inner:], dt_s, dtb, alog, dskip, state_ssm[j])
            yns = gated_rms_norm(y_col.reshape(db, d_inner), zs, ssm_norm_g[j])
            (xp_f, xp_b), (xs_f, xs_b) = matmul_postnorm(yn.reshape(mp, d_inner), yns, w_out, None, xp_f, xs_f,
                                                         g1, b1, 1.0)
            outs["ssm_hs"].append(hs)
            outs["ssm_cs"].append(jnp.concatenate([state_ssm_conv[j], xbc_s], axis=1)[:, 1:, :])

        else:
            ba, bg = conf_b_pw1[j][:d_model], conf_b_pw1[j][d_model:]
            w_pw2 = bf(conf_w_pw2[j])
            hp = 32
            u, us = gated_matmul(xp_b, xs_b, conf_w_pw1, conf_w_pw1, ba, bg, "glu", F32, lead=(j,),
                                 cola=0, colb=d_model, n=d_model)
            u = u.reshape(bsz, seq, d_model)
            us = us.reshape(db, 1, d_model)
            uc = causal_dwconv(u, jnp.zeros((bsz, hp, d_model), F32), conf_w_dw[j], conf_b_dw[j], "ln_silu",
                               conf_ln_g[j], conf_ln_b[j], BF16)
            outs["conf_cp"].append(u[:, seq - (CONF_WIDTH - 1):, :])
            hist = _front_pad_rows(state_conf_conv[j], hp)
            ucs = causal_dwconv(_pad_rows(us, 8), hist, conf_w_dw[j], conf_b_dw[j], "ln_silu",
                                conf_ln_g[j], conf_ln_b[j], BF16)[:, 0, :]
            (xp_f, xp_b), (xs_f, xs_b) = matmul_postnorm(uc.reshape(mp, d_model), ucs, w_pw2, conf_b_pw2[j],
                                                         xp_f, xs_f, g1, b1, 1.0)
            outs["conf_cs"].append(jnp.concatenate([state_conf_conv[j], us], axis=1)[:, 1:, :])

        (xp_f, xp_b), (xs_f, xs_b) = _ffn_half(xp_f, xp_b, xs_f, xs_b, ffn_w1, ffn_w3, (i, 1), ffn_w2_b,
                                               ln_g[i, 2], ln_b[i, 2])
        wg_, wp_ = bf(ple_w_gate[i]), bf(ple_w_proj[i])
        xp_f, xp_b = ple_add(xp_b, xp_f, p_prompt[i].reshape(mp, -1), wg_, wp_, ln_g[i, 3], ln_b[i, 3])
        xs_f, xs_b = ple_add(xs_b, xs_f, p_sample[i].reshape(db, -1), wg_, wp_, ln_g[i, 3], ln_b[i, 3])

    st = lambda name: jnp.stack(outs[name])
    return (xp_f.reshape(bsz, seq, d_model), xs_f.reshape(db, 1, d_model),
            st("sb_kp"), st("sb_vp"), st("sb_ks"), st("sb_vs"),
            st("ssm_hp"), st("ssm_hs"), st("ssm_cp"), st("ssm_cs"),
            st("conf_cp"), st("conf_cs"),
            st("mo_kp"), st("mo_vp"), st("mo_ks"), st("mo_vs"))
```

```python
import functools

import jax
import jax.numpy as jnp
from jax import lax
from jax.experimental import pallas as pl
from jax.experimental.pallas import tpu as pltpu

F32 = jnp.float32
BF16 = jnp.bfloat16

N_HEADS = 16
HEAD_DIM = 128
N_KV_HEADS = 4
Q_PER_KV = N_HEADS // N_KV_HEADS
Q_DIM = N_HEADS * HEAD_DIM
KV_DIM = N_KV_HEADS * HEAD_DIM
SB_BLOCK = 128
SB_KV_PER_STEP = 2
MOBA_BLOCK = 256
MOBA_TOPK = 3
PAGE_SIZE = 128
SSM_HEAD_DIM = 64
SSM_GROUPS = 8
SSM_D_STATE = 128
SSM_CONV = 4
SSM_CHUNK = 128
CONF_WIDTH = 31
LN_EPS = 1e-5
DEPTH = 4
DN_ALPHA = (2 * DEPTH) ** 0.25
LANES = 128
NEG_BIG = -1e30
DWCONV_ROW_TILE = 64
DWCONV_LANE_CHUNK = 256
SB_LOG_FLOOR = -104.0
VMEM_LIMIT = 56 * 1024 * 1024


def _params(*sem):
    return pltpu.CompilerParams(dimension_semantics=sem, vmem_limit_bytes=VMEM_LIMIT)


def _layer_norm_rows(y, g, b):
    mu = jnp.mean(y, axis=-1, keepdims=True)
    d = y - mu
    var = jnp.mean(d * d, axis=-1, keepdims=True)
    return d * lax.rsqrt(var + LN_EPS) * g + b


def _split3(x):
    hi = x.astype(BF16)
    r1 = x - hi.astype(F32)
    mid = r1.astype(BF16)
    lo = (r1 - mid.astype(F32)).astype(BF16)
    return hi, mid, lo


def _dot(a, b):
    return jnp.dot(a, b, preferred_element_type=F32)


def _dot_nt(a, b):
    return lax.dot_general(a, b, (((1,), (1,)), ((), ())), preferred_element_type=F32)


def _dot01(x, m01):
    hi, mid, lo = _split3(x)
    return _dot(hi, m01) + _dot(mid, m01) + _dot(lo, m01)


def _pick_tile(n, pref):
    for t in pref:
        if n % t == 0:
            return t
    return n


def _mm_kernel(x_ref, xs_ref, w_ref, b_ref, o_ref, os_ref):
    o_ref[...] = (_dot(x_ref[...], w_ref[...]) + b_ref[...]).astype(o_ref.dtype)
    first = pl.program_id(0) == 0

    @pl.when(first)
    def _():
        os_ref[0] = (_dot(xs_ref[...], w_ref[...]) + b_ref[...]).astype(os_ref.dtype)

    @pl.when(jnp.logical_not(first))
    def _():
        os_ref[...] = jnp.zeros_like(os_ref)


def matmul(x, xs, w, bias=None, out_dtype=F32, col0=0, n=None):
    m, k = x.shape
    s = xs.shape[0]
    n = w.shape[1] - col0 if n is None else n
    tm = _pick_tile(m, (1024, 512, 256, 128, 8))
    tn = _pick_tile(n, (512, 256, 128))
    assert col0 % tn == 0 and n % tn == 0
    j0 = col0 // tn
    if bias is None:
        bias = jnp.zeros((n,), F32)
    out, out_s = pl.pallas_call(
        _mm_kernel,
        grid=(m // tm, n // tn),
        in_specs=[pl.BlockSpec((tm, k), lambda i, j: (i, 0)),
                  pl.BlockSpec((s, k), lambda i, j: (0, 0)),
                  pl.BlockSpec((k, tn), lambda i, j: (0, j + j0)),
                  pl.BlockSpec((1, tn), lambda i, j: (0, j))],
        out_specs=[pl.BlockSpec((tm, tn), lambda i, j: (i, j)),
                   pl.BlockSpec((1, s, tn), lambda i, j: (i, 0, j))],
        out_shape=[jax.ShapeDtypeStruct((m, n), out_dtype),
                   jax.ShapeDtypeStruct((m // tm, s, n), F32)],
        compiler_params=_params("arbitrary", "arbitrary"),
        name="mm",
    )(x, xs, w, bias.reshape(1, n).astype(F32))
    return out, out_s[0]


def _gated_mm_kernel(x_ref, xs_ref, wa_ref, wb_ref, ba_ref, bb_ref, o_ref, os_ref, wa_s, wb_s, *, mode):
    first = pl.program_id(1) == 0

    def gated(x):
        a = _dot(x, wa_s[...]) + ba_ref[...]
        b = _dot(x, wb_s[...]) + bb_ref[...]
        return a * jax.nn.sigmoid(a) * b if mode == "swiglu" else a * jax.nn.sigmoid(b)

    @pl.when(first)
    def _():
        wa_s[...] = wa_ref[...].astype(BF16)
        wb_s[...] = wb_ref[...].astype(BF16)
        os_ref[0] = gated(xs_ref[...]).astype(os_ref.dtype)

    @pl.when(jnp.logical_not(first))
    def _():
        os_ref[...] = jnp.zeros_like(os_ref)

    o_ref[...] = gated(x_ref[...]).astype(o_ref.dtype)


def gated_matmul(x, xs, wa, wb, ba, bb, mode, out_dtype, lead=(), cola=0, colb=0, n=None):
    m, k = x.shape
    s = xs.shape[0]
    n = wa.shape[-1] - cola if n is None else n
    wblock = (None,) * len(lead) + (k, None)
    tm = _pick_tile(m, (2048, 1024, 512, 256, 128, 8) if jnp.dtype(out_dtype).itemsize == 2
                    else (1024, 512, 256, 128, 8))
    tn = _pick_tile(n, (512, 256, 128))
    assert cola % tn == 0 and colb % tn == 0 and n % tn == 0
    ja, jb = cola // tn, colb // tn
    if ba is None:
        ba = jnp.zeros((n,), F32)
        bb = jnp.zeros((n,), F32)
    out, out_s = pl.pallas_call(
        functools.partial(_gated_mm_kernel, mode=mode),
        grid=(n // tn, m // tm),
        in_specs=[pl.BlockSpec((tm, k), lambda j, i: (i, 0)),
                  pl.BlockSpec((s, k), lambda j, i: (0, 0)),
                  pl.BlockSpec(wblock[:-1] + (tn,), lambda j, i: lead + (0, j + ja)),
                  pl.BlockSpec(wblock[:-1] + (tn,), lambda j, i: lead + (0, j + jb)),
                  pl.BlockSpec((1, tn), lambda j, i: (0, j)),
                  pl.BlockSpec((1, tn), lambda j, i: (0, j))],
        out_specs=[pl.BlockSpec((tm, tn), lambda j, i: (i, j)),
                   pl.BlockSpec((1, s, tn), lambda j, i: (i, 0, j))],
        out_shape=[jax.ShapeDtypeStruct((m, n), out_dtype),
                   jax.ShapeDtypeStruct((m // tm, s, n), out_dtype)],
        scratch_shapes=[pltpu.VMEM((k, tn), BF16), pltpu.VMEM((k, tn), BF16)],
        compiler_params=_params("arbitrary", "arbitrary"),
        name="gated_mm",
    )(x, xs, wa, wb, ba.reshape(1, n).astype(F32), bb.reshape(1, n).astype(F32))
    return out, out_s[0]


def _mm_ln_kernel(a_ref, as_ref, w_ref, bias_ref, res_ref, ress_ref, g_ref, b_ref, of_ref, ob_ref, ofs_ref, obs_ref,
                  accs_ref, *, scale, nk):
    i, k = pl.program_id(0), pl.program_id(1)

    def post_norm(res, acc):
        y = DN_ALPHA * res + scale * (acc + bias_ref[...])
        return _layer_norm_rows(y, g_ref[...], b_ref[...])

    def finish(acc):
        out = post_norm(res_ref[...], acc)
        of_ref[...] = out
        ob_ref[...] = out.astype(BF16)

    @pl.when(i == 0)
    def _():
        part = _dot(as_ref[...], w_ref[...])
        if nk > 1:
            @pl.when(k > 0)
            def _():
                accs_ref[...] += part

            @pl.when(k == 0)
            def _():
                accs_ref[...] = part
        else:
            accs_ref[...] = part

    @pl.when(k == nk - 1)
    def _():
        out = post_norm(ress_ref[...], accs_ref[...])
        first = i == 0
        ofs_ref[0] = jnp.where(first, out, 0.0)
        obs_ref[0] = jnp.where(first, out, 0.0).astype(BF16)

    if nk == 1:
        finish(_dot(a_ref[...], w_ref[...]))
        return

    @pl.when(k == 0)
    def _():
        of_ref[...] = _dot(a_ref[...], w_ref[...])

    @pl.when(jnp.logical_and(k > 0, k < nk - 1))
    def _():
        of_ref[...] += _dot(a_ref[...], w_ref[...])

    @pl.when(k == nk - 1)
    def _():
        finish(of_ref[...] + _dot(a_ref[...], w_ref[...]))


def matmul_postnorm(a, a_s, w, bias, res, res_s, g, b, scale, lead=()):
    m, kdim = a.shape
    s = a_s.shape[0]
    n = w.shape[-1]
    tm = _pick_tile(m, (512, 256, 128, 8))
    tk = _pick_tile(kdim, (2816, 2048, 1408, 1024, 512, 256, 128))
    nk = kdim // tk
    if bias is None:
        bias = jnp.zeros((n,), F32)
    row = lambda v: v.reshape(1, n).astype(F32)
    vec = pl.BlockSpec((1, n), lambda i, k: (0, 0))
    of, ob, ofs, obs = pl.pallas_call(
        functools.partial(_mm_ln_kernel, scale=scale, nk=nk),
        grid=(m // tm, nk),
        in_specs=[pl.BlockSpec((tm, tk), lambda i, k: (i, k)),
                  pl.BlockSpec((s, tk), lambda i, k: (0, k)),
                  pl.BlockSpec((None,) * len(lead) + (tk, n), lambda i, k: lead + (k, 0)),
                  vec,
                  pl.BlockSpec((tm, n), lambda i, k: (i, 0)),
                  pl.BlockSpec((s, n), lambda i, k: (0, 0)),
                  vec, vec],
        out_specs=[pl.BlockSpec((tm, n), lambda i, k: (i, 0)),
                   pl.BlockSpec((tm, n), lambda i, k: (i, 0)),
                   pl.BlockSpec((1, s, n), lambda i, k: (i, 0, 0)),
                   pl.BlockSpec((1, s, n), lambda i, k: (i, 0, 0))],
        out_shape=[jax.ShapeDtypeStruct((m, n), F32), jax.ShapeDtypeStruct((m, n), BF16),
                   jax.ShapeDtypeStruct((m // tm, s, n), F32), jax.ShapeDtypeStruct((m // tm, s, n), BF16)],
        scratch_shapes=[pltpu.VMEM((s, n), F32)],
        compiler_params=_params("arbitrary", "arbitrary"),
        name="mm_postnorm",
    )(a, a_s, w, row(bias), res, res_s, row(g), row(b))
    return (of, ob), (ofs[0], obs[0])


def _ple_kernel(xb_ref, xf_ref, p_ref, wg_ref, wp_ref, g_ref, b_ref, of_ref, ob_ref):
    gate = jax.nn.sigmoid(_dot(xb_ref[...], wg_ref[...]))
    proj = _dot(p_ref[...].astype(BF16), wp_ref[...])
    y = DN_ALPHA * xf_ref[...] + gate * proj
    out = _layer_norm_rows(y, g_ref[...], b_ref[...])
    of_ref[...] = out
    ob_ref[...] = out.astype(BF16)


def ple_add(xb, xf, p, wg, wp, g, b):
    m, d = xf.shape
    pd = p.shape[1]
    tm = _pick_tile(m, (512, 256, 128, 8))
    row = lambda v: v.reshape(1, d).astype(F32)
    return pl.pallas_call(
        _ple_kernel,
        grid=(m // tm,),
        in_specs=[pl.BlockSpec((tm, d), lambda i: (i, 0)),
                  pl.BlockSpec((tm, d), lambda i: (i, 0)),
                  pl.BlockSpec((tm, pd), lambda i: (i, 0)),
                  pl.BlockSpec((d, d), lambda i: (0, 0)),
                  pl.BlockSpec((pd, d), lambda i: (0, 0)),
                  pl.BlockSpec((1, d), lambda i: (0, 0)),
                  pl.BlockSpec((1, d), lambda i: (0, 0))],
        out_specs=[pl.BlockSpec((tm, d), lambda i: (i, 0)),
                   pl.BlockSpec((tm, d), lambda i: (i, 0))],
        out_shape=[jax.ShapeDtypeStruct((m, d), F32), jax.ShapeDtypeStruct((m, d), BF16)],
        compiler_params=_params("parallel"),
        name="ple_add",
    )(xb, xf, p, wg, wp, row(g), row(b))


def _softplus(z):
    return jnp.maximum(z, 0.0) + jnp.log1p(jnp.exp(-jnp.abs(z)))


def _sb_prompt_kernel(q_ref, k_ref, v_ref, o_ref, *, tq, nkv):
    i = pl.program_id(2)
    rows = Q_PER_KV * tq
    scale = HEAD_DIM ** -0.5
    gw = Q_PER_KV * HEAD_DIM
    qs = [jnp.concatenate([q_ref[0, :, h * gw + g * HEAD_DIM:h * gw + (g + 1) * HEAD_DIM] for g in range(Q_PER_KV)],
                          axis=0) for h in range(nkv)]
    t_loc = lax.broadcasted_iota(jnp.int32, (rows, tq), 0) & (tq - 1)
    s_loc = lax.broadcasted_iota(jnp.int32, (rows, tq), 1)
    later = (lax.broadcasted_iota(jnp.int32, (tq, tq), 0) > lax.broadcasted_iota(jnp.int32, (tq, tq), 1)).astype(BF16)

    def block(h, j, run, acc, diagonal):
        start = pl.multiple_of(j * tq, tq)
        kb = k_ref[0, pl.ds(start, tq), h * HEAD_DIM:(h + 1) * HEAD_DIM].astype(BF16)
        vb = v_ref[0, pl.ds(start, tq), h * HEAD_DIM:(h + 1) * HEAD_DIM].astype(BF16)
        z = _dot_nt(qs[h], kb) * scale
        log_keep = -_softplus(z)
        if diagonal:
            valid = s_loc < t_loc
            log_keep = jnp.where(valid, log_keep, 0.0)
        log_w = z + log_keep + _dot01(log_keep, later) + run
        w = jnp.exp(log_w)
        if diagonal:
            w = jnp.where(valid, w, 0.0)
        acc = acc + _dot(w.astype(BF16), vb)
        run = run + jnp.sum(log_keep, axis=1, keepdims=True)
        return run, acc

    run0 = jnp.zeros((rows, 1), F32)
    acc0 = jnp.zeros((rows, HEAD_DIM), F32)
    state = []
    for h in range(nkv):
        state += list(block(h, i, run0, acc0, True))

    def cond(carry):
        live = jnp.max(carry[1])
        for h in range(1, nkv):
            live = jnp.maximum(live, jnp.max(carry[1 + 2 * h]))
        return jnp.logical_and(carry[0] < i, live > SB_LOG_FLOOR)

    def body(carry):
        it = carry[0]
        out = [it + 1]
        for h in range(nkv):
            out += list(block(h, i - 1 - it, carry[1 + 2 * h], carry[2 + 2 * h], False))
        return tuple(out)

    final = lax.while_loop(cond, body, (jnp.int32(0),) + tuple(state))
    for h in range(nkv):
        acc = final[2 + 2 * h]
        for g in range(Q_PER_KV):
            o_ref[0, :, h * gw + g * HEAD_DIM:h * gw + (g + 1) * HEAD_DIM] = acc[g * tq:(g + 1) * tq].astype(o_ref.dtype)


def sb_prompt_attention(q, k, v):
    bsz, L, _ = q.shape
    tq = SB_BLOCK
    nkv = SB_KV_PER_STEP
    gw = nkv * Q_PER_KV * HEAD_DIM
    return pl.pallas_call(
        functools.partial(_sb_prompt_kernel, tq=tq, nkv=nkv),
        grid=(bsz, N_KV_HEADS // nkv, L // tq),
        in_specs=[pl.BlockSpec((1, tq, gw), lambda b, h, i: (b, i, h)),
                  pl.BlockSpec((1, L, nkv * HEAD_DIM), lambda b, h, i: (b, 0, h)),
                  pl.BlockSpec((1, L, nkv * HEAD_DIM), lambda b, h, i: (b, 0, h))],
        out_specs=pl.BlockSpec((1, tq, gw), lambda b, h, i: (b, i, h)),
        out_shape=jax.ShapeDtypeStruct(q.shape, BF16),
        compiler_params=_params("parallel", "parallel", "arbitrary"),
        name="sb_prompt",
    )(q, k, v)


PAGE_ROWS = PAGE_SIZE * N_KV_HEADS
SB_PAGES_PER_GROUP = 4
MOBA_BLOCKS_PER_STEP = 4


def _own_lane_mask(kvh_of_row, shape):
    lane = lax.broadcasted_iota(jnp.int32, shape, 1)
    return (lane & (N_KV_HEADS - 1)) == kvh_of_row


def _suffix_sum_keys(x):
    n = x.shape[1]
    lane = lax.broadcasted_iota(jnp.int32, x.shape, 1)
    d = N_KV_HEADS
    while d < n:
        shifted = pltpu.roll(x, n - d, axis=1)
        x = x + jnp.where(lane + d < n, shifted, 0.0)
        d *= 2
    return x


def _sb_decode_kernel(pt_ref, q_ref, k_hbm, v_hbm, o_ref, kbuf, vbuf, sems, acc_ref, run_ref, *, n_pages, gp):
    b = pl.program_id(0)
    n_groups = n_pages // gp

    def page_copies(g, slot):
        copies = []
        for u in range(gp):
            page = pt_ref[b * n_pages + (n_pages - 1 - (g * gp + u))]
            rows = pl.ds(pl.multiple_of(page * PAGE_ROWS, PAGE_ROWS), PAGE_ROWS)
            copies.append(pltpu.make_async_copy(k_hbm.at[rows], kbuf.at[slot, u], sems.at[0, slot, u]))
            copies.append(pltpu.make_async_copy(v_hbm.at[rows], vbuf.at[slot, u], sems.at[1, slot, u]))
        return copies

    def start(g, slot):
        for c in page_copies(g, slot):
            c.start()

    def wait(g, slot):
        for c in page_copies(g, slot):
            c.wait()

    acc_ref[...] = jnp.zeros_like(acc_ref)
    run_ref[...] = jnp.zeros_like(run_ref)
    q = q_ref[0].astype(BF16)
    kvh = lax.broadcasted_iota(jnp.int32, (N_HEADS, PAGE_ROWS), 0) >> 2
    own = _own_lane_mask(kvh, (N_HEADS, PAGE_ROWS))
    start(0, 0)

    def live(_):
        return jnp.max(run_ref[...]) > SB_LOG_FLOOR

    def body(g):
        slot = g & 1
        wait(g, slot)

        @pl.when(g + 1 < n_groups)
        def _():
            start(g + 1, 1 - slot)

        for u in range(gp):
            @pl.when(live(None))
            def _():
                z = _dot_nt(q, kbuf[slot, u].astype(BF16)) * (HEAD_DIM ** -0.5)
                log_keep = jnp.where(own, -_softplus(z), 0.0)
                incl = _suffix_sum_keys(log_keep)
                w = jnp.where(own, jnp.exp(z + incl + run_ref[:, 0:1]), 0.0)
                acc_ref[...] += _dot(w.astype(BF16), vbuf[slot, u].astype(BF16))
                run_ref[...] += jnp.sum(log_keep, axis=1, keepdims=True)
        return g + 1

    g_end = lax.while_loop(lambda g: jnp.logical_and(g < n_groups, live(None)), body, jnp.int32(0))

    @pl.when(g_end < n_groups)
    def _():
        wait(g_end, g_end & 1)

    o_ref[0] = acc_ref[...]


def sb_decode_attention(q, k_rows, v_rows, page_table):
    db, n_pages = page_table.shape
    gp = SB_PAGES_PER_GROUP
    assert n_pages % gp == 0
    return pl.pallas_call(
        functools.partial(_sb_decode_kernel, n_pages=n_pages, gp=gp),
        grid_spec=pltpu.PrefetchScalarGridSpec(
            num_scalar_prefetch=1,
            grid=(db,),
            in_specs=[pl.BlockSpec((1, N_HEADS, HEAD_DIM), lambda b, pt: (b, 0, 0)),
                      pl.BlockSpec(memory_space=pl.ANY), pl.BlockSpec(memory_space=pl.ANY)],
            out_specs=pl.BlockSpec((1, N_HEADS, HEAD_DIM), lambda b, pt: (b, 0, 0)),
            scratch_shapes=[pltpu.VMEM((2, gp, PAGE_ROWS, HEAD_DIM), F32),
                            pltpu.VMEM((2, gp, PAGE_ROWS, HEAD_DIM), F32),
                            pltpu.SemaphoreType.DMA((2, 2, gp)),
                            pltpu.VMEM((N_HEADS, HEAD_DIM), F32), pltpu.VMEM((N_HEADS, LANES), F32)]),
        out_shape=jax.ShapeDtypeStruct((db, N_HEADS, HEAD_DIM), F32),
        compiler_params=_params("arbitrary"),
        name="sb_decode",
    )(page_table.reshape(-1), q, k_rows, v_rows)


def _top_k_mask(gate, n_valid, k):
    lane = lax.broadcasted_iota(jnp.int32, gate.shape, 1)
    sel = jnp.zeros(gate.shape, F32)
    picks = []
    for r in range(k):
        m = jnp.max(gate, axis=1, keepdims=True)
        idx = jnp.min(jnp.where(gate == m, lane, gate.shape[1] - 1), axis=1, keepdims=True)
        pick = lane == idx
        sel = jnp.maximum(sel, jnp.where(pick, jnp.where(r < n_valid, 1.0, 0.0), 0.0))
        gate = jnp.where(pick, -jnp.inf, gate)
        picks.append(idx)
    return sel, picks


def _moba_prompt_kernel(q_ref, k_ref, v_ref, o_ref, km_ref, *, tq, nb):
    i = pl.program_id(2)
    rows = Q_PER_KV * tq
    scale = HEAD_DIM ** -0.5

    @pl.when(i == 0)
    def _():
        km_ref[...] = jnp.zeros_like(km_ref)
        for jb in range(nb):
            km_ref[jb:jb + 1, :] = jnp.mean(k_ref[0, jb * tq:(jb + 1) * tq, :], axis=0, keepdims=True)

    q = jnp.concatenate([q_ref[0, :, g * HEAD_DIM:(g + 1) * HEAD_DIM] for g in range(Q_PER_KV)], axis=0)
    lane = lax.broadcasted_iota(jnp.int32, (rows, LANES), 1)
    gate = _dot_nt(q, km_ref[...].astype(BF16))
    gate = jnp.where(lane < i, gate, -jnp.inf)
    sel, _ = _top_k_mask(gate, i, MOBA_TOPK)
    sel = sel.astype(BF16)

    def scores(j):
        start = pl.multiple_of(j * tq, tq)
        kb = k_ref[0, pl.ds(start, tq), :].astype(BF16)
        vb = v_ref[0, pl.ds(start, tq), :].astype(BF16)
        return _dot_nt(q, kb) * scale, vb

    s, vb = scores(i)
    t_loc = lax.broadcasted_iota(jnp.int32, (rows, tq), 0) & (tq - 1)
    s_loc = lax.broadcasted_iota(jnp.int32, (rows, tq), 1)
    s = jnp.where(s_loc <= t_loc, s, NEG_BIG)
    m = jnp.max(s, axis=1, keepdims=True)
    p = jnp.exp(s - m)
    l = jnp.sum(p, axis=1, keepdims=True)
    acc = _dot(p.astype(BF16), vb)

    def body(j, carry):
        m, l, acc = carry
        s, vb = scores(j)
        onehot = (lax.broadcasted_iota(jnp.int32, (LANES, tq), 0) == j).astype(BF16)
        chosen = _dot(sel, onehot) > 0.5
        s = jnp.where(chosen, s, NEG_BIG)
        m_new = jnp.maximum(m, jnp.max(s, axis=1, keepdims=True))
        alpha = jnp.exp(m - m_new)
        p = jnp.exp(s - m_new)
        l = alpha * l + jnp.sum(p, axis=1, keepdims=True)
        acc = alpha * acc + _dot(p.astype(BF16), vb)
        return m_new, l, acc

    m, l, acc = lax.fori_loop(0, i, body, (m, l, acc))
    out = acc / l
    for g in range(Q_PER_KV):
        o_ref[0, :, g * HEAD_DIM:(g + 1) * HEAD_DIM] = out[g * tq:(g + 1) * tq].astype(o_ref.dtype)


def moba_prompt_attention(q, k, v):
    bsz, L, _ = q.shape
    tq = MOBA_BLOCK
    nb = L // tq
    assert L % tq == 0 and nb <= LANES
    gw = Q_PER_KV * HEAD_DIM
    return pl.pallas_call(
        functools.partial(_moba_prompt_kernel, tq=tq, nb=nb),
        grid=(bsz, N_KV_HEADS, nb),
        in_specs=[pl.BlockSpec((1, tq, gw), lambda b, h, i: (b, i, h)),
                  pl.BlockSpec((1, L, HEAD_DIM), lambda b, h, i: (b, 0, h)),
                  pl.BlockSpec((1, L, HEAD_DIM), lambda b, h, i: (b, 0, h))],
        out_specs=pl.BlockSpec((1, tq, gw), lambda b, h, i: (b, i, h)),
        out_shape=jax.ShapeDtypeStruct(q.shape, BF16),
        scratch_shapes=[pltpu.VMEM((LANES, HEAD_DIM), F32)],
        compiler_params=_params("parallel", "arbitrary", "arbitrary"),
        name="moba_prompt",
    )(q, k, v)


def _moba_kmean_kernel(pt_ref, *refs, bps, ppb):
    k_refs, o_ref = refs[:bps * ppb], refs[bps * ppb]
    sub = 8
    for j in range(bps):
        tot = jnp.zeros((sub, HEAD_DIM), F32)
        for h in range(ppb):
            k_ref = k_refs[j * ppb + h]
            for c in range(PAGE_ROWS // sub):
                tot = tot + k_ref[c * sub:(c + 1) * sub, :]
        mean = (tot[:N_KV_HEADS] + tot[N_KV_HEADS:]) * (1.0 / MOBA_BLOCK)
        o_ref[0, j * N_KV_HEADS:(j + 1) * N_KV_HEADS, :] = mean


def moba_cache_block_means(k_rows, page_table):
    db, n_pages = page_table.shape
    ppb = MOBA_BLOCK // PAGE_SIZE
    nblk = n_pages // ppb
    bps = MOBA_BLOCKS_PER_STEP
    assert nblk % bps == 0

    def page_spec(u):
        return pl.BlockSpec((PAGE_ROWS, HEAD_DIM), lambda b, j, pt: (pt[b * n_pages + j * (bps * ppb) + u], 0))

    return pl.pallas_call(
        functools.partial(_moba_kmean_kernel, bps=bps, ppb=ppb),
        grid_spec=pltpu.PrefetchScalarGridSpec(
            num_scalar_prefetch=1,
            grid=(db, nblk // bps),
            in_specs=[page_spec(u) for u in range(bps * ppb)],
            out_specs=pl.BlockSpec((1, bps * N_KV_HEADS, HEAD_DIM), lambda b, j, pt: (b, j, 0))),
        out_shape=jax.ShapeDtypeStruct((db, nblk * N_KV_HEADS, HEAD_DIM), F32),
        compiler_params=_params("parallel", "arbitrary"),
        name="moba_kmean",
    )(page_table.reshape(-1), *([k_rows] * (bps * ppb)))


def _moba_gate_kernel(q_ref, km_ref, o_ref, *, nblk):
    gate = _dot_nt(q_ref[0].astype(BF16), km_ref[0].astype(BF16))
    kvh = lax.broadcasted_iota(jnp.int32, gate.shape, 0) >> 2
    gate = jnp.where(_own_lane_mask(kvh, gate.shape), gate, -jnp.inf)
    _, picks = _top_k_mask(gate, nblk, MOBA_TOPK)
    lane = lax.broadcasted_iota(jnp.int32, (N_HEADS, LANES), 1)
    out = jnp.zeros((N_HEADS, LANES), jnp.int32)
    for r, idx in enumerate(picks):
        out = jnp.where(lane == r, idx >> 2, out)
    o_ref[0] = out


def moba_decode_select(q, kmean):
    db, rows, _ = kmean.shape
    nblk = rows // N_KV_HEADS
    out = pl.pallas_call(
        functools.partial(_moba_gate_kernel, nblk=nblk),
        grid=(db,),
        in_specs=[pl.BlockSpec((1, N_HEADS, HEAD_DIM), lambda b: (b, 0, 0)),
                  pl.BlockSpec((1, rows, HEAD_DIM), lambda b: (b, 0, 0))],
        out_specs=pl.BlockSpec((1, N_HEADS, LANES), lambda b: (b, 0, 0)),
        out_shape=jax.ShapeDtypeStruct((db, N_HEADS, LANES), jnp.int32),
        compiler_params=_params("parallel"),
        name="moba_gate",
    )(q, kmean)
    return out[:, :, :MOBA_TOPK]


def _moba_decode_kernel(sel_ref, pt_ref, q_ref, kn_ref, vn_ref, *refs, n_sel):
    k_refs, v_refs, o_ref = refs[:n_sel], refs[n_sel:2 * n_sel], refs[2 * n_sel]
    scale = HEAD_DIM ** -0.5
    q = jnp.broadcast_to(q_ref[0, 0], (8, HEAD_DIM))
    own = _own_lane_mask(pl.program_id(1) >> 2, (8, PAGE_ROWS))
    m = jnp.sum(q * kn_ref[0, 0], axis=1, keepdims=True) * scale
    l = jnp.ones((8, 1), F32)
    acc = jnp.broadcast_to(vn_ref[0, 0], (8, HEAD_DIM))
    qb = q.astype(BF16)
    for u in range(n_sel):
        s = jnp.where(own, _dot_nt(qb, k_refs[u][...].astype(BF16)) * scale, NEG_BIG)
        m_new = jnp.maximum(m, jnp.max(s, axis=1, keepdims=True))
        alpha = jnp.exp(m - m_new)
        p = jnp.exp(s - m_new)
        l = alpha * l + jnp.sum(p, axis=1, keepdims=True)
        acc = alpha * acc + _dot(p.astype(BF16), v_refs[u][...].astype(BF16))
        m = m_new
    o_ref[0, 0] = acc / l


def moba_decode_attention(q, k_new, v_new, sel, k_rows, v_rows, page_table):
    db, n_pages = page_table.shape
    ppb = MOBA_BLOCK // PAGE_SIZE
    n_sel = MOBA_TOPK * ppb

    def page_spec(u):
        def index(b, h, sel_r, pt):
            blk = sel_r[(b * N_HEADS + h) * MOBA_TOPK + u // ppb]
            return (pt[b * n_pages + blk * ppb + u % ppb], 0)
        return pl.BlockSpec((PAGE_ROWS, HEAD_DIM), index)

    specs = [page_spec(u) for u in range(n_sel)]
    new_spec = pl.BlockSpec((1, 1, 1, HEAD_DIM), lambda b, h, sel_r, pt: (b, h // Q_PER_KV, 0, 0))
    out = pl.pallas_call(
        functools.partial(_moba_decode_kernel, n_sel=n_sel),
        grid_spec=pltpu.PrefetchScalarGridSpec(
            num_scalar_prefetch=2,
            grid=(db, N_HEADS),
            in_specs=[pl.BlockSpec((1, 1, 1, HEAD_DIM), lambda b, h, sel_r, pt: (b, h, 0, 0)),
                      new_spec, new_spec] + specs + specs,
            out_specs=pl.BlockSpec((1, 1, 8, HEAD_DIM), lambda b, h, sel_r, pt: (b, h, 0, 0))),
        out_shape=jax.ShapeDtypeStruct((db, N_HEADS, 8, HEAD_DIM), F32),
        compiler_params=_params("parallel", "parallel"),
        name="moba_decode",
    )(sel.reshape(-1), page_table.reshape(-1), q, k_new, v_new, *([k_rows] * n_sel), *([v_rows] * n_sel))
    return out[:, :, 0, :]


def _dwconv_kernel(x_ref, hist_ref, w_ref, b_ref, g_ref, beta_ref, o_ref, pad_ref, *shift_refs, width, tl, hp, post):
    t = pl.program_id(2)

    @pl.when(t == 0)
    def _():
        pad_ref[0:hp, :] = hist_ref[0]

    @pl.when(t > 0)
    def _():
        pad_ref[0:hp, :] = pad_ref[tl:tl + hp, :]

    pad_ref[hp:hp + tl, :] = x_ref[0]

    def finish(acc):
        if post == "ln_silu":
            acc = _layer_norm_rows(acc, g_ref[...], beta_ref[...])
        return (acc * jax.nn.sigmoid(acc)).astype(o_ref.dtype)

    if not shift_refs:
        base = hp - (width - 1)
        acc = jnp.zeros(o_ref.shape[1:], F32) + b_ref[...]
        for k in range(width):
            acc = acc + pad_ref[base + k:base + k + tl, :] * w_ref[k:k + 1, :]
        o_ref[0] = finish(acc)
        return

    sh_ref, wb_ref = shift_refs
    halo = 8 * ((width - 1) // 8)
    for r in range(1, 8):
        sh_ref[r - 1] = pad_ref[hp - halo - r:hp + tl - r, :]
    rt = min(DWCONV_ROW_TILE, tl)
    tc = o_ref.shape[2]
    lc = DWCONV_LANE_CHUNK if tc % DWCONV_LANE_CHUNK == 0 else tc
    for k in range(width):
        wb_ref[k] = jnp.broadcast_to(w_ref[k:k + 1, :], (8, tc))

    def row_tile(row0):
        parts = []
        for c0 in range(0, tc, lc):
            acc = jnp.zeros((rt // 8, 8, lc), F32) + b_ref[:, c0:c0 + lc]
            for r in range(8):
                for a in range((width - 1 - r) // 8 + 1):
                    k = width - 1 - (8 * a + r)
                    if r == 0:
                        src = pad_ref[pl.ds(row0 + (hp - 8 * a), rt), c0:c0 + lc]
                    else:
                        src = sh_ref[r - 1, pl.ds(row0 + (halo - 8 * a), rt), c0:c0 + lc]
                    acc = acc + src.reshape(rt // 8, 8, lc) * wb_ref[k, :, c0:c0 + lc]
            parts.append(acc.reshape(rt, lc))
        o_ref[0, pl.ds(row0, rt), :] = finish(jnp.concatenate(parts, axis=1))

    if tl == rt:
        row_tile(0)
    else:
        def body(it, carry):
            row_tile(pl.multiple_of(it * rt, rt))
            return carry

        lax.fori_loop(0, tl // rt, body, 0)


def causal_dwconv(x, hist, w, b, post, ln_g=None, ln_b=None, out_dtype=F32):
    bsz, L, C = x.shape
    width = w.shape[0]
    hp = hist.shape[1]
    assert hp % 8 == 0 and hp >= 8 * ((width - 1) // 8) + min(7, width - 1)
    tc = C if post == "ln_silu" else _pick_tile(C, (1024, 512, 256, 128))
    tl = _pick_tile(L, (256, 128)) if L >= hp else L
    assert tl >= hp or tl == L
    wp = jnp.zeros((-(-width // 8) * 8, C), F32).at[:width].set(w.astype(F32))
    if ln_g is None:
        ln_g = jnp.ones((C,), F32)
        ln_b = jnp.zeros((C,), F32)
    row = lambda v: v.reshape(1, C).astype(F32)
    vec_spec = pl.BlockSpec((1, tc), lambda bb, c, t: (0, c))
    return pl.pallas_call(
        functools.partial(_dwconv_kernel, width=width, tl=tl, hp=hp, post=post),
        grid=(bsz, C // tc, L // tl),
        in_specs=[pl.BlockSpec((1, tl, tc), lambda bb, c, t: (bb, t, c)),
                  pl.BlockSpec((1, hp, tc), lambda bb, c, t: (bb, 0, c)),
                  pl.BlockSpec((wp.shape[0], tc), lambda bb, c, t: (0, c)),
                  vec_spec, vec_spec, vec_spec],
        out_specs=pl.BlockSpec((1, tl, tc), lambda bb, c, t: (bb, t, c)),
        out_shape=jax.ShapeDtypeStruct((bsz, L, C), out_dtype),
        scratch_shapes=[pltpu.VMEM((hp + tl, tc), F32)] + (
            [pltpu.VMEM((7, 8 * ((width - 1) // 8) + tl, tc), F32),
             pltpu.VMEM((width, 8, tc), F32)] if width > 8 else []),
        compiler_params=_params("parallel", "parallel", "arbitrary"),
        name="dwconv_" + post,
    )(x, hist, wp, row(b), row(ln_g), row(ln_b))


def _ssd_prompt_kernel(x_ref, bm_ref, cm_ref, dt_ref, z_ref, dtb_ref, alog_ref, d_ref, ng_ref, h0_ref,
                       yn_ref, h_ref, y_buf, *, chunk, hpg):
    c = pl.program_id(1)
    P, N = SSM_HEAD_DIM, SSM_D_STATE
    gw = hpg * P

    @pl.when(c == 0)
    def _():
        h_ref[...] = h0_ref[...]

    li = lax.broadcasted_iota(jnp.int32, (chunk, chunk), 0)
    si = lax.broadcasted_iota(jnp.int32, (chunk, chunk), 1)
    causal = li >= si
    lower = causal.astype(BF16)
    dt = _softplus(dt_ref[0] + dtb_ref[...])
    a = -jnp.exp(alog_ref[...])
    hi, mid, lo = _split3(dt * a)
    cum = _dot(lower, hi) + _dot(lower, mid) + _dot(lower, lo)
    cum_t = cum.T
    dt_t = dt.T
    for g in range(SSM_GROUPS):
        x = x_ref[0, :, g * gw:(g + 1) * gw]
        cmb = cm_ref[0, :, g * N:(g + 1) * N].astype(BF16)
        bmb = bm_ref[0, :, g * N:(g + 1) * N].astype(BF16)
        cb = _dot_nt(cmb, bmb)
        x_t = x.T
        for r in range(hpg):
            head = g * hpg + r
            cum_col = cum[:, head:head + 1]
            cum_row = cum_t[head:head + 1, :]
            dt_row = dt_t[head:head + 1, :]
            total = cum_t[head:head + 1, chunk - 1:chunk]
            xh = x[:, r * P:(r + 1) * P]
            decay = jnp.exp(jnp.where(causal, cum_col - cum_row, -jnp.inf))
            mix = (cb * decay * dt_row).astype(BF16)
            y = _dot(mix, xh.astype(BF16))
            h_old = h_ref[0, head]
            y = y + _dot_nt(cmb, h_old.astype(BF16)) * jnp.exp(cum_col)
            y_buf[:, r * P:(r + 1) * P] = y + d_ref[:, head:head + 1] * xh
            to_end = jnp.exp(total - cum_row) * dt_row
            s_chunk = _dot((x_t[r * P:(r + 1) * P, :] * to_end).astype(BF16), bmb)
            h_ref[0, head] = jnp.exp(total) * h_old + s_chunk
        z = z_ref[0, :, g * gw:(g + 1) * gw]
        hg = y_buf[...] * (z * jax.nn.sigmoid(z))
        hg = hg * lax.rsqrt(jnp.mean(hg * hg, axis=-1, keepdims=True) + LN_EPS)
        yn_ref[0, :, g * gw:(g + 1) * gw] = (hg * ng_ref[:, g * gw:(g + 1) * gw]).astype(yn_ref.dtype)


def ssd_prompt(xbc, dt_raw, z, dt_bias_g, a_log_g, d_g, norm_g, h0):
    bsz, L, _ = xbc.shape
    n_heads = h0.shape[1]
    G, N, P = SSM_GROUPS, SSM_D_STATE, SSM_HEAD_DIM
    hpg = n_heads // G
    d_inner = n_heads * P
    gn = G * N
    chunk = SSM_CHUNK
    assert d_inner % gn == 0 and (hpg * P) % LANES == 0 and n_heads <= LANES
    b_off = d_inner // gn
    vec_spec = pl.BlockSpec((1, LANES), lambda b, c: (0, 0))
    state_spec = pl.BlockSpec((1, n_heads, P, N), lambda b, c: (b, 0, 0, 0))
    return pl.pallas_call(
        functools.partial(_ssd_prompt_kernel, chunk=chunk, hpg=hpg),
        grid=(bsz, L // chunk),
        in_specs=[pl.BlockSpec((1, chunk, d_inner), lambda b, c: (b, c, 0)),
                  pl.BlockSpec((1, chunk, gn), lambda b, c: (b, c, b_off)),
                  pl.BlockSpec((1, chunk, gn), lambda b, c: (b, c, b_off + 1)),
                  pl.BlockSpec((1, chunk, LANES), lambda b, c: (b, c, 0)),
                  pl.BlockSpec((1, chunk, d_inner), lambda b, c: (b, c, 0)),
                  vec_spec, vec_spec, vec_spec,
                  pl.BlockSpec((1, d_inner), lambda b, c: (0, 0)),
                  state_spec],
        out_specs=[pl.BlockSpec((1, chunk, d_inner), lambda b, c: (b, c, 0)), state_spec],
        out_shape=[jax.ShapeDtypeStruct((bsz, L, d_inner), BF16),
                   jax.ShapeDtypeStruct((bsz, n_heads, P, N), F32)],
        scratch_shapes=[pltpu.VMEM((chunk, hpg * P), F32)],
        compiler_params=_params("parallel", "arbitrary"),
        name="ssd_prompt",
    )(xbc, xbc, xbc, dt_raw, z, dt_bias_g, a_log_g, d_g, norm_g.reshape(1, d_inner).astype(F32), h0)


def _ssd_step_kernel(xc_ref, bc_ref, dt_ref, dtb_ref, alog_ref, d_ref, h0_ref, y_ref, h_ref, *, n_heads):
    N = SSM_D_STATE
    hpg = n_heads // SSM_GROUPS
    dt = _softplus(dt_ref[0] + dtb_ref[...])
    da = jnp.exp(dt * (-jnp.exp(alog_ref[...])))
    for h in range(n_heads):
        g = h // hpg
        bm = bc_ref[0, :, g * N:(g + 1) * N]
        cm = bc_ref[0, :, (SSM_GROUPS + g) * N:(SSM_GROUPS + g + 1) * N]
        xcol = xc_ref[0, h]
        h_new = da[:, h:h + 1] * h0_ref[0, h] + xcol * (dt[:, h:h + 1] * bm)
        h_ref[0, h] = h_new
        y_ref[0, h] = jnp.sum(h_new * cm, axis=1, keepdims=True) + d_ref[:, h:h + 1] * xcol


def ssd_step(x_col, bc, dt_raw, dt_bias, a_log, d_skip, h0):
    db, n_heads, P, _ = x_col.shape
    N = SSM_D_STATE
    vec_spec = pl.BlockSpec((1, LANES), lambda b: (0, 0))
    state_spec = pl.BlockSpec((1, n_heads, P, N), lambda b: (b, 0, 0, 0))
    col_spec = pl.BlockSpec((1, n_heads, P, 1), lambda b: (b, 0, 0, 0))
    return pl.pallas_call(
        functools.partial(_ssd_step_kernel, n_heads=n_heads),
        grid=(db,),
        in_specs=[col_spec,
                  pl.BlockSpec((1, 1, bc.shape[2]), lambda b: (b, 0, 0)),
                  pl.BlockSpec((1, 1, LANES), lambda b: (b, 0, 0)),
                  vec_spec, vec_spec, vec_spec, state_spec],
        out_specs=[col_spec, state_spec],
        out_shape=[jax.ShapeDtypeStruct((db, n_heads, P, 1), F32),
                   jax.ShapeDtypeStruct((db, n_heads, P, N), F32)],
        compiler_params=_params("parallel"),
        name="ssd_step",
    )(x_col, bc, dt_raw, dt_bias, a_log, d_skip, h0)


def _gated_norm_kernel(y_ref, z_ref, g_ref, o_ref):
    z = z_ref[...]
    h = y_ref[...] * (z * jax.nn.sigmoid(z))
    h = h * lax.rsqrt(jnp.mean(h * h, axis=-1, keepdims=True) + LN_EPS)
    o_ref[...] = (h * g_ref[...]).astype(o_ref.dtype)


def gated_rms_norm(y, z, g):
    m, d = y.shape
    gw = d // SSM_GROUPS
    tm = _pick_tile(m, (512, 256, 128, 8))
    spec = pl.BlockSpec((tm, gw), lambda i, j: (i, j))
    return pl.pallas_call(
        _gated_norm_kernel,
        grid=(m // tm, SSM_GROUPS),
        in_specs=[spec, spec, pl.BlockSpec((1, gw), lambda i, j: (0, j))],
        out_specs=spec,
        out_shape=jax.ShapeDtypeStruct((m, d), BF16),
        compiler_params=_params("parallel", "parallel"),
        name="gated_rms_norm",
    )(y, z, g.reshape(1, d).astype(F32))


def _lane_pad(v):
    return jnp.zeros((1, LANES), F32).at[0, :v.shape[0]].set(v.astype(F32))


def _ffn_half(xp_f, xp_b, xs_f, xs_b, w1, w3, lead, w2, g, b):
    hp, hs = gated_matmul(xp_b, xs_b, w1, w3, None, None, "swiglu", BF16, lead=lead)
    return matmul_postnorm(hp, hs, w2, None, xp_f, xs_f, g, b, 0.5, lead=lead)


def _pad_rows(x, rows):
    return jnp.concatenate([x, jnp.zeros((x.shape[0], rows - x.shape[1]) + x.shape[2:], x.dtype)], axis=1)


def _front_pad_rows(x, rows):
    return jnp.concatenate([jnp.zeros((x.shape[0], rows - x.shape[1]) + x.shape[2:], x.dtype), x], axis=1)


def kernel(x_prompt, x_sample, p_prompt, p_sample, cache_sb_k, cache_sb_v, cache_moba_k, cache_moba_v, state_ssm, state_ssm_conv, state_conf_conv, page_table, ln_g, ln_b, ffn_w1, ffn_w3, ffn_w2, ple_w_proj, ple_w_gate, sb_w_qkv, sb_w_o, ssm_w_in, ssm_conv_w, ssm_conv_b, ssm_dt_bias, ssm_a_log, ssm_d, ssm_norm_g, ssm_w_out, conf_w_pw1, conf_b_pw1, conf_w_dw, conf_b_dw, conf_ln_g, conf_ln_b, conf_w_pw2, conf_b_pw2, moba_w_qkv, moba_w_o):
    bsz, seq, d_model = x_prompt.shape
    db = x_sample.shape[0]
    assert x_sample.shape[1] == 1
    mp = bsz * seq
    depth = ffn_w1.shape[0]
    bf = lambda w: w.astype(BF16)

    xp_f = x_prompt.reshape(mp, d_model)
    xs_f = x_sample.reshape(db, d_model)
    xp_b, xs_b = bf(xp_f), bf(xs_f)
    n_phys = cache_sb_k.shape[1]
    ffn_w2_b = bf(ffn_w2)
    outs ={k: [] for k in ("sb_kp", "sb_vp", "sb_ks", "sb_vs", "ssm_hp", "ssm_hs", "ssm_cp", "ssm_cs",
                            "conf_cp", "conf_cs", "mo_kp", "mo_vp", "mo_ks", "mo_vs")}

    for i in range(depth):
        m, j = i % 4, i // 4
        (xp_f, xp_b), (xs_f, xs_b) = _ffn_half(xp_f, xp_b, xs_f, xs_b, ffn_w1, ffn_w3, (i, 0), ffn_w2_b,
                                               ln_g[i, 0], ln_b[i, 0])
        g1, b1 = ln_g[i, 1], ln_b[i, 1]

        if m == 0 or m == 3:
            w_qkv = bf(sb_w_qkv[j] if m == 0 else moba_w_qkv[j])
            w_o = bf(sb_w_o[j] if m == 0 else moba_w_o[j])
            q, qs = matmul(xp_b, xs_b, w_qkv, None, BF16, 0, Q_DIM)
            k, ks = matmul(xp_b, xs_b, w_qkv, None, F32, Q_DIM, KV_DIM)
            v, vs = matmul(xp_b, xs_b, w_qkv, None, F32, Q_DIM + KV_DIM, KV_DIM)
            q = q.reshape(bsz, seq, Q_DIM)
            k = k.reshape(bsz, seq, KV_DIM)
            v = v.reshape(bsz, seq, KV_DIM)
            attend = sb_prompt_attention if m == 0 else moba_prompt_attention
            o = attend(q, k, v).reshape(mp, Q_DIM)
            kp = k.reshape(bsz, seq, N_KV_HEADS, HEAD_DIM)
            vp = v.reshape(bsz, seq, N_KV_HEADS, HEAD_DIM)
            if m == 0:
                kc = cache_sb_k[j].reshape(n_phys * PAGE_ROWS, HEAD_DIM)
                vc = cache_sb_v[j].reshape(n_phys * PAGE_ROWS, HEAD_DIM)
                os_ = sb_decode_attention(qs.reshape(db, N_HEADS, HEAD_DIM), kc, vc, page_table)
            else:
                kc = cache_moba_k[j].reshape(n_phys * PAGE_ROWS, HEAD_DIM)
                vc = cache_moba_v[j].reshape(n_phys * PAGE_ROWS, HEAD_DIM)
                qh = qs.reshape(db, N_HEADS, HEAD_DIM)
                kmean = moba_cache_block_means(kc, page_table)
                sel = moba_decode_select(qh, kmean)
                os_ = moba_decode_attention(qh.reshape(db, N_HEADS, 1, HEAD_DIM),
                                            ks.reshape(db, N_KV_HEADS, 1, HEAD_DIM),
                                            vs.reshape(db, N_KV_HEADS, 1, HEAD_DIM),
                                            sel, kc, vc, page_table)
            (xp_f, xp_b), (xs_f, xs_b) = matmul_postnorm(o, bf(os_.reshape(db, Q_DIM)), w_o, None, xp_f, xs_f,
                                                         g1, b1, 1.0)
            ks4 = ks.reshape(db, 1, N_KV_HEADS, HEAD_DIM)
            vs4 = vs.reshape(db, 1, N_KV_HEADS, HEAD_DIM)
            if m == 0:
                outs["sb_kp"].append(kp); outs["sb_vp"].append(vp); outs["sb_ks"].append(ks4); outs["sb_vs"].append(vs4)
            else:
                outs["mo_kp"].append(kp); outs["mo_vp"].append(vp); outs["mo_ks"].append(ks4); outs["mo_vs"].append(vs4)

        elif m == 1:
            n_heads = ssm_dt_bias.shape[1]
            d_inner = n_heads * SSM_HEAD_DIM
            conv_dim = ssm_conv_w.shape[2]
            hpg = n_heads // SSM_GROUPS
            w_in = bf(ssm_w_in[j])
            w_dt_cols = ssm_w_in[j][:, d_inner + conv_dim:]
            w_dt = bf(jnp.zeros((d_model, LANES), F32).at[:, :n_heads].set(w_dt_cols))
            dtb, alog, dskip = _lane_pad(ssm_dt_bias[j]), _lane_pad(ssm_a_log[j]), _lane_pad(ssm_d[j])
            w_out = bf(ssm_w_out[j])
            hp = 8
            z, zs = matmul(xp_b, xs_b, w_in, None, F32, 0, d_inner)
            xbc_raw, xbc_s = matmul(xp_b, xs_b, w_in, None, F32, d_inner, conv_dim)
            dt_raw, dt_s = matmul(xp_b, xs_b, w_dt, None, F32)
            xbc_raw = xbc_raw.reshape(bsz, seq, conv_dim)
            dt_raw = dt_raw.reshape(bsz, seq, LANES)
            xbc = causal_dwconv(xbc_raw, jnp.zeros((bsz, hp, conv_dim), F32), ssm_conv_w[j], ssm_conv_b[j], "silu")
            h0 = jnp.zeros((bsz, n_heads, SSM_HEAD_DIM, SSM_D_STATE), F32)
            yn, h_last = ssd_prompt(xbc, dt_raw, z.reshape(bsz, seq, d_inner), dtb, alog, dskip, ssm_norm_g[j], h0)
            outs["ssm_hp"].append(h_last)
            outs["ssm_cp"].append(xbc_raw[:, seq - (SSM_CONV - 1):, :])
            xbc_s = xbc_s.reshape(db, 1, conv_dim)
            dt_s = dt_s.reshape(db, 1, LANES)
            hist = _front_pad_rows(state_ssm_conv[j], hp)
            xbc_sa = causal_dwconv(_pad_rows(xbc_s, 8), hist, ssm_conv_w[j], ssm_conv_b[j], "silu")[:, :1, :]
            x_col = xbc_sa[:, 0, :d_inner].reshape(db, n_heads, SSM_HEAD_DIM, 1)
            y_col, hs = ssd_step(x_col, xbc_sa[:, :, d_inner:], dt_s, dtb, alog, dskip, state_ssm[j])
            yns = gated_rms_norm(y_col.reshape(db, d_inner), zs, ssm_norm_g[j])
            (xp_f, xp_b), (xs_f, xs_b) = matmul_postnorm(yn.reshape(mp, d_inner), yns, w_out, None, xp_f, xs_f,
                                                         g1, b1, 1.0)
            outs["ssm_hs"].append(hs)
            outs["ssm_cs"].append(jnp.concatenate([state_ssm_conv[j], xbc_s], axis=1)[:, 1:, :])

        else:
            ba, bg = conf_b_pw1[j][:d_model], conf_b_pw1[j][d_model:]
            w_pw2 = bf(conf_w_pw2[j])
            hp = 32
            u, us = gated_matmul(xp_b, xs_b, conf_w_pw1, conf_w_pw1, ba, bg, "glu", F32, lead=(j,),
                                 cola=0, colb=d_model, n=d_model)
            u = u.reshape(bsz, seq, d_model)
            us = us.reshape(db, 1, d_model)
            uc = causal_dwconv(u, jnp.zeros((bsz, hp, d_model), F32), conf_w_dw[j], conf_b_dw[j], "ln_silu",
                               conf_ln_g[j], conf_ln_b[j], BF16)
            outs["conf_cp"].append(u[:, seq - (CONF_WIDTH - 1):, :])
            hist = _front_pad_rows(state_conf_conv[j], hp)
            ucs = causal_dwconv(_pad_rows(us, 8), hist, conf_w_dw[j], conf_b_dw[j], "ln_silu",
                                conf_ln_g[j], conf_ln_b[j], BF16)[:, 0, :]
            (xp_f, xp_b), (xs_f, xs_b) = matmul_postnorm(uc.reshape(mp, d_model), ucs, w_pw2, conf_b_pw2[j],
                                                         xp_f, xs_f, g1, b1, 1.0)
            outs["conf_cs"].append(jnp.concatenate([state_conf_conv[j], us], axis=1)[:, 1:, :])

        (xp_f, xp_b), (xs_f, xs_b) = _ffn_half(xp_f, xp_b, xs_f, xs_b, ffn_w1, ffn_w3, (i, 1), ffn_w2_b,
                                               ln_g[i, 2], ln_b[i, 2])
        wg_, wp_ = bf(ple_w_gate[i]), bf(ple_w_proj[i])
        xp_f, xp_b = ple_add(xp_b, xp_f, p_prompt[i].reshape(mp, -1), wg_, wp_, ln_g[i, 3], ln_b[i, 3])
        xs_f, xs_b = ple_add(xs_b, xs_f, p_sample[i].reshape(db, -1), wg_, wp_, ln_g[i, 3], ln_b[i, 3])

    st = lambda name: jnp.stack(outs[name])
    return (xp_f.reshape(bsz, seq, d_model), xs_f.reshape(db, 1, d_model),
            st("sb_kp"), st("sb_vp"), st("sb_ks"), st("sb_vs"),
            st("ssm_hp"), st("ssm_hs"), st("ssm_cp"), st("ssm_cs"),
            st("conf_cp"), st("conf_cs"),
            st("mo_kp"), st("mo_vp"), st("mo_ks"), st("mo_vs"))
```

```python
import functools

import jax
import jax.numpy as jnp
from jax import lax
from jax.experimental import pallas as pl
from jax.experimental.pallas import tpu as pltpu

F32 = jnp.float32
BF16 = jnp.bfloat16

N_HEADS = 16
HEAD_DIM = 128
N_KV_HEADS = 4
Q_PER_KV = N_HEADS // N_KV_HEADS
Q_DIM = N_HEADS * HEAD_DIM
KV_DIM = N_KV_HEADS * HEAD_DIM
SB_BLOCK = 128
SB_KV_PER_STEP = 2
MOBA_BLOCK = 256
MOBA_TOPK = 3
PAGE_SIZE = 128
SSM_HEAD_DIM = 64
SSM_GROUPS = 8
SSM_D_STATE = 128
SSM_CONV = 4
SSM_CHUNK = 128
CONF_WIDTH = 31
LN_EPS = 1e-5
DEPTH = 4
DN_ALPHA = (2 * DEPTH) ** 0.25
LANES = 128
NEG_BIG = -1e30
DWCONV_ROW_TILE = 64
DWCONV_LANE_CHUNK = 256
SB_LOG_FLOOR = -104.0
VMEM_LIMIT = 56 * 1024 * 1024


def _params(*sem):
    return pltpu.CompilerParams(dimension_semantics=sem, vmem_limit_bytes=VMEM_LIMIT)


def _layer_norm_rows(y, g, b):
    mu = jnp.mean(y, axis=-1, keepdims=True)
    d = y - mu
    var = jnp.mean(d * d, axis=-1, keepdims=True)
    return d * lax.rsqrt(var + LN_EPS) * g + b


def _split3(x):
    hi = x.astype(BF16)
    r1 = x - hi.astype(F32)
    mid = r1.astype(BF16)
    lo = (r1 - mid.astype(F32)).astype(BF16)
    return hi, mid, lo


def _dot(a, b):
    return jnp.dot(a, b, preferred_element_type=F32)


def _dot_nt(a, b):
    return lax.dot_general(a, b, (((1,), (1,)), ((), ())), preferred_element_type=F32)


def _dot01(x, m01):
    hi, mid, lo = _split3(x)
    return _dot(hi, m01) + _dot(mid, m01) + _dot(lo, m01)


def _pick_tile(n, pref):
    for t in pref:
        if n % t == 0:
            return t
    return n


def _mm_kernel(x_ref, xs_ref, w_ref, b_ref, o_ref, os_ref):
    o_ref[...] = (_dot(x_ref[...], w_ref[...]) + b_ref[...]).astype(o_ref.dtype)
    first = pl.program_id(0) == 0

    @pl.when(first)
    def _():
        os_ref[0] = (_dot(xs_ref[...], w_ref[...]) + b_ref[...]).astype(os_ref.dtype)

    @pl.when(jnp.logical_not(first))
    def _():
        os_ref[...] = jnp.zeros_like(os_ref)


def matmul(x, xs, w, bias=None, out_dtype=F32, col0=0, n=None):
    m, k = x.shape
    s = xs.shape[0]
    n = w.shape[1] - col0 if n is None else n
    tm = _pick_tile(m, (1024, 512, 256, 128, 8))
    tn = _pick_tile(n, (512, 256, 128))
    assert col0 % tn == 0 and n % tn == 0
    j0 = col0 // tn
    if bias is None:
        bias = jnp.zeros((n,), F32)
    out, out_s = pl.pallas_call(
        _mm_kernel,
        grid=(m // tm, n // tn),
        in_specs=[pl.BlockSpec((tm, k), lambda i, j: (i, 0)),
                  pl.BlockSpec((s, k), lambda i, j: (0, 0)),
                  pl.BlockSpec((k, tn), lambda i, j: (0, j + j0)),
                  pl.BlockSpec((1, tn), lambda i, j: (0, j))],
        out_specs=[pl.BlockSpec((tm, tn), lambda i, j: (i, j)),
                   pl.BlockSpec((1, s, tn), lambda i, j: (i, 0, j))],
        out_shape=[jax.ShapeDtypeStruct((m, n), out_dtype),
                   jax.ShapeDtypeStruct((m // tm, s, n), F32)],
        compiler_params=_params("arbitrary", "arbitrary"),
        name="mm",
    )(x, xs, w, bias.reshape(1, n).astype(F32))
    return out, out_s[0]


def _gated_mm_kernel(x_ref, xs_ref, wa_ref, wb_ref, ba_ref, bb_ref, o_ref, os_ref, wa_s, wb_s, *, mode):
    first = pl.program_id(1) == 0

    def gated(x):
        a = _dot(x, wa_s[...]) + ba_ref[...]
        b = _dot(x, wb_s[...]) + bb_ref[...]
        return a * jax.nn.sigmoid(a) * b if mode == "swiglu" else a * jax.nn.sigmoid(b)

    @pl.when(first)
    def _():
        wa_s[...] = wa_ref[...].astype(BF16)
        wb_s[...] = wb_ref[...].astype(BF16)
        os_ref[0] = gated(xs_ref[...]).astype(os_ref.dtype)

    @pl.when(jnp.logical_not(first))
    def _():
        os_ref[...] = jnp.zeros_like(os_ref)

    o_ref[...] = gated(x_ref[...]).astype(o_ref.dtype)


def gated_matmul(x, xs, wa, wb, ba, bb, mode, out_dtype, lead=(), cola=0, colb=0, n=None):
    m, k = x.shape
    s = xs.shape[0]
    n = wa.shape[-1] - cola if n is None else n
    wblock = (None,) * len(lead) + (k, None)
    tm = _pick_tile(m, (2048, 1024, 512, 256, 128, 8) if jnp.dtype(out_dtype).itemsize == 2
                    else (1024, 512, 256, 128, 8))
    tn = _pick_tile(n, (512, 256, 128))
    assert cola % tn == 0 and colb % tn == 0 and n % tn == 0
    ja, jb = cola // tn, colb // tn
    if ba is None:
        ba = jnp.zeros((n,), F32)
        bb = jnp.zeros((n,), F32)
    out, out_s = pl.pallas_call(
        functools.partial(_gated_mm_kernel, mode=mode),
        grid=(n // tn, m // tm),
        in_specs=[pl.BlockSpec((tm, k), lambda j, i: (i, 0)),
                  pl.BlockSpec((s, k), lambda j, i: (0, 0)),
                  pl.BlockSpec(wblock[:-1] + (tn,), lambda j, i: lead + (0, j + ja)),
                  pl.BlockSpec(wblock[:-1] + (tn,), lambda j, i: lead + (0, j + jb)),
                  pl.BlockSpec((1, tn), lambda j, i: (0, j)),
                  pl.BlockSpec((1, tn), lambda j, i: (0, j))],
        out_specs=[pl.BlockSpec((tm, tn), lambda j, i: (i, j)),
                   pl.BlockSpec((1, s, tn), lambda j, i: (i, 0, j))],
        out_shape=[jax.ShapeDtypeStruct((m, n), out_dtype),
                   jax.ShapeDtypeStruct((m // tm, s, n), out_dtype)],
        scratch_shapes=[pltpu.VMEM((k, tn), BF16), pltpu.VMEM((k, tn), BF16)],
        compiler_params=_params("arbitrary", "arbitrary"),
        name="gated_mm",
    )(x, xs, wa, wb, ba.reshape(1, n).astype(F32), bb.reshape(1, n).astype(F32))
    return out, out_s[0]


def _mm_ln_kernel(a_ref, as_ref, w_ref, bias_ref, res_ref, ress_ref, g_ref, b_ref, of_ref, ob_ref, ofs_ref, obs_ref,
                  accs_ref, *, scale, nk):
    i, k = pl.program_id(0), pl.program_id(1)

    def post_norm(res, acc):
        y = DN_ALPHA * res + scale * (acc + bias_ref[...])
        return _layer_norm_rows(y, g_ref[...], b_ref[...])

    def finish(acc):
        out = post_norm(res_ref[...], acc)
        of_ref[...] = out
        ob_ref[...] = out.astype(BF16)

    @pl.when(i == 0)
    def _():
        part = _dot(as_ref[...], w_ref[...])
        if nk > 1:
            @pl.when(k > 0)
            def _():
                accs_ref[...] += part

            @pl.when(k == 0)
            def _():
                accs_ref[...] = part
        else:
            accs_ref[...] = part

    @pl.when(k == nk - 1)
    def _():
        out = post_norm(ress_ref[...], accs_ref[...])
        first = i == 0
        ofs_ref[0] = jnp.where(first, out, 0.0)
        obs_ref[0] = jnp.where(first, out, 0.0).astype(BF16)

    if nk == 1:
        finish(_dot(a_ref[...], w_ref[...]))
        return

    @pl.when(k == 0)
    def _():
        of_ref[...] = _dot(a_ref[...], w_ref[...])

    @pl.when(jnp.logical_and(k > 0, k < nk - 1))
    def _():
        of_ref[...] += _dot(a_ref[...], w_ref[...])

    @pl.when(k == nk - 1)
    def _():
        finish(of_ref[...] + _dot(a_ref[...], w_ref[...]))


def matmul_postnorm(a, a_s, w, bias, res, res_s, g, b, scale, lead=()):
    m, kdim = a.shape
    s = a_s.shape[0]
    n = w.shape[-1]
    tm = _pick_tile(m, (512, 256, 128, 8))
    tk = _pick_tile(kdim, (2816, 2048, 1408, 1024, 512, 256, 128))
    nk = kdim // tk
    if bias is None:
        bias = jnp.zeros((n,), F32)
    row = lambda v: v.reshape(1, n).astype(F32)
    vec = pl.BlockSpec((1, n), lambda i, k: (0, 0))
    of, ob, ofs, obs = pl.pallas_call(
        functools.partial(_mm_ln_kernel, scale=scale, nk=nk),
        grid=(m // tm, nk),
        in_specs=[pl.BlockSpec((tm, tk), lambda i, k: (i, k)),
                  pl.BlockSpec((s, tk), lambda i, k: (0, k)),
                  pl.BlockSpec((None,) * len(lead) + (tk, n), lambda i, k: lead + (k, 0)),
                  vec,
                  pl.BlockSpec((tm, n), lambda i, k: (i, 0)),
                  pl.BlockSpec((s, n), lambda i, k: (0, 0)),
                  vec, vec],
        out_specs=[pl.BlockSpec((tm, n), lambda i, k: (i, 0)),
                   pl.BlockSpec((tm, n), lambda i, k: (i, 0)),
                   pl.BlockSpec((1, s, n), lambda i, k: (i, 0, 0)),
                   pl.BlockSpec((1, s, n), lambda i, k: (i, 0, 0))],
        out_shape=[jax.ShapeDtypeStruct((m, n), F32), jax.ShapeDtypeStruct((m, n), BF16),
                   jax.ShapeDtypeStruct((m // tm, s, n), F32), jax.ShapeDtypeStruct((m // tm, s, n), BF16)],
        scratch_shapes=[pltpu.VMEM((s, n), F32)],
        compiler_params=_params("arbitrary", "arbitrary"),
        name="mm_postnorm",
    )(a, a_s, w, row(bias), res, res_s, row(g), row(b))
    return (of, ob), (ofs[0], obs[0])


def _ple_kernel(xb_ref, xf_ref, p_ref, wg_ref, wp_ref, g_ref, b_ref, of_ref, ob_ref):
    gate = jax.nn.sigmoid(_dot(xb_ref[...], wg_ref[...]))
    proj = _dot(p_ref[...].astype(BF16), wp_ref[...])
    y = DN_ALPHA * xf_ref[...] + gate * proj
    out = _layer_norm_rows(y, g_ref[...], b_ref[...])
    of_ref[...] = out
    ob_ref[...] = out.astype(BF16)


def ple_add(xb, xf, p, wg, wp, g, b):
    m, d = xf.shape
    pd = p.shape[1]
    tm = _pick_tile(m, (512, 256, 128, 8))
    row = lambda v: v.reshape(1, d).astype(F32)
    return pl.pallas_call(
        _ple_kernel,
        grid=(m // tm,),
        in_specs=[pl.BlockSpec((tm, d), lambda i: (i, 0)),
                  pl.BlockSpec((tm, d), lambda i: (i, 0)),
                  pl.BlockSpec((tm, pd), lambda i: (i, 0)),
                  pl.BlockSpec((d, d), lambda i: (0, 0)),
                  pl.BlockSpec((pd, d), lambda i: (0, 0)),
                  pl.BlockSpec((1, d), lambda i: (0, 0)),
                  pl.BlockSpec((1, d), lambda i: (0, 0))],
        out_specs=[pl.BlockSpec((tm, d), lambda i: (i, 0)),
                   pl.BlockSpec((tm, d), lambda i: (i, 0))],
        out_shape=[jax.ShapeDtypeStruct((m, d), F32), jax.ShapeDtypeStruct((m, d), BF16)],
        compiler_params=_params("parallel"),
        name="ple_add",
    )(xb, xf, p, wg, wp, row(g), row(b))


def _softplus(z):
    return jnp.maximum(z, 0.0) + jnp.log1p(jnp.exp(-jnp.abs(z)))


def _sb_prompt_kernel(q_ref, k_ref, v_ref, o_ref, *, tq, nkv):
    i = pl.program_id(2)
    rows = Q_PER_KV * tq
    scale = HEAD_DIM ** -0.5
    gw = Q_PER_KV * HEAD_DIM
    qs = [jnp.concatenate([q_ref[0, :, h * gw + g * HEAD_DIM:h * gw + (g + 1) * HEAD_DIM] for g in range(Q_PER_KV)],
                          axis=0) for h in range(nkv)]
    t_loc = lax.broadcasted_iota(jnp.int32, (rows, tq), 0) & (tq - 1)
    s_loc = lax.broadcasted_iota(jnp.int32, (rows, tq), 1)
    later = (lax.broadcasted_iota(jnp.int32, (tq, tq), 0) > lax.broadcasted_iota(jnp.int32, (tq, tq), 1)).astype(BF16)

    def block(h, j, run, acc, diagonal):
        start = pl.multiple_of(j * tq, tq)
        kb = k_ref[0, pl.ds(start, tq), h * HEAD_DIM:(h + 1) * HEAD_DIM].astype(BF16)
        vb = v_ref[0, pl.ds(start, tq), h * HEAD_DIM:(h + 1) * HEAD_DIM].astype(BF16)
        z = _dot_nt(qs[h], kb) * scale
        log_keep = -_softplus(z)
        if diagonal:
            valid = s_loc < t_loc
            log_keep = jnp.where(valid, log_keep, 0.0)
        log_w = z + log_keep + _dot01(log_keep, later) + run
        w = jnp.exp(log_w)
        if diagonal:
            w = jnp.where(valid, w, 0.0)
        acc = acc + _dot(w.astype(BF16), vb)
        run = run + jnp.sum(log_keep, axis=1, keepdims=True)
        return run, acc

    run0 = jnp.zeros((rows, 1), F32)
    acc0 = jnp.zeros((rows, HEAD_DIM), F32)
    state = []
    for h in range(nkv):
        state += list(block(h, i, run0, acc0, True))

    def cond(carry):
        live = jnp.max(carry[1])
        for h in range(1, nkv):
            live = jnp.maximum(live, jnp.max(carry[1 + 2 * h]))
        return jnp.logical_and(carry[0] < i, live > SB_LOG_FLOOR)

    def body(carry):
        it = carry[0]
        out = [it + 1]
        for h in range(nkv):
            out += list(block(h, i - 1 - it, carry[1 + 2 * h], carry[2 + 2 * h], False))
        return tuple(out)

    final = lax.while_loop(cond, body, (jnp.int32(0),) + tuple(state))
    for h in range(nkv):
        acc = final[2 + 2 * h]
        for g in range(Q_PER_KV):
            o_ref[0, :, h * gw + g * HEAD_DIM:h * gw + (g + 1) * HEAD_DIM] = acc[g * tq:(g + 1) * tq].astype(o_ref.dtype)


def sb_prompt_attention(q, k, v):
    bsz, L, _ = q.shape
    tq = SB_BLOCK
    nkv = SB_KV_PER_STEP
    gw = nkv * Q_PER_KV * HEAD_DIM
    return pl.pallas_call(
        functools.partial(_sb_prompt_kernel, tq=tq, nkv=nkv),
        grid=(bsz, N_KV_HEADS // nkv, L // tq),
        in_specs=[pl.BlockSpec((1, tq, gw), lambda b, h, i: (b, i, h)),
                  pl.BlockSpec((1, L, nkv * HEAD_DIM), lambda b, h, i: (b, 0, h)),
                  pl.BlockSpec((1, L, nkv * HEAD_DIM), lambda b, h, i: (b, 0, h))],
        out_specs=pl.BlockSpec((1, tq, gw), lambda b, h, i: (b, i, h)),
        out_shape=jax.ShapeDtypeStruct(q.shape, BF16),
        compiler_params=_params("parallel", "parallel", "arbitrary"),
        name="sb_prompt",
    )(q, k, v)


PAGE_ROWS = PAGE_SIZE * N_KV_HEADS
SB_PAGES_PER_GROUP = 4
MOBA_BLOCKS_PER_STEP = 4


def _own_lane_mask(kvh_of_row, shape):
    lane = lax.broadcasted_iota(jnp.int32, shape, 1)
    return (lane & (N_KV_HEADS - 1)) == kvh_of_row


def _suffix_sum_keys(x):
    n = x.shape[1]
    lane = lax.broadcasted_iota(jnp.int32, x.shape, 1)
    d = N_KV_HEADS
    while d < n:
        shifted = pltpu.roll(x, n - d, axis=1)
        x = x + jnp.where(lane + d < n, shifted, 0.0)
        d *= 2
    return x


def _sb_decode_kernel(pt_ref, q_ref, k_hbm, v_hbm, o_ref, kbuf, vbuf, sems, acc_ref, run_ref, *, n_pages, gp):
    b = pl.program_id(0)
    n_groups = n_pages // gp

    def page_copies(g, slot):
        copies = []
        for u in range(gp):
            page = pt_ref[b * n_pages + (n_pages - 1 - (g * gp + u))]
            rows = pl.ds(pl.multiple_of(page * PAGE_ROWS, PAGE_ROWS), PAGE_ROWS)
            copies.append(pltpu.make_async_copy(k_hbm.at[rows], kbuf.at[slot, u], sems.at[0, slot, u]))
            copies.append(pltpu.make_async_copy(v_hbm.at[rows], vbuf.at[slot, u], sems.at[1, slot, u]))
        return copies

    def start(g, slot):
        for c in page_copies(g, slot):
            c.start()

    def wait(g, slot):
        for c in page_copies(g, slot):
            c.wait()

    acc_ref[...] = jnp.zeros_like(acc_ref)
    run_ref[...] = jnp.zeros_like(run_ref)
    q = q_ref[0].astype(BF16)
    kvh = lax.broadcasted_iota(jnp.int32, (N_HEADS, PAGE_ROWS), 0) >> 2
    own = _own_lane_mask(kvh, (N_HEADS, PAGE_ROWS))
    start(0, 0)

    def live(_):
        return jnp.max(run_ref[...]) > SB_LOG_FLOOR

    def body(g):
        slot = g & 1
        wait(g, slot)

        @pl.when(g + 1 < n_groups)
        def _():
            start(g + 1, 1 - slot)

        for u in range(gp):
            @pl.when(live(None))
            def _():
                z = _dot_nt(q, kbuf[slot, u].astype(BF16)) * (HEAD_DIM ** -0.5)
                log_keep = jnp.where(own, -_softplus(z), 0.0)
                incl = _suffix_sum_keys(log_keep)
                w = jnp.where(own, jnp.exp(z + incl + run_ref[:, 0:1]), 0.0)
                acc_ref[...] += _dot(w.astype(BF16), vbuf[slot, u].astype(BF16))
                run_ref[...] += jnp.sum(log_keep, axis=1, keepdims=True)
        return g + 1

    g_end = lax.while_loop(lambda g: jnp.logical_and(g < n_groups, live(None)), body, jnp.int32(0))

    @pl.when(g_end < n_groups)
    def _():
        wait(g_end, g_end & 1)

    o_ref[0] = acc_ref[...]


def sb_decode_attention(q, k_rows, v_rows, page_table):
    db, n_pages = page_table.shape
    gp = SB_PAGES_PER_GROUP
    assert n_pages % gp == 0
    return pl.pallas_call(
        functools.partial(_sb_decode_kernel, n_pages=n_pages, gp=gp),
        grid_spec=pltpu.PrefetchScalarGridSpec(
            num_scalar_prefetch=1,
            grid=(db,),
            in_specs=[pl.BlockSpec((1, N_HEADS, HEAD_DIM), lambda b, pt: (b, 0, 0)),
                      pl.BlockSpec(memory_space=pl.ANY), pl.BlockSpec(memory_space=pl.ANY)],
            out_specs=pl.BlockSpec((1, N_HEADS, HEAD_DIM), lambda b, pt: (b, 0, 0)),
            scratch_shapes=[pltpu.VMEM((2, gp, PAGE_ROWS, HEAD_DIM), F32),
                            pltpu.VMEM((2, gp, PAGE_ROWS, HEAD_DIM), F32),
                            pltpu.SemaphoreType.DMA((2, 2, gp)),
                            pltpu.VMEM((N_HEADS, HEAD_DIM), F32), pltpu.VMEM((N_HEADS, LANES), F32)]),
        out_shape=jax.ShapeDtypeStruct((db, N_HEADS, HEAD_DIM), F32),
        compiler_params=_params("arbitrary"),
        name="sb_decode",
    )(page_table.reshape(-1), q, k_rows, v_rows)


def _top_k_mask(gate, n_valid, k):
    lane = lax.broadcasted_iota(jnp.int32, gate.shape, 1)
    sel = jnp.zeros(gate.shape, F32)
    picks = []
    for r in range(k):
        m = jnp.max(gate, axis=1, keepdims=True)
        idx = jnp.min(jnp.where(gate == m, lane, gate.shape[1] - 1), axis=1, keepdims=True)
        pick = lane == idx
        sel = jnp.maximum(sel, jnp.where(pick, jnp.where(r < n_valid, 1.0, 0.0), 0.0))
        gate = jnp.where(pick, -jnp.inf, gate)
        picks.append(idx)
    return sel, picks


def _top_k_mask_sublanes(gate_t, n_valid, k):
    sub = lax.broadcasted_iota(jnp.int32, gate_t.shape, 0)
    sel = jnp.zeros(gate_t.shape, F32)
    for r in range(k):
        m = jnp.max(gate_t, axis=0, keepdims=True)
        idx = jnp.min(jnp.where(gate_t == m, sub, gate_t.shape[0] - 1), axis=0, keepdims=True)
        pick = sub == idx
        sel = jnp.maximum(sel, jnp.where(pick, jnp.where(r < n_valid, 1.0, 0.0), 0.0))
        gate_t = jnp.where(pick, -jnp.inf, gate_t)
    return sel


def _moba_prompt_kernel(q_ref, k_ref, v_ref, o_ref, km_ref, *, tq, nb):
    i = pl.program_id(2)
    rows = Q_PER_KV * tq
    scale = HEAD_DIM ** -0.5

    @pl.when(i == 0)
    def _():
        km_ref[...] = jnp.zeros_like(km_ref)
        for jb in range(nb):
            km_ref[jb:jb + 1, :] = jnp.mean(k_ref[0, jb * tq:(jb + 1) * tq, :], axis=0, keepdims=True)

    q = jnp.concatenate([q_ref[0, :, g * HEAD_DIM:(g + 1) * HEAD_DIM] for g in range(Q_PER_KV)], axis=0)
    cands = -(-nb // 16) * 16
    gate_t = _dot_nt(km_ref[0:cands, :].astype(BF16), q)
    gate_t = jnp.where(lax.broadcasted_iota(jnp.int32, (cands, rows), 0) < i, gate_t, -jnp.inf)
    sel_t = _top_k_mask_sublanes(gate_t, i, MOBA_TOPK)
    sel = jnp.concatenate([sel_t, jnp.zeros((LANES - cands, rows), F32)], axis=0).T.astype(BF16)

    def scores(j):
        start = pl.multiple_of(j * tq, tq)
        kb = k_ref[0, pl.ds(start, tq), :].astype(BF16)
        vb = v_ref[0, pl.ds(start, tq), :].astype(BF16)
        return _dot_nt(q, kb) * scale, vb

    s, vb = scores(i)
    t_loc = lax.broadcasted_iota(jnp.int32, (rows, tq), 0) & (tq - 1)
    s_loc = lax.broadcasted_iota(jnp.int32, (rows, tq), 1)
    s = jnp.where(s_loc <= t_loc, s, NEG_BIG)
    m = jnp.max(s, axis=1, keepdims=True)
    p = jnp.exp(s - m)
    l = jnp.sum(p, axis=1, keepdims=True)
    acc = _dot(p.astype(BF16), vb)

    def body(j, carry):
        m, l, acc = carry
        s, vb = scores(j)
        onehot = (lax.broadcasted_iota(jnp.int32, (LANES, tq), 0) == j).astype(BF16)
        chosen = _dot(sel, onehot) > 0.5
        s = jnp.where(chosen, s, NEG_BIG)
        m_new = jnp.maximum(m, jnp.max(s, axis=1, keepdims=True))
        alpha = jnp.exp(m - m_new)
        p = jnp.exp(s - m_new)
        l = alpha * l + jnp.sum(p, axis=1, keepdims=True)
        acc = alpha * acc + _dot(p.astype(BF16), vb)
        return m_new, l, acc

    m, l, acc = lax.fori_loop(0, i, body, (m, l, acc))
    out = acc / l
    for g in range(Q_PER_KV):
        o_ref[0, :, g * HEAD_DIM:(g + 1) * HEAD_DIM] = out[g * tq:(g + 1) * tq].astype(o_ref.dtype)


def moba_prompt_attention(q, k, v):
    bsz, L, _ = q.shape
    tq = MOBA_BLOCK
    nb = L // tq
    assert L % tq == 0 and nb <= LANES
    gw = Q_PER_KV * HEAD_DIM
    return pl.pallas_call(
        functools.partial(_moba_prompt_kernel, tq=tq, nb=nb),
        grid=(bsz, N_KV_HEADS, nb),
        in_specs=[pl.BlockSpec((1, tq, gw), lambda b, h, i: (b, i, h)),
                  pl.BlockSpec((1, L, HEAD_DIM), lambda b, h, i: (b, 0, h)),
                  pl.BlockSpec((1, L, HEAD_DIM), lambda b, h, i: (b, 0, h))],
        out_specs=pl.BlockSpec((1, tq, gw), lambda b, h, i: (b, i, h)),
        out_shape=jax.ShapeDtypeStruct(q.shape, BF16),
        scratch_shapes=[pltpu.VMEM((LANES, HEAD_DIM), F32)],
        compiler_params=_params("parallel", "arbitrary", "arbitrary"),
        name="moba_prompt",
    )(q, k, v)


def _moba_kmean_kernel(pt_ref, *refs, bps, ppb):
    k_refs, o_ref = refs[:bps * ppb], refs[bps * ppb]
    sub = 8
    for j in range(bps):
        tot = jnp.zeros((sub, HEAD_DIM), F32)
        for h in range(ppb):
            k_ref = k_refs[j * ppb + h]
            for c in range(PAGE_ROWS // sub):
                tot = tot + k_ref[c * sub:(c + 1) * sub, :]
        mean = (tot[:N_KV_HEADS] + tot[N_KV_HEADS:]) * (1.0 / MOBA_BLOCK)
        o_ref[0, j * N_KV_HEADS:(j + 1) * N_KV_HEADS, :] = mean


def moba_cache_block_means(k_rows, page_table):
    db, n_pages = page_table.shape
    ppb = MOBA_BLOCK // PAGE_SIZE
    nblk = n_pages // ppb
    bps = MOBA_BLOCKS_PER_STEP
    assert nblk % bps == 0

    def page_spec(u):
        return pl.BlockSpec((PAGE_ROWS, HEAD_DIM), lambda b, j, pt: (pt[b * n_pages + j * (bps * ppb) + u], 0))

    return pl.pallas_call(
        functools.partial(_moba_kmean_kernel, bps=bps, ppb=ppb),
        grid_spec=pltpu.PrefetchScalarGridSpec(
            num_scalar_prefetch=1,
            grid=(db, nblk // bps),
            in_specs=[page_spec(u) for u in range(bps * ppb)],
            out_specs=pl.BlockSpec((1, bps * N_KV_HEADS, HEAD_DIM), lambda b, j, pt: (b, j, 0))),
        out_shape=jax.ShapeDtypeStruct((db, nblk * N_KV_HEADS, HEAD_DIM), F32),
        compiler_params=_params("parallel", "arbitrary"),
        name="moba_kmean",
    )(page_table.reshape(-1), *([k_rows] * (bps * ppb)))


def _moba_gate_kernel(q_ref, km_ref, o_ref, *, nblk):
    gate = _dot_nt(q_ref[0].astype(BF16), km_ref[0].astype(BF16))
    kvh = lax.broadcasted_iota(jnp.int32, gate.shape, 0) >> 2
    gate = jnp.where(_own_lane_mask(kvh, gate.shape), gate, -jnp.inf)
    _, picks = _top_k_mask(gate, nblk, MOBA_TOPK)
    lane = lax.broadcasted_iota(jnp.int32, (N_HEADS, LANES), 1)
    out = jnp.zeros((N_HEADS, LANES), jnp.int32)
    for r, idx in enumerate(picks):
        out = jnp.where(lane == r, idx >> 2, out)
    o_ref[0] = out


def moba_decode_select(q, kmean):
    db, rows, _ = kmean.shape
    nblk = rows // N_KV_HEADS
    out = pl.pallas_call(
        functools.partial(_moba_gate_kernel, nblk=nblk),
        grid=(db,),
        in_specs=[pl.BlockSpec((1, N_HEADS, HEAD_DIM), lambda b: (b, 0, 0)),
                  pl.BlockSpec((1, rows, HEAD_DIM), lambda b: (b, 0, 0))],
        out_specs=pl.BlockSpec((1, N_HEADS, LANES), lambda b: (b, 0, 0)),
        out_shape=jax.ShapeDtypeStruct((db, N_HEADS, LANES), jnp.int32),
        compiler_params=_params("parallel"),
        name="moba_gate",
    )(q, kmean)
    return out[:, :, :MOBA_TOPK]


def _moba_decode_kernel(sel_ref, pt_ref, q_ref, kn_ref, vn_ref, *refs, n_sel):
    k_refs, v_refs, o_ref = refs[:n_sel], refs[n_sel:2 * n_sel], refs[2 * n_sel]
    scale = HEAD_DIM ** -0.5
    q = jnp.broadcast_to(q_ref[0, 0], (8, HEAD_DIM))
    own = _own_lane_mask(pl.program_id(1) >> 2, (8, PAGE_ROWS))
    m = jnp.sum(q * kn_ref[0, 0], axis=1, keepdims=True) * scale
    l = jnp.ones((8, 1), F32)
    acc = jnp.broadcast_to(vn_ref[0, 0], (8, HEAD_DIM))
    qb = q.astype(BF16)
    for u in range(n_sel):
        s = jnp.where(own, _dot_nt(qb, k_refs[u][...].astype(BF16)) * scale, NEG_BIG)
        m_new = jnp.maximum(m, jnp.max(s, axis=1, keepdims=True))
        alpha = jnp.exp(m - m_new)
        p = jnp.exp(s - m_new)
        l = alpha * l + jnp.sum(p, axis=1, keepdims=True)
        acc = alpha * acc + _dot(p.astype(BF16), v_refs[u][...].astype(BF16))
        m = m_new
    o_ref[0, 0] = acc / l


def moba_decode_attention(q, k_new, v_new, sel, k_rows, v_rows, page_table):
    db, n_pages = page_table.shape
    ppb = MOBA_BLOCK // PAGE_SIZE
    n_sel = MOBA_TOPK * ppb

    def page_spec(u):
        def index(b, h, sel_r, pt):
            blk = sel_r[(b * N_HEADS + h) * MOBA_TOPK + u // ppb]
            return (pt[b * n_pages + blk * ppb + u % ppb], 0)
        return pl.BlockSpec((PAGE_ROWS, HEAD_DIM), index)

    specs = [page_spec(u) for u in range(n_sel)]
    new_spec = pl.BlockSpec((1, 1, 1, HEAD_DIM), lambda b, h, sel_r, pt: (b, h // Q_PER_KV, 0, 0))
    out = pl.pallas_call(
        functools.partial(_moba_decode_kernel, n_sel=n_sel),
        grid_spec=pltpu.PrefetchScalarGridSpec(
            num_scalar_prefetch=2,
            grid=(db, N_HEADS),
            in_specs=[pl.BlockSpec((1, 1, 1, HEAD_DIM), lambda b, h, sel_r, pt: (b, h, 0, 0)),
                      new_spec, new_spec] + specs + specs,
            out_specs=pl.BlockSpec((1, 1, 8, HEAD_DIM), lambda b, h, sel_r, pt: (b, h, 0, 0))),
        out_shape=jax.ShapeDtypeStruct((db, N_HEADS, 8, HEAD_DIM), F32),
        compiler_params=_params("parallel", "parallel"),
        name="moba_decode",
    )(sel.reshape(-1), page_table.reshape(-1), q, k_new, v_new, *([k_rows] * n_sel), *([v_rows] * n_sel))
    return out[:, :, 0, :]


def _dwconv_kernel(x_ref, hist_ref, w_ref, b_ref, g_ref, beta_ref, o_ref, pad_ref, *shift_refs, width, tl, hp, post):
    t = pl.program_id(2)

    @pl.when(t == 0)
    def _():
        pad_ref[0:hp, :] = hist_ref[0]

    @pl.when(t > 0)
    def _():
        pad_ref[0:hp, :] = pad_ref[tl:tl + hp, :]

    pad_ref[hp:hp + tl, :] = x_ref[0]

    def finish(acc):
        if post == "ln_silu":
            acc = _layer_norm_rows(acc, g_ref[...], beta_ref[...])
        return (acc * jax.nn.sigmoid(acc)).astype(o_ref.dtype)

    if not shift_refs:
        base = hp - (width - 1)
        acc = jnp.zeros(o_ref.shape[1:], F32) + b_ref[...]
        for k in range(width):
            acc = acc + pad_ref[base + k:base + k + tl, :] * w_ref[k:k + 1, :]
        o_ref[0] = finish(acc)
        return

    sh_ref, wb_ref = shift_refs
    halo = 8 * ((width - 1) // 8)
    for r in range(1, 8):
        sh_ref[r - 1] = pad_ref[hp - halo - r:hp + tl - r, :]
    rt = min(DWCONV_ROW_TILE, tl)
    tc = o_ref.shape[2]
    lc = DWCONV_LANE_CHUNK if tc % DWCONV_LANE_CHUNK == 0 else tc
    for k in range(width):
        wb_ref[k] = jnp.broadcast_to(w_ref[k:k + 1, :], (8, tc))

    def row_tile(row0):
        parts = []
        for c0 in range(0, tc, lc):
            acc = jnp.zeros((rt // 8, 8, lc), F32) + b_ref[:, c0:c0 + lc]
            for r in range(8):
                for a in range((width - 1 - r) // 8 + 1):
                    k = width - 1 - (8 * a + r)
                    if r == 0:
                        src = pad_ref[pl.ds(row0 + (hp - 8 * a), rt), c0:c0 + lc]
                    else:
                        src = sh_ref[r - 1, pl.ds(row0 + (halo - 8 * a), rt), c0:c0 + lc]
                    acc = acc + src.reshape(rt // 8, 8, lc) * wb_ref[k, :, c0:c0 + lc]
            parts.append(acc.reshape(rt, lc))
        o_ref[0, pl.ds(row0, rt), :] = finish(jnp.concatenate(parts, axis=1))

    if tl == rt:
        row_tile(0)
    else:
        def body(it, carry):
            row_tile(pl.multiple_of(it * rt, rt))
            return carry

        lax.fori_loop(0, tl // rt, body, 0)


def causal_dwconv(x, hist, w, b, post, ln_g=None, ln_b=None, out_dtype=F32):
    bsz, L, C = x.shape
    width = w.shape[0]
    hp = hist.shape[1]
    assert hp % 8 == 0 and hp >= 8 * ((width - 1) // 8) + min(7, width - 1)
    tc = C if post == "ln_silu" else _pick_tile(C, (1024, 512, 256, 128))
    tl = _pick_tile(L, (256, 128)) if L >= hp else L
    assert tl >= hp or tl == L
    wp = jnp.zeros((-(-width // 8) * 8, C), F32).at[:width].set(w.astype(F32))
    if ln_g is None:
        ln_g = jnp.ones((C,), F32)
        ln_b = jnp.zeros((C,), F32)
    row = lambda v: v.reshape(1, C).astype(F32)
    vec_spec = pl.BlockSpec((1, tc), lambda bb, c, t: (0, c))
    return pl.pallas_call(
        functools.partial(_dwconv_kernel, width=width, tl=tl, hp=hp, post=post),
        grid=(bsz, C // tc, L // tl),
        in_specs=[pl.BlockSpec((1, tl, tc), lambda bb, c, t: (bb, t, c)),
                  pl.BlockSpec((1, hp, tc), lambda bb, c, t: (bb, 0, c)),
                  pl.BlockSpec((wp.shape[0], tc), lambda bb, c, t: (0, c)),
                  vec_spec, vec_spec, vec_spec],
        out_specs=pl.BlockSpec((1, tl, tc), lambda bb, c, t: (bb, t, c)),
        out_shape=jax.ShapeDtypeStruct((bsz, L, C), out_dtype),
        scratch_shapes=[pltpu.VMEM((hp + tl, tc), F32)] + (
            [pltpu.VMEM((7, 8 * ((width - 1) // 8) + tl, tc), F32),
             pltpu.VMEM((width, 8, tc), F32)] if width > 8 else []),
        compiler_params=_params("parallel", "parallel", "arbitrary"),
        name="dwconv_" + post,
    )(x, hist, wp, row(b), row(ln_g), row(ln_b))


def _ssd_prompt_kernel(x_ref, bm_ref, cm_ref, dt_ref, z_ref, dtb_ref, alog_ref, d_ref, ng_ref, h0_ref,
                       yn_ref, h_ref, y_buf, *, chunk, hpg):
    c = pl.program_id(1)
    P, N = SSM_HEAD_DIM, SSM_D_STATE
    gw = hpg * P

    @pl.when(c == 0)
    def _():
        h_ref[...] = h0_ref[...]

    li = lax.broadcasted_iota(jnp.int32, (chunk, chunk), 0)
    si = lax.broadcasted_iota(jnp.int32, (chunk, chunk), 1)
    causal = li >= si
    lower = causal.astype(BF16)
    dt = _softplus(dt_ref[0] + dtb_ref[...])
    a = -jnp.exp(alog_ref[...])
    hi, mid, lo = _split3(dt * a)
    cum = _dot(lower, hi) + _dot(lower, mid) + _dot(lower, lo)
    cum_t = cum.T
    dt_t = dt.T
    for g in range(SSM_GROUPS):
        x = x_ref[0, :, g * gw:(g + 1) * gw]
        cmb = cm_ref[0, :, g * N:(g + 1) * N].astype(BF16)
        bmb = bm_ref[0, :, g * N:(g + 1) * N].astype(BF16)
        cb = _dot_nt(cmb, bmb)
        x_t = x.T
        for r in range(hpg):
            head = g * hpg + r
            cum_col = cum[:, head:head + 1]
            cum_row = cum_t[head:head + 1, :]
            dt_row = dt_t[head:head + 1, :]
            total = cum_t[head:head + 1, chunk - 1:chunk]
            xh = x[:, r * P:(r + 1) * P]
            decay = jnp.exp(jnp.where(causal, cum_col - cum_row, -jnp.inf))
            mix = (cb * decay * dt_row).astype(BF16)
            y = _dot(mix, xh.astype(BF16))
            h_old = h_ref[0, head]
            y = y + _dot_nt(cmb, h_old.astype(BF16)) * jnp.exp(cum_col)
            y_buf[:, r * P:(r + 1) * P] = y + d_ref[:, head:head + 1] * xh
            to_end = jnp.exp(total - cum_row) * dt_row
            s_chunk = _dot((x_t[r * P:(r + 1) * P, :] * to_end).astype(BF16), bmb)
            h_ref[0, head] = jnp.exp(total) * h_old + s_chunk
        z = z_ref[0, :, g * gw:(g + 1) * gw]
        hg = y_buf[...] * (z * jax.nn.sigmoid(z))
        hg = hg * lax.rsqrt(jnp.mean(hg * hg, axis=-1, keepdims=True) + LN_EPS)
        yn_ref[0, :, g * gw:(g + 1) * gw] = (hg * ng_ref[:, g * gw:(g + 1) * gw]).astype(yn_ref.dtype)


def ssd_prompt(xbc, dt_raw, z, dt_bias_g, a_log_g, d_g, norm_g, h0):
    bsz, L, _ = xbc.shape
    n_heads = h0.shape[1]
    G, N, P = SSM_GROUPS, SSM_D_STATE, SSM_HEAD_DIM
    hpg = n_heads // G
    d_inner = n_heads * P
    gn = G * N
    chunk = SSM_CHUNK
    assert d_inner % gn == 0 and (hpg * P) % LANES == 0 and n_heads <= LANES
    b_off = d_inner // gn
    vec_spec = pl.BlockSpec((1, LANES), lambda b, c: (0, 0))
    state_spec = pl.BlockSpec((1, n_heads, P, N), lambda b, c: (b, 0, 0, 0))
    return pl.pallas_call(
        functools.partial(_ssd_prompt_kernel, chunk=chunk, hpg=hpg),
        grid=(bsz, L // chunk),
        in_specs=[pl.BlockSpec((1, chunk, d_inner), lambda b, c: (b, c, 0)),
                  pl.BlockSpec((1, chunk, gn), lambda b, c: (b, c, b_off)),
                  pl.BlockSpec((1, chunk, gn), lambda b, c: (b, c, b_off + 1)),
                  pl.BlockSpec((1, chunk, LANES), lambda b, c: (b, c, 0)),
                  pl.BlockSpec((1, chunk, d_inner), lambda b, c: (b, c, 0)),
                  vec_spec, vec_spec, vec_spec,
                  pl.BlockSpec((1, d_inner), lambda b, c: (0, 0)),
                  state_spec],
        out_specs=[pl.BlockSpec((1, chunk, d_inner), lambda b, c: (b, c, 0)), state_spec],
        out_shape=[jax.ShapeDtypeStruct((bsz, L, d_inner), BF16),
                   jax.ShapeDtypeStruct((bsz, n_heads, P, N), F32)],
        scratch_shapes=[pltpu.VMEM((chunk, hpg * P), F32)],
        compiler_params=_params("parallel", "arbitrary"),
        name="ssd_prompt",
    )(xbc, xbc, xbc, dt_raw, z, dt_bias_g, a_log_g, d_g, norm_g.reshape(1, d_inner).astype(F32), h0)


def _ssd_step_kernel(xc_ref, bc_ref, dt_ref, dtb_ref, alog_ref, d_ref, h0_ref, y_ref, h_ref, *, n_heads):
    N = SSM_D_STATE
    hpg = n_heads // SSM_GROUPS
    dt = _softplus(dt_ref[0] + dtb_ref[...])
    da = jnp.exp(dt * (-jnp.exp(alog_ref[...])))
    for h in range(n_heads):
        g = h // hpg
        bm = bc_ref[0, :, g * N:(g + 1) * N]
        cm = bc_ref[0, :, (SSM_GROUPS + g) * N:(SSM_GROUPS + g + 1) * N]
        xcol = xc_ref[0, h]
        h_new = da[:, h:h + 1] * h0_ref[0, h] + xcol * (dt[:, h:h + 1] * bm)
        h_ref[0, h] = h_new
        y_ref[0, h] = jnp.sum(h_new * cm, axis=1, keepdims=True) + d_ref[:, h:h + 1] * xcol


def ssd_step(x_col, bc, dt_raw, dt_bias, a_log, d_skip, h0):
    db, n_heads, P, _ = x_col.shape
    N = SSM_D_STATE
    vec_spec = pl.BlockSpec((1, LANES), lambda b: (0, 0))
    state_spec = pl.BlockSpec((1, n_heads, P, N), lambda b: (b, 0, 0, 0))
    col_spec = pl.BlockSpec((1, n_heads, P, 1), lambda b: (b, 0, 0, 0))
    return pl.pallas_call(
        functools.partial(_ssd_step_kernel, n_heads=n_heads),
        grid=(db,),
        in_specs=[col_spec,
                  pl.BlockSpec((1, 1, bc.shape[2]), lambda b: (b, 0, 0)),
                  pl.BlockSpec((1, 1, LANES), lambda b: (b, 0, 0)),
                  vec_spec, vec_spec, vec_spec, state_spec],
        out_specs=[col_spec, state_spec],
        out_shape=[jax.ShapeDtypeStruct((db, n_heads, P, 1), F32),
                   jax.ShapeDtypeStruct((db, n_heads, P, N), F32)],
        compiler_params=_params("parallel"),
        name="ssd_step",
    )(x_col, bc, dt_raw, dt_bias, a_log, d_skip, h0)


def _gated_norm_kernel(y_ref, z_ref, g_ref, o_ref):
    z = z_ref[...]
    h = y_ref[...] * (z * jax.nn.sigmoid(z))
    h = h * lax.rsqrt(jnp.mean(h * h, axis=-1, keepdims=True) + LN_EPS)
    o_ref[...] = (h * g_ref[...]).astype(o_ref.dtype)


def gated_rms_norm(y, z, g):
    m, d = y.shape
    gw = d // SSM_GROUPS
    tm = _pick_tile(m, (512, 256, 128, 8))
    spec = pl.BlockSpec((tm, gw), lambda i, j: (i, j))
    return pl.pallas_call(
        _gated_norm_kernel,
        grid=(m // tm, SSM_GROUPS),
        in_specs=[spec, spec, pl.BlockSpec((1, gw), lambda i, j: (0, j))],
        out_specs=spec,
        out_shape=jax.ShapeDtypeStruct((m, d), BF16),
        compiler_params=_params("parallel", "parallel"),
        name="gated_rms_norm",
    )(y, z, g.reshape(1, d).astype(F32))


def _lane_pad(v):
    return jnp.zeros((1, LANES), F32).at[0, :v.shape[0]].set(v.astype(F32))


def _ffn_half(xp_f, xp_b, xs_f, xs_b, w1, w3, lead, w2, g, b):
    hp, hs = gated_matmul(xp_b, xs_b, w1, w3, None, None, "swiglu", BF16, lead=lead)
    return matmul_postnorm(hp, hs, w2, None, xp_f, xs_f, g, b, 0.5, lead=lead)


def _pad_rows(x, rows):
    return jnp.concatenate([x, jnp.zeros((x.shape[0], rows - x.shape[1]) + x.shape[2:], x.dtype)], axis=1)


def _front_pad_rows(x, rows):
    return jnp.concatenate([jnp.zeros((x.shape[0], rows - x.shape[1]) + x.shape[2:], x.dtype), x], axis=1)


def kernel(x_prompt, x_sample, p_prompt, p_sample, cache_sb_k, cache_sb_v, cache_moba_k, cache_moba_v, state_ssm, state_ssm_conv, state_conf_conv, page_table, ln_g, ln_b, ffn_w1, ffn_w3, ffn_w2, ple_w_proj, ple_w_gate, sb_w_qkv, sb_w_o, ssm_w_in, ssm_conv_w, ssm_conv_b, ssm_dt_bias, ssm_a_log, ssm_d, ssm_norm_g, ssm_w_out, conf_w_pw1, conf_b_pw1, conf_w_dw, conf_b_dw, conf_ln_g, conf_ln_b, conf_w_pw2, conf_b_pw2, moba_w_qkv, moba_w_o):
    bsz, seq, d_model = x_prompt.shape
    db = x_sample.shape[0]
    assert x_sample.shape[1] == 1
    mp = bsz * seq
    depth = ffn_w1.shape[0]
    bf = lambda w: w.astype(BF16)

    xp_f = x_prompt.reshape(mp, d_model)
    xs_f = x_sample.reshape(db, d_model)
    xp_b, xs_b = bf(xp_f), bf(xs_f)
    n_phys = cache_sb_k.shape[1]
    ffn_w2_b = bf(ffn_w2)
    outs ={k: [] for k in ("sb_kp", "sb_vp", "sb_ks", "sb_vs", "ssm_hp", "ssm_hs", "ssm_cp", "ssm_cs",
                            "conf_cp", "conf_cs", "mo_kp", "mo_vp", "mo_ks", "mo_vs")}

    for i in range(depth):
        m, j = i % 4, i // 4
        (xp_f, xp_b), (xs_f, xs_b) = _ffn_half(xp_f, xp_b, xs_f, xs_b, ffn_w1, ffn_w3, (i, 0), ffn_w2_b,
                                               ln_g[i, 0], ln_b[i, 0])
        g1, b1 = ln_g[i, 1], ln_b[i, 1]

        if m == 0 or m == 3:
            w_qkv = bf(sb_w_qkv[j] if m == 0 else moba_w_qkv[j])
            w_o = bf(sb_w_o[j] if m == 0 else moba_w_o[j])
            q, qs = matmul(xp_b, xs_b, w_qkv, None, BF16, 0, Q_DIM)
            k, ks = matmul(xp_b, xs_b, w_qkv, None, F32, Q_DIM, KV_DIM)
            v, vs = matmul(xp_b, xs_b, w_qkv, None, F32, Q_DIM + KV_DIM, KV_DIM)
            q = q.reshape(bsz, seq, Q_DIM)
            k = k.reshape(bsz, seq, KV_DIM)
            v = v.reshape(bsz, seq, KV_DIM)
            attend = sb_prompt_attention if m == 0 else moba_prompt_attention
            o = attend(q, k, v).reshape(mp, Q_DIM)
            kp = k.reshape(bsz, seq, N_KV_HEADS, HEAD_DIM)
            vp = v.reshape(bsz, seq, N_KV_HEADS, HEAD_DIM)
            if m == 0:
                kc = cache_sb_k[j].reshape(n_phys * PAGE_ROWS, HEAD_DIM)
                vc = cache_sb_v[j].reshape(n_phys * PAGE_ROWS, HEAD_DIM)
                os_ = sb_decode_attention(qs.reshape(db, N_HEADS, HEAD_DIM), kc, vc, page_table)
            else:
                kc = cache_moba_k[j].reshape(n_phys * PAGE_ROWS, HEAD_DIM)
                vc = cache_moba_v[j].reshape(n_phys * PAGE_ROWS, HEAD_DIM)
                qh = qs.reshape(db, N_HEADS, HEAD_DIM)
                kmean = moba_cache_block_means(kc, page_table)
                sel = moba_decode_select(qh, kmean)
                os_ = moba_decode_attention(qh.reshape(db, N_HEADS, 1, HEAD_DIM),
                                            ks.reshape(db, N_KV_HEADS, 1, HEAD_DIM),
                                            vs.reshape(db, N_KV_HEADS, 1, HEAD_DIM),
                                            sel, kc, vc, page_table)
            (xp_f, xp_b), (xs_f, xs_b) = matmul_postnorm(o, bf(os_.reshape(db, Q_DIM)), w_o, None, xp_f, xs_f,
                                                         g1, b1, 1.0)
            ks4 = ks.reshape(db, 1, N_KV_HEADS, HEAD_DIM)
            vs4 = vs.reshape(db, 1, N_KV_HEADS, HEAD_DIM)
            if m == 0:
                outs["sb_kp"].append(kp); outs["sb_vp"].append(vp); outs["sb_ks"].append(ks4); outs["sb_vs"].append(vs4)
            else:
                outs["mo_kp"].append(kp); outs["mo_vp"].append(vp); outs["mo_ks"].append(ks4); outs["mo_vs"].append(vs4)

        elif m == 1:
            n_heads = ssm_dt_bias.shape[1]
            d_inner = n_heads * SSM_HEAD_DIM
            conv_dim = ssm_conv_w.shape[2]
            hpg = n_heads // SSM_GROUPS
            w_in = bf(ssm_w_in[j])
            w_dt_cols = ssm_w_in[j][:, d_inner + conv_dim:]
            w_dt = bf(jnp.zeros((d_model, LANES), F32).at[:, :n_heads].set(w_dt_cols))
            dtb, alog, dskip = _lane_pad(ssm_dt_bias[j]), _lane_pad(ssm_a_log[j]), _lane_pad(ssm_d[j])
            w_out = bf(ssm_w_out[j])
            hp = 8
            z, zs = matmul(xp_b, xs_b, w_in, None, F32, 0, d_inner)
            xbc_raw, xbc_s = matmul(xp_b, xs_b, w_in, None, F32, d_inner, conv_dim)
            dt_raw, dt_s = matmul(xp_b, xs_b, w_dt, None, F32)
            xbc_raw = xbc_raw.reshape(bsz, seq, conv_dim)
            dt_raw = dt_raw.reshape(bsz, seq, LANES)
            xbc = causal_dwconv(xbc_raw, jnp.zeros((bsz, hp, conv_dim), F32), ssm_conv_w[j], ssm_conv_b[j], "silu")
            h0 = jnp.zeros((bsz, n_heads, SSM_HEAD_DIM, SSM_D_STATE), F32)
            yn, h_last = ssd_prompt(xbc, dt_raw, z.reshape(bsz, seq, d_inner), dtb, alog, dskip, ssm_norm_g[j], h0)
            outs["ssm_hp"].append(h_last)
            outs["ssm_cp"].append(xbc_raw[:, seq - (SSM_CONV - 1):, :])
            xbc_s = xbc_s.reshape(db, 1, conv_dim)
            dt_s = dt_s.reshape(db, 1, LANES)
            hist = _front_pad_rows(state_ssm_conv[j], hp)
            xbc_sa = causal_dwconv(_pad_rows(xbc_s, 8), hist, ssm_conv_w[j], ssm_conv_b[j], "silu")[:, :1, :]
            x_col = xbc_sa[:, 0, :d_inner].reshape(db, n_heads, SSM_HEAD_DIM, 1)
            y_col, hs = ssd_step(x_col, xbc_sa[:, :, d_inner:], dt_s, dtb, alog, dskip, state_ssm[j])
            yns = gated_rms_norm(y_col.reshape(db, d_inner), zs, ssm_norm_g[j])
            (xp_f, xp_b), (xs_f, xs_b) = matmul_postnorm(yn.reshape(mp, d_inner), yns, w_out, None, xp_f, xs_f,
                                                         g1, b1, 1.0)
            outs["ssm_hs"].append(hs)
            outs["ssm_cs"].append(jnp.concatenate([state_ssm_conv[j], xbc_s], axis=1)[:, 1:, :])

        else:
            ba, bg = conf_b_pw1[j][:d_model], conf_b_pw1[j][d_model:]
            w_pw2 = bf(conf_w_pw2[j])
            hp = 32
            u, us = gated_matmul(xp_b, xs_b, conf_w_pw1, conf_w_pw1, ba, bg, "glu", F32, lead=(j,),
                                 cola=0, colb=d_model, n=d_model)
            u = u.reshape(bsz, seq, d_model)
            us = us.reshape(db, 1, d_model)
            uc = causal_dwconv(u, jnp.zeros((bsz, hp, d_model), F32), conf_w_dw[j], conf_b_dw[j], "ln_silu",
                               conf_ln_g[j], conf_ln_b[j], BF16)
            outs["conf_cp"].append(u[:, seq - (CONF_WIDTH - 1):, :])
            hist = _front_pad_rows(state_conf_conv[j], hp)
            ucs = causal_dwconv(_pad_rows(us, 8), hist, conf_w_dw[j], conf_b_dw[j], "ln_silu",
                                conf_ln_g[j], conf_ln_b[j], BF16)[:, 0, :]
            (xp_f, xp_b), (xs_f, xs_b) = matmul_postnorm(uc.reshape(mp, d_model), ucs, w_pw2, conf_b_pw2[j],
                                                         xp_f, xs_f, g1, b1, 1.0)
            outs["conf_cs"].append(jnp.concatenate([state_conf_conv[j], us], axis=1)[:, 1:, :])

        (xp_f, xp_b), (xs_f, xs_b) = _ffn_half(xp_f, xp_b, xs_f, xs_b, ffn_w1, ffn_w3, (i, 1), ffn_w2_b,
                                               ln_g[i, 2], ln_b[i, 2])
        wg_, wp_ = bf(ple_w_gate[i]), bf(ple_w_proj[i])
        xp_f, xp_b = ple_add(xp_b, xp_f, p_prompt[i].reshape(mp, -1), wg_, wp_, ln_g[i, 3], ln_b[i, 3])
        xs_f, xs_b = ple_add(xs_b, xs_f, p_sample[i].reshape(db, -1), wg_, wp_, ln_g[i, 3], ln_b[i, 3])

    st = lambda name: jnp.stack(outs[name])
    return (xp_f.reshape(bsz, seq, d_model), xs_f.reshape(db, 1, d_model),
            st("sb_kp"), st("sb_vp"), st("sb_ks"), st("sb_vs"),
            st("ssm_hp"), st("ssm_hs"), st("ssm_cp"), st("ssm_cs"),
            st("conf_cp"), st("conf_cs"),
            st("mo_kp"), st("mo_vp"), st("mo_ks"), st("mo_vs"))
```

```python
import functools

import jax
import jax.numpy as jnp
from jax import lax
from jax.experimental import pallas as pl
from jax.experimental.pallas import tpu as pltpu

F32 = jnp.float32
BF16 = jnp.bfloat16

N_HEADS = 16
HEAD_DIM = 128
N_KV_HEADS = 4
Q_PER_KV = N_HEADS // N_KV_HEADS
Q_DIM = N_HEADS * HEAD_DIM
KV_DIM = N_KV_HEADS * HEAD_DIM
SB_BLOCK = 128
SB_KV_PER_STEP = 2
MOBA_BLOCK = 256
MOBA_TOPK = 3
PAGE_SIZE = 128
SSM_HEAD_DIM = 64
SSM_GROUPS = 8
SSM_D_STATE = 128
SSM_CONV = 4
SSM_CHUNK = 128
CONF_WIDTH = 31
LN_EPS = 1e-5
DEPTH = 4
DN_ALPHA = (2 * DEPTH) ** 0.25
LANES = 128
NEG_BIG = -1e30
DWCONV_ROW_TILE = 64
DWCONV_LANE_CHUNK = 256
SB_LOG_FLOOR = -104.0
VMEM_LIMIT = 56 * 1024 * 1024


def _params(*sem):
    return pltpu.CompilerParams(dimension_semantics=sem, vmem_limit_bytes=VMEM_LIMIT)


def _layer_norm_rows(y, g, b):
    mu = jnp.mean(y, axis=-1, keepdims=True)
    d = y - mu
    var = jnp.mean(d * d, axis=-1, keepdims=True)
    return d * lax.rsqrt(var + LN_EPS) * g + b


def _split3(x):
    hi = x.astype(BF16)
    r1 = x - hi.astype(F32)
    mid = r1.astype(BF16)
    lo = (r1 - mid.astype(F32)).astype(BF16)
    return hi, mid, lo


def _dot(a, b):
    return jnp.dot(a, b, preferred_element_type=F32)


def _dot_nt(a, b):
    return lax.dot_general(a, b, (((1,), (1,)), ((), ())), preferred_element_type=F32)


def _dot01(x, m01):
    hi, mid, lo = _split3(x)
    return _dot(hi, m01) + _dot(mid, m01) + _dot(lo, m01)


def _pick_tile(n, pref):
    for t in pref:
        if n % t == 0:
            return t
    return n


def _mm_kernel(x_ref, xs_ref, w_ref, b_ref, o_ref, os_ref):
    o_ref[...] = (_dot(x_ref[...], w_ref[...]) + b_ref[...]).astype(o_ref.dtype)
    first = pl.program_id(0) == 0

    @pl.when(first)
    def _():
        os_ref[0] = (_dot(xs_ref[...], w_ref[...]) + b_ref[...]).astype(os_ref.dtype)

    @pl.when(jnp.logical_not(first))
    def _():
        os_ref[...] = jnp.zeros_like(os_ref)


def matmul(x, xs, w, bias=None, out_dtype=F32, col0=0, n=None):
    m, k = x.shape
    s = xs.shape[0]
    n = w.shape[1] - col0 if n is None else n
    tm = _pick_tile(m, (1024, 512, 256, 128, 8))
    tn = _pick_tile(n, (512, 256, 128))
    assert col0 % tn == 0 and n % tn == 0
    j0 = col0 // tn
    if bias is None:
        bias = jnp.zeros((n,), F32)
    out, out_s = pl.pallas_call(
        _mm_kernel,
        grid=(m // tm, n // tn),
        in_specs=[pl.BlockSpec((tm, k), lambda i, j: (i, 0)),
                  pl.BlockSpec((s, k), lambda i, j: (0, 0)),
                  pl.BlockSpec((k, tn), lambda i, j: (0, j + j0)),
                  pl.BlockSpec((1, tn), lambda i, j: (0, j))],
        out_specs=[pl.BlockSpec((tm, tn), lambda i, j: (i, j)),
                   pl.BlockSpec((1, s, tn), lambda i, j: (i, 0, j))],
        out_shape=[jax.ShapeDtypeStruct((m, n), out_dtype),
                   jax.ShapeDtypeStruct((m // tm, s, n), F32)],
        compiler_params=_params("arbitrary", "arbitrary"),
        name="mm",
    )(x, xs, w, bias.reshape(1, n).astype(F32))
    return out, out_s[0]


def _gated_mm_kernel(x_ref, xs_ref, wa_ref, wb_ref, ba_ref, bb_ref, o_ref, os_ref, wa_s, wb_s, *, mode):
    first = pl.program_id(1) == 0

    def gated(x):
        a = _dot(x, wa_s[...]) + ba_ref[...]
        b = _dot(x, wb_s[...]) + bb_ref[...]
        return a * jax.nn.sigmoid(a) * b if mode == "swiglu" else a * jax.nn.sigmoid(b)

    @pl.when(first)
    def _():
        wa_s[...] = wa_ref[...].astype(BF16)
        wb_s[...] = wb_ref[...].astype(BF16)
        os_ref[0] = gated(xs_ref[...]).astype(os_ref.dtype)

    @pl.when(jnp.logical_not(first))
    def _():
        os_ref[...] = jnp.zeros_like(os_ref)

    o_ref[...] = gated(x_ref[...]).astype(o_ref.dtype)


def gated_matmul(x, xs, wa, wb, ba, bb, mode, out_dtype, lead=(), cola=0, colb=0, n=None):
    m, k = x.shape
    s = xs.shape[0]
    n = wa.shape[-1] - cola if n is None else n
    wblock = (None,) * len(lead) + (k, None)
    tm = _pick_tile(m, (2048, 1024, 512, 256, 128, 8) if jnp.dtype(out_dtype).itemsize == 2
                    else (1024, 512, 256, 128, 8))
    tn = _pick_tile(n, (512, 256, 128))
    assert cola % tn == 0 and colb % tn == 0 and n % tn == 0
    ja, jb = cola // tn, colb // tn
    if ba is None:
        ba = jnp.zeros((n,), F32)
        bb = jnp.zeros((n,), F32)
    out, out_s = pl.pallas_call(
        functools.partial(_gated_mm_kernel, mode=mode),
        grid=(n // tn, m // tm),
        in_specs=[pl.BlockSpec((tm, k), lambda j, i: (i, 0)),
                  pl.BlockSpec((s, k), lambda j, i: (0, 0)),
                  pl.BlockSpec(wblock[:-1] + (tn,), lambda j, i: lead + (0, j + ja)),
                  pl.BlockSpec(wblock[:-1] + (tn,), lambda j, i: lead + (0, j + jb)),
                  pl.BlockSpec((1, tn), lambda j, i: (0, j)),
                  pl.BlockSpec((1, tn), lambda j, i: (0, j))],
        out_specs=[pl.BlockSpec((tm, tn), lambda j, i: (i, j)),
                   pl.BlockSpec((1, s, tn), lambda j, i: (i, 0, j))],
        out_shape=[jax.ShapeDtypeStruct((m, n), out_dtype),
                   jax.ShapeDtypeStruct((m // tm, s, n), out_dtype)],
        scratch_shapes=[pltpu.VMEM((k, tn), BF16), pltpu.VMEM((k, tn), BF16)],
        compiler_params=_params("arbitrary", "arbitrary"),
        name="gated_mm",
    )(x, xs, wa, wb, ba.reshape(1, n).astype(F32), bb.reshape(1, n).astype(F32))
    return out, out_s[0]


def _mm_ln_kernel(a_ref, as_ref, w_ref, bias_ref, res_ref, ress_ref, g_ref, b_ref, of_ref, ob_ref, ofs_ref, obs_ref,
                  accs_ref, *, scale, nk):
    i, k = pl.program_id(0), pl.program_id(1)

    def post_norm(res, acc):
        y = DN_ALPHA * res + scale * (acc + bias_ref[...])
        return _layer_norm_rows(y, g_ref[...], b_ref[...])

    def finish(acc):
        out = post_norm(res_ref[...], acc)
        of_ref[...] = out
        ob_ref[...] = out.astype(BF16)

    @pl.when(i == 0)
    def _():
        part = _dot(as_ref[...], w_ref[...])
        if nk > 1:
            @pl.when(k > 0)
            def _():
                accs_ref[...] += part

            @pl.when(k == 0)
            def _():
                accs_ref[...] = part
        else:
            accs_ref[...] = part

    @pl.when(k == nk - 1)
    def _():
        out = post_norm(ress_ref[...], accs_ref[...])
        first = i == 0
        ofs_ref[0] = jnp.where(first, out, 0.0)
        obs_ref[0] = jnp.where(first, out, 0.0).astype(BF16)

    if nk == 1:
        finish(_dot(a_ref[...], w_ref[...]))
        return

    @pl.when(k == 0)
    def _():
        of_ref[...] = _dot(a_ref[...], w_ref[...])

    @pl.when(jnp.logical_and(k > 0, k < nk - 1))
    def _():
        of_ref[...] += _dot(a_ref[...], w_ref[...])

    @pl.when(k == nk - 1)
    def _():
        finish(of_ref[...] + _dot(a_ref[...], w_ref[...]))


def matmul_postnorm(a, a_s, w, bias, res, res_s, g, b, scale, lead=()):
    m, kdim = a.shape
    s = a_s.shape[0]
    n = w.shape[-1]
    tm = _pick_tile(m, (512, 256, 128, 8))
    tk = _pick_tile(kdim, (2816, 2048, 1408, 1024, 512, 256, 128))
    nk = kdim // tk
    if bias is None:
        bias = jnp.zeros((n,), F32)
    row = lambda v: v.reshape(1, n).astype(F32)
    vec = pl.BlockSpec((1, n), lambda i, k: (0, 0))
    of, ob, ofs, obs = pl.pallas_call(
        functools.partial(_mm_ln_kernel, scale=scale, nk=nk),
        grid=(m // tm, nk),
        in_specs=[pl.BlockSpec((tm, tk), lambda i, k: (i, k)),
                  pl.BlockSpec((s, tk), lambda i, k: (0, k)),
                  pl.BlockSpec((None,) * len(lead) + (tk, n), lambda i, k: lead + (k, 0)),
                  vec,
                  pl.BlockSpec((tm, n), lambda i, k: (i, 0)),
                  pl.BlockSpec((s, n), lambda i, k: (0, 0)),
                  vec, vec],
        out_specs=[pl.BlockSpec((tm, n), lambda i, k: (i, 0)),
                   pl.BlockSpec((tm, n), lambda i, k: (i, 0)),
                   pl.BlockSpec((1, s, n), lambda i, k: (i, 0, 0)),
                   pl.BlockSpec((1, s, n), lambda i, k: (i, 0, 0))],
        out_shape=[jax.ShapeDtypeStruct((m, n), F32), jax.ShapeDtypeStruct((m, n), BF16),
                   jax.ShapeDtypeStruct((m // tm, s, n), F32), jax.ShapeDtypeStruct((m // tm, s, n), BF16)],
        scratch_shapes=[pltpu.VMEM((s, n), F32)],
        compiler_params=_params("arbitrary", "arbitrary"),
        name="mm_postnorm",
    )(a, a_s, w, row(bias), res, res_s, row(g), row(b))
    return (of, ob), (ofs[0], obs[0])


def _ple_kernel(xb_ref, xf_ref, p_ref, xsb_ref, xsf_ref, ps_ref, wg_ref, wp_ref, g_ref, b_ref,
                of_ref, ob_ref, ofs_ref, obs_ref):
    def ple(xb, xf, p):
        gate = jax.nn.sigmoid(_dot(xb, wg_ref[...]))
        proj = _dot(p.astype(BF16), wp_ref[...])
        return _layer_norm_rows(DN_ALPHA * xf + gate * proj, g_ref[...], b_ref[...])

    out = ple(xb_ref[...], xf_ref[...], p_ref[...])
    of_ref[...] = out
    ob_ref[...] = out.astype(BF16)
    first = pl.program_id(0) == 0

    @pl.when(first)
    def _():
        outs = ple(xsb_ref[...], xsf_ref[...], ps_ref[...])
        ofs_ref[0] = outs
        obs_ref[0] = outs.astype(BF16)

    @pl.when(jnp.logical_not(first))
    def _():
        ofs_ref[...] = jnp.zeros_like(ofs_ref)
        obs_ref[...] = jnp.zeros_like(obs_ref)


def ple_add(xb, xf, p, xsb, xsf, ps, layer, wg, wp, g, b):
    m, d = xf.shape
    s = xsf.shape[0]
    pd = p.shape[-1]
    tm = _pick_tile(m, (512, 256, 128, 8))
    row = lambda v: v.reshape(1, d).astype(F32)
    const = lambda shape: pl.BlockSpec(shape, lambda i: (0,) * len(shape))
    of, ob, ofs, obs = pl.pallas_call(
        _ple_kernel,
        grid=(m // tm,),
        in_specs=[pl.BlockSpec((tm, d), lambda i: (i, 0)),
                  pl.BlockSpec((tm, d), lambda i: (i, 0)),
                  pl.BlockSpec((None, tm, pd), lambda i: (layer, i, 0)),
                  const((s, d)), const((s, d)),
                  pl.BlockSpec((None, s, pd), lambda i: (layer, 0, 0)),
                  const((d, d)), const((pd, d)), const((1, d)), const((1, d))],
        out_specs=[pl.BlockSpec((tm, d), lambda i: (i, 0)),
                   pl.BlockSpec((tm, d), lambda i: (i, 0)),
                   pl.BlockSpec((1, s, d), lambda i: (i, 0, 0)),
                   pl.BlockSpec((1, s, d), lambda i: (i, 0, 0))],
        out_shape=[jax.ShapeDtypeStruct((m, d), F32), jax.ShapeDtypeStruct((m, d), BF16),
                   jax.ShapeDtypeStruct((m // tm, s, d), F32), jax.ShapeDtypeStruct((m // tm, s, d), BF16)],
        compiler_params=_params("arbitrary"),
        name="ple_add",
    )(xb, xf, p, xsb, xsf, ps, wg, wp, row(g), row(b))
    return (of, ob), (ofs[0], obs[0])


def _softplus(z):
    return jnp.maximum(z, 0.0) + jnp.log1p(jnp.exp(-jnp.abs(z)))


def _sb_prompt_kernel(q_ref, k_ref, v_ref, o_ref, *, tq, nkv):
    i = pl.program_id(2)
    rows = Q_PER_KV * tq
    scale = HEAD_DIM ** -0.5
    gw = Q_PER_KV * HEAD_DIM
    qs = [jnp.concatenate([q_ref[0, :, h * gw + g * HEAD_DIM:h * gw + (g + 1) * HEAD_DIM] for g in range(Q_PER_KV)],
                          axis=0) for h in range(nkv)]
    t_loc = lax.broadcasted_iota(jnp.int32, (rows, tq), 0) & (tq - 1)
    s_loc = lax.broadcasted_iota(jnp.int32, (rows, tq), 1)
    later = (lax.broadcasted_iota(jnp.int32, (tq, tq), 0) > lax.broadcasted_iota(jnp.int32, (tq, tq), 1)).astype(BF16)

    def block(h, j, run, acc, diagonal):
        start = pl.multiple_of(j * tq, tq)
        kb = k_ref[0, pl.ds(start, tq), h * HEAD_DIM:(h + 1) * HEAD_DIM].astype(BF16)
        vb = v_ref[0, pl.ds(start, tq), h * HEAD_DIM:(h + 1) * HEAD_DIM].astype(BF16)
        z = _dot_nt(qs[h], kb) * scale
        log_keep = -_softplus(z)
        if diagonal:
            valid = s_loc < t_loc
            log_keep = jnp.where(valid, log_keep, 0.0)
        log_w = z + log_keep + _dot01(log_keep, later) + run
        w = jnp.exp(log_w)
        if diagonal:
            w = jnp.where(valid, w, 0.0)
        acc = acc + _dot(w.astype(BF16), vb)
        run = run + jnp.sum(log_keep, axis=1, keepdims=True)
        return run, acc

    run0 = jnp.zeros((rows, 1), F32)
    acc0 = jnp.zeros((rows, HEAD_DIM), F32)
    state = []
    for h in range(nkv):
        state += list(block(h, i, run0, acc0, True))

    def cond(carry):
        live = jnp.max(carry[1])
        for h in range(1, nkv):
            live = jnp.maximum(live, jnp.max(carry[1 + 2 * h]))
        return jnp.logical_and(carry[0] < i, live > SB_LOG_FLOOR)

    def body(carry):
        it = carry[0]
        out = [it + 1]
        for h in range(nkv):
            out += list(block(h, i - 1 - it, carry[1 + 2 * h], carry[2 + 2 * h], False))
        return tuple(out)

    final = lax.while_loop(cond, body, (jnp.int32(0),) + tuple(state))
    for h in range(nkv):
        acc = final[2 + 2 * h]
        for g in range(Q_PER_KV):
            o_ref[0, :, h * gw + g * HEAD_DIM:h * gw + (g + 1) * HEAD_DIM] = acc[g * tq:(g + 1) * tq].astype(o_ref.dtype)


def sb_prompt_attention(q, k, v):
    bsz, L, _ = q.shape
    tq = SB_BLOCK
    nkv = SB_KV_PER_STEP
    gw = nkv * Q_PER_KV * HEAD_DIM
    return pl.pallas_call(
        functools.partial(_sb_prompt_kernel, tq=tq, nkv=nkv),
        grid=(bsz, N_KV_HEADS // nkv, L // tq),
        in_specs=[pl.BlockSpec((1, tq, gw), lambda b, h, i: (b, i, h)),
                  pl.BlockSpec((1, L, nkv * HEAD_DIM), lambda b, h, i: (b, 0, h)),
                  pl.BlockSpec((1, L, nkv * HEAD_DIM), lambda b, h, i: (b, 0, h))],
        out_specs=pl.BlockSpec((1, tq, gw), lambda b, h, i: (b, i, h)),
        out_shape=jax.ShapeDtypeStruct(q.shape, BF16),
        compiler_params=_params("parallel", "parallel", "arbitrary"),
        name="sb_prompt",
    )(q, k, v)


PAGE_ROWS = PAGE_SIZE * N_KV_HEADS
SB_PAGES_PER_GROUP = 4
MOBA_BLOCKS_PER_STEP = 4


def _own_lane_mask(kvh_of_row, shape):
    lane = lax.broadcasted_iota(jnp.int32, shape, 1)
    return (lane & (N_KV_HEADS - 1)) == kvh_of_row


def _suffix_sum_keys(x):
    n = x.shape[1]
    lane = lax.broadcasted_iota(jnp.int32, x.shape, 1)
    d = N_KV_HEADS
    while d < n:
        shifted = pltpu.roll(x, n - d, axis=1)
        x = x + jnp.where(lane + d < n, shifted, 0.0)
        d *= 2
    return x


def _sb_decode_kernel(pt_ref, q_ref, k_hbm, v_hbm, o_ref, kbuf, vbuf, sems, acc_ref, run_ref, *, n_pages, gp):
    b = pl.program_id(0)
    n_groups = n_pages // gp

    def page_copies(g, slot):
        copies = []
        for u in range(gp):
            page = pt_ref[b * n_pages + (n_pages - 1 - (g * gp + u))]
            rows = pl.ds(pl.multiple_of(page * PAGE_ROWS, PAGE_ROWS), PAGE_ROWS)
            copies.append(pltpu.make_async_copy(k_hbm.at[rows], kbuf.at[slot, u], sems.at[0, slot, u]))
            copies.append(pltpu.make_async_copy(v_hbm.at[rows], vbuf.at[slot, u], sems.at[1, slot, u]))
        return copies

    def start(g, slot):
        for c in page_copies(g, slot):
            c.start()

    def wait(g, slot):
        for c in page_copies(g, slot):
            c.wait()

    acc_ref[...] = jnp.zeros_like(acc_ref)
    run_ref[...] = jnp.zeros_like(run_ref)
    q = q_ref[0].astype(BF16)
    kvh = lax.broadcasted_iota(jnp.int32, (N_HEADS, PAGE_ROWS), 0) >> 2
    own = _own_lane_mask(kvh, (N_HEADS, PAGE_ROWS))
    start(0, 0)

    def live(_):
        return jnp.max(run_ref[...]) > SB_LOG_FLOOR

    def body(g):
        slot = g & 1
        wait(g, slot)

        @pl.when(g + 1 < n_groups)
        def _():
            start(g + 1, 1 - slot)

        for u in range(gp):
            @pl.when(live(None))
            def _():
                z = _dot_nt(q, kbuf[slot, u].astype(BF16)) * (HEAD_DIM ** -0.5)
                log_keep = jnp.where(own, -_softplus(z), 0.0)
                incl = _suffix_sum_keys(log_keep)
                w = jnp.where(own, jnp.exp(z + incl + run_ref[:, 0:1]), 0.0)
                acc_ref[...] += _dot(w.astype(BF16), vbuf[slot, u].astype(BF16))
                run_ref[...] += jnp.sum(log_keep, axis=1, keepdims=True)
        return g + 1

    g_end = lax.while_loop(lambda g: jnp.logical_and(g < n_groups, live(None)), body, jnp.int32(0))

    @pl.when(g_end < n_groups)
    def _():
        wait(g_end, g_end & 1)

    o_ref[0] = acc_ref[...]


def sb_decode_attention(q, k_rows, v_rows, page_table):
    db, n_pages = page_table.shape
    gp = SB_PAGES_PER_GROUP
    assert n_pages % gp == 0
    return pl.pallas_call(
        functools.partial(_sb_decode_kernel, n_pages=n_pages, gp=gp),
        grid_spec=pltpu.PrefetchScalarGridSpec(
            num_scalar_prefetch=1,
            grid=(db,),
            in_specs=[pl.BlockSpec((1, N_HEADS, HEAD_DIM), lambda b, pt: (b, 0, 0)),
                      pl.BlockSpec(memory_space=pl.ANY), pl.BlockSpec(memory_space=pl.ANY)],
            out_specs=pl.BlockSpec((1, N_HEADS, HEAD_DIM), lambda b, pt: (b, 0, 0)),
            scratch_shapes=[pltpu.VMEM((2, gp, PAGE_ROWS, HEAD_DIM), F32),
                            pltpu.VMEM((2, gp, PAGE_ROWS, HEAD_DIM), F32),
                            pltpu.SemaphoreType.DMA((2, 2, gp)),
                            pltpu.VMEM((N_HEADS, HEAD_DIM), F32), pltpu.VMEM((N_HEADS, LANES), F32)]),
        out_shape=jax.ShapeDtypeStruct((db, N_HEADS, HEAD_DIM), F32),
        compiler_params=_params("arbitrary"),
        name="sb_decode",
    )(page_table.reshape(-1), q, k_rows, v_rows)


def _top_k_mask(gate, n_valid, k):
    lane = lax.broadcasted_iota(jnp.int32, gate.shape, 1)
    sel = jnp.zeros(gate.shape, F32)
    picks = []
    for r in range(k):
        m = jnp.max(gate, axis=1, keepdims=True)
        idx = jnp.min(jnp.where(gate == m, lane, gate.shape[1] - 1), axis=1, keepdims=True)
        pick = lane == idx
        sel = jnp.maximum(sel, jnp.where(pick, jnp.where(r < n_valid, 1.0, 0.0), 0.0))
        gate = jnp.where(pick, -jnp.inf, gate)
        picks.append(idx)
    return sel, picks


def _top_k_mask_sublanes(gate_t, n_valid, k):
    sub = lax.broadcasted_iota(jnp.int32, gate_t.shape, 0)
    sel = jnp.zeros(gate_t.shape, F32)
    for r in range(k):
        m = jnp.max(gate_t, axis=0, keepdims=True)
        idx = jnp.min(jnp.where(gate_t == m, sub, gate_t.shape[0] - 1), axis=0, keepdims=True)
        pick = sub == idx
        sel = jnp.maximum(sel, jnp.where(pick, jnp.where(r < n_valid, 1.0, 0.0), 0.0))
        gate_t = jnp.where(pick, -jnp.inf, gate_t)
    return sel


def _moba_prompt_kernel(q_ref, k_ref, v_ref, o_ref, km_ref, *, tq, nb):
    i = pl.program_id(2)
    rows = Q_PER_KV * tq
    scale = HEAD_DIM ** -0.5

    @pl.when(i == 0)
    def _():
        km_ref[...] = jnp.zeros_like(km_ref)
        for jb in range(nb):
            km_ref[jb:jb + 1, :] = jnp.mean(k_ref[0, jb * tq:(jb + 1) * tq, :], axis=0, keepdims=True)

    q = jnp.concatenate([q_ref[0, :, g * HEAD_DIM:(g + 1) * HEAD_DIM] for g in range(Q_PER_KV)], axis=0)
    cands = -(-nb // 16) * 16
    gate_t = _dot_nt(km_ref[0:cands, :].astype(BF16), q)
    gate_t = jnp.where(lax.broadcasted_iota(jnp.int32, (cands, rows), 0) < i, gate_t, -jnp.inf)
    sel_t = _top_k_mask_sublanes(gate_t, i, MOBA_TOPK)
    sel = jnp.concatenate([sel_t, jnp.zeros((LANES - cands, rows), F32)], axis=0).T.astype(BF16)

    def scores(j):
        start = pl.multiple_of(j * tq, tq)
        kb = k_ref[0, pl.ds(start, tq), :].astype(BF16)
        vb = v_ref[0, pl.ds(start, tq), :].astype(BF16)
        return _dot_nt(q, kb) * scale, vb

    s, vb = scores(i)
    t_loc = lax.broadcasted_iota(jnp.int32, (rows, tq), 0) & (tq - 1)
    s_loc = lax.broadcasted_iota(jnp.int32, (rows, tq), 1)
    s = jnp.where(s_loc <= t_loc, s, NEG_BIG)
    m = jnp.max(s, axis=1, keepdims=True)
    p = jnp.exp(s - m)
    l = jnp.sum(p, axis=1, keepdims=True)
    acc = _dot(p.astype(BF16), vb)

    def body(j, carry):
        m, l, acc = carry
        s, vb = scores(j)
        onehot = (lax.broadcasted_iota(jnp.int32, (LANES, tq), 0) == j).astype(BF16)
        chosen = _dot(sel, onehot) > 0.5
        s = jnp.where(chosen, s, NEG_BIG)
        m_new = jnp.maximum(m, jnp.max(s, axis=1, keepdims=True))
        alpha = jnp.exp(m - m_new)
        p = jnp.exp(s - m_new)
        l = alpha * l + jnp.sum(p, axis=1, keepdims=True)
        acc = alpha * acc + _dot(p.astype(BF16), vb)
        return m_new, l, acc

    m, l, acc = lax.fori_loop(0, i, body, (m, l, acc))
    out = acc / l
    for g in range(Q_PER_KV):
        o_ref[0, :, g * HEAD_DIM:(g + 1) * HEAD_DIM] = out[g * tq:(g + 1) * tq].astype(o_ref.dtype)


def moba_prompt_attention(q, k, v):
    bsz, L, _ = q.shape
    tq = MOBA_BLOCK
    nb = L // tq
    assert L % tq == 0 and nb <= LANES
    gw = Q_PER_KV * HEAD_DIM
    return pl.pallas_call(
        functools.partial(_moba_prompt_kernel, tq=tq, nb=nb),
        grid=(bsz, N_KV_HEADS, nb),
        in_specs=[pl.BlockSpec((1, tq, gw), lambda b, h, i: (b, i, h)),
                  pl.BlockSpec((1, L, HEAD_DIM), lambda b, h, i: (b, 0, h)),
                  pl.BlockSpec((1, L, HEAD_DIM), lambda b, h, i: (b, 0, h))],
        out_specs=pl.BlockSpec((1, tq, gw), lambda b, h, i: (b, i, h)),
        out_shape=jax.ShapeDtypeStruct(q.shape, BF16),
        scratch_shapes=[pltpu.VMEM((LANES, HEAD_DIM), F32)],
        compiler_params=_params("parallel", "arbitrary", "arbitrary"),
        name="moba_prompt",
    )(q, k, v)


def _moba_kmean_kernel(pt_ref, *refs, bps, ppb):
    k_refs, o_ref = refs[:bps * ppb], refs[bps * ppb]
    sub = 8
    for j in range(bps):
        tot = jnp.zeros((sub, HEAD_DIM), F32)
        for h in range(ppb):
            k_ref = k_refs[j * ppb + h]
            for c in range(PAGE_ROWS // sub):
                tot = tot + k_ref[c * sub:(c + 1) * sub, :]
        mean = (tot[:N_KV_HEADS] + tot[N_KV_HEADS:]) * (1.0 / MOBA_BLOCK)
        o_ref[0, j * N_KV_HEADS:(j + 1) * N_KV_HEADS, :] = mean


def moba_cache_block_means(k_rows, page_table):
    db, n_pages = page_table.shape
    ppb = MOBA_BLOCK // PAGE_SIZE
    nblk = n_pages // ppb
    bps = MOBA_BLOCKS_PER_STEP
    assert nblk % bps == 0

    def page_spec(u):
        return pl.BlockSpec((PAGE_ROWS, HEAD_DIM), lambda b, j, pt: (pt[b * n_pages + j * (bps * ppb) + u], 0))

    return pl.pallas_call(
        functools.partial(_moba_kmean_kernel, bps=bps, ppb=ppb),
        grid_spec=pltpu.PrefetchScalarGridSpec(
            num_scalar_prefetch=1,
            grid=(db, nblk // bps),
            in_specs=[page_spec(u) for u in range(bps * ppb)],
            out_specs=pl.BlockSpec((1, bps * N_KV_HEADS, HEAD_DIM), lambda b, j, pt: (b, j, 0))),
        out_shape=jax.ShapeDtypeStruct((db, nblk * N_KV_HEADS, HEAD_DIM), F32),
        compiler_params=_params("parallel", "arbitrary"),
        name="moba_kmean",
    )(page_table.reshape(-1), *([k_rows] * (bps * ppb)))


def _moba_gate_kernel(q_ref, km_ref, o_ref, *, nblk):
    gate = _dot_nt(q_ref[0].astype(BF16), km_ref[0].astype(BF16))
    kvh = lax.broadcasted_iota(jnp.int32, gate.shape, 0) >> 2
    gate = jnp.where(_own_lane_mask(kvh, gate.shape), gate, -jnp.inf)
    _, picks = _top_k_mask(gate, nblk, MOBA_TOPK)
    lane = lax.broadcasted_iota(jnp.int32, (N_HEADS, LANES), 1)
    out = jnp.zeros((N_HEADS, LANES), jnp.int32)
    for r, idx in enumerate(picks):
        out = jnp.where(lane == r, idx >> 2, out)
    o_ref[0] = out


def moba_decode_select(q, kmean):
    db, rows, _ = kmean.shape
    nblk = rows // N_KV_HEADS
    out = pl.pallas_call(
        functools.partial(_moba_gate_kernel, nblk=nblk),
        grid=(db,),
        in_specs=[pl.BlockSpec((1, N_HEADS, HEAD_DIM), lambda b: (b, 0, 0)),
                  pl.BlockSpec((1, rows, HEAD_DIM), lambda b: (b, 0, 0))],
        out_specs=pl.BlockSpec((1, N_HEADS, LANES), lambda b: (b, 0, 0)),
        out_shape=jax.ShapeDtypeStruct((db, N_HEADS, LANES), jnp.int32),
        compiler_params=_params("parallel"),
        name="moba_gate",
    )(q, kmean)
    return out[:, :, :MOBA_TOPK]


def _moba_decode_kernel(sel_ref, pt_ref, q_ref, kn_ref, vn_ref, *refs, n_sel):
    k_refs, v_refs, o_ref = refs[:n_sel], refs[n_sel:2 * n_sel], refs[2 * n_sel]
    scale = HEAD_DIM ** -0.5
    q = jnp.broadcast_to(q_ref[0, 0], (8, HEAD_DIM))
    own = _own_lane_mask(pl.program_id(1) >> 2, (8, PAGE_ROWS))
    m = jnp.sum(q * kn_ref[0, 0], axis=1, keepdims=True) * scale
    l = jnp.ones((8, 1), F32)
    acc = jnp.broadcast_to(vn_ref[0, 0], (8, HEAD_DIM))
    qb = q.astype(BF16)
    for u in range(n_sel):
        s = jnp.where(own, _dot_nt(qb, k_refs[u][...].astype(BF16)) * scale, NEG_BIG)
        m_new = jnp.maximum(m, jnp.max(s, axis=1, keepdims=True))
        alpha = jnp.exp(m - m_new)
        p = jnp.exp(s - m_new)
        l = alpha * l + jnp.sum(p, axis=1, keepdims=True)
        acc = alpha * acc + _dot(p.astype(BF16), v_refs[u][...].astype(BF16))
        m = m_new
    o_ref[0, 0] = acc / l


def moba_decode_attention(q, k_new, v_new, sel, k_rows, v_rows, page_table):
    db, n_pages = page_table.shape
    ppb = MOBA_BLOCK // PAGE_SIZE
    n_sel = MOBA_TOPK * ppb

    def page_spec(u):
        def index(b, h, sel_r, pt):
            blk = sel_r[(b * N_HEADS + h) * MOBA_TOPK + u // ppb]
            return (pt[b * n_pages + blk * ppb + u % ppb], 0)
        return pl.BlockSpec((PAGE_ROWS, HEAD_DIM), index)

    specs = [page_spec(u) for u in range(n_sel)]
    new_spec = pl.BlockSpec((1, 1, 1, HEAD_DIM), lambda b, h, sel_r, pt: (b, h // Q_PER_KV, 0, 0))
    out = pl.pallas_call(
        functools.partial(_moba_decode_kernel, n_sel=n_sel),
        grid_spec=pltpu.PrefetchScalarGridSpec(
            num_scalar_prefetch=2,
            grid=(db, N_HEADS),
            in_specs=[pl.BlockSpec((1, 1, 1, HEAD_DIM), lambda b, h, sel_r, pt: (b, h, 0, 0)),
                      new_spec, new_spec] + specs + specs,
            out_specs=pl.BlockSpec((1, 1, 8, HEAD_DIM), lambda b, h, sel_r, pt: (b, h, 0, 0))),
        out_shape=jax.ShapeDtypeStruct((db, N_HEADS, 8, HEAD_DIM), F32),
        compiler_params=_params("parallel", "parallel"),
        name="moba_decode",
    )(sel.reshape(-1), page_table.reshape(-1), q, k_new, v_new, *([k_rows] * n_sel), *([v_rows] * n_sel))
    return out[:, :, 0, :]


def _dwconv_kernel(x_ref, hist_ref, w_ref, b_ref, g_ref, beta_ref, o_ref, pad_ref, *shift_refs, width, tl, hp, post):
    t = pl.program_id(2)

    @pl.when(t == 0)
    def _():
        pad_ref[0:hp, :] = hist_ref[0]

    @pl.when(t > 0)
    def _():
        pad_ref[0:hp, :] = pad_ref[tl:tl + hp, :]

    pad_ref[hp:hp + tl, :] = x_ref[0]

    def finish(acc):
        if post == "ln_silu":
            acc = _layer_norm_rows(acc, g_ref[...], beta_ref[...])
        return (acc * jax.nn.sigmoid(acc)).astype(o_ref.dtype)

    if not shift_refs:
        base = hp - (width - 1)
        acc = jnp.zeros(o_ref.shape[1:], F32) + b_ref[...]
        for k in range(width):
            acc = acc + pad_ref[base + k:base + k + tl, :] * w_ref[k:k + 1, :]
        o_ref[0] = finish(acc)
        return

    sh_ref, wb_ref = shift_refs
    halo = 8 * ((width - 1) // 8)
    for r in range(1, 8):
        sh_ref[r - 1] = pad_ref[hp - halo - r:hp + tl - r, :]
    rt = min(DWCONV_ROW_TILE, tl)
    tc = o_ref.shape[2]
    lc = DWCONV_LANE_CHUNK if tc % DWCONV_LANE_CHUNK == 0 else tc
    for k in range(width):
        wb_ref[k] = jnp.broadcast_to(w_ref[k:k + 1, :], (8, tc))

    def row_tile(row0):
        parts = []
        for c0 in range(0, tc, lc):
            acc = jnp.zeros((rt // 8, 8, lc), F32) + b_ref[:, c0:c0 + lc]
            for r in range(8):
                for a in range((width - 1 - r) // 8 + 1):
                    k = width - 1 - (8 * a + r)
                    if r == 0:
                        src = pad_ref[pl.ds(row0 + (hp - 8 * a), rt), c0:c0 + lc]
                    else:
                        src = sh_ref[r - 1, pl.ds(row0 + (halo - 8 * a), rt), c0:c0 + lc]
                    acc = acc + src.reshape(rt // 8, 8, lc) * wb_ref[k, :, c0:c0 + lc]
            parts.append(acc.reshape(rt, lc))
        o_ref[0, pl.ds(row0, rt), :] = finish(jnp.concatenate(parts, axis=1))

    if tl == rt:
        row_tile(0)
    else:
        def body(it, carry):
            row_tile(pl.multiple_of(it * rt, rt))
            return carry

        lax.fori_loop(0, tl // rt, body, 0)


def causal_dwconv(x, hist, w, b, post, ln_g=None, ln_b=None, out_dtype=F32):
    bsz, L, C = x.shape
    width = w.shape[0]
    hp = hist.shape[1]
    assert hp % 8 == 0 and hp >= 8 * ((width - 1) // 8) + min(7, width - 1)
    tc = C if post == "ln_silu" else _pick_tile(C, (1024, 512, 256, 128))
    tl = _pick_tile(L, (256, 128)) if L >= hp else L
    assert tl >= hp or tl == L
    wp = jnp.zeros((-(-width // 8) * 8, C), F32).at[:width].set(w.astype(F32))
    if ln_g is None:
        ln_g = jnp.ones((C,), F32)
        ln_b = jnp.zeros((C,), F32)
    row = lambda v: v.reshape(1, C).astype(F32)
    vec_spec = pl.BlockSpec((1, tc), lambda bb, c, t: (0, c))
    return pl.pallas_call(
        functools.partial(_dwconv_kernel, width=width, tl=tl, hp=hp, post=post),
        grid=(bsz, C // tc, L // tl),
        in_specs=[pl.BlockSpec((1, tl, tc), lambda bb, c, t: (bb, t, c)),
                  pl.BlockSpec((1, hp, tc), lambda bb, c, t: (bb, 0, c)),
                  pl.BlockSpec((wp.shape[0], tc), lambda bb, c, t: (0, c)),
                  vec_spec, vec_spec, vec_spec],
        out_specs=pl.BlockSpec((1, tl, tc), lambda bb, c, t: (bb, t, c)),
        out_shape=jax.ShapeDtypeStruct((bsz, L, C), out_dtype),
        scratch_shapes=[pltpu.VMEM((hp + tl, tc), F32)] + (
            [pltpu.VMEM((7, 8 * ((width - 1) // 8) + tl, tc), F32),
             pltpu.VMEM((width, 8, tc), F32)] if width > 8 else []),
        compiler_params=_params("parallel", "parallel", "arbitrary"),
        name="dwconv_" + post,
    )(x, hist, wp, row(b), row(ln_g), row(ln_b))


def _ssd_prompt_kernel(x_ref, bm_ref, cm_ref, dt_ref, z_ref, dtb_ref, alog_ref, d_ref, ng_ref, h0_ref,
                       yn_ref, h_ref, y_buf, *, chunk, hpg):
    c = pl.program_id(1)
    P, N = SSM_HEAD_DIM, SSM_D_STATE
    gw = hpg * P

    @pl.when(c == 0)
    def _():
        h_ref[...] = h0_ref[...]

    li = lax.broadcasted_iota(jnp.int32, (chunk, chunk), 0)
    si = lax.broadcasted_iota(jnp.int32, (chunk, chunk), 1)
    causal = li >= si
    lower = causal.astype(BF16)
    dt = _softplus(dt_ref[0] + dtb_ref[...])
    a = -jnp.exp(alog_ref[...])
    hi, mid, lo = _split3(dt * a)
    cum = _dot(lower, hi) + _dot(lower, mid) + _dot(lower, lo)
    cum_t = cum.T
    dt_t = dt.T
    for g in range(SSM_GROUPS):
        x = x_ref[0, :, g * gw:(g + 1) * gw]
        cmb = cm_ref[0, :, g * N:(g + 1) * N].astype(BF16)
        bmb = bm_ref[0, :, g * N:(g + 1) * N].astype(BF16)
        cb = _dot_nt(cmb, bmb)
        x_t = x.T
        for r in range(hpg):
            head = g * hpg + r
            cum_col = cum[:, head:head + 1]
            cum_row = cum_t[head:head + 1, :]
            dt_row = dt_t[head:head + 1, :]
            total = cum_t[head:head + 1, chunk - 1:chunk]
            xh = x[:, r * P:(r + 1) * P]
            decay = jnp.exp(jnp.where(causal, cum_col - cum_row, -jnp.inf))
            mix = (cb * decay * dt_row).astype(BF16)
            y = _dot(mix, xh.astype(BF16))
            h_old = h_ref[0, head]
            y = y + _dot_nt(cmb, h_old.astype(BF16)) * jnp.exp(cum_col)
            y_buf[:, r * P:(r + 1) * P] = y + d_ref[:, head:head + 1] * xh
            to_end = jnp.exp(total - cum_row) * dt_row
            s_chunk = _dot((x_t[r * P:(r + 1) * P, :] * to_end).astype(BF16), bmb)
            h_ref[0, head] = jnp.exp(total) * h_old + s_chunk
        z = z_ref[0, :, g * gw:(g + 1) * gw]
        hg = y_buf[...] * (z * jax.nn.sigmoid(z))
        hg = hg * lax.rsqrt(jnp.mean(hg * hg, axis=-1, keepdims=True) + LN_EPS)
        yn_ref[0, :, g * gw:(g + 1) * gw] = (hg * ng_ref[:, g * gw:(g + 1) * gw]).astype(yn_ref.dtype)


def ssd_prompt(xbc, dt_raw, z, dt_bias_g, a_log_g, d_g, norm_g, h0):
    bsz, L, _ = xbc.shape
    n_heads = h0.shape[1]
    G, N, P = SSM_GROUPS, SSM_D_STATE, SSM_HEAD_DIM
    hpg = n_heads // G
    d_inner = n_heads * P
    gn = G * N
    chunk = SSM_CHUNK
    assert d_inner % gn == 0 and (hpg * P) % LANES == 0 and n_heads <= LANES
    b_off = d_inner // gn
    vec_spec = pl.BlockSpec((1, LANES), lambda b, c: (0, 0))
    state_spec = pl.BlockSpec((1, n_heads, P, N), lambda b, c: (b, 0, 0, 0))
    return pl.pallas_call(
        functools.partial(_ssd_prompt_kernel, chunk=chunk, hpg=hpg),
        grid=(bsz, L // chunk),
        in_specs=[pl.BlockSpec((1, chunk, d_inner), lambda b, c: (b, c, 0)),
                  pl.BlockSpec((1, chunk, gn), lambda b, c: (b, c, b_off)),
                  pl.BlockSpec((1, chunk, gn), lambda b, c: (b, c, b_off + 1)),
                  pl.BlockSpec((1, chunk, LANES), lambda b, c: (b, c, 0)),
                  pl.BlockSpec((1, chunk, d_inner), lambda b, c: (b, c, 0)),
                  vec_spec, vec_spec, vec_spec,
                  pl.BlockSpec((1, d_inner), lambda b, c: (0, 0)),
                  state_spec],
        out_specs=[pl.BlockSpec((1, chunk, d_inner), lambda b, c: (b, c, 0)), state_spec],
        out_shape=[jax.ShapeDtypeStruct((bsz, L, d_inner), BF16),
                   jax.ShapeDtypeStruct((bsz, n_heads, P, N), F32)],
        scratch_shapes=[pltpu.VMEM((chunk, hpg * P), F32)],
        compiler_params=_params("parallel", "arbitrary"),
        name="ssd_prompt",
    )(xbc, xbc, xbc, dt_raw, z, dt_bias_g, a_log_g, d_g, norm_g.reshape(1, d_inner).astype(F32), h0)


def _ssd_step_kernel(xc_ref, bc_ref, dt_ref, dtb_ref, alog_ref, d_ref, h0_ref, y_ref, h_ref, *, n_heads):
    N = SSM_D_STATE
    hpg = n_heads // SSM_GROUPS
    dt = _softplus(dt_ref[0] + dtb_ref[...])
    da = jnp.exp(dt * (-jnp.exp(alog_ref[...])))
    for h in range(n_heads):
        g = h // hpg
        bm = bc_ref[0, :, g * N:(g + 1) * N]
        cm = bc_ref[0, :, (SSM_GROUPS + g) * N:(SSM_GROUPS + g + 1) * N]
        xcol = xc_ref[0, h]
        h_new = da[:, h:h + 1] * h0_ref[0, h] + xcol * (dt[:, h:h + 1] * bm)
        h_ref[0, h] = h_new
        y_ref[0, h] = jnp.sum(h_new * cm, axis=1, keepdims=True) + d_ref[:, h:h + 1] * xcol


def ssd_step(x_col, bc, dt_raw, dt_bias, a_log, d_skip, h0):
    db, n_heads, P, _ = x_col.shape
    N = SSM_D_STATE
    vec_spec = pl.BlockSpec((1, LANES), lambda b: (0, 0))
    state_spec = pl.BlockSpec((1, n_heads, P, N), lambda b: (b, 0, 0, 0))
    col_spec = pl.BlockSpec((1, n_heads, P, 1), lambda b: (b, 0, 0, 0))
    return pl.pallas_call(
        functools.partial(_ssd_step_kernel, n_heads=n_heads),
        grid=(db,),
        in_specs=[col_spec,
                  pl.BlockSpec((1, 1, bc.shape[2]), lambda b: (b, 0, 0)),
                  pl.BlockSpec((1, 1, LANES), lambda b: (b, 0, 0)),
                  vec_spec, vec_spec, vec_spec, state_spec],
        out_specs=[col_spec, state_spec],
        out_shape=[jax.ShapeDtypeStruct((db, n_heads, P, 1), F32),
                   jax.ShapeDtypeStruct((db, n_heads, P, N), F32)],
        compiler_params=_params("parallel"),
        name="ssd_step",
    )(x_col, bc, dt_raw, dt_bias, a_log, d_skip, h0)


def _gated_norm_kernel(y_ref, z_ref, g_ref, o_ref):
    z = z_ref[...]
    h = y_ref[...] * (z * jax.nn.sigmoid(z))
    h = h * lax.rsqrt(jnp.mean(h * h, axis=-1, keepdims=True) + LN_EPS)
    o_ref[...] = (h * g_ref[...]).astype(o_ref.dtype)


def gated_rms_norm(y, z, g):
    m, d = y.shape
    gw = d // SSM_GROUPS
    tm = _pick_tile(m, (512, 256, 128, 8))
    spec = pl.BlockSpec((tm, gw), lambda i, j: (i, j))
    return pl.pallas_call(
        _gated_norm_kernel,
        grid=(m // tm, SSM_GROUPS),
        in_specs=[spec, spec, pl.BlockSpec((1, gw), lambda i, j: (0, j))],
        out_specs=spec,
        out_shape=jax.ShapeDtypeStruct((m, d), BF16),
        compiler_params=_params("parallel", "parallel"),
        name="gated_rms_norm",
    )(y, z, g.reshape(1, d).astype(F32))


def _lane_pad(v):
    return jnp.zeros((1, LANES), F32).at[0, :v.shape[0]].set(v.astype(F32))


def _ffn_half(xp_f, xp_b, xs_f, xs_b, w1, w3, lead, w2, g, b):
    hp, hs = gated_matmul(xp_b, xs_b, w1, w3, None, None, "swiglu", BF16, lead=lead)
    return matmul_postnorm(hp, hs, w2, None, xp_f, xs_f, g, b, 0.5, lead=lead)


def _pad_rows(x, rows):
    return jnp.concatenate([x, jnp.zeros((x.shape[0], rows - x.shape[1]) + x.shape[2:], x.dtype)], axis=1)


def _front_pad_rows(x, rows):
    return jnp.concatenate([jnp.zeros((x.shape[0], rows - x.shape[1]) + x.shape[2:], x.dtype), x], axis=1)


def kernel(x_prompt, x_sample, p_prompt, p_sample, cache_sb_k, cache_sb_v, cache_moba_k, cache_moba_v, state_ssm, state_ssm_conv, state_conf_conv, page_table, ln_g, ln_b, ffn_w1, ffn_w3, ffn_w2, ple_w_proj, ple_w_gate, sb_w_qkv, sb_w_o, ssm_w_in, ssm_conv_w, ssm_conv_b, ssm_dt_bias, ssm_a_log, ssm_d, ssm_norm_g, ssm_w_out, conf_w_pw1, conf_b_pw1, conf_w_dw, conf_b_dw, conf_ln_g, conf_ln_b, conf_w_pw2, conf_b_pw2, moba_w_qkv, moba_w_o):
    bsz, seq, d_model = x_prompt.shape
    db = x_sample.shape[0]
    assert x_sample.shape[1] == 1
    mp = bsz * seq
    depth = ffn_w1.shape[0]
    bf = lambda w: w.astype(BF16)

    xp_f = x_prompt.reshape(mp, d_model)
    xs_f = x_sample.reshape(db, d_model)
    xp_b, xs_b = bf(xp_f), bf(xs_f)
    n_phys = cache_sb_k.shape[1]
    ffn_w2_b = bf(ffn_w2)
    outs ={k: [] for k in ("sb_kp", "sb_vp", "sb_ks", "sb_vs", "ssm_hp", "ssm_hs", "ssm_cp", "ssm_cs",
                            "conf_cp", "conf_cs", "mo_kp", "mo_vp", "mo_ks", "mo_vs")}

    for i in range(depth):
        m, j = i % 4, i // 4
        (xp_f, xp_b), (xs_f, xs_b) = _ffn_half(xp_f, xp_b, xs_f, xs_b, ffn_w1, ffn_w3, (i, 0), ffn_w2_b,
                                               ln_g[i, 0], ln_b[i, 0])
        g1, b1 = ln_g[i, 1], ln_b[i, 1]

        if m == 0 or m == 3:
            w_qkv = bf(sb_w_qkv[j] if m == 0 else moba_w_qkv[j])
            w_o = bf(sb_w_o[j] if m == 0 else moba_w_o[j])
            q, qs = matmul(xp_b, xs_b, w_qkv, None, BF16, 0, Q_DIM)
            k, ks = matmul(xp_b, xs_b, w_qkv, None, F32, Q_DIM, KV_DIM)
            v, vs = matmul(xp_b, xs_b, w_qkv, None, F32, Q_DIM + KV_DIM, KV_DIM)
            q = q.reshape(bsz, seq, Q_DIM)
            k = k.reshape(bsz, seq, KV_DIM)
            v = v.reshape(bsz, seq, KV_DIM)
            attend = sb_prompt_attention if m == 0 else moba_prompt_attention
            o = attend(q, k, v).reshape(mp, Q_DIM)
            kp = k.reshape(bsz, seq, N_KV_HEADS, HEAD_DIM)
            vp = v.reshape(bsz, seq, N_KV_HEADS, HEAD_DIM)
            if m == 0:
                kc = cache_sb_k[j].reshape(n_phys * PAGE_ROWS, HEAD_DIM)
                vc = cache_sb_v[j].reshape(n_phys * PAGE_ROWS, HEAD_DIM)
                os_ = sb_decode_attention(qs.reshape(db, N_HEADS, HEAD_DIM), kc, vc, page_table)
            else:
                kc = cache_moba_k[j].reshape(n_phys * PAGE_ROWS, HEAD_DIM)
                vc = cache_moba_v[j].reshape(n_phys * PAGE_ROWS, HEAD_DIM)
                qh = qs.reshape(db, N_HEADS, HEAD_DIM)
                kmean = moba_cache_block_means(kc, page_table)
                sel = moba_decode_select(qh, kmean)
                os_ = moba_decode_attention(qh.reshape(db, N_HEADS, 1, HEAD_DIM),
                                            ks.reshape(db, N_KV_HEADS, 1, HEAD_DIM),
                                            vs.reshape(db, N_KV_HEADS, 1, HEAD_DIM),
                                            sel, kc, vc, page_table)
            (xp_f, xp_b), (xs_f, xs_b) = matmul_postnorm(o, bf(os_.reshape(db, Q_DIM)), w_o, None, xp_f, xs_f,
                                                         g1, b1, 1.0)
            ks4 = ks.reshape(db, 1, N_KV_HEADS, HEAD_DIM)
            vs4 = vs.reshape(db, 1, N_KV_HEADS, HEAD_DIM)
            if m == 0:
                outs["sb_kp"].append(kp); outs["sb_vp"].append(vp); outs["sb_ks"].append(ks4); outs["sb_vs"].append(vs4)
            else:
                outs["mo_kp"].append(kp); outs["mo_vp"].append(vp); outs["mo_ks"].append(ks4); outs["mo_vs"].append(vs4)

        elif m == 1:
            n_heads = ssm_dt_bias.shape[1]
            d_inner = n_heads * SSM_HEAD_DIM
            conv_dim = ssm_conv_w.shape[2]
            hpg = n_heads // SSM_GROUPS
            w_in = bf(ssm_w_in[j])
            w_dt_cols = ssm_w_in[j][:, d_inner + conv_dim:]
            w_dt = bf(jnp.zeros((d_model, LANES), F32).at[:, :n_heads].set(w_dt_cols))
            dtb, alog, dskip = _lane_pad(ssm_dt_bias[j]), _lane_pad(ssm_a_log[j]), _lane_pad(ssm_d[j])
            w_out = bf(ssm_w_out[j])
            hp = 8
            z, zs = matmul(xp_b, xs_b, w_in, None, F32, 0, d_inner)
            xbc_raw, xbc_s = matmul(xp_b, xs_b, w_in, None, F32, d_inner, conv_dim)
            dt_raw, dt_s = matmul(xp_b, xs_b, w_dt, None, F32)
            xbc_raw = xbc_raw.reshape(bsz, seq, conv_dim)
            dt_raw = dt_raw.reshape(bsz, seq, LANES)
            xbc = causal_dwconv(xbc_raw, jnp.zeros((bsz, hp, conv_dim), F32), ssm_conv_w[j], ssm_conv_b[j], "silu")
            h0 = jnp.zeros((bsz, n_heads, SSM_HEAD_DIM, SSM_D_STATE), F32)
            yn, h_last = ssd_prompt(xbc, dt_raw, z.reshape(bsz, seq, d_inner), dtb, alog, dskip, ssm_norm_g[j], h0)
            outs["ssm_hp"].append(h_last)
            outs["ssm_cp"].append(xbc_raw[:, seq - (SSM_CONV - 1):, :])
            xbc_s = xbc_s.reshape(db, 1, conv_dim)
            dt_s = dt_s.reshape(db, 1, LANES)
            hist = _front_pad_rows(state_ssm_conv[j], hp)
            xbc_sa = causal_dwconv(_pad_rows(xbc_s, 8), hist, ssm_conv_w[j], ssm_conv_b[j], "silu")[:, :1, :]
            x_col = xbc_sa[:, 0, :d_inner].reshape(db, n_heads, SSM_HEAD_DIM, 1)
            y_col, hs = ssd_step(x_col, xbc_sa[:, :, d_inner:], dt_s, dtb, alog, dskip, state_ssm[j])
            yns = gated_rms_norm(y_col.reshape(db, d_inner), zs, ssm_norm_g[j])
            (xp_f, xp_b), (xs_f, xs_b) = matmul_postnorm(yn.reshape(mp, d_inner), yns, w_out, None, xp_f, xs_f,
                                                         g1, b1, 1.0)
            outs["ssm_hs"].append(hs)
            outs["ssm_cs"].append(jnp.concatenate([state_ssm_conv[j], xbc_s], axis=1)[:, 1:, :])

        else:
            ba, bg = conf_b_pw1[j][:d_model], conf_b_pw1[j][d_model:]
            w_pw2 = bf(conf_w_pw2[j])
            hp = 32
            u, us = gated_matmul(xp_b, xs_b, conf_w_pw1, conf_w_pw1, ba, bg, "glu", F32, lead=(j,),
                                 cola=0, colb=d_model, n=d_model)
            u = u.reshape(bsz, seq, d_model)
            us = us.reshape(db, 1, d_model)
            uc = causal_dwconv(u, jnp.zeros((bsz, hp, d_model), F32), conf_w_dw[j], conf_b_dw[j], "ln_silu",
                               conf_ln_g[j], conf_ln_b[j], BF16)
            outs["conf_cp"].append(u[:, seq - (CONF_WIDTH - 1):, :])
            hist = _front_pad_rows(state_conf_conv[j], hp)
            ucs = causal_dwconv(_pad_rows(us, 8), hist, conf_w_dw[j], conf_b_dw[j], "ln_silu",
                                conf_ln_g[j], conf_ln_b[j], BF16)[:, 0, :]
            (xp_f, xp_b), (xs_f, xs_b) = matmul_postnorm(uc.reshape(mp, d_model), ucs, w_pw2, conf_b_pw2[j],
                                                         xp_f, xs_f, g1, b1, 1.0)
            outs["conf_cs"].append(jnp.concatenate([state_conf_conv[j], us], axis=1)[:, 1:, :])

        (xp_f, xp_b), (xs_f, xs_b) = _ffn_half(xp_f, xp_b, xs_f, xs_b, ffn_w1, ffn_w3, (i, 1), ffn_w2_b,
                                               ln_g[i, 2], ln_b[i, 2])
        wg_, wp_ = bf(ple_w_gate[i]), bf(ple_w_proj[i])
        (xp_f, xp_b), (xs_f, xs_b) = ple_add(xp_b, xp_f, p_prompt.reshape(depth, mp, -1), xs_b, xs_f,
                                             p_sample.reshape(depth, db, -1), i, wg_, wp_, ln_g[i, 3], ln_b[i, 3])

    st = lambda name: jnp.stack(outs[name])
    return (xp_f.reshape(bsz, seq, d_model), xs_f.reshape(db, 1, d_model),
            st("sb_kp"), st("sb_vp"), st("sb_ks"), st("sb_vs"),
            st("ssm_hp"), st("ssm_hs"), st("ssm_cp"), st("ssm_cs"),
            st("conf_cp"), st("conf_cs"),
            st("mo_kp"), st("mo_vp"), st("mo_ks"), st("mo_vs"))
```

```python
import functools

import jax
import jax.numpy as jnp
from jax import lax
from jax.experimental import pallas as pl
from jax.experimental.pallas import tpu as pltpu

F32 = jnp.float32
BF16 = jnp.bfloat16

N_HEADS = 16
HEAD_DIM = 128
N_KV_HEADS = 4
Q_PER_KV = N_HEADS // N_KV_HEADS
Q_DIM = N_HEADS * HEAD_DIM
KV_DIM = N_KV_HEADS * HEAD_DIM
SB_BLOCK = 128
SB_KV_PER_STEP = 2
MOBA_BLOCK = 256
MOBA_TOPK = 3
PAGE_SIZE = 128
SSM_HEAD_DIM = 64
SSM_GROUPS = 8
SSM_D_STATE = 128
SSM_CONV = 4
SSM_CHUNK = 128
CONF_WIDTH = 31
LN_EPS = 1e-5
DEPTH = 4
DN_ALPHA = (2 * DEPTH) ** 0.25
LANES = 128
NEG_BIG = -1e30
DWCONV_ROW_TILE = 64
DWCONV_LANE_CHUNK = 256
SB_LOG_FLOOR = -104.0
VMEM_LIMIT = 56 * 1024 * 1024


def _params(*sem):
    return pltpu.CompilerParams(dimension_semantics=sem, vmem_limit_bytes=VMEM_LIMIT)


def _layer_norm_rows(y, g, b):
    mu = jnp.mean(y, axis=-1, keepdims=True)
    d = y - mu
    var = jnp.mean(d * d, axis=-1, keepdims=True)
    return d * lax.rsqrt(var + LN_EPS) * g + b


def _split3(x):
    hi = x.astype(BF16)
    r1 = x - hi.astype(F32)
    mid = r1.astype(BF16)
    lo = (r1 - mid.astype(F32)).astype(BF16)
    return hi, mid, lo


def _dot(a, b):
    return jnp.dot(a, b, preferred_element_type=F32)


def _dot_nt(a, b):
    return lax.dot_general(a, b, (((1,), (1,)), ((), ())), preferred_element_type=F32)


def _dot01(x, m01):
    hi, mid, lo = _split3(x)
    return _dot(hi, m01) + _dot(mid, m01) + _dot(lo, m01)


def _pick_tile(n, pref):
    for t in pref:
        if n % t == 0:
            return t
    return n


def _mm_kernel(x_ref, xs_ref, w_ref, b_ref, o_ref, os_ref):
    o_ref[...] = (_dot(x_ref[...], w_ref[...]) + b_ref[...]).astype(o_ref.dtype)
    first = pl.program_id(0) == 0

    @pl.when(first)
    def _():
        os_ref[0] = (_dot(xs_ref[...], w_ref[...]) + b_ref[...]).astype(os_ref.dtype)

    @pl.when(jnp.logical_not(first))
    def _():
        os_ref[...] = jnp.zeros_like(os_ref)


def matmul(x, xs, w, bias=None, out_dtype=F32, col0=0, n=None):
    m, k = x.shape
    s = xs.shape[0]
    n = w.shape[1] - col0 if n is None else n
    tm = _pick_tile(m, (1024, 512, 256, 128, 8))
    tn = _pick_tile(n, (512, 256, 128))
    assert col0 % tn == 0 and n % tn == 0
    j0 = col0 // tn
    if bias is None:
        bias = jnp.zeros((n,), F32)
    out, out_s = pl.pallas_call(
        _mm_kernel,
        grid=(m // tm, n // tn),
        in_specs=[pl.BlockSpec((tm, k), lambda i, j: (i, 0)),
                  pl.BlockSpec((s, k), lambda i, j: (0, 0)),
                  pl.BlockSpec((k, tn), lambda i, j: (0, j + j0)),
                  pl.BlockSpec((1, tn), lambda i, j: (0, j))],
        out_specs=[pl.BlockSpec((tm, tn), lambda i, j: (i, j)),
                   pl.BlockSpec((1, s, tn), lambda i, j: (i, 0, j))],
        out_shape=[jax.ShapeDtypeStruct((m, n), out_dtype),
                   jax.ShapeDtypeStruct((m // tm, s, n), F32)],
        compiler_params=_params("arbitrary", "arbitrary"),
        name="mm",
    )(x, xs, w, bias.reshape(1, n).astype(F32))
    return out, out_s[0]


def _gated_mm_kernel(x_ref, xs_ref, wa_ref, wb_ref, ba_ref, bb_ref, o_ref, os_ref, wa_s, wb_s, *, mode):
    first = pl.program_id(1) == 0

    def gated(x):
        a = _dot(x, wa_s[...]) + ba_ref[...]
        b = _dot(x, wb_s[...]) + bb_ref[...]
        return a * jax.nn.sigmoid(a) * b if mode == "swiglu" else a * jax.nn.sigmoid(b)

    @pl.when(first)
    def _():
        wa_s[...] = wa_ref[...].astype(BF16)
        wb_s[...] = wb_ref[...].astype(BF16)
        os_ref[0] = gated(xs_ref[...]).astype(os_ref.dtype)

    @pl.when(jnp.logical_not(first))
    def _():
        os_ref[...] = jnp.zeros_like(os_ref)

    o_ref[...] = gated(x_ref[...]).astype(o_ref.dtype)


def gated_matmul(x, xs, wa, wb, ba, bb, mode, out_dtype, lead=(), cola=0, colb=0, n=None):
    m, k = x.shape
    s = xs.shape[0]
    n = wa.shape[-1] - cola if n is None else n
    wblock = (None,) * len(lead) + (k, None)
    tm = _pick_tile(m, (2048, 1024, 512, 256, 128, 8) if jnp.dtype(out_dtype).itemsize == 2
                    else (1024, 512, 256, 128, 8))
    tn = _pick_tile(n, (512, 256, 128))
    assert cola % tn == 0 and colb % tn == 0 and n % tn == 0
    ja, jb = cola // tn, colb // tn
    if ba is None:
        ba = jnp.zeros((n,), F32)
        bb = jnp.zeros((n,), F32)
    out, out_s = pl.pallas_call(
        functools.partial(_gated_mm_kernel, mode=mode),
        grid=(n // tn, m // tm),
        in_specs=[pl.BlockSpec((tm, k), lambda j, i: (i, 0)),
                  pl.BlockSpec((s, k), lambda j, i: (0, 0)),
                  pl.BlockSpec(wblock[:-1] + (tn,), lambda j, i: lead + (0, j + ja)),
                  pl.BlockSpec(wblock[:-1] + (tn,), lambda j, i: lead + (0, j + jb)),
                  pl.BlockSpec((1, tn), lambda j, i: (0, j)),
                  pl.BlockSpec((1, tn), lambda j, i: (0, j))],
        out_specs=[pl.BlockSpec((tm, tn), lambda j, i: (i, j)),
                   pl.BlockSpec((1, s, tn), lambda j, i: (i, 0, j))],
        out_shape=[jax.ShapeDtypeStruct((m, n), out_dtype),
                   jax.ShapeDtypeStruct((m // tm, s, n), out_dtype)],
        scratch_shapes=[pltpu.VMEM((k, tn), BF16), pltpu.VMEM((k, tn), BF16)],
        compiler_params=_params("arbitrary", "arbitrary"),
        name="gated_mm",
    )(x, xs, wa, wb, ba.reshape(1, n).astype(F32), bb.reshape(1, n).astype(F32))
    return out, out_s[0]


def _mm_ln_kernel(a_ref, as_ref, w_ref, bias_ref, res_ref, ress_ref, g_ref, b_ref, of_ref, ob_ref, ofs_ref, obs_ref,
                  accs_ref, *, scale, nk):
    i, k = pl.program_id(0), pl.program_id(1)

    def post_norm(res, acc):
        y = DN_ALPHA * res + scale * (acc + bias_ref[...])
        return _layer_norm_rows(y, g_ref[...], b_ref[...])

    def finish(acc):
        out = post_norm(res_ref[...], acc)
        of_ref[...] = out
        ob_ref[...] = out.astype(BF16)

    @pl.when(i == 0)
    def _():
        part = _dot(as_ref[...], w_ref[...])
        if nk > 1:
            @pl.when(k > 0)
            def _():
                accs_ref[...] += part

            @pl.when(k == 0)
            def _():
                accs_ref[...] = part
        else:
            accs_ref[...] = part

    @pl.when(k == nk - 1)
    def _():
        out = post_norm(ress_ref[...], accs_ref[...])
        first = i == 0
        ofs_ref[0] = jnp.where(first, out, 0.0)
        obs_ref[0] = jnp.where(first, out, 0.0).astype(BF16)

    if nk == 1:
        finish(_dot(a_ref[...], w_ref[...]))
        return

    @pl.when(k == 0)
    def _():
        of_ref[...] = _dot(a_ref[...], w_ref[...])

    @pl.when(jnp.logical_and(k > 0, k < nk - 1))
    def _():
        of_ref[...] += _dot(a_ref[...], w_ref[...])

    @pl.when(k == nk - 1)
    def _():
        finish(of_ref[...] + _dot(a_ref[...], w_ref[...]))


def matmul_postnorm(a, a_s, w, bias, res, res_s, g, b, scale, lead=()):
    m, kdim = a.shape
    s = a_s.shape[0]
    n = w.shape[-1]
    tm = _pick_tile(m, (512, 256, 128, 8))
    tk = _pick_tile(kdim, (2816, 2048, 1408, 1024, 512, 256, 128))
    nk = kdim // tk
    if bias is None:
        bias = jnp.zeros((n,), F32)
    row = lambda v: v.reshape(1, n).astype(F32)
    vec = pl.BlockSpec((1, n), lambda i, k: (0, 0))
    of, ob, ofs, obs = pl.pallas_call(
        functools.partial(_mm_ln_kernel, scale=scale, nk=nk),
        grid=(m // tm, nk),
        in_specs=[pl.BlockSpec((tm, tk), lambda i, k: (i, k)),
                  pl.BlockSpec((s, tk), lambda i, k: (0, k)),
                  pl.BlockSpec((None,) * len(lead) + (tk, n), lambda i, k: lead + (k, 0)),
                  vec,
                  pl.BlockSpec((tm, n), lambda i, k: (i, 0)),
                  pl.BlockSpec((s, n), lambda i, k: (0, 0)),
                  vec, vec],
        out_specs=[pl.BlockSpec((tm, n), lambda i, k: (i, 0)),
                   pl.BlockSpec((tm, n), lambda i, k: (i, 0)),
                   pl.BlockSpec((1, s, n), lambda i, k: (i, 0, 0)),
                   pl.BlockSpec((1, s, n), lambda i, k: (i, 0, 0))],
        out_shape=[jax.ShapeDtypeStruct((m, n), F32), jax.ShapeDtypeStruct((m, n), BF16),
                   jax.ShapeDtypeStruct((m // tm, s, n), F32), jax.ShapeDtypeStruct((m // tm, s, n), BF16)],
        scratch_shapes=[pltpu.VMEM((s, n), F32)],
        compiler_params=_params("arbitrary", "arbitrary"),
        name="mm_postnorm",
    )(a, a_s, w, row(bias), res, res_s, row(g), row(b))
    return (of, ob), (ofs[0], obs[0])


def _ple_kernel(xb_ref, xf_ref, p_ref, xsb_ref, xsf_ref, ps_ref, wg_ref, wp_ref, g_ref, b_ref,
                of_ref, ob_ref, ofs_ref, obs_ref):
    def ple(xb, xf, p):
        gate = jax.nn.sigmoid(_dot(xb, wg_ref[...]))
        proj = _dot(p.astype(BF16), wp_ref[...])
        return _layer_norm_rows(DN_ALPHA * xf + gate * proj, g_ref[...], b_ref[...])

    out = ple(xb_ref[...], xf_ref[...], p_ref[...])
    of_ref[...] = out
    ob_ref[...] = out.astype(BF16)
    first = pl.program_id(0) == 0

    @pl.when(first)
    def _():
        outs = ple(xsb_ref[...], xsf_ref[...], ps_ref[...])
        ofs_ref[0] = outs
        obs_ref[0] = outs.astype(BF16)

    @pl.when(jnp.logical_not(first))
    def _():
        ofs_ref[...] = jnp.zeros_like(ofs_ref)
        obs_ref[...] = jnp.zeros_like(obs_ref)


def ple_add(xb, xf, p, xsb, xsf, ps, layer, wg, wp, g, b):
    m, d = xf.shape
    s = xsf.shape[0]
    pd = p.shape[-1]
    tm = _pick_tile(m, (512, 256, 128, 8))
    row = lambda v: v.reshape(1, d).astype(F32)
    const = lambda shape: pl.BlockSpec(shape, lambda i: (0,) * len(shape))
    of, ob, ofs, obs = pl.pallas_call(
        _ple_kernel,
        grid=(m // tm,),
        in_specs=[pl.BlockSpec((tm, d), lambda i: (i, 0)),
                  pl.BlockSpec((tm, d), lambda i: (i, 0)),
                  pl.BlockSpec((None, tm, pd), lambda i: (layer, i, 0)),
                  const((s, d)), const((s, d)),
                  pl.BlockSpec((None, s, pd), lambda i: (layer, 0, 0)),
                  const((d, d)), const((pd, d)), const((1, d)), const((1, d))],
        out_specs=[pl.BlockSpec((tm, d), lambda i: (i, 0)),
                   pl.BlockSpec((tm, d), lambda i: (i, 0)),
                   pl.BlockSpec((1, s, d), lambda i: (i, 0, 0)),
                   pl.BlockSpec((1, s, d), lambda i: (i, 0, 0))],
        out_shape=[jax.ShapeDtypeStruct((m, d), F32), jax.ShapeDtypeStruct((m, d), BF16),
                   jax.ShapeDtypeStruct((m // tm, s, d), F32), jax.ShapeDtypeStruct((m // tm, s, d), BF16)],
        compiler_params=_params("arbitrary"),
        name="ple_add",
    )(xb, xf, p, xsb, xsf, ps, wg, wp, row(g), row(b))
    return (of, ob), (ofs[0], obs[0])


def _softplus(z):
    return jnp.maximum(z, 0.0) + jnp.log1p(jnp.exp(-jnp.abs(z)))


def _sb_prompt_kernel(q_ref, k_ref, v_ref, o_ref, *, tq, nkv):
    i = pl.program_id(2)
    rows = Q_PER_KV * tq
    scale = HEAD_DIM ** -0.5
    gw = Q_PER_KV * HEAD_DIM
    qs = [jnp.concatenate([q_ref[0, :, h * gw + g * HEAD_DIM:h * gw + (g + 1) * HEAD_DIM] for g in range(Q_PER_KV)],
                          axis=0) for h in range(nkv)]
    t_loc = lax.broadcasted_iota(jnp.int32, (rows, tq), 0) & (tq - 1)
    s_loc = lax.broadcasted_iota(jnp.int32, (rows, tq), 1)
    later = (lax.broadcasted_iota(jnp.int32, (tq, tq), 0) > lax.broadcasted_iota(jnp.int32, (tq, tq), 1)).astype(BF16)

    def block(h, j, run, acc, diagonal):
        start = pl.multiple_of(j * tq, tq)
        kb = k_ref[0, pl.ds(start, tq), h * HEAD_DIM:(h + 1) * HEAD_DIM].astype(BF16)
        vb = v_ref[0, pl.ds(start, tq), h * HEAD_DIM:(h + 1) * HEAD_DIM].astype(BF16)
        z = _dot_nt(qs[h], kb) * scale
        log_keep = -_softplus(z)
        if diagonal:
            valid = s_loc < t_loc
            log_keep = jnp.where(valid, log_keep, 0.0)
        log_w = z + log_keep + _dot01(log_keep, later) + run
        w = jnp.exp(log_w)
        if diagonal:
            w = jnp.where(valid, w, 0.0)
        acc = acc + _dot(w.astype(BF16), vb)
        run = run + jnp.sum(log_keep, axis=1, keepdims=True)
        return run, acc

    run0 = jnp.zeros((rows, 1), F32)
    acc0 = jnp.zeros((rows, HEAD_DIM), F32)
    state = []
    for h in range(nkv):
        state += list(block(h, i, run0, acc0, True))

    def cond(carry):
        live = jnp.max(carry[1])
        for h in range(1, nkv):
            live = jnp.maximum(live, jnp.max(carry[1 + 2 * h]))
        return jnp.logical_and(carry[0] < i, live > SB_LOG_FLOOR)

    def body(carry):
        it = carry[0]
        out = [it + 1]
        for h in range(nkv):
            out += list(block(h, i - 1 - it, carry[1 + 2 * h], carry[2 + 2 * h], False))
        return tuple(out)

    final = lax.while_loop(cond, body, (jnp.int32(0),) + tuple(state))
    for h in range(nkv):
        acc = final[2 + 2 * h]
        for g in range(Q_PER_KV):
            o_ref[0, :, h * gw + g * HEAD_DIM:h * gw + (g + 1) * HEAD_DIM] = acc[g * tq:(g + 1) * tq].astype(o_ref.dtype)


def sb_prompt_attention(q, k, v):
    bsz, L, _ = q.shape
    tq = SB_BLOCK
    nkv = SB_KV_PER_STEP
    gw = nkv * Q_PER_KV * HEAD_DIM
    return pl.pallas_call(
        functools.partial(_sb_prompt_kernel, tq=tq, nkv=nkv),
        grid=(bsz, N_KV_HEADS // nkv, L // tq),
        in_specs=[pl.BlockSpec((1, tq, gw), lambda b, h, i: (b, i, h)),
                  pl.BlockSpec((1, L, nkv * HEAD_DIM), lambda b, h, i: (b, 0, h)),
                  pl.BlockSpec((1, L, nkv * HEAD_DIM), lambda b, h, i: (b, 0, h))],
        out_specs=pl.BlockSpec((1, tq, gw), lambda b, h, i: (b, i, h)),
        out_shape=jax.ShapeDtypeStruct(q.shape, BF16),
        compiler_params=_params("parallel", "parallel", "arbitrary"),
        name="sb_prompt",
    )(q, k, v)


PAGE_ROWS = PAGE_SIZE * N_KV_HEADS
SB_PAGES_PER_GROUP = 2
MOBA_BLOCKS_PER_STEP = 8


def _own_lane_mask(kvh_of_row, shape):
    lane = lax.broadcasted_iota(jnp.int32, shape, 1)
    return (lane & (N_KV_HEADS - 1)) == kvh_of_row


def _suffix_sum_keys(x):
    n = x.shape[1]
    lane = lax.broadcasted_iota(jnp.int32, x.shape, 1)
    d = N_KV_HEADS
    while d < n:
        shifted = pltpu.roll(x, n - d, axis=1)
        x = x + jnp.where(lane + d < n, shifted, 0.0)
        d *= 2
    return x


def _sb_decode_kernel(pt_ref, q_ref, k_hbm, v_hbm, o_ref, kbuf, vbuf, sems, acc_ref, run_ref, *, n_pages, gp):
    b = pl.program_id(0)
    n_groups = n_pages // gp

    def page_copies(g, slot):
        copies = []
        for u in range(gp):
            page = pt_ref[b * n_pages + (n_pages - 1 - (g * gp + u))]
            rows = pl.ds(pl.multiple_of(page * PAGE_ROWS, PAGE_ROWS), PAGE_ROWS)
            copies.append(pltpu.make_async_copy(k_hbm.at[rows], kbuf.at[slot, u], sems.at[0, slot, u]))
            copies.append(pltpu.make_async_copy(v_hbm.at[rows], vbuf.at[slot, u], sems.at[1, slot, u]))
        return copies

    def start(g, slot):
        for c in page_copies(g, slot):
            c.start()

    def wait(g, slot):
        for c in page_copies(g, slot):
            c.wait()

    acc_ref[...] = jnp.zeros_like(acc_ref)
    run_ref[...] = jnp.zeros_like(run_ref)
    q = q_ref[0].astype(BF16)
    kvh = lax.broadcasted_iota(jnp.int32, (N_HEADS, PAGE_ROWS), 0) >> 2
    own = _own_lane_mask(kvh, (N_HEADS, PAGE_ROWS))
    start(0, 0)

    def live(_):
        return jnp.max(run_ref[...]) > SB_LOG_FLOOR

    def body(g):
        slot = g & 1
        wait(g, slot)

        @pl.when(g + 1 < n_groups)
        def _():
            start(g + 1, 1 - slot)

        for u in range(gp):
            @pl.when(live(None))
            def _():
                z = _dot_nt(q, kbuf[slot, u].astype(BF16)) * (HEAD_DIM ** -0.5)
                log_keep = jnp.where(own, -_softplus(z), 0.0)
                incl = _suffix_sum_keys(log_keep)
                w = jnp.where(own, jnp.exp(z + incl + run_ref[:, 0:1]), 0.0)
                acc_ref[...] += _dot(w.astype(BF16), vbuf[slot, u].astype(BF16))
                run_ref[...] += jnp.sum(log_keep, axis=1, keepdims=True)
        return g + 1

    g_end = lax.while_loop(lambda g: jnp.logical_and(g < n_groups, live(None)), body, jnp.int32(0))

    @pl.when(g_end < n_groups)
    def _():
        wait(g_end, g_end & 1)

    o_ref[0] = acc_ref[...]


def sb_decode_attention(q, k_rows, v_rows, page_table):
    db, n_pages = page_table.shape
    gp = SB_PAGES_PER_GROUP
    assert n_pages % gp == 0
    return pl.pallas_call(
        functools.partial(_sb_decode_kernel, n_pages=n_pages, gp=gp),
        grid_spec=pltpu.PrefetchScalarGridSpec(
            num_scalar_prefetch=1,
            grid=(db,),
            in_specs=[pl.BlockSpec((1, N_HEADS, HEAD_DIM), lambda b, pt: (b, 0, 0)),
                      pl.BlockSpec(memory_space=pl.ANY), pl.BlockSpec(memory_space=pl.ANY)],
            out_specs=pl.BlockSpec((1, N_HEADS, HEAD_DIM), lambda b, pt: (b, 0, 0)),
            scratch_shapes=[pltpu.VMEM((2, gp, PAGE_ROWS, HEAD_DIM), F32),
                            pltpu.VMEM((2, gp, PAGE_ROWS, HEAD_DIM), F32),
                            pltpu.SemaphoreType.DMA((2, 2, gp)),
                            pltpu.VMEM((N_HEADS, HEAD_DIM), F32), pltpu.VMEM((N_HEADS, LANES), F32)]),
        out_shape=jax.ShapeDtypeStruct((db, N_HEADS, HEAD_DIM), F32),
        compiler_params=_params("arbitrary"),
        name="sb_decode",
    )(page_table.reshape(-1), q, k_rows, v_rows)


def _top_k_mask(gate, n_valid, k):
    lane = lax.broadcasted_iota(jnp.int32, gate.shape, 1)
    sel = jnp.zeros(gate.shape, F32)
    picks = []
    for r in range(k):
        m = jnp.max(gate, axis=1, keepdims=True)
        idx = jnp.min(jnp.where(gate == m, lane, gate.shape[1] - 1), axis=1, keepdims=True)
        pick = lane == idx
        sel = jnp.maximum(sel, jnp.where(pick, jnp.where(r < n_valid, 1.0, 0.0), 0.0))
        gate = jnp.where(pick, -jnp.inf, gate)
        picks.append(idx)
    return sel, picks


def _top_k_mask_sublanes(gate_t, n_valid, k):
    sub = lax.broadcasted_iota(jnp.int32, gate_t.shape, 0)
    sel = jnp.zeros(gate_t.shape, F32)
    for r in range(k):
        m = jnp.max(gate_t, axis=0, keepdims=True)
        idx = jnp.min(jnp.where(gate_t == m, sub, gate_t.shape[0] - 1), axis=0, keepdims=True)
        pick = sub == idx
        sel = jnp.maximum(sel, jnp.where(pick, jnp.where(r < n_valid, 1.0, 0.0), 0.0))
        gate_t = jnp.where(pick, -jnp.inf, gate_t)
    return sel


def _moba_prompt_kernel(q_ref, k_ref, v_ref, o_ref, km_ref, *, tq, nb):
    i = pl.program_id(2)
    rows = Q_PER_KV * tq
    scale = HEAD_DIM ** -0.5

    @pl.when(i == 0)
    def _():
        km_ref[...] = jnp.zeros_like(km_ref)
        for jb in range(nb):
            km_ref[jb:jb + 1, :] = jnp.mean(k_ref[0, jb * tq:(jb + 1) * tq, :], axis=0, keepdims=True)

    q = jnp.concatenate([q_ref[0, :, g * HEAD_DIM:(g + 1) * HEAD_DIM] for g in range(Q_PER_KV)], axis=0)
    cands = -(-nb // 16) * 16
    gate_t = _dot_nt(km_ref[0:cands, :].astype(BF16), q)
    gate_t = jnp.where(lax.broadcasted_iota(jnp.int32, (cands, rows), 0) < i, gate_t, -jnp.inf)
    sel_t = _top_k_mask_sublanes(gate_t, i, MOBA_TOPK)
    sel = jnp.concatenate([sel_t, jnp.zeros((LANES - cands, rows), F32)], axis=0).T.astype(BF16)

    def scores(j):
        start = pl.multiple_of(j * tq, tq)
        kb = k_ref[0, pl.ds(start, tq), :].astype(BF16)
        vb = v_ref[0, pl.ds(start, tq), :].astype(BF16)
        return _dot_nt(q, kb) * scale, vb

    s, vb = scores(i)
    t_loc = lax.broadcasted_iota(jnp.int32, (rows, tq), 0) & (tq - 1)
    s_loc = lax.broadcasted_iota(jnp.int32, (rows, tq), 1)
    s = jnp.where(s_loc <= t_loc, s, NEG_BIG)
    m = jnp.max(s, axis=1, keepdims=True)
    p = jnp.exp(s - m)
    l = jnp.sum(p, axis=1, keepdims=True)
    acc = _dot(p.astype(BF16), vb)

    def body(j, carry):
        m, l, acc = carry
        s, vb = scores(j)
        onehot = (lax.broadcasted_iota(jnp.int32, (LANES, tq), 0) == j).astype(BF16)
        chosen = _dot(sel, onehot) > 0.5
        s = jnp.where(chosen, s, NEG_BIG)
        m_new = jnp.maximum(m, jnp.max(s, axis=1, keepdims=True))
        alpha = jnp.exp(m - m_new)
        p = jnp.exp(s - m_new)
        l = alpha * l + jnp.sum(p, axis=1, keepdims=True)
        acc = alpha * acc + _dot(p.astype(BF16), vb)
        return m_new, l, acc

    m, l, acc = lax.fori_loop(0, i, body, (m, l, acc))
    out = acc / l
    for g in range(Q_PER_KV):
        o_ref[0, :, g * HEAD_DIM:(g + 1) * HEAD_DIM] = out[g * tq:(g + 1) * tq].astype(o_ref.dtype)


def moba_prompt_attention(q, k, v):
    bsz, L, _ = q.shape
    tq = MOBA_BLOCK
    nb = L // tq
    assert L % tq == 0 and nb <= LANES
    gw = Q_PER_KV * HEAD_DIM
    return pl.pallas_call(
        functools.partial(_moba_prompt_kernel, tq=tq, nb=nb),
        grid=(bsz, N_KV_HEADS, nb),
        in_specs=[pl.BlockSpec((1, tq, gw), lambda b, h, i: (b, i, h)),
                  pl.BlockSpec((1, L, HEAD_DIM), lambda b, h, i: (b, 0, h)),
                  pl.BlockSpec((1, L, HEAD_DIM), lambda b, h, i: (b, 0, h))],
        out_specs=pl.BlockSpec((1, tq, gw), lambda b, h, i: (b, i, h)),
        out_shape=jax.ShapeDtypeStruct(q.shape, BF16),
        scratch_shapes=[pltpu.VMEM((LANES, HEAD_DIM), F32)],
        compiler_params=_params("parallel", "arbitrary", "arbitrary"),
        name="moba_prompt",
    )(q, k, v)


def _moba_kmean_kernel(pt_ref, *refs, bps, ppb):
    k_refs, o_ref = refs[:bps * ppb], refs[bps * ppb]
    sub = 8
    for j in range(bps):
        tot = jnp.zeros((sub, HEAD_DIM), F32)
        for h in range(ppb):
            k_ref = k_refs[j * ppb + h]
            for c in range(PAGE_ROWS // sub):
                tot = tot + k_ref[c * sub:(c + 1) * sub, :]
        mean = (tot[:N_KV_HEADS] + tot[N_KV_HEADS:]) * (1.0 / MOBA_BLOCK)
        o_ref[0, j * N_KV_HEADS:(j + 1) * N_KV_HEADS, :] = mean


def moba_cache_block_means(k_rows, page_table):
    db, n_pages = page_table.shape
    ppb = MOBA_BLOCK // PAGE_SIZE
    nblk = n_pages // ppb
    bps = MOBA_BLOCKS_PER_STEP
    assert nblk % bps == 0

    def page_spec(u):
        return pl.BlockSpec((PAGE_ROWS, HEAD_DIM), lambda b, j, pt: (pt[b * n_pages + j * (bps * ppb) + u], 0))

    return pl.pallas_call(
        functools.partial(_moba_kmean_kernel, bps=bps, ppb=ppb),
        grid_spec=pltpu.PrefetchScalarGridSpec(
            num_scalar_prefetch=1,
            grid=(db, nblk // bps),
            in_specs=[page_spec(u) for u in range(bps * ppb)],
            out_specs=pl.BlockSpec((1, bps * N_KV_HEADS, HEAD_DIM), lambda b, j, pt: (b, j, 0))),
        out_shape=jax.ShapeDtypeStruct((db, nblk * N_KV_HEADS, HEAD_DIM), F32),
        compiler_params=_params("parallel", "arbitrary"),
        name="moba_kmean",
    )(page_table.reshape(-1), *([k_rows] * (bps * ppb)))


def _moba_gate_kernel(q_ref, km_ref, o_ref, *, nblk):
    gate = _dot_nt(q_ref[0].astype(BF16), km_ref[0].astype(BF16))
    kvh = lax.broadcasted_iota(jnp.int32, gate.shape, 0) >> 2
    gate = jnp.where(_own_lane_mask(kvh, gate.shape), gate, -jnp.inf)
    _, picks = _top_k_mask(gate, nblk, MOBA_TOPK)
    lane = lax.broadcasted_iota(jnp.int32, (N_HEADS, LANES), 1)
    out = jnp.zeros((N_HEADS, LANES), jnp.int32)
    for r, idx in enumerate(picks):
        out = jnp.where(lane == r, idx >> 2, out)
    o_ref[0] = out


def moba_decode_select(q, kmean):
    db, rows, _ = kmean.shape
    nblk = rows // N_KV_HEADS
    out = pl.pallas_call(
        functools.partial(_moba_gate_kernel, nblk=nblk),
        grid=(db,),
        in_specs=[pl.BlockSpec((1, N_HEADS, HEAD_DIM), lambda b: (b, 0, 0)),
                  pl.BlockSpec((1, rows, HEAD_DIM), lambda b: (b, 0, 0))],
        out_specs=pl.BlockSpec((1, N_HEADS, LANES), lambda b: (b, 0, 0)),
        out_shape=jax.ShapeDtypeStruct((db, N_HEADS, LANES), jnp.int32),
        compiler_params=_params("parallel"),
        name="moba_gate",
    )(q, kmean)
    return out[:, :, :MOBA_TOPK]


def _moba_decode_kernel(sel_ref, pt_ref, q_ref, kn_ref, vn_ref, *refs, n_sel):
    k_refs, v_refs, o_ref = refs[:n_sel], refs[n_sel:2 * n_sel], refs[2 * n_sel]
    scale = HEAD_DIM ** -0.5
    q = jnp.broadcast_to(q_ref[0, 0], (8, HEAD_DIM))
    own = _own_lane_mask(pl.program_id(1) >> 2, (8, PAGE_ROWS))
    m = jnp.sum(q * kn_ref[0, 0], axis=1, keepdims=True) * scale
    l = jnp.ones((8, 1), F32)
    acc = jnp.broadcast_to(vn_ref[0, 0], (8, HEAD_DIM))
    qb = q.astype(BF16)
    for u in range(n_sel):
        s = jnp.where(own, _dot_nt(qb, k_refs[u][...].astype(BF16)) * scale, NEG_BIG)
        m_new = jnp.maximum(m, jnp.max(s, axis=1, keepdims=True))
        alpha = jnp.exp(m - m_new)
        p = jnp.exp(s - m_new)
        l = alpha * l + jnp.sum(p, axis=1, keepdims=True)
        acc = alpha * acc + _dot(p.astype(BF16), v_refs[u][...].astype(BF16))
        m = m_new
    o_ref[0, 0] = acc / l


def moba_decode_attention(q, k_new, v_new, sel, k_rows, v_rows, page_table):
    db, n_pages = page_table.shape
    ppb = MOBA_BLOCK // PAGE_SIZE
    n_sel = MOBA_TOPK * ppb

    def page_spec(u):
        def index(b, h, sel_r, pt):
            blk = sel_r[(b * N_HEADS + h) * MOBA_TOPK + u // ppb]
            return (pt[b * n_pages + blk * ppb + u % ppb], 0)
        return pl.BlockSpec((PAGE_ROWS, HEAD_DIM), index)

    specs = [page_spec(u) for u in range(n_sel)]
    new_spec = pl.BlockSpec((1, 1, 1, HEAD_DIM), lambda b, h, sel_r, pt: (b, h // Q_PER_KV, 0, 0))
    out = pl.pallas_call(
        functools.partial(_moba_decode_kernel, n_sel=n_sel),
        grid_spec=pltpu.PrefetchScalarGridSpec(
            num_scalar_prefetch=2,
            grid=(db, N_HEADS),
            in_specs=[pl.BlockSpec((1, 1, 1, HEAD_DIM), lambda b, h, sel_r, pt: (b, h, 0, 0)),
                      new_spec, new_spec] + specs + specs,
            out_specs=pl.BlockSpec((1, 1, 8, HEAD_DIM), lambda b, h, sel_r, pt: (b, h, 0, 0))),
        out_shape=jax.ShapeDtypeStruct((db, N_HEADS, 8, HEAD_DIM), F32),
        compiler_params=_params("parallel", "parallel"),
        name="moba_decode",
    )(sel.reshape(-1), page_table.reshape(-1), q, k_new, v_new, *([k_rows] * n_sel), *([v_rows] * n_sel))
    return out[:, :, 0, :]


def _dwconv_kernel(x_ref, hist_ref, w_ref, b_ref, g_ref, beta_ref, o_ref, pad_ref, *shift_refs, width, tl, hp, post):
    t = pl.program_id(2)

    @pl.when(t == 0)
    def _():
        pad_ref[0:hp, :] = hist_ref[0]

    @pl.when(t > 0)
    def _():
        pad_ref[0:hp, :] = pad_ref[tl:tl + hp, :]

    pad_ref[hp:hp + tl, :] = x_ref[0]

    def finish(acc):
        if post == "ln_silu":
            acc = _layer_norm_rows(acc, g_ref[...], beta_ref[...])
        return (acc * jax.nn.sigmoid(acc)).astype(o_ref.dtype)

    if not shift_refs:
        base = hp - (width - 1)
        acc = jnp.zeros(o_ref.shape[1:], F32) + b_ref[...]
        for k in range(width):
            acc = acc + pad_ref[base + k:base + k + tl, :] * w_ref[k:k + 1, :]
        o_ref[0] = finish(acc)
        return

    sh_ref, wb_ref = shift_refs
    halo = 8 * ((width - 1) // 8)
    for r in range(1, 8):
        sh_ref[r - 1] = pad_ref[hp - halo - r:hp + tl - r, :]
    rt = min(DWCONV_ROW_TILE, tl)
    tc = o_ref.shape[2]
    lc = DWCONV_LANE_CHUNK if tc % DWCONV_LANE_CHUNK == 0 else tc
    for k in range(width):
        wb_ref[k] = jnp.broadcast_to(w_ref[k:k + 1, :], (8, tc))

    def row_tile(row0):
        parts = []
        for c0 in range(0, tc, lc):
            acc = jnp.zeros((rt // 8, 8, lc), F32) + b_ref[:, c0:c0 + lc]
            for r in range(8):
                for a in range((width - 1 - r) // 8 + 1):
                    k = width - 1 - (8 * a + r)
                    if r == 0:
                        src = pad_ref[pl.ds(row0 + (hp - 8 * a), rt), c0:c0 + lc]
                    else:
                        src = sh_ref[r - 1, pl.ds(row0 + (halo - 8 * a), rt), c0:c0 + lc]
                    acc = acc + src.reshape(rt // 8, 8, lc) * wb_ref[k, :, c0:c0 + lc]
            parts.append(acc.reshape(rt, lc))
        o_ref[0, pl.ds(row0, rt), :] = finish(jnp.concatenate(parts, axis=1))

    if tl == rt:
        row_tile(0)
    else:
        def body(it, carry):
            row_tile(pl.multiple_of(it * rt, rt))
            return carry

        lax.fori_loop(0, tl // rt, body, 0)


def causal_dwconv(x, hist, w, b, post, ln_g=None, ln_b=None, out_dtype=F32):
    bsz, L, C = x.shape
    width = w.shape[0]
    hp = hist.shape[1]
    assert hp % 8 == 0 and hp >= 8 * ((width - 1) // 8) + min(7, width - 1)
    tc = C if post == "ln_silu" else _pick_tile(C, (1024, 512, 256, 128))
    tl = _pick_tile(L, (256, 128)) if L >= hp else L
    assert tl >= hp or tl == L
    wp = jnp.zeros((-(-width // 8) * 8, C), F32).at[:width].set(w.astype(F32))
    if ln_g is None:
        ln_g = jnp.ones((C,), F32)
        ln_b = jnp.zeros((C,), F32)
    row = lambda v: v.reshape(1, C).astype(F32)
    vec_spec = pl.BlockSpec((1, tc), lambda bb, c, t: (0, c))
    return pl.pallas_call(
        functools.partial(_dwconv_kernel, width=width, tl=tl, hp=hp, post=post),
        grid=(bsz, C // tc, L // tl),
        in_specs=[pl.BlockSpec((1, tl, tc), lambda bb, c, t: (bb, t, c)),
                  pl.BlockSpec((1, hp, tc), lambda bb, c, t: (bb, 0, c)),
                  pl.BlockSpec((wp.shape[0], tc), lambda bb, c, t: (0, c)),
                  vec_spec, vec_spec, vec_spec],
        out_specs=pl.BlockSpec((1, tl, tc), lambda bb, c, t: (bb, t, c)),
        out_shape=jax.ShapeDtypeStruct((bsz, L, C), out_dtype),
        scratch_shapes=[pltpu.VMEM((hp + tl, tc), F32)] + (
            [pltpu.VMEM((7, 8 * ((width - 1) // 8) + tl, tc), F32),
             pltpu.VMEM((width, 8, tc), F32)] if width > 8 else []),
        compiler_params=_params("parallel", "parallel", "arbitrary"),
        name="dwconv_" + post,
    )(x, hist, wp, row(b), row(ln_g), row(ln_b))


def _ssd_prompt_kernel(x_ref, bm_ref, cm_ref, dt_ref, z_ref, dtb_ref, alog_ref, d_ref, ng_ref, h0_ref,
                       yn_ref, h_ref, y_buf, *, chunk, hpg):
    c = pl.program_id(1)
    P, N = SSM_HEAD_DIM, SSM_D_STATE
    gw = hpg * P

    @pl.when(c == 0)
    def _():
        h_ref[...] = h0_ref[...]

    li = lax.broadcasted_iota(jnp.int32, (chunk, chunk), 0)
    si = lax.broadcasted_iota(jnp.int32, (chunk, chunk), 1)
    causal = li >= si
    lower = causal.astype(BF16)
    dt = _softplus(dt_ref[0] + dtb_ref[...])
    a = -jnp.exp(alog_ref[...])
    hi, mid, lo = _split3(dt * a)
    cum = _dot(lower, hi) + _dot(lower, mid) + _dot(lower, lo)
    cum_t = cum.T
    dt_t = dt.T
    for g in range(SSM_GROUPS):
        x = x_ref[0, :, g * gw:(g + 1) * gw]
        cmb = cm_ref[0, :, g * N:(g + 1) * N].astype(BF16)
        bmb = bm_ref[0, :, g * N:(g + 1) * N].astype(BF16)
        cb = _dot_nt(cmb, bmb)
        x_t = x.T
        for r in range(hpg):
            head = g * hpg + r
            cum_col = cum[:, head:head + 1]
            cum_row = cum_t[head:head + 1, :]
            dt_row = dt_t[head:head + 1, :]
            total = cum_t[head:head + 1, chunk - 1:chunk]
            xh = x[:, r * P:(r + 1) * P]
            decay = jnp.exp(jnp.where(causal, cum_col - cum_row, -jnp.inf))
            mix = (cb * decay * dt_row).astype(BF16)
            y = _dot(mix, xh.astype(BF16))
            h_old = h_ref[0, head]
            y = y + _dot_nt(cmb, h_old.astype(BF16)) * jnp.exp(cum_col)
            y_buf[:, r * P:(r + 1) * P] = y + d_ref[:, head:head + 1] * xh
            to_end = jnp.exp(total - cum_row) * dt_row
            s_chunk = _dot((x_t[r * P:(r + 1) * P, :] * to_end).astype(BF16), bmb)
            h_ref[0, head] = jnp.exp(total) * h_old + s_chunk
        z = z_ref[0, :, g * gw:(g + 1) * gw]
        hg = y_buf[...] * (z * jax.nn.sigmoid(z))
        hg = hg * lax.rsqrt(jnp.mean(hg * hg, axis=-1, keepdims=True) + LN_EPS)
        yn_ref[0, :, g * gw:(g + 1) * gw] = (hg * ng_ref[:, g * gw:(g + 1) * gw]).astype(yn_ref.dtype)


def ssd_prompt(xbc, dt_raw, z, dt_bias_g, a_log_g, d_g, norm_g, h0):
    bsz, L, _ = xbc.shape
    n_heads = h0.shape[1]
    G, N, P = SSM_GROUPS, SSM_D_STATE, SSM_HEAD_DIM
    hpg = n_heads // G
    d_inner = n_heads * P
    gn = G * N
    chunk = SSM_CHUNK
    assert d_inner % gn == 0 and (hpg * P) % LANES == 0 and n_heads <= LANES
    b_off = d_inner // gn
    vec_spec = pl.BlockSpec((1, LANES), lambda b, c: (0, 0))
    state_spec = pl.BlockSpec((1, n_heads, P, N), lambda b, c: (b, 0, 0, 0))
    return pl.pallas_call(
        functools.partial(_ssd_prompt_kernel, chunk=chunk, hpg=hpg),
        grid=(bsz, L // chunk),
        in_specs=[pl.BlockSpec((1, chunk, d_inner), lambda b, c: (b, c, 0)),
                  pl.BlockSpec((1, chunk, gn), lambda b, c: (b, c, b_off)),
                  pl.BlockSpec((1, chunk, gn), lambda b, c: (b, c, b_off + 1)),
                  pl.BlockSpec((1, chunk, LANES), lambda b, c: (b, c, 0)),
                  pl.BlockSpec((1, chunk, d_inner), lambda b, c: (b, c, 0)),
                  vec_spec, vec_spec, vec_spec,
                  pl.BlockSpec((1, d_inner), lambda b, c: (0, 0)),
                  state_spec],
        out_specs=[pl.BlockSpec((1, chunk, d_inner), lambda b, c: (b, c, 0)), state_spec],
        out_shape=[jax.ShapeDtypeStruct((bsz, L, d_inner), BF16),
                   jax.ShapeDtypeStruct((bsz, n_heads, P, N), F32)],
        scratch_shapes=[pltpu.VMEM((chunk, hpg * P), F32)],
        compiler_params=_params("parallel", "arbitrary"),
        name="ssd_prompt",
    )(xbc, xbc, xbc, dt_raw, z, dt_bias_g, a_log_g, d_g, norm_g.reshape(1, d_inner).astype(F32), h0)


def _ssd_step_kernel(xc_ref, bc_ref, dt_ref, dtb_ref, alog_ref, d_ref, h0_ref, y_ref, h_ref, *, n_heads):
    N = SSM_D_STATE
    hpg = n_heads // SSM_GROUPS
    dt = _softplus(dt_ref[0] + dtb_ref[...])
    da = jnp.exp(dt * (-jnp.exp(alog_ref[...])))
    for h in range(n_heads):
        g = h // hpg
        bm = bc_ref[0, :, g * N:(g + 1) * N]
        cm = bc_ref[0, :, (SSM_GROUPS + g) * N:(SSM_GROUPS + g + 1) * N]
        xcol = xc_ref[0, h]
        h_new = da[:, h:h + 1] * h0_ref[0, h] + xcol * (dt[:, h:h + 1] * bm)
        h_ref[0, h] = h_new
        y_ref[0, h] = jnp.sum(h_new * cm, axis=1, keepdims=True) + d_ref[:, h:h + 1] * xcol


def ssd_step(x_col, bc, dt_raw, dt_bias, a_log, d_skip, h0):
    db, n_heads, P, _ = x_col.shape
    N = SSM_D_STATE
    vec_spec = pl.BlockSpec((1, LANES), lambda b: (0, 0))
    state_spec = pl.BlockSpec((1, n_heads, P, N), lambda b: (b, 0, 0, 0))
    col_spec = pl.BlockSpec((1, n_heads, P, 1), lambda b: (b, 0, 0, 0))
    return pl.pallas_call(
        functools.partial(_ssd_step_kernel, n_heads=n_heads),
        grid=(db,),
        in_specs=[col_spec,
                  pl.BlockSpec((1, 1, bc.shape[2]), lambda b: (b, 0, 0)),
                  pl.BlockSpec((1, 1, LANES), lambda b: (b, 0, 0)),
                  vec_spec, vec_spec, vec_spec, state_spec],
        out_specs=[col_spec, state_spec],
        out_shape=[jax.ShapeDtypeStruct((db, n_heads, P, 1), F32),
                   jax.ShapeDtypeStruct((db, n_heads, P, N), F32)],
        compiler_params=_params("parallel"),
        name="ssd_step",
    )(x_col, bc, dt_raw, dt_bias, a_log, d_skip, h0)


def _gated_norm_kernel(y_ref, z_ref, g_ref, o_ref):
    z = z_ref[...]
    h = y_ref[...] * (z * jax.nn.sigmoid(z))
    h = h * lax.rsqrt(jnp.mean(h * h, axis=-1, keepdims=True) + LN_EPS)
    o_ref[...] = (h * g_ref[...]).astype(o_ref.dtype)


def gated_rms_norm(y, z, g):
    m, d = y.shape
    gw = d // SSM_GROUPS
    tm = _pick_tile(m, (512, 256, 128, 8))
    spec = pl.BlockSpec((tm, gw), lambda i, j: (i, j))
    return pl.pallas_call(
        _gated_norm_kernel,
        grid=(m // tm, SSM_GROUPS),
        in_specs=[spec, spec, pl.BlockSpec((1, gw), lambda i, j: (0, j))],
        out_specs=spec,
        out_shape=jax.ShapeDtypeStruct((m, d), BF16),
        compiler_params=_params("parallel", "parallel"),
        name="gated_rms_norm",
    )(y, z, g.reshape(1, d).astype(F32))


def _lane_pad(v):
    return jnp.zeros((1, LANES), F32).at[0, :v.shape[0]].set(v.astype(F32))


def _ffn_half(xp_f, xp_b, xs_f, xs_b, w1, w3, lead, w2, g, b):
    hp, hs = gated_matmul(xp_b, xs_b, w1, w3, None, None, "swiglu", BF16, lead=lead)
    return matmul_postnorm(hp, hs, w2, None, xp_f, xs_f, g, b, 0.5, lead=lead)


def _pad_rows(x, rows):
    return jnp.concatenate([x, jnp.zeros((x.shape[0], rows - x.shape[1]) + x.shape[2:], x.dtype)], axis=1)


def _front_pad_rows(x, rows):
    return jnp.concatenate([jnp.zeros((x.shape[0], rows - x.shape[1]) + x.shape[2:], x.dtype), x], axis=1)


def kernel(x_prompt, x_sample, p_prompt, p_sample, cache_sb_k, cache_sb_v, cache_moba_k, cache_moba_v, state_ssm, state_ssm_conv, state_conf_conv, page_table, ln_g, ln_b, ffn_w1, ffn_w3, ffn_w2, ple_w_proj, ple_w_gate, sb_w_qkv, sb_w_o, ssm_w_in, ssm_conv_w, ssm_conv_b, ssm_dt_bias, ssm_a_log, ssm_d, ssm_norm_g, ssm_w_out, conf_w_pw1, conf_b_pw1, conf_w_dw, conf_b_dw, conf_ln_g, conf_ln_b, conf_w_pw2, conf_b_pw2, moba_w_qkv, moba_w_o):
    bsz, seq, d_model = x_prompt.shape
    db = x_sample.shape[0]
    assert x_sample.shape[1] == 1
    mp = bsz * seq
    depth = ffn_w1.shape[0]
    bf = lambda w: w.astype(BF16)

    xp_f = x_prompt.reshape(mp, d_model)
    xs_f = x_sample.reshape(db, d_model)
    xp_b, xs_b = bf(xp_f), bf(xs_f)
    n_phys = cache_sb_k.shape[1]
    ffn_w2_b = bf(ffn_w2)
    outs ={k: [] for k in ("sb_kp", "sb_vp", "sb_ks", "sb_vs", "ssm_hp", "ssm_hs", "ssm_cp", "ssm_cs",
                            "conf_cp", "conf_cs", "mo_kp", "mo_vp", "mo_ks", "mo_vs")}

    for i in range(depth):
        m, j = i % 4, i // 4
        (xp_f, xp_b), (xs_f, xs_b) = _ffn_half(xp_f, xp_b, xs_f, xs_b, ffn_w1, ffn_w3, (i, 0), ffn_w2_b,
                                               ln_g[i, 0], ln_b[i, 0])
        g1, b1 = ln_g[i, 1], ln_b[i, 1]

        if m == 0 or m == 3:
            w_qkv = bf(sb_w_qkv[j] if m == 0 else moba_w_qkv[j])
            w_o = bf(sb_w_o[j] if m == 0 else moba_w_o[j])
            q, qs = matmul(xp_b, xs_b, w_qkv, None, BF16, 0, Q_DIM)
            k, ks = matmul(xp_b, xs_b, w_qkv, None, F32, Q_DIM, KV_DIM)
            v, vs = matmul(xp_b, xs_b, w_qkv, None, F32, Q_DIM + KV_DIM, KV_DIM)
            q = q.reshape(bsz, seq, Q_DIM)
            k = k.reshape(bsz, seq, KV_DIM)
            v = v.reshape(bsz, seq, KV_DIM)
            attend = sb_prompt_attention if m == 0 else moba_prompt_attention
            o = attend(q, k, v).reshape(mp, Q_DIM)
            kp = k.reshape(bsz, seq, N_KV_HEADS, HEAD_DIM)
            vp = v.reshape(bsz, seq, N_KV_HEADS, HEAD_DIM)
            if m == 0:
                kc = cache_sb_k[j].reshape(n_phys * PAGE_ROWS, HEAD_DIM)
                vc = cache_sb_v[j].reshape(n_phys * PAGE_ROWS, HEAD_DIM)
                os_ = sb_decode_attention(qs.reshape(db, N_HEADS, HEAD_DIM), kc, vc, page_table)
            else:
                kc = cache_moba_k[j].reshape(n_phys * PAGE_ROWS, HEAD_DIM)
                vc = cache_moba_v[j].reshape(n_phys * PAGE_ROWS, HEAD_DIM)
                qh = qs.reshape(db, N_HEADS, HEAD_DIM)
                kmean = moba_cache_block_means(kc, page_table)
                sel = moba_decode_select(qh, kmean)
                os_ = moba_decode_attention(qh.reshape(db, N_HEADS, 1, HEAD_DIM),
                                            ks.reshape(db, N_KV_HEADS, 1, HEAD_DIM),
                                            vs.reshape(db, N_KV_HEADS, 1, HEAD_DIM),
                                            sel, kc, vc, page_table)
            (xp_f, xp_b), (xs_f, xs_b) = matmul_postnorm(o, bf(os_.reshape(db, Q_DIM)), w_o, None, xp_f, xs_f,
                                                         g1, b1, 1.0)
            ks4 = ks.reshape(db, 1, N_KV_HEADS, HEAD_DIM)
            vs4 = vs.reshape(db, 1, N_KV_HEADS, HEAD_DIM)
            if m == 0:
                outs["sb_kp"].append(kp); outs["sb_vp"].append(vp); outs["sb_ks"].append(ks4); outs["sb_vs"].append(vs4)
            else:
                outs["mo_kp"].append(kp); outs["mo_vp"].append(vp); outs["mo_ks"].append(ks4); outs["mo_vs"].append(vs4)

        elif m == 1:
            n_heads = ssm_dt_bias.shape[1]
            d_inner = n_heads * SSM_HEAD_DIM
            conv_dim = ssm_conv_w.shape[2]
            hpg = n_heads // SSM_GROUPS
            w_in = bf(ssm_w_in[j])
            w_dt_cols = ssm_w_in[j][:, d_inner + conv_dim:]
            w_dt = bf(jnp.zeros((d_model, LANES), F32).at[:, :n_heads].set(w_dt_cols))
            dtb, alog, dskip = _lane_pad(ssm_dt_bias[j]), _lane_pad(ssm_a_log[j]), _lane_pad(ssm_d[j])
            w_out = bf(ssm_w_out[j])
            hp = 8
            z, zs = matmul(xp_b, xs_b, w_in, None, F32, 0, d_inner)
            xbc_raw, xbc_s = matmul(xp_b, xs_b, w_in, None, F32, d_inner, conv_dim)
            dt_raw, dt_s = matmul(xp_b, xs_b, w_dt, None, F32)
            xbc_raw = xbc_raw.reshape(bsz, seq, conv_dim)
            dt_raw = dt_raw.reshape(bsz, seq, LANES)
            xbc = causal_dwconv(xbc_raw, jnp.zeros((bsz, hp, conv_dim), F32), ssm_conv_w[j], ssm_conv_b[j], "silu")
            h0 = jnp.zeros((bsz, n_heads, SSM_HEAD_DIM, SSM_D_STATE), F32)
            yn, h_last = ssd_prompt(xbc, dt_raw, z.reshape(bsz, seq, d_inner), dtb, alog, dskip, ssm_norm_g[j], h0)
            outs["ssm_hp"].append(h_last)
            outs["ssm_cp"].append(xbc_raw[:, seq - (SSM_CONV - 1):, :])
            xbc_s = xbc_s.reshape(db, 1, conv_dim)
            dt_s = dt_s.reshape(db, 1, LANES)
            hist = _front_pad_rows(state_ssm_conv[j], hp)
            xbc_sa = causal_dwconv(_pad_rows(xbc_s, 8), hist, ssm_conv_w[j], ssm_conv_b[j], "silu")[:, :1, :]
            x_col = xbc_sa[:, 0, :d_inner].reshape(db, n_heads, SSM_HEAD_DIM, 1)
            y_col, hs = ssd_step(x_col, xbc_sa[:, :, d_inner:], dt_s, dtb, alog, dskip, state_ssm[j])
            yns = gated_rms_norm(y_col.reshape(db, d_inner), zs, ssm_norm_g[j])
            (xp_f, xp_b), (xs_f, xs_b) = matmul_postnorm(yn.reshape(mp, d_inner), yns, w_out, None, xp_f, xs_f,
                                                         g1, b1, 1.0)
            outs["ssm_hs"].append(hs)
            outs["ssm_cs"].append(jnp.concatenate([state_ssm_conv[j], xbc_s], axis=1)[:, 1:, :])

        else:
            ba, bg = conf_b_pw1[j][:d_model], conf_b_pw1[j][d_model:]
            w_pw2 = bf(conf_w_pw2[j])
            hp = 32
            u, us = gated_matmul(xp_b, xs_b, conf_w_pw1, conf_w_pw1, ba, bg, "glu", F32, lead=(j,),
                                 cola=0, colb=d_model, n=d_model)
            u = u.reshape(bsz, seq, d_model)
            us = us.reshape(db, 1, d_model)
            uc = causal_dwconv(u, jnp.zeros((bsz, hp, d_model), F32), conf_w_dw[j], conf_b_dw[j], "ln_silu",
                               conf_ln_g[j], conf_ln_b[j], BF16)
            outs["conf_cp"].append(u[:, seq - (CONF_WIDTH - 1):, :])
            hist = _front_pad_rows(state_conf_conv[j], hp)
            ucs = causal_dwconv(_pad_rows(us, 8), hist, conf_w_dw[j], conf_b_dw[j], "ln_silu",
                                conf_ln_g[j], conf_ln_b[j], BF16)[:, 0, :]
            (xp_f, xp_b), (xs_f, xs_b) = matmul_postnorm(uc.reshape(mp, d_model), ucs, w_pw2, conf_b_pw2[j],
                                                         xp_f, xs_f, g1, b1, 1.0)
            outs["conf_cs"].append(jnp.concatenate([state_conf_conv[j], us], axis=1)[:, 1:, :])

        (xp_f, xp_b), (xs_f, xs_b) = _ffn_half(xp_f, xp_b, xs_f, xs_b, ffn_w1, ffn_w3, (i, 1), ffn_w2_b,
                                               ln_g[i, 2], ln_b[i, 2])
        wg_, wp_ = bf(ple_w_gate[i]), bf(ple_w_proj[i])
        (xp_f, xp_b), (xs_f, xs_b) = ple_add(xp_b, xp_f, p_prompt.reshape(depth, mp, -1), xs_b, xs_f,
                                             p_sample.reshape(depth, db, -1), i, wg_, wp_, ln_g[i, 3], ln_b[i, 3])

    st = lambda name: jnp.stack(outs[name])
    return (xp_f.reshape(bsz, seq, d_model), xs_f.reshape(db, 1, d_model),
            st("sb_kp"), st("sb_vp"), st("sb_ks"), st("sb_vs"),
            st("ssm_hp"), st("ssm_hs"), st("ssm_cp"), st("ssm_cs"),
            st("conf_cp"), st("conf_cs"),
            st("mo_kp"), st("mo_vp"), st("mo_ks"), st("mo_vs"))
```

```python
import functools

import jax
import jax.numpy as jnp
from jax import lax
from jax.experimental import pallas as pl
from jax.experimental.pallas import tpu as pltpu

F32 = jnp.float32
BF16 = jnp.bfloat16

N_HEADS = 16
HEAD_DIM = 128
N_KV_HEADS = 4
Q_PER_KV = N_HEADS // N_KV_HEADS
Q_DIM = N_HEADS * HEAD_DIM
KV_DIM = N_KV_HEADS * HEAD_DIM
SB_BLOCK = 128
SB_KV_PER_STEP = 2
MOBA_BLOCK = 256
MOBA_TOPK = 3
MOBA_HEADS_PER_STEP = 2
PAGE_SIZE = 128
SSM_HEAD_DIM = 64
SSM_GROUPS = 8
SSM_D_STATE = 128
SSM_CONV = 4
SSM_CHUNK = 128
CONF_WIDTH = 31
LN_EPS = 1e-5
DEPTH = 4
DN_ALPHA = (2 * DEPTH) ** 0.25
LANES = 128
NEG_BIG = -1e30
DWCONV_ROW_TILE = 64
DWCONV_LANE_CHUNK = 256
SB_LOG_FLOOR = -104.0
VMEM_LIMIT = 56 * 1024 * 1024


def _params(*sem):
    return pltpu.CompilerParams(dimension_semantics=sem, vmem_limit_bytes=VMEM_LIMIT)


def _layer_norm_rows(y, g, b):
    mu = jnp.mean(y, axis=-1, keepdims=True)
    d = y - mu
    var = jnp.mean(d * d, axis=-1, keepdims=True)
    return d * lax.rsqrt(var + LN_EPS) * g + b


def _split3(x):
    hi = x.astype(BF16)
    r1 = x - hi.astype(F32)
    mid = r1.astype(BF16)
    lo = (r1 - mid.astype(F32)).astype(BF16)
    return hi, mid, lo


def _dot(a, b):
    return jnp.dot(a, b, preferred_element_type=F32)


def _dot_nt(a, b):
    return lax.dot_general(a, b, (((1,), (1,)), ((), ())), preferred_element_type=F32)


def _dot01(x, m01):
    hi, mid, lo = _split3(x)
    return _dot(hi, m01) + _dot(mid, m01) + _dot(lo, m01)


def _pick_tile(n, pref):
    for t in pref:
        if n % t == 0:
            return t
    return n


def _mm_kernel(x_ref, xs_ref, w_ref, b_ref, o_ref, os_ref):
    o_ref[...] = (_dot(x_ref[...], w_ref[...]) + b_ref[...]).astype(o_ref.dtype)
    first = pl.program_id(0) == 0

    @pl.when(first)
    def _():
        os_ref[0] = (_dot(xs_ref[...], w_ref[...]) + b_ref[...]).astype(os_ref.dtype)

    @pl.when(jnp.logical_not(first))
    def _():
        os_ref[...] = jnp.zeros_like(os_ref)


def matmul(x, xs, w, bias=None, out_dtype=F32, col0=0, n=None):
    m, k = x.shape
    s = xs.shape[0]
    n = w.shape[1] - col0 if n is None else n
    tm = _pick_tile(m, (1024, 512, 256, 128, 8))
    tn = _pick_tile(n, (512, 256, 128))
    assert col0 % tn == 0 and n % tn == 0
    j0 = col0 // tn
    if bias is None:
        bias = jnp.zeros((n,), F32)
    out, out_s = pl.pallas_call(
        _mm_kernel,
        grid=(m // tm, n // tn),
        in_specs=[pl.BlockSpec((tm, k), lambda i, j: (i, 0)),
                  pl.BlockSpec((s, k), lambda i, j: (0, 0)),
                  pl.BlockSpec((k, tn), lambda i, j: (0, j + j0)),
                  pl.BlockSpec((1, tn), lambda i, j: (0, j))],
        out_specs=[pl.BlockSpec((tm, tn), lambda i, j: (i, j)),
                   pl.BlockSpec((1, s, tn), lambda i, j: (i, 0, j))],
        out_shape=[jax.ShapeDtypeStruct((m, n), out_dtype),
                   jax.ShapeDtypeStruct((m // tm, s, n), F32)],
        compiler_params=_params("arbitrary", "arbitrary"),
        name="mm",
    )(x, xs, w, bias.reshape(1, n).astype(F32))
    return out, out_s[0]


def _gated_mm_kernel(x_ref, xs_ref, wa_ref, wb_ref, ba_ref, bb_ref, o_ref, os_ref, wa_s, wb_s, *, mode):
    first = pl.program_id(1) == 0

    def gated(x):
        a = _dot(x, wa_s[...]) + ba_ref[...]
        b = _dot(x, wb_s[...]) + bb_ref[...]
        return a * jax.nn.sigmoid(a) * b if mode == "swiglu" else a * jax.nn.sigmoid(b)

    @pl.when(first)
    def _():
        wa_s[...] = wa_ref[...].astype(BF16)
        wb_s[...] = wb_ref[...].astype(BF16)
        os_ref[0] = gated(xs_ref[...]).astype(os_ref.dtype)

    @pl.when(jnp.logical_not(first))
    def _():
        os_ref[...] = jnp.zeros_like(os_ref)

    o_ref[...] = gated(x_ref[...]).astype(o_ref.dtype)


def gated_matmul(x, xs, wa, wb, ba, bb, mode, out_dtype, lead=(), cola=0, colb=0, n=None):
    m, k = x.shape
    s = xs.shape[0]
    n = wa.shape[-1] - cola if n is None else n
    wblock = (None,) * len(lead) + (k, None)
    tm = _pick_tile(m, (2048, 1024, 512, 256, 128, 8) if jnp.dtype(out_dtype).itemsize == 2
                    else (1024, 512, 256, 128, 8))
    tn = _pick_tile(n, (512, 256, 128))
    assert cola % tn == 0 and colb % tn == 0 and n % tn == 0
    ja, jb = cola // tn, colb // tn
    if ba is None:
        ba = jnp.zeros((n,), F32)
        bb = jnp.zeros((n,), F32)
    out, out_s = pl.pallas_call(
        functools.partial(_gated_mm_kernel, mode=mode),
        grid=(n // tn, m // tm),
        in_specs=[pl.BlockSpec((tm, k), lambda j, i: (i, 0)),
                  pl.BlockSpec((s, k), lambda j, i: (0, 0)),
                  pl.BlockSpec(wblock[:-1] + (tn,), lambda j, i: lead + (0, j + ja)),
                  pl.BlockSpec(wblock[:-1] + (tn,), lambda j, i: lead + (0, j + jb)),
                  pl.BlockSpec((1, tn), lambda j, i: (0, j)),
                  pl.BlockSpec((1, tn), lambda j, i: (0, j))],
        out_specs=[pl.BlockSpec((tm, tn), lambda j, i: (i, j)),
                   pl.BlockSpec((1, s, tn), lambda j, i: (i, 0, j))],
        out_shape=[jax.ShapeDtypeStruct((m, n), out_dtype),
                   jax.ShapeDtypeStruct((m // tm, s, n), out_dtype)],
        scratch_shapes=[pltpu.VMEM((k, tn), BF16), pltpu.VMEM((k, tn), BF16)],
        compiler_params=_params("arbitrary", "arbitrary"),
        name="gated_mm",
    )(x, xs, wa, wb, ba.reshape(1, n).astype(F32), bb.reshape(1, n).astype(F32))
    return out, out_s[0]


def _mm_ln_kernel(a_ref, as_ref, w_ref, bias_ref, res_ref, ress_ref, g_ref, b_ref, of_ref, ob_ref, ofs_ref, obs_ref,
                  accs_ref, *, scale, nk):
    i, k = pl.program_id(0), pl.program_id(1)

    def post_norm(res, acc):
        y = DN_ALPHA * res + scale * (acc + bias_ref[...])
        return _layer_norm_rows(y, g_ref[...], b_ref[...])

    def finish(acc):
        out = post_norm(res_ref[...], acc)
        of_ref[...] = out
        ob_ref[...] = out.astype(BF16)

    @pl.when(i == 0)
    def _():
        part = _dot(as_ref[...], w_ref[...])
        if nk > 1:
            @pl.when(k > 0)
            def _():
                accs_ref[...] += part

            @pl.when(k == 0)
            def _():
                accs_ref[...] = part
        else:
            accs_ref[...] = part

    @pl.when(k == nk - 1)
    def _():
        out = post_norm(ress_ref[...], accs_ref[...])
        first = i == 0
        ofs_ref[0] = jnp.where(first, out, 0.0)
        obs_ref[0] = jnp.where(first, out, 0.0).astype(BF16)

    if nk == 1:
        finish(_dot(a_ref[...], w_ref[...]))
        return

    @pl.when(k == 0)
    def _():
        of_ref[...] = _dot(a_ref[...], w_ref[...])

    @pl.when(jnp.logical_and(k > 0, k < nk - 1))
    def _():
        of_ref[...] += _dot(a_ref[...], w_ref[...])

    @pl.when(k == nk - 1)
    def _():
        finish(of_ref[...] + _dot(a_ref[...], w_ref[...]))


def matmul_postnorm(a, a_s, w, bias, res, res_s, g, b, scale, lead=()):
    m, kdim = a.shape
    s = a_s.shape[0]
    n = w.shape[-1]
    tm = _pick_tile(m, (512, 256, 128, 8))
    tk = _pick_tile(kdim, (2816, 2048, 1408, 1024, 512, 256, 128))
    nk = kdim // tk
    if bias is None:
        bias = jnp.zeros((n,), F32)
    row = lambda v: v.reshape(1, n).astype(F32)
    vec = pl.BlockSpec((1, n), lambda i, k: (0, 0))
    of, ob, ofs, obs = pl.pallas_call(
        functools.partial(_mm_ln_kernel, scale=scale, nk=nk),
        grid=(m // tm, nk),
        in_specs=[pl.BlockSpec((tm, tk), lambda i, k: (i, k)),
                  pl.BlockSpec((s, tk), lambda i, k: (0, k)),
                  pl.BlockSpec((None,) * len(lead) + (tk, n), lambda i, k: lead + (k, 0)),
                  vec,
                  pl.BlockSpec((tm, n), lambda i, k: (i, 0)),
                  pl.BlockSpec((s, n), lambda i, k: (0, 0)),
                  vec, vec],
        out_specs=[pl.BlockSpec((tm, n), lambda i, k: (i, 0)),
                   pl.BlockSpec((tm, n), lambda i, k: (i, 0)),
                   pl.BlockSpec((1, s, n), lambda i, k: (i, 0, 0)),
                   pl.BlockSpec((1, s, n), lambda i, k: (i, 0, 0))],
        out_shape=[jax.ShapeDtypeStruct((m, n), F32), jax.ShapeDtypeStruct((m, n), BF16),
                   jax.ShapeDtypeStruct((m // tm, s, n), F32), jax.ShapeDtypeStruct((m // tm, s, n), BF16)],
        scratch_shapes=[pltpu.VMEM((s, n), F32)],
        compiler_params=_params("arbitrary", "arbitrary"),
        name="mm_postnorm",
    )(a, a_s, w, row(bias), res, res_s, row(g), row(b))
    return (of, ob), (ofs[0], obs[0])


def _ple_kernel(xb_ref, xf_ref, p_ref, xsb_ref, xsf_ref, ps_ref, wg_ref, wp_ref, g_ref, b_ref,
                of_ref, ob_ref, ofs_ref, obs_ref):
    def ple(xb, xf, p):
        gate = jax.nn.sigmoid(_dot(xb, wg_ref[...]))
        proj = _dot(p.astype(BF16), wp_ref[...])
        return _layer_norm_rows(DN_ALPHA * xf + gate * proj, g_ref[...], b_ref[...])

    out = ple(xb_ref[...], xf_ref[...], p_ref[...])
    of_ref[...] = out
    ob_ref[...] = out.astype(BF16)
    first = pl.program_id(0) == 0

    @pl.when(first)
    def _():
        outs = ple(xsb_ref[...], xsf_ref[...], ps_ref[...])
        ofs_ref[0] = outs
        obs_ref[0] = outs.astype(BF16)

    @pl.when(jnp.logical_not(first))
    def _():
        ofs_ref[...] = jnp.zeros_like(ofs_ref)
        obs_ref[...] = jnp.zeros_like(obs_ref)


def ple_add(xb, xf, p, xsb, xsf, ps, layer, wg, wp, g, b):
    m, d = xf.shape
    s = xsf.shape[0]
    pd = p.shape[-1]
    tm = _pick_tile(m, (512, 256, 128, 8))
    row = lambda v: v.reshape(1, d).astype(F32)
    const = lambda shape: pl.BlockSpec(shape, lambda i: (0,) * len(shape))
    of, ob, ofs, obs = pl.pallas_call(
        _ple_kernel,
        grid=(m // tm,),
        in_specs=[pl.BlockSpec((tm, d), lambda i: (i, 0)),
                  pl.BlockSpec((tm, d), lambda i: (i, 0)),
                  pl.BlockSpec((None, tm, pd), lambda i: (layer, i, 0)),
                  const((s, d)), const((s, d)),
                  pl.BlockSpec((None, s, pd), lambda i: (layer, 0, 0)),
                  const((d, d)), const((pd, d)), const((1, d)), const((1, d))],
        out_specs=[pl.BlockSpec((tm, d), lambda i: (i, 0)),
                   pl.BlockSpec((tm, d), lambda i: (i, 0)),
                   pl.BlockSpec((1, s, d), lambda i: (i, 0, 0)),
                   pl.BlockSpec((1, s, d), lambda i: (i, 0, 0))],
        out_shape=[jax.ShapeDtypeStruct((m, d), F32), jax.ShapeDtypeStruct((m, d), BF16),
                   jax.ShapeDtypeStruct((m // tm, s, d), F32), jax.ShapeDtypeStruct((m // tm, s, d), BF16)],
        compiler_params=_params("arbitrary"),
        name="ple_add",
    )(xb, xf, p, xsb, xsf, ps, wg, wp, row(g), row(b))
    return (of, ob), (ofs[0], obs[0])


def _softplus(z):
    return jnp.maximum(z, 0.0) + jnp.log1p(jnp.exp(-jnp.abs(z)))


def _sb_prompt_kernel(q_ref, k_ref, v_ref, o_ref, *, tq, nkv):
    i = pl.program_id(2)
    rows = Q_PER_KV * tq
    scale = HEAD_DIM ** -0.5
    gw = Q_PER_KV * HEAD_DIM
    qs = [jnp.concatenate([q_ref[0, :, h * gw + g * HEAD_DIM:h * gw + (g + 1) * HEAD_DIM] for g in range(Q_PER_KV)],
                          axis=0) for h in range(nkv)]
    t_loc = lax.broadcasted_iota(jnp.int32, (rows, tq), 0) & (tq - 1)
    s_loc = lax.broadcasted_iota(jnp.int32, (rows, tq), 1)
    later = (lax.broadcasted_iota(jnp.int32, (tq, tq), 0) > lax.broadcasted_iota(jnp.int32, (tq, tq), 1)).astype(BF16)

    def block(h, j, run, acc, diagonal):
        start = pl.multiple_of(j * tq, tq)
        kb = k_ref[0, pl.ds(start, tq), h * HEAD_DIM:(h + 1) * HEAD_DIM].astype(BF16)
        vb = v_ref[0, pl.ds(start, tq), h * HEAD_DIM:(h + 1) * HEAD_DIM].astype(BF16)
        z = _dot_nt(qs[h], kb) * scale
        log_keep = -_softplus(z)
        if diagonal:
            valid = s_loc < t_loc
            log_keep = jnp.where(valid, log_keep, 0.0)
        log_w = z + log_keep + _dot01(log_keep, later) + run
        w = jnp.exp(log_w)
        if diagonal:
            w = jnp.where(valid, w, 0.0)
        acc = acc + _dot(w.astype(BF16), vb)
        run = run + jnp.sum(log_keep, axis=1, keepdims=True)
        return run, acc

    run0 = jnp.zeros((rows, 1), F32)
    acc0 = jnp.zeros((rows, HEAD_DIM), F32)
    state = []
    for h in range(nkv):
        state += list(block(h, i, run0, acc0, True))

    def cond(carry):
        live = jnp.max(carry[1])
        for h in range(1, nkv):
            live = jnp.maximum(live, jnp.max(carry[1 + 2 * h]))
        return jnp.logical_and(carry[0] < i, live > SB_LOG_FLOOR)

    def body(carry):
        it = carry[0]
        out = [it + 1]
        for h in range(nkv):
            out += list(block(h, i - 1 - it, carry[1 + 2 * h], carry[2 + 2 * h], False))
        return tuple(out)

    final = lax.while_loop(cond, body, (jnp.int32(0),) + tuple(state))
    for h in range(nkv):
        acc = final[2 + 2 * h]
        for g in range(Q_PER_KV):
            o_ref[0, :, h * gw + g * HEAD_DIM:h * gw + (g + 1) * HEAD_DIM] = acc[g * tq:(g + 1) * tq].astype(o_ref.dtype)


def sb_prompt_attention(q, k, v):
    bsz, L, _ = q.shape
    tq = SB_BLOCK
    nkv = SB_KV_PER_STEP
    gw = nkv * Q_PER_KV * HEAD_DIM
    return pl.pallas_call(
        functools.partial(_sb_prompt_kernel, tq=tq, nkv=nkv),
        grid=(bsz, N_KV_HEADS // nkv, L // tq),
        in_specs=[pl.BlockSpec((1, tq, gw), lambda b, h, i: (b, i, h)),
                  pl.BlockSpec((1, L, nkv * HEAD_DIM), lambda b, h, i: (b, 0, h)),
                  pl.BlockSpec((1, L, nkv * HEAD_DIM), lambda b, h, i: (b, 0, h))],
        out_specs=pl.BlockSpec((1, tq, gw), lambda b, h, i: (b, i, h)),
        out_shape=jax.ShapeDtypeStruct(q.shape, BF16),
        compiler_params=_params("parallel", "parallel", "arbitrary"),
        name="sb_prompt",
    )(q, k, v)


PAGE_ROWS = PAGE_SIZE * N_KV_HEADS
SB_PAGES_PER_GROUP = 2
MOBA_BLOCKS_PER_STEP = 8


def _own_lane_mask(kvh_of_row, shape):
    lane = lax.broadcasted_iota(jnp.int32, shape, 1)
    return (lane & (N_KV_HEADS - 1)) == kvh_of_row


def _suffix_sum_keys(x):
    n = x.shape[1]
    lane = lax.broadcasted_iota(jnp.int32, x.shape, 1)
    d = N_KV_HEADS
    while d < n:
        shifted = pltpu.roll(x, n - d, axis=1)
        x = x + jnp.where(lane + d < n, shifted, 0.0)
        d *= 2
    return x


def _sb_decode_kernel(pt_ref, q_ref, k_hbm, v_hbm, o_ref, kbuf, vbuf, sems, acc_ref, run_ref, *, n_pages, gp):
    b = pl.program_id(0)
    n_groups = n_pages // gp

    def page_copies(g, slot):
        copies = []
        for u in range(gp):
            page = pt_ref[b * n_pages + (n_pages - 1 - (g * gp + u))]
            rows = pl.ds(pl.multiple_of(page * PAGE_ROWS, PAGE_ROWS), PAGE_ROWS)
            copies.append(pltpu.make_async_copy(k_hbm.at[rows], kbuf.at[slot, u], sems.at[0, slot, u]))
            copies.append(pltpu.make_async_copy(v_hbm.at[rows], vbuf.at[slot, u], sems.at[1, slot, u]))
        return copies

    def start(g, slot):
        for c in page_copies(g, slot):
            c.start()

    def wait(g, slot):
        for c in page_copies(g, slot):
            c.wait()

    acc_ref[...] = jnp.zeros_like(acc_ref)
    run_ref[...] = jnp.zeros_like(run_ref)
    q = q_ref[0].astype(BF16)
    kvh = lax.broadcasted_iota(jnp.int32, (N_HEADS, PAGE_ROWS), 0) >> 2
    own = _own_lane_mask(kvh, (N_HEADS, PAGE_ROWS))
    start(0, 0)

    def live(_):
        return jnp.max(run_ref[...]) > SB_LOG_FLOOR

    def body(g):
        slot = g & 1
        wait(g, slot)

        @pl.when(g + 1 < n_groups)
        def _():
            start(g + 1, 1 - slot)

        for u in range(gp):
            @pl.when(live(None))
            def _():
                z = _dot_nt(q, kbuf[slot, u].astype(BF16)) * (HEAD_DIM ** -0.5)
                log_keep = jnp.where(own, -_softplus(z), 0.0)
                incl = _suffix_sum_keys(log_keep)
                w = jnp.where(own, jnp.exp(z + incl + run_ref[:, 0:1]), 0.0)
                acc_ref[...] += _dot(w.astype(BF16), vbuf[slot, u].astype(BF16))
                run_ref[...] += jnp.sum(log_keep, axis=1, keepdims=True)
        return g + 1

    g_end = lax.while_loop(lambda g: jnp.logical_and(g < n_groups, live(None)), body, jnp.int32(0))

    @pl.when(g_end < n_groups)
    def _():
        wait(g_end, g_end & 1)

    o_ref[0] = acc_ref[...]


def sb_decode_attention(q, k_rows, v_rows, page_table):
    db, n_pages = page_table.shape
    gp = SB_PAGES_PER_GROUP
    assert n_pages % gp == 0
    return pl.pallas_call(
        functools.partial(_sb_decode_kernel, n_pages=n_pages, gp=gp),
        grid_spec=pltpu.PrefetchScalarGridSpec(
            num_scalar_prefetch=1,
            grid=(db,),
            in_specs=[pl.BlockSpec((1, N_HEADS, HEAD_DIM), lambda b, pt: (b, 0, 0)),
                      pl.BlockSpec(memory_space=pl.ANY), pl.BlockSpec(memory_space=pl.ANY)],
            out_specs=pl.BlockSpec((1, N_HEADS, HEAD_DIM), lambda b, pt: (b, 0, 0)),
            scratch_shapes=[pltpu.VMEM((2, gp, PAGE_ROWS, HEAD_DIM), F32),
                            pltpu.VMEM((2, gp, PAGE_ROWS, HEAD_DIM), F32),
                            pltpu.SemaphoreType.DMA((2, 2, gp)),
                            pltpu.VMEM((N_HEADS, HEAD_DIM), F32), pltpu.VMEM((N_HEADS, LANES), F32)]),
        out_shape=jax.ShapeDtypeStruct((db, N_HEADS, HEAD_DIM), F32),
        compiler_params=_params("arbitrary"),
        name="sb_decode",
    )(page_table.reshape(-1), q, k_rows, v_rows)


def _top_k_mask(gate, n_valid, k):
    lane = lax.broadcasted_iota(jnp.int32, gate.shape, 1)
    sel = jnp.zeros(gate.shape, F32)
    picks = []
    for r in range(k):
        m = jnp.max(gate, axis=1, keepdims=True)
        idx = jnp.min(jnp.where(gate == m, lane, gate.shape[1] - 1), axis=1, keepdims=True)
        pick = lane == idx
        sel = jnp.maximum(sel, jnp.where(pick, jnp.where(r < n_valid, 1.0, 0.0), 0.0))
        gate = jnp.where(pick, -jnp.inf, gate)
        picks.append(idx)
    return sel, picks


def _top_k_mask_sublanes(gate_t, n_valid, k):
    sub = lax.broadcasted_iota(jnp.int32, gate_t.shape, 0)
    sel = jnp.zeros(gate_t.shape, F32)
    for r in range(k):
        m = jnp.max(gate_t, axis=0, keepdims=True)
        idx = jnp.min(jnp.where(gate_t == m, sub, gate_t.shape[0] - 1), axis=0, keepdims=True)
        pick = sub == idx
        sel = jnp.maximum(sel, jnp.where(pick, jnp.where(r < n_valid, 1.0, 0.0), 0.0))
        gate_t = jnp.where(pick, -jnp.inf, gate_t)
    return sel


def _moba_prompt_kernel(q_ref, k_ref, v_ref, o_ref, km_ref, *, tq, nb, nh):
    i = pl.program_id(3)
    rows = nh * tq
    scale = HEAD_DIM ** -0.5

    @pl.when(i == 0)
    def _():
        km_ref[...] = jnp.zeros_like(km_ref)
        for jb in range(nb):
            km_ref[jb:jb + 1, :] = jnp.mean(k_ref[0, jb * tq:(jb + 1) * tq, :], axis=0, keepdims=True)

    q = jnp.concatenate([q_ref[0, :, g * HEAD_DIM:(g + 1) * HEAD_DIM] for g in range(nh)], axis=0)
    cands = -(-nb // 16) * 16
    gate_t = _dot_nt(km_ref[0:cands, :].astype(BF16), q)
    gate_t = jnp.where(lax.broadcasted_iota(jnp.int32, (cands, rows), 0) < i, gate_t, -jnp.inf)
    sel_t = _top_k_mask_sublanes(gate_t, i, MOBA_TOPK)
    sel = jnp.concatenate([sel_t, jnp.zeros((LANES - cands, rows), F32)], axis=0).T.astype(BF16)

    def scores(j):
        start = pl.multiple_of(j * tq, tq)
        kb = k_ref[0, pl.ds(start, tq), :].astype(BF16)
        vb = v_ref[0, pl.ds(start, tq), :].astype(BF16)
        return _dot_nt(q, kb) * scale, vb

    s, vb = scores(i)
    t_loc = lax.broadcasted_iota(jnp.int32, (rows, tq), 0) & (tq - 1)
    s_loc = lax.broadcasted_iota(jnp.int32, (rows, tq), 1)
    s = jnp.where(s_loc <= t_loc, s, NEG_BIG)
    m = jnp.max(s, axis=1, keepdims=True)
    p = jnp.exp(s - m)
    l = jnp.sum(p, axis=1, keepdims=True)
    acc = _dot(p.astype(BF16), vb)

    def body(j, carry):
        m, l, acc = carry
        s, vb = scores(j)
        onehot = (lax.broadcasted_iota(jnp.int32, (LANES, tq), 0) == j).astype(BF16)
        chosen = _dot(sel, onehot) > 0.5
        s = jnp.where(chosen, s, NEG_BIG)
        m_new = jnp.maximum(m, jnp.max(s, axis=1, keepdims=True))
        alpha = jnp.exp(m - m_new)
        p = jnp.exp(s - m_new)
        l = alpha * l + jnp.sum(p, axis=1, keepdims=True)
        acc = alpha * acc + _dot(p.astype(BF16), vb)
        return m_new, l, acc

    m, l, acc = lax.fori_loop(0, i, body, (m, l, acc))
    out = acc / l
    for g in range(nh):
        o_ref[0, :, g * HEAD_DIM:(g + 1) * HEAD_DIM] = out[g * tq:(g + 1) * tq].astype(o_ref.dtype)


def moba_prompt_attention(q, k, v):
    bsz, L, _ = q.shape
    tq = MOBA_BLOCK
    nb = L // tq
    assert L % tq == 0 and nb <= LANES
    nh = MOBA_HEADS_PER_STEP
    parts = Q_PER_KV // nh
    gw = nh * HEAD_DIM
    return pl.pallas_call(
        functools.partial(_moba_prompt_kernel, tq=tq, nb=nb, nh=nh),
        grid=(bsz, N_KV_HEADS, parts, nb),
        in_specs=[pl.BlockSpec((1, tq, gw), lambda b, h, p, i: (b, i, h * parts + p)),
                  pl.BlockSpec((1, L, HEAD_DIM), lambda b, h, p, i: (b, 0, h)),
                  pl.BlockSpec((1, L, HEAD_DIM), lambda b, h, p, i: (b, 0, h))],
        out_specs=pl.BlockSpec((1, tq, gw), lambda b, h, p, i: (b, i, h * parts + p)),
        out_shape=jax.ShapeDtypeStruct(q.shape, BF16),
        scratch_shapes=[pltpu.VMEM((LANES, HEAD_DIM), F32)],
        compiler_params=_params("parallel", "arbitrary", "arbitrary", "arbitrary"),
        name="moba_prompt",
    )(q, k, v)


def _moba_kmean_kernel(pt_ref, *refs, bps, ppb):
    k_refs, o_ref = refs[:bps * ppb], refs[bps * ppb]
    sub = 8
    for j in range(bps):
        tot = jnp.zeros((sub, HEAD_DIM), F32)
        for h in range(ppb):
            k_ref = k_refs[j * ppb + h]
            for c in range(PAGE_ROWS // sub):
                tot = tot + k_ref[c * sub:(c + 1) * sub, :]
        mean = (tot[:N_KV_HEADS] + tot[N_KV_HEADS:]) * (1.0 / MOBA_BLOCK)
        o_ref[0, j * N_KV_HEADS:(j + 1) * N_KV_HEADS, :] = mean


def moba_cache_block_means(k_rows, page_table):
    db, n_pages = page_table.shape
    ppb = MOBA_BLOCK // PAGE_SIZE
    nblk = n_pages // ppb
    bps = MOBA_BLOCKS_PER_STEP
    assert nblk % bps == 0

    def page_spec(u):
        return pl.BlockSpec((PAGE_ROWS, HEAD_DIM), lambda b, j, pt: (pt[b * n_pages + j * (bps * ppb) + u], 0))

    return pl.pallas_call(
        functools.partial(_moba_kmean_kernel, bps=bps, ppb=ppb),
        grid_spec=pltpu.PrefetchScalarGridSpec(
            num_scalar_prefetch=1,
            grid=(db, nblk // bps),
            in_specs=[page_spec(u) for u in range(bps * ppb)],
            out_specs=pl.BlockSpec((1, bps * N_KV_HEADS, HEAD_DIM), lambda b, j, pt: (b, j, 0))),
        out_shape=jax.ShapeDtypeStruct((db, nblk * N_KV_HEADS, HEAD_DIM), F32),
        compiler_params=_params("parallel", "arbitrary"),
        name="moba_kmean",
    )(page_table.reshape(-1), *([k_rows] * (bps * ppb)))


def _moba_gate_kernel(q_ref, km_ref, o_ref, *, nblk):
    gate = _dot_nt(q_ref[0].astype(BF16), km_ref[0].astype(BF16))
    kvh = lax.broadcasted_iota(jnp.int32, gate.shape, 0) >> 2
    gate = jnp.where(_own_lane_mask(kvh, gate.shape), gate, -jnp.inf)
    _, picks = _top_k_mask(gate, nblk, MOBA_TOPK)
    lane = lax.broadcasted_iota(jnp.int32, (N_HEADS, LANES), 1)
    out = jnp.zeros((N_HEADS, LANES), jnp.int32)
    for r, idx in enumerate(picks):
        out = jnp.where(lane == r, idx >> 2, out)
    o_ref[0] = out


def moba_decode_select(q, kmean):
    db, rows, _ = kmean.shape
    nblk = rows // N_KV_HEADS
    out = pl.pallas_call(
        functools.partial(_moba_gate_kernel, nblk=nblk),
        grid=(db,),
        in_specs=[pl.BlockSpec((1, N_HEADS, HEAD_DIM), lambda b: (b, 0, 0)),
                  pl.BlockSpec((1, rows, HEAD_DIM), lambda b: (b, 0, 0))],
        out_specs=pl.BlockSpec((1, N_HEADS, LANES), lambda b: (b, 0, 0)),
        out_shape=jax.ShapeDtypeStruct((db, N_HEADS, LANES), jnp.int32),
        compiler_params=_params("parallel"),
        name="moba_gate",
    )(q, kmean)
    return out[:, :, :MOBA_TOPK]


def _moba_decode_kernel(sel_ref, pt_ref, q_ref, kn_ref, vn_ref, *refs, n_sel):
    k_refs, v_refs, o_ref = refs[:n_sel], refs[n_sel:2 * n_sel], refs[2 * n_sel]
    scale = HEAD_DIM ** -0.5
    q = jnp.broadcast_to(q_ref[0, 0], (8, HEAD_DIM))
    own = _own_lane_mask(pl.program_id(1) >> 2, (8, PAGE_ROWS))
    m = jnp.sum(q * kn_ref[0, 0], axis=1, keepdims=True) * scale
    l = jnp.ones((8, 1), F32)
    acc = jnp.broadcast_to(vn_ref[0, 0], (8, HEAD_DIM))
    qb = q.astype(BF16)
    for u in range(n_sel):
        s = jnp.where(own, _dot_nt(qb, k_refs[u][...].astype(BF16)) * scale, NEG_BIG)
        m_new = jnp.maximum(m, jnp.max(s, axis=1, keepdims=True))
        alpha = jnp.exp(m - m_new)
        p = jnp.exp(s - m_new)
        l = alpha * l + jnp.sum(p, axis=1, keepdims=True)
        acc = alpha * acc + _dot(p.astype(BF16), v_refs[u][...].astype(BF16))
        m = m_new
    o_ref[0, 0] = acc / l


def moba_decode_attention(q, k_new, v_new, sel, k_rows, v_rows, page_table):
    db, n_pages = page_table.shape
    ppb = MOBA_BLOCK // PAGE_SIZE
    n_sel = MOBA_TOPK * ppb

    def page_spec(u):
        def index(b, h, sel_r, pt):
            blk = sel_r[(b * N_HEADS + h) * MOBA_TOPK + u // ppb]
            return (pt[b * n_pages + blk * ppb + u % ppb], 0)
        return pl.BlockSpec((PAGE_ROWS, HEAD_DIM), index)

    specs = [page_spec(u) for u in range(n_sel)]
    new_spec = pl.BlockSpec((1, 1, 1, HEAD_DIM), lambda b, h, sel_r, pt: (b, h // Q_PER_KV, 0, 0))
    out = pl.pallas_call(
        functools.partial(_moba_decode_kernel, n_sel=n_sel),
        grid_spec=pltpu.PrefetchScalarGridSpec(
            num_scalar_prefetch=2,
            grid=(db, N_HEADS),
            in_specs=[pl.BlockSpec((1, 1, 1, HEAD_DIM), lambda b, h, sel_r, pt: (b, h, 0, 0)),
                      new_spec, new_spec] + specs + specs,
            out_specs=pl.BlockSpec((1, 1, 8, HEAD_DIM), lambda b, h, sel_r, pt: (b, h, 0, 0))),
        out_shape=jax.ShapeDtypeStruct((db, N_HEADS, 8, HEAD_DIM), F32),
        compiler_params=_params("parallel", "parallel"),
        name="moba_decode",
    )(sel.reshape(-1), page_table.reshape(-1), q, k_new, v_new, *([k_rows] * n_sel), *([v_rows] * n_sel))
    return out[:, :, 0, :]


def _dwconv_kernel(x_ref, hist_ref, w_ref, b_ref, g_ref, beta_ref, o_ref, pad_ref, *shift_refs, width, tl, hp, post):
    t = pl.program_id(2)

    @pl.when(t == 0)
    def _():
        pad_ref[0:hp, :] = hist_ref[0]

    @pl.when(t > 0)
    def _():
        pad_ref[0:hp, :] = pad_ref[tl:tl + hp, :]

    pad_ref[hp:hp + tl, :] = x_ref[0]

    def finish(acc):
        if post == "ln_silu":
            acc = _layer_norm_rows(acc, g_ref[...], beta_ref[...])
        return (acc * jax.nn.sigmoid(acc)).astype(o_ref.dtype)

    if not shift_refs:
        base = hp - (width - 1)
        acc = jnp.zeros(o_ref.shape[1:], F32) + b_ref[...]
        for k in range(width):
            acc = acc + pad_ref[base + k:base + k + tl, :] * w_ref[k:k + 1, :]
        o_ref[0] = finish(acc)
        return

    sh_ref, wb_ref = shift_refs
    halo = 8 * ((width - 1) // 8)
    for r in range(1, 8):
        sh_ref[r - 1] = pad_ref[hp - halo - r:hp + tl - r, :]
    rt = min(DWCONV_ROW_TILE, tl)
    tc = o_ref.shape[2]
    lc = DWCONV_LANE_CHUNK if tc % DWCONV_LANE_CHUNK == 0 else tc
    for k in range(width):
        wb_ref[k] = jnp.broadcast_to(w_ref[k:k + 1, :], (8, tc))

    def row_tile(row0):
        parts = []
        for c0 in range(0, tc, lc):
            acc = jnp.zeros((rt // 8, 8, lc), F32) + b_ref[:, c0:c0 + lc]
            for r in range(8):
                for a in range((width - 1 - r) // 8 + 1):
                    k = width - 1 - (8 * a + r)
                    if r == 0:
                        src = pad_ref[pl.ds(row0 + (hp - 8 * a), rt), c0:c0 + lc]
                    else:
                        src = sh_ref[r - 1, pl.ds(row0 + (halo - 8 * a), rt), c0:c0 + lc]
                    acc = acc + src.reshape(rt // 8, 8, lc) * wb_ref[k, :, c0:c0 + lc]
            parts.append(acc.reshape(rt, lc))
        o_ref[0, pl.ds(row0, rt), :] = finish(jnp.concatenate(parts, axis=1))

    if tl == rt:
        row_tile(0)
    else:
        def body(it, carry):
            row_tile(pl.multiple_of(it * rt, rt))
            return carry

        lax.fori_loop(0, tl // rt, body, 0)


def causal_dwconv(x, hist, w, b, post, ln_g=None, ln_b=None, out_dtype=F32):
    bsz, L, C = x.shape
    width = w.shape[0]
    hp = hist.shape[1]
    assert hp % 8 == 0 and hp >= 8 * ((width - 1) // 8) + min(7, width - 1)
    tc = C if post == "ln_silu" else _pick_tile(C, (1024, 512, 256, 128))
    tl = _pick_tile(L, (256, 128)) if L >= hp else L
    assert tl >= hp or tl == L
    wp = jnp.zeros((-(-width // 8) * 8, C), F32).at[:width].set(w.astype(F32))
    if ln_g is None:
        ln_g = jnp.ones((C,), F32)
        ln_b = jnp.zeros((C,), F32)
    row = lambda v: v.reshape(1, C).astype(F32)
    vec_spec = pl.BlockSpec((1, tc), lambda bb, c, t: (0, c))
    return pl.pallas_call(
        functools.partial(_dwconv_kernel, width=width, tl=tl, hp=hp, post=post),
        grid=(bsz, C // tc, L // tl),
        in_specs=[pl.BlockSpec((1, tl, tc), lambda bb, c, t: (bb, t, c)),
                  pl.BlockSpec((1, hp, tc), lambda bb, c, t: (bb, 0, c)),
                  pl.BlockSpec((wp.shape[0], tc), lambda bb, c, t: (0, c)),
                  vec_spec, vec_spec, vec_spec],
        out_specs=pl.BlockSpec((1, tl, tc), lambda bb, c, t: (bb, t, c)),
        out_shape=jax.ShapeDtypeStruct((bsz, L, C), out_dtype),
        scratch_shapes=[pltpu.VMEM((hp + tl, tc), F32)] + (
            [pltpu.VMEM((7, 8 * ((width - 1) // 8) + tl, tc), F32),
             pltpu.VMEM((width, 8, tc), F32)] if width > 8 else []),
        compiler_params=_params("parallel", "parallel", "arbitrary"),
        name="dwconv_" + post,
    )(x, hist, wp, row(b), row(ln_g), row(ln_b))


def _ssd_prompt_kernel(x_ref, bm_ref, cm_ref, dt_ref, z_ref, dtb_ref, alog_ref, d_ref, ng_ref, h0_ref,
                       yn_ref, h_ref, y_buf, *, chunk, hpg):
    c = pl.program_id(1)
    P, N = SSM_HEAD_DIM, SSM_D_STATE
    gw = hpg * P

    @pl.when(c == 0)
    def _():
        h_ref[...] = h0_ref[...]

    li = lax.broadcasted_iota(jnp.int32, (chunk, chunk), 0)
    si = lax.broadcasted_iota(jnp.int32, (chunk, chunk), 1)
    causal = li >= si
    lower = causal.astype(BF16)
    dt = _softplus(dt_ref[0] + dtb_ref[...])
    a = -jnp.exp(alog_ref[...])
    hi, mid, lo = _split3(dt * a)
    cum = _dot(lower, hi) + _dot(lower, mid) + _dot(lower, lo)
    cum_t = cum.T
    dt_t = dt.T
    for g in range(SSM_GROUPS):
        x = x_ref[0, :, g * gw:(g + 1) * gw]
        cmb = cm_ref[0, :, g * N:(g + 1) * N].astype(BF16)
        bmb = bm_ref[0, :, g * N:(g + 1) * N].astype(BF16)
        cb = _dot_nt(cmb, bmb)
        x_t = x.T
        for r in range(hpg):
            head = g * hpg + r
            cum_col = cum[:, head:head + 1]
            cum_row = cum_t[head:head + 1, :]
            dt_row = dt_t[head:head + 1, :]
            total = cum_t[head:head + 1, chunk - 1:chunk]
            xh = x[:, r * P:(r + 1) * P]
            decay = jnp.exp(jnp.where(causal, cum_col - cum_row, -jnp.inf))
            mix = (cb * decay * dt_row).astype(BF16)
            y = _dot(mix, xh.astype(BF16))
            h_old = h_ref[0, head]
            y = y + _dot_nt(cmb, h_old.astype(BF16)) * jnp.exp(cum_col)
            y_buf[:, r * P:(r + 1) * P] = y + d_ref[:, head:head + 1] * xh
            to_end = jnp.exp(total - cum_row) * dt_row
            s_chunk = _dot((x_t[r * P:(r + 1) * P, :] * to_end).astype(BF16), bmb)
            h_ref[0, head] = jnp.exp(total) * h_old + s_chunk
        z = z_ref[0, :, g * gw:(g + 1) * gw]
        hg = y_buf[...] * (z * jax.nn.sigmoid(z))
        hg = hg * lax.rsqrt(jnp.mean(hg * hg, axis=-1, keepdims=True) + LN_EPS)
        yn_ref[0, :, g * gw:(g + 1) * gw] = (hg * ng_ref[:, g * gw:(g + 1) * gw]).astype(yn_ref.dtype)


def ssd_prompt(xbc, dt_raw, z, dt_bias_g, a_log_g, d_g, norm_g, h0):
    bsz, L, _ = xbc.shape
    n_heads = h0.shape[1]
    G, N, P = SSM_GROUPS, SSM_D_STATE, SSM_HEAD_DIM
    hpg = n_heads // G
    d_inner = n_heads * P
    gn = G * N
    chunk = SSM_CHUNK
    assert d_inner % gn == 0 and (hpg * P) % LANES == 0 and n_heads <= LANES
    b_off = d_inner // gn
    vec_spec = pl.BlockSpec((1, LANES), lambda b, c: (0, 0))
    state_spec = pl.BlockSpec((1, n_heads, P, N), lambda b, c: (b, 0, 0, 0))
    return pl.pallas_call(
        functools.partial(_ssd_prompt_kernel, chunk=chunk, hpg=hpg),
        grid=(bsz, L // chunk),
        in_specs=[pl.BlockSpec((1, chunk, d_inner), lambda b, c: (b, c, 0)),
                  pl.BlockSpec((1, chunk, gn), lambda b, c: (b, c, b_off)),
                  pl.BlockSpec((1, chunk, gn), lambda b, c: (b, c, b_off + 1)),
                  pl.BlockSpec((1, chunk, LANES), lambda b, c: (b, c, 0)),
                  pl.BlockSpec((1, chunk, d_inner), lambda b, c: (b, c, 0)),
                  vec_spec, vec_spec, vec_spec,
                  pl.BlockSpec((1, d_inner), lambda b, c: (0, 0)),
                  state_spec],
        out_specs=[pl.BlockSpec((1, chunk, d_inner), lambda b, c: (b, c, 0)), state_spec],
        out_shape=[jax.ShapeDtypeStruct((bsz, L, d_inner), BF16),
                   jax.ShapeDtypeStruct((bsz, n_heads, P, N), F32)],
        scratch_shapes=[pltpu.VMEM((chunk, hpg * P), F32)],
        compiler_params=_params("parallel", "arbitrary"),
        name="ssd_prompt",
    )(xbc, xbc, xbc, dt_raw, z, dt_bias_g, a_log_g, d_g, norm_g.reshape(1, d_inner).astype(F32), h0)


def _ssd_step_kernel(xc_ref, bc_ref, dt_ref, dtb_ref, alog_ref, d_ref, h0_ref, y_ref, h_ref, *, n_heads):
    N = SSM_D_STATE
    hpg = n_heads // SSM_GROUPS
    dt = _softplus(dt_ref[0] + dtb_ref[...])
    da = jnp.exp(dt * (-jnp.exp(alog_ref[...])))
    for h in range(n_heads):
        g = h // hpg
        bm = bc_ref[0, :, g * N:(g + 1) * N]
        cm = bc_ref[0, :, (SSM_GROUPS + g) * N:(SSM_GROUPS + g + 1) * N]
        xcol = xc_ref[0, h]
        h_new = da[:, h:h + 1] * h0_ref[0, h] + xcol * (dt[:, h:h + 1] * bm)
        h_ref[0, h] = h_new
        y_ref[0, h] = jnp.sum(h_new * cm, axis=1, keepdims=True) + d_ref[:, h:h + 1] * xcol


def ssd_step(x_col, bc, dt_raw, dt_bias, a_log, d_skip, h0):
    db, n_heads, P, _ = x_col.shape
    N = SSM_D_STATE
    vec_spec = pl.BlockSpec((1, LANES), lambda b: (0, 0))
    state_spec = pl.BlockSpec((1, n_heads, P, N), lambda b: (b, 0, 0, 0))
    col_spec = pl.BlockSpec((1, n_heads, P, 1), lambda b: (b, 0, 0, 0))
    return pl.pallas_call(
        functools.partial(_ssd_step_kernel, n_heads=n_heads),
        grid=(db,),
        in_specs=[col_spec,
                  pl.BlockSpec((1, 1, bc.shape[2]), lambda b: (b, 0, 0)),
                  pl.BlockSpec((1, 1, LANES), lambda b: (b, 0, 0)),
                  vec_spec, vec_spec, vec_spec, state_spec],
        out_specs=[col_spec, state_spec],
        out_shape=[jax.ShapeDtypeStruct((db, n_heads, P, 1), F32),
                   jax.ShapeDtypeStruct((db, n_heads, P, N), F32)],
        compiler_params=_params("parallel"),
        name="ssd_step",
    )(x_col, bc, dt_raw, dt_bias, a_log, d_skip, h0)


def _gated_norm_kernel(y_ref, z_ref, g_ref, o_ref):
    z = z_ref[...]
    h = y_ref[...] * (z * jax.nn.sigmoid(z))
    h = h * lax.rsqrt(jnp.mean(h * h, axis=-1, keepdims=True) + LN_EPS)
    o_ref[...] = (h * g_ref[...]).astype(o_ref.dtype)


def gated_rms_norm(y, z, g):
    m, d = y.shape
    gw = d // SSM_GROUPS
    tm = _pick_tile(m, (512, 256, 128, 8))
    spec = pl.BlockSpec((tm, gw), lambda i, j: (i, j))
    return pl.pallas_call(
        _gated_norm_kernel,
        grid=(m // tm, SSM_GROUPS),
        in_specs=[spec, spec, pl.BlockSpec((1, gw), lambda i, j: (0, j))],
        out_specs=spec,
        out_shape=jax.ShapeDtypeStruct((m, d), BF16),
        compiler_params=_params("parallel", "parallel"),
        name="gated_rms_norm",
    )(y, z, g.reshape(1, d).astype(F32))


def _lane_pad(v):
    return jnp.zeros((1, LANES), F32).at[0, :v.shape[0]].set(v.astype(F32))


def _ffn_half(xp_f, xp_b, xs_f, xs_b, w1, w3, lead, w2, g, b):
    hp, hs = gated_matmul(xp_b, xs_b, w1, w3, None, None, "swiglu", BF16, lead=lead)
    return matmul_postnorm(hp, hs, w2, None, xp_f, xs_f, g, b, 0.5, lead=lead)


def _pad_rows(x, rows):
    return jnp.concatenate([x, jnp.zeros((x.shape[0], rows - x.shape[1]) + x.shape[2:], x.dtype)], axis=1)


def _front_pad_rows(x, rows):
    return jnp.concatenate([jnp.zeros((x.shape[0], rows - x.shape[1]) + x.shape[2:], x.dtype), x], axis=1)


def kernel(x_prompt, x_sample, p_prompt, p_sample, cache_sb_k, cache_sb_v, cache_moba_k, cache_moba_v, state_ssm, state_ssm_conv, state_conf_conv, page_table, ln_g, ln_b, ffn_w1, ffn_w3, ffn_w2, ple_w_proj, ple_w_gate, sb_w_qkv, sb_w_o, ssm_w_in, ssm_conv_w, ssm_conv_b, ssm_dt_bias, ssm_a_log, ssm_d, ssm_norm_g, ssm_w_out, conf_w_pw1, conf_b_pw1, conf_w_dw, conf_b_dw, conf_ln_g, conf_ln_b, conf_w_pw2, conf_b_pw2, moba_w_qkv, moba_w_o):
    bsz, seq, d_model = x_prompt.shape
    db = x_sample.shape[0]
    assert x_sample.shape[1] == 1
    mp = bsz * seq
    depth = ffn_w1.shape[0]
    bf = lambda w: w.astype(BF16)

    xp_f = x_prompt.reshape(mp, d_model)
    xs_f = x_sample.reshape(db, d_model)
    xp_b, xs_b = bf(xp_f), bf(xs_f)
    n_phys = cache_sb_k.shape[1]
    ffn_w2_b = bf(ffn_w2)
    outs ={k: [] for k in ("sb_kp", "sb_vp", "sb_ks", "sb_vs", "ssm_hp", "ssm_hs", "ssm_cp", "ssm_cs",
                            "conf_cp", "conf_cs", "mo_kp", "mo_vp", "mo_ks", "mo_vs")}

    for i in range(depth):
        m, j = i % 4, i // 4
        (xp_f, xp_b), (xs_f, xs_b) = _ffn_half(xp_f, xp_b, xs_f, xs_b, ffn_w1, ffn_w3, (i, 0), ffn_w2_b,
                                               ln_g[i, 0], ln_b[i, 0])
        g1, b1 = ln_g[i, 1], ln_b[i, 1]

        if m == 0 or m == 3:
            w_qkv = bf(sb_w_qkv[j] if m == 0 else moba_w_qkv[j])
            w_o = bf(sb_w_o[j] if m == 0 else moba_w_o[j])
            q, qs = matmul(xp_b, xs_b, w_qkv, None, BF16, 0, Q_DIM)
            k, ks = matmul(xp_b, xs_b, w_qkv, None, F32, Q_DIM, KV_DIM)
            v, vs = matmul(xp_b, xs_b, w_qkv, None, F32, Q_DIM + KV_DIM, KV_DIM)
            q = q.reshape(bsz, seq, Q_DIM)
            k = k.reshape(bsz, seq, KV_DIM)
            v = v.reshape(bsz, seq, KV_DIM)
            attend = sb_prompt_attention if m == 0 else moba_prompt_attention
            o = attend(q, k, v).reshape(mp, Q_DIM)
            kp = k.reshape(bsz, seq, N_KV_HEADS, HEAD_DIM)
            vp = v.reshape(bsz, seq, N_KV_HEADS, HEAD_DIM)
            if m == 0:
                kc = cache_sb_k[j].reshape(n_phys * PAGE_ROWS, HEAD_DIM)
                vc = cache_sb_v[j].reshape(n_phys * PAGE_ROWS, HEAD_DIM)
                os_ = sb_decode_attention(qs.reshape(db, N_HEADS, HEAD_DIM), kc, vc, page_table)
            else:
                kc = cache_moba_k[j].reshape(n_phys * PAGE_ROWS, HEAD_DIM)
                vc = cache_moba_v[j].reshape(n_phys * PAGE_ROWS, HEAD_DIM)
                qh = qs.reshape(db, N_HEADS, HEAD_DIM)
                kmean = moba_cache_block_means(kc, page_table)
                sel = moba_decode_select(qh, kmean)
                os_ = moba_decode_attention(qh.reshape(db, N_HEADS, 1, HEAD_DIM),
                                            ks.reshape(db, N_KV_HEADS, 1, HEAD_DIM),
                                            vs.reshape(db, N_KV_HEADS, 1, HEAD_DIM),
                                            sel, kc, vc, page_table)
            (xp_f, xp_b), (xs_f, xs_b) = matmul_postnorm(o, bf(os_.reshape(db, Q_DIM)), w_o, None, xp_f, xs_f,
                                                         g1, b1, 1.0)
            ks4 = ks.reshape(db, 1, N_KV_HEADS, HEAD_DIM)
            vs4 = vs.reshape(db, 1, N_KV_HEADS, HEAD_DIM)
            if m == 0:
                outs["sb_kp"].append(kp); outs["sb_vp"].append(vp); outs["sb_ks"].append(ks4); outs["sb_vs"].append(vs4)
            else:
                outs["mo_kp"].append(kp); outs["mo_vp"].append(vp); outs["mo_ks"].append(ks4); outs["mo_vs"].append(vs4)

        elif m == 1:
            n_heads = ssm_dt_bias.shape[1]
            d_inner = n_heads * SSM_HEAD_DIM
            conv_dim = ssm_conv_w.shape[2]
            hpg = n_heads // SSM_GROUPS
            w_in = bf(ssm_w_in[j])
            w_dt_cols = ssm_w_in[j][:, d_inner + conv_dim:]
            w_dt = bf(jnp.zeros((d_model, LANES), F32).at[:, :n_heads].set(w_dt_cols))
            dtb, alog, dskip = _lane_pad(ssm_dt_bias[j]), _lane_pad(ssm_a_log[j]), _lane_pad(ssm_d[j])
            w_out = bf(ssm_w_out[j])
            hp = 8
            z, zs = matmul(xp_b, xs_b, w_in, None, F32, 0, d_inner)
            xbc_raw, xbc_s = matmul(xp_b, xs_b, w_in, None, F32, d_inner, conv_dim)
            dt_raw, dt_s = matmul(xp_b, xs_b, w_dt, None, F32)
            xbc_raw = xbc_raw.reshape(bsz, seq, conv_dim)
            dt_raw = dt_raw.reshape(bsz, seq, LANES)
            xbc = causal_dwconv(xbc_raw, jnp.zeros((bsz, hp, conv_dim), F32), ssm_conv_w[j], ssm_conv_b[j], "silu")
            h0 = jnp.zeros((bsz, n_heads, SSM_HEAD_DIM, SSM_D_STATE), F32)
            yn, h_last = ssd_prompt(xbc, dt_raw, z.reshape(bsz, seq, d_inner), dtb, alog, dskip, ssm_norm_g[j], h0)
            outs["ssm_hp"].append(h_last)
            outs["ssm_cp"].append(xbc_raw[:, seq - (SSM_CONV - 1):, :])
            xbc_s = xbc_s.reshape(db, 1, conv_dim)
            dt_s = dt_s.reshape(db, 1, LANES)
            hist = _front_pad_rows(state_ssm_conv[j], hp)
            xbc_sa = causal_dwconv(_pad_rows(xbc_s, 8), hist, ssm_conv_w[j], ssm_conv_b[j], "silu")[:, :1, :]
            x_col = xbc_sa[:, 0, :d_inner].reshape(db, n_heads, SSM_HEAD_DIM, 1)
            y_col, hs = ssd_step(x_col, xbc_sa[:, :, d_inner:], dt_s, dtb, alog, dskip, state_ssm[j])
            yns = gated_rms_norm(y_col.reshape(db, d_inner), zs, ssm_norm_g[j])
            (xp_f, xp_b), (xs_f, xs_b) = matmul_postnorm(yn.reshape(mp, d_inner), yns, w_out, None, xp_f, xs_f,
                                                         g1, b1, 1.0)
            outs["ssm_hs"].append(hs)
            outs["ssm_cs"].append(jnp.concatenate([state_ssm_conv[j], xbc_s], axis=1)[:, 1:, :])

        else:
            ba, bg = conf_b_pw1[j][:d_model], conf_b_pw1[j][d_model:]
            w_pw2 = bf(conf_w_pw2[j])
            hp = 32
            u, us = gated_matmul(xp_b, xs_b, conf_w_pw1, conf_w_pw1, ba, bg, "glu", F32, lead=(j,),
                                 cola=0, colb=d_model, n=d_model)
            u = u.reshape(bsz, seq, d_model)
            us = us.reshape(db, 1, d_model)
            uc = causal_dwconv(u, jnp.zeros((bsz, hp, d_model), F32), conf_w_dw[j], conf_b_dw[j], "ln_silu",
                               conf_ln_g[j], conf_ln_b[j], BF16)
            outs["conf_cp"].append(u[:, seq - (CONF_WIDTH - 1):, :])
            hist = _front_pad_rows(state_conf_conv[j], hp)
            ucs = causal_dwconv(_pad_rows(us, 8), hist, conf_w_dw[j], conf_b_dw[j], "ln_silu",
                                conf_ln_g[j], conf_ln_b[j], BF16)[:, 0, :]
            (xp_f, xp_b), (xs_f, xs_b) = matmul_postnorm(uc.reshape(mp, d_model), ucs, w_pw2, conf_b_pw2[j],
                                                         xp_f, xs_f, g1, b1, 1.0)
            outs["conf_cs"].append(jnp.concatenate([state_conf_conv[j], us], axis=1)[:, 1:, :])

        (xp_f, xp_b), (xs_f, xs_b) = _ffn_half(xp_f, xp_b, xs_f, xs_b, ffn_w1, ffn_w3, (i, 1), ffn_w2_b,
                                               ln_g[i, 2], ln_b[i, 2])
        wg_, wp_ = bf(ple_w_gate[i]), bf(ple_w_proj[i])
        (xp_f, xp_b), (xs_f, xs_b) = ple_add(xp_b, xp_f, p_prompt.reshape(depth, mp, -1), xs_b, xs_f,
                                             p_sample.reshape(depth, db, -1), i, wg_, wp_, ln_g[i, 3], ln_b[i, 3])

    st = lambda name: jnp.stack(outs[name])
    return (xp_f.reshape(bsz, seq, d_model), xs_f.reshape(db, 1, d_model),
            st("sb_kp"), st("sb_vp"), st("sb_ks"), st("sb_vs"),
            st("ssm_hp"), st("ssm_hs"), st("ssm_cp"), st("ssm_cs"),
            st("conf_cp"), st("conf_cs"),
            st("mo_kp"), st("mo_vp"), st("mo_ks"), st("mo_vs"))
```
